```python
import jax
import jax.numpy as jnp
from jax import lax
import numpy as np

D_MODEL = 2048
BATCH = 2
SEQ = 4096
DEPTH = 2
DEC_BATCH = 128
DEC_SEQ = 4
PAST_LEN = 8192
PAGE_SIZE = 128

N_MIXERS = 2
HEAD_DIM = 64
N_HEADS = D_MODEL // HEAD_DIM
N_KV_HEADS = max(1, N_HEADS // 8)
GQA_GROUP = N_HEADS // N_KV_HEADS
QKV_WIDTH = (N_HEADS + 2 * N_KV_HEADS) * HEAD_DIM
WINDOW = 128
ATTN_BLOCK = WINDOW
ATTN_SCALE = HEAD_DIM ** -0.5
ROPE_THETA = 10000.0
D_FF = ((8 * D_MODEL // 3 + 255) // 256) * 256
D_DECAY_LORA = 96
D_AAA_LORA = 96
D_GATE_LORA = 256
N_SUBLAYERS = 3
N_MOD = 3 * N_SUBLAYERS
MACARON_WEIGHT = 0.5
RMS_EPS = 1e-6
GN_EPS = 64e-5
N_RWKV = (DEPTH + N_MIXERS - 1) // N_MIXERS
N_ATTN = DEPTH // N_MIXERS

kernel_name = 'rwkv7_swa_sink_macaron_adaln_step'


def rms_norm(x, g):
    xf = x.astype(jnp.float32)
    y = xf * lax.rsqrt(jnp.mean(xf * xf, axis=-1, keepdims=True) + RMS_EPS)
    return (y * g.astype(jnp.float32)).astype(x.dtype)


def modulate(h, shift, scale):
    return h * (1.0 + scale[:, None, :]) + shift[:, None, :]


def swiglu(h, w_in, w_out):
    gate, up = jnp.split(h @ w_in, 2, axis=-1)
    return (jax.nn.silu(gate) * up) @ w_out


def rope(x, pos):
    half = HEAD_DIM // 2
    inv_freq = ROPE_THETA ** (-jnp.arange(half, dtype=jnp.float32) / half)
    ang = pos.astype(jnp.float32)[:, None] * inv_freq[None, :]
    cos = jnp.cos(ang)[None, :, None, :]
    sin = jnp.sin(ang)[None, :, None, :]
    xf = x.astype(jnp.float32)
    x1, x2 = xf[..., :half], xf[..., half:]
    return jnp.concatenate([x1 * cos - x2 * sin, x2 * cos + x1 * sin], axis=-1).astype(x.dtype)


def rwkv7_time_mix(h, shift0, wkv0, p):
    mu, w_rkv, w0, w1, w2, a0, a1, a2, g1, g2, k_k, k_a, r_k, gn_w, gn_b, w_o = p
    B, T, D = h.shape
    f32 = jnp.float32
    prev = jnp.concatenate([shift0[:, None, :].astype(h.dtype), h[:, :-1]], axis=1)
    xx = prev - h
    xr, xw, xk, xv, xa, xg = [h + xx * mu[i] for i in range(6)]
    r = xr @ w_rkv[0]
    k = xk @ w_rkv[1]
    v = xv @ w_rkv[2]
    log_w = -jax.nn.softplus(-(w0 + jnp.tanh(xw @ w1) @ w2)) - 0.5
    a = jax.nn.sigmoid(a0 + (xa @ a1) @ a2)
    g = jax.nn.sigmoid(xg @ g1) @ g2
    heads = lambda t: t.astype(f32).reshape(B, T, N_HEADS, HEAD_DIM)
    kk = heads(k * k_k)
    kk = kk / jnp.maximum(jnp.sqrt(jnp.sum(kk * kk, axis=-1, keepdims=True)), 1e-12)
    k = k * (1.0 + (a - 1.0) * k_a)
    rh, kh, vh, ah = heads(r), heads(k), heads(v), heads(a)
    decay = jnp.exp(-jnp.exp(heads(log_w)))

    def step(S, inp):
        r_t, w_t, k_t, v_t, kk_t, a_t = inp
        s_kk = jnp.einsum('bhvk,bhk->bhv', S, kk_t)
        S = (S * w_t[:, :, None, :]
             - s_kk[..., None] * (kk_t * a_t)[:, :, None, :]
             + v_t[..., None] * k_t[:, :, None, :])
        return S, jnp.einsum('bhvk,bhk->bhv', S, r_t)

    xs = tuple(jnp.swapaxes(t, 0, 1) for t in (rh, decay, kh, vh, kk, ah))
    wkv_T, y = lax.scan(step, wkv0.astype(f32), xs)
    y = jnp.swapaxes(y, 0, 1)
    mean = jnp.mean(y, axis=-1, keepdims=True)
    var = jnp.mean(jnp.square(y - mean), axis=-1, keepdims=True)
    y = ((y - mean) * lax.rsqrt(var + GN_EPS)).reshape(B, T, D) * gn_w.astype(f32) + gn_b.astype(f32)
    bonus = jnp.sum(rh * kh * r_k.astype(f32), axis=-1, keepdims=True) * vh
    y = y + bonus.reshape(B, T, D)
    out = (y * g.astype(f32)).astype(h.dtype) @ w_o
    return out, wkv_T.astype(wkv0.dtype), h[:, -1]


def swa_qkv(h, pos, w_qkv, b_qkv):
    B, T, _ = h.shape
    qkv = h @ w_qkv + b_qkv
    nq = N_HEADS * HEAD_DIM
    nkv = N_KV_HEADS * HEAD_DIM
    q = qkv[..., :nq].reshape(B, T, N_HEADS, HEAD_DIM)
    k = qkv[..., nq:nq + nkv].reshape(B, T, N_KV_HEADS, HEAD_DIM)
    v = qkv[..., nq + nkv:].reshape(B, T, N_KV_HEADS, HEAD_DIM)
    return rope(q, pos), rope(k, pos), v


def sink_softmax(scores, visible, sink):
    s = jnp.where(visible, scores, -jnp.inf)
    sk = sink.astype(jnp.float32).reshape(N_KV_HEADS, GQA_GROUP)[:, :, None, None]
    m = jnp.maximum(jnp.max(s, axis=-1, keepdims=True), sk)
    p = jnp.exp(s - m)
    return p / (jnp.sum(p, axis=-1, keepdims=True) + jnp.exp(sk - m))


def swa_prompt_attend(q, k, v, sink):
    B, S = q.shape[:2]
    nb = S // ATTN_BLOCK
    qb = q.reshape(B, nb, ATTN_BLOCK, N_KV_HEADS, GQA_GROUP, HEAD_DIM)

    def with_prev(t):
        tb = t.reshape(B, nb, ATTN_BLOCK, N_KV_HEADS, HEAD_DIM)
        prev = jnp.concatenate([jnp.zeros_like(tb[:, :1]), tb[:, :-1]], axis=1)
        return jnp.concatenate([prev, tb], axis=2)

    kw, vw = with_prev(k), with_prev(v)
    scores = jnp.einsum('bnqkgd,bnskd->bnkgqs', qb, kw,
                        preferred_element_type=jnp.float32) * ATTN_SCALE
    blk = jnp.arange(nb)[:, None] * ATTN_BLOCK
    qpos = blk + jnp.arange(ATTN_BLOCK)[None, :]
    kpos = blk - ATTN_BLOCK + jnp.arange(2 * ATTN_BLOCK)[None, :]
    rel = qpos[:, :, None] - kpos[:, None, :]
    visible = (rel >= 0) & (rel < WINDOW) & (kpos[:, None, :] >= 0)
    probs = sink_softmax(scores, visible[None, :, None, None], sink)
    out = jnp.einsum('bnkgqs,bnskd->bnqkgd', probs.astype(v.dtype), vw)
    return out.reshape(B, S, N_HEADS * HEAD_DIM)


def swa_cached_attend(q, k_new, v_new, k_cache, v_cache, sink, past_len):
    B, T = q.shape[:2]
    W = k_cache.shape[1]
    kc = jnp.concatenate([k_cache, k_new.astype(k_cache.dtype)], axis=1)
    vc = jnp.concatenate([v_cache, v_new.astype(v_cache.dtype)], axis=1)
    qg = q.reshape(B, T, N_KV_HEADS, GQA_GROUP, HEAD_DIM)
    scores = jnp.einsum('btkgd,bskd->bkgts', qg, kc,
                        preferred_element_type=jnp.float32) * ATTN_SCALE
    qpos = past_len + jnp.arange(T)
    kpos = past_len - W + jnp.arange(W + T)
    rel = qpos[:, None] - kpos[None, :]
    visible = (rel >= 0) & (rel < WINDOW)
    probs = sink_softmax(scores, visible, sink)
    out = jnp.einsum('bkgts,bskd->btkgd', probs.astype(vc.dtype), vc)
    return out.reshape(B, T, N_HEADS * HEAD_DIM), kc[:, -W:], vc[:, -W:]


def trunk(x, c, pos0, wkv_in, shift_in, k_in, v_in, shared, rw, sw):
    norm_g, w_ada, b_ada, w_ffn_in, w_ffn_out, final_g = shared
    B, T, D = x.shape
    new_wkv, new_shift, new_k, new_v = [], [], [], []
    for i in range(DEPTH):
        mod = (jax.nn.silu(c) @ w_ada[i] + b_ada[i]).reshape(B, N_MOD, D)
        h = modulate(rms_norm(x, norm_g[i, 0]), mod[:, 0], mod[:, 1])
        x = x + MACARON_WEIGHT * mod[:, 2, None, :] * swiglu(h, w_ffn_in[i, 0], w_ffn_out[i, 0])
        h = modulate(rms_norm(x, norm_g[i, 1]), mod[:, 3], mod[:, 4])
        j = i // N_MIXERS
        if i % N_MIXERS == 0:
            wkv0 = jnp.zeros((B, N_HEADS, HEAD_DIM, HEAD_DIM), x.dtype) if wkv_in is None else wkv_in[j]
            sh0 = jnp.zeros((B, D), x.dtype) if shift_in is None else shift_in[j]
            out, wkv_T, last = rwkv7_time_mix(h, sh0, wkv0, tuple(t[j] for t in rw))
            new_wkv.append(wkv_T)
            new_shift.append(last)
        else:
            w_qkv, b_qkv, sink, w_o, b_o = (t[j] for t in sw)
            q, k, v = swa_qkv(h, pos0 + jnp.arange(T), w_qkv, b_qkv)
            if k_in is None:
                att = swa_prompt_attend(q, k, v, sink)
                kw, vw = k[:, -WINDOW:], v[:, -WINDOW:]
            else:
                att, kw, vw = swa_cached_attend(q, k, v, k_in[j], v_in[j], sink, pos0)
            out = att @ w_o + b_o
            new_k.append(kw)
            new_v.append(vw)
        x = x + mod[:, 5, None, :] * out
        h = modulate(rms_norm(x, norm_g[i, 2]), mod[:, 6], mod[:, 7])
        x = x + MACARON_WEIGHT * mod[:, 8, None, :] * swiglu(h, w_ffn_in[i, 1], w_ffn_out[i, 1])
    y = rms_norm(x, final_g)
    return y, jnp.stack(new_wkv), jnp.stack(new_shift), jnp.stack(new_k), jnp.stack(new_v)


def setup_inputs(seed: int = 0) -> dict:
    key = jax.random.key(seed)
    keys = iter(jax.random.split(key, 48))
    f32 = jnp.float32
    D = D_MODEL
    win = min(WINDOW, PAST_LEN)

    def normal(shape, scale):
        return jax.random.normal(next(keys), shape, f32) * scale

    inp = {}
    inp['x_prompt'] = normal((BATCH, SEQ, D), 1.0)
    inp['x_sample'] = normal((DEC_BATCH, DEC_SEQ, D), 1.0)
    inp['state_rwkv_wkv'] = normal((N_RWKV, DEC_BATCH, N_HEADS, HEAD_DIM, HEAD_DIM), 1.0)
    inp['state_rwkv_shift'] = normal((N_RWKV, DEC_BATCH, D), 1.0)
    inp['cache_swa_k'] = normal((N_ATTN, DEC_BATCH, win, N_KV_HEADS, HEAD_DIM), 1.0)
    inp['cache_swa_v'] = normal((N_ATTN, DEC_BATCH, win, N_KV_HEADS, HEAD_DIM), 1.0)
    inp['c_prompt'] = normal((BATCH, D), 1.0)
    inp['c_sample'] = normal((DEC_BATCH, D), 1.0)
    inp['norm_g'] = 1.0 + normal((DEPTH, N_SUBLAYERS, D), 0.02)
    inp['w_ada'] = normal((DEPTH, D, N_MOD * D), D ** -0.5)
    inp['b_ada'] = normal((DEPTH, N_MOD * D), 0.02)
    inp['w_ffn_in'] = normal((DEPTH, 2, D, 2 * D_FF), D ** -0.5)
    inp['w_ffn_out'] = normal((DEPTH, 2, D_FF, D), D_FF ** -0.5)
    inp['rw_mu'] = jax.random.uniform(next(keys), (N_RWKV, 6, D), f32)
    inp['rw_wrkv'] = normal((N_RWKV, 3, D, D), D ** -0.5)
    inp['rw_w0'] = normal((N_RWKV, D), 0.5) - 1.0
    inp['rw_w1'] = normal((N_RWKV, D, D_DECAY_LORA), D ** -0.5)
    inp['rw_w2'] = normal((N_RWKV, D_DECAY_LORA, D), 0.5 * D_DECAY_LORA ** -0.5)
    inp['rw_a0'] = normal((N_RWKV, D), 0.1)
    inp['rw_a1'] = normal((N_RWKV, D, D_AAA_LORA), D ** -0.5)
    inp['rw_a2'] = normal((N_RWKV, D_AAA_LORA, D), D_AAA_LORA ** -0.5)
    inp['rw_g1'] = normal((N_RWKV, D, D_GATE_LORA), D ** -0.5)
    inp['rw_g2'] = normal((N_RWKV, D_GATE_LORA, D), D_GATE_LORA ** -0.5)
    inp['rw_kk'] = 0.85 + normal((N_RWKV, D), 0.02)
    inp['rw_ka'] = 1.0 + normal((N_RWKV, D), 0.02)
    inp['rw_rk'] = normal((N_RWKV, N_HEADS, HEAD_DIM), 0.1)
    inp['rw_gn_w'] = 1.0 + normal((N_RWKV, D), 0.02)
    inp['rw_gn_b'] = normal((N_RWKV, D), 0.02)
    inp['rw_wo'] = normal((N_RWKV, D, D), D ** -0.5)
    inp['sw_wqkv'] = normal((N_ATTN, D, QKV_WIDTH), D ** -0.5)
    inp['sw_bqkv'] = normal((N_ATTN, QKV_WIDTH), 0.02)
    inp['sw_sink'] = normal((N_ATTN, N_HEADS), 0.5)
    inp['sw_wo'] = normal((N_ATTN, N_HEADS * HEAD_DIM, D), (N_HEADS * HEAD_DIM) ** -0.5)
    inp['sw_bo'] = normal((N_ATTN, D), 0.02)
    inp['final_g'] = 1.0 + normal((D,), 0.02)
    return inp


def reference(x_prompt, x_sample, state_rwkv_wkv, state_rwkv_shift, cache_swa_k, cache_swa_v,
              c_prompt, c_sample, norm_g, w_ada, b_ada, w_ffn_in, w_ffn_out,
              rw_mu, rw_wrkv, rw_w0, rw_w1, rw_w2, rw_a0, rw_a1, rw_a2, rw_g1, rw_g2,
              rw_kk, rw_ka, rw_rk, rw_gn_w, rw_gn_b, rw_wo,
              sw_wqkv, sw_bqkv, sw_sink, sw_wo, sw_bo, final_g):
    shared = (norm_g, w_ada, b_ada, w_ffn_in, w_ffn_out, final_g)
    rw = (rw_mu, rw_wrkv, rw_w0, rw_w1, rw_w2, rw_a0, rw_a1, rw_a2, rw_g1, rw_g2,
          rw_kk, rw_ka, rw_rk, rw_gn_w, rw_gn_b, rw_wo)
    sw = (sw_wqkv, sw_bqkv, sw_sink, sw_wo, sw_bo)
    y_prompt, p_wkv, p_shift, p_k, p_v = trunk(
        x_prompt, c_prompt, 0, None, None, None, None, shared, rw, sw)
    y_sample, s_wkv, s_shift, s_k, s_v = trunk(
        x_sample, c_sample, PAST_LEN, state_rwkv_wkv, state_rwkv_shift,
        cache_swa_k, cache_swa_v, shared, rw, sw)
    return (y_prompt, y_sample, p_wkv, p_shift, p_k, p_v, s_wkv, s_shift, s_k, s_v)
```

```python
import functools
import math

import jax
import jax.numpy as jnp
from jax import lax
from jax.experimental import pallas as pl
from jax.experimental.pallas import tpu as pltpu

f32 = jnp.float32
bf16 = jnp.bfloat16

D_MODEL = 2048
HEAD_DIM = 64
N_HEADS = D_MODEL // HEAD_DIM
N_KV_HEADS = 4
GQA_GROUP = N_HEADS // N_KV_HEADS
WINDOW = 128
ATTN_SCALE = HEAD_DIM ** -0.5
ROPE_THETA = 10000.0
D_FF = 5632
N_MOD = 9
RMS_EPS = 1e-6
GN_EPS = 64e-5
PAST_LEN = 8192
LANES = 128
N_PAIRS = D_MODEL // LANES
LORA_PAD = 128
SLAB = 64
VMEM_LIMIT = 56 * 1024 * 1024
NEG_BIG = -1e30
ROW_TILE = 512


def _cparams(n_axes):
    return pltpu.CompilerParams(dimension_semantics=("arbitrary",) * n_axes,
                                vmem_limit_bytes=VMEM_LIMIT)


def _dot(a, b):
    return jnp.dot(a, b, preferred_element_type=f32)


def _dot_nt(a, b):
    return lax.dot_general(a, b, (((1,), (1,)), ((), ())), preferred_element_type=f32)


def _split(x, n):
    if x.dtype == bf16:
        return [x]
    parts = []
    rem = x
    for i in range(n):
        p = rem.astype(bf16)
        parts.append(p)
        if i + 1 < n:
            rem = rem - p.astype(f32)
    return parts


def _mm(a, b, pa=1, pb=1, nt=False):
    a_parts = _split(a, pa)
    b_parts = _split(b, pb)
    order = max(len(a_parts), len(b_parts))
    acc = None
    for i, x in enumerate(a_parts):
        for j, y in enumerate(b_parts):
            if i + j >= order:
                continue
            t = _dot_nt(x, y) if nt else _dot(x, y)
            acc = t if acc is None else acc + t
    return acc


def _rms_mod(x, g, shift, scale):
    y = x * lax.rsqrt(jnp.mean(x * x, axis=-1, keepdims=True) + RMS_EPS) * g
    return y * (1.0 + scale) + shift


def _mod_spec(per_row, tm, seq_len, idx):
    if per_row:
        return pl.BlockSpec((tm, D_MODEL), lambda i, j: (i, idx))
    return pl.BlockSpec((None, 1, D_MODEL), lambda i, j: ((i * tm) // seq_len, 0, idx))


def _ada_kernel(c_ref, w_ref, b_ref, o_ref):
    c = c_ref[...]
    s = (c * jax.nn.sigmoid(c)).astype(bf16)
    o_ref[...] = _dot(s, w_ref[...].astype(bf16)) + b_ref[...]


def _ada(c_all, w_ada, b_ada):
    n_layers, _, n_out = w_ada.shape
    rows = c_all.shape[0]
    tn = 1024
    return pl.pallas_call(
        _ada_kernel,
        grid=(n_layers, n_out // tn),
        in_specs=[pl.BlockSpec((rows, D_MODEL), lambda l, j: (0, 0)),
                  pl.BlockSpec((None, D_MODEL, tn), lambda l, j: (l, 0, j)),
                  pl.BlockSpec((None, 1, tn), lambda l, j: (l, 0, j))],
        out_specs=pl.BlockSpec((None, rows, tn), lambda l, j: (l, 0, j)),
        out_shape=jax.ShapeDtypeStruct((n_layers, rows, n_out), f32),
        compiler_params=_cparams(2),
        name="ada_mod",
    )(c_all, w_ada, b_ada.reshape(n_layers, 1, n_out))


def _ffn_kernel(final_norm, x_ref, ng_ref, sh_ref, sc_ref, gt_ref, wg_ref, wu_ref, wo_ref, fg_ref,
                o_ref, h_scr, acc_scr):
    f = pl.program_id(1)

    @pl.when(f == 0)
    def _():
        h_scr[...] = _rms_mod(x_ref[...], ng_ref[...], sh_ref[...], sc_ref[...]).astype(bf16)
        acc_scr[...] = jnp.zeros_like(acc_scr)

    h = h_scr[...]
    gate = _dot(h, wg_ref[...])
    up = _dot(h, wu_ref[...])
    act = (gate * jax.nn.sigmoid(gate) * up).astype(bf16)
    acc_scr[...] += _dot(act, wo_ref[...])

    @pl.when(f == pl.num_programs(1) - 1)
    def _():
        y = x_ref[...] + 0.5 * gt_ref[...] * acc_scr[...]
        if final_norm:
            y = y * lax.rsqrt(jnp.mean(y * y, axis=-1, keepdims=True) + RMS_EPS) * fg_ref[...]
        o_ref[...] = y


def _ffn(x, mod, per_row, seq_len, mod_base, norm_g, w_in, w_out, layer, slot, final_g=None):
    m = x.shape[0]
    tm = min(ROW_TILE, m)
    tf = 512
    nf = D_FF // tf
    vec = pl.BlockSpec((1, D_MODEL), lambda i, j: (0, 0))
    fg = jnp.ones((1, D_MODEL), f32) if final_g is None else final_g.reshape(1, D_MODEL)
    return pl.pallas_call(
        functools.partial(_ffn_kernel, final_g is not None),
        grid=(m // tm, nf),
        in_specs=[pl.BlockSpec((tm, D_MODEL), lambda i, j: (i, 0)),
                  vec,
                  _mod_spec(per_row, tm, seq_len, mod_base),
                  _mod_spec(per_row, tm, seq_len, mod_base + 1),
                  _mod_spec(per_row, tm, seq_len, mod_base + 2),
                  pl.BlockSpec((None, None, D_MODEL, tf), lambda i, j: (layer, slot, 0, j)),
                  pl.BlockSpec((None, None, D_MODEL, tf), lambda i, j: (layer, slot, 0, j + nf)),
                  pl.BlockSpec((None, None, tf, D_MODEL), lambda i, j: (layer, slot, j, 0)),
                  vec],
        out_specs=pl.BlockSpec((tm, D_MODEL), lambda i, j: (i, 0)),
        out_shape=jax.ShapeDtypeStruct((m, D_MODEL), f32),
        scratch_shapes=[pltpu.VMEM((tm, D_MODEL), bf16), pltpu.VMEM((tm, D_MODEL), f32)],
        compiler_params=_cparams(2),
        name="ffn",
    )(x, norm_g.reshape(1, D_MODEL), mod, mod, mod, w_in, w_in, w_out, fg)


def _rwkv_proj_kernel(seq_len, tm, h_rows,
                      x_ref, ng_ref, sh_ref, sc_ref, s0_ref, mu_ref, w1_ref, a1_ref, g1_ref,
                      wr_ref, wk_ref, wv_ref, w2_ref, a2_ref, g2_ref, w0_ref, a0_ref,
                      r_ref, ld_ref, k_ref, v_ref, a_ref, g_ref, h_ref,
                      hs, xr, xk, xv, tw, ta, sg):
    i = pl.program_id(0)
    j = pl.program_id(1)

    @pl.when(j == 0)
    def _():
        @pl.when(i == 0)
        def _():
            hs[0:8, :] = jnp.zeros((8, D_MODEL), f32)

        @pl.when(i > 0)
        def _():
            hs[0:8, :] = hs[tm:tm + 8, :]

        h = _rms_mod(x_ref[...], ng_ref[...], sh_ref[...], sc_ref[...])
        hs[8:tm + 8, :] = h
        h_ref[...] = h[tm - h_rows:tm, :]
        row = i * tm + lax.broadcasted_iota(jnp.int32, (tm, 1), 0)
        prev = jnp.where(row % seq_len == 0, s0_ref[...], hs[7:tm + 7, :])
        xx = prev - h
        mu = mu_ref[...]
        xr[...] = (h + xx * mu[0:1, :]).astype(bf16)
        xk[...] = (h + xx * mu[2:3, :]).astype(bf16)
        xv[...] = (h + xx * mu[3:4, :]).astype(bf16)
        xw = (h + xx * mu[1:2, :]).astype(bf16)
        tw[...] = jnp.tanh(_dot(xw, w1_ref[...])).astype(bf16)
        xa = (h + xx * mu[4:5, :]).astype(bf16)
        ta[...] = _dot(xa, a1_ref[...]).astype(bf16)
        xg = (h + xx * mu[5:6, :]).astype(bf16)
        sg[...] = jax.nn.sigmoid(_dot(xg, g1_ref[...])).astype(bf16)

    r_ref[...] = _dot(xr[...], wr_ref[...])
    k_ref[...] = _dot(xk[...], wk_ref[...])
    v_ref[...] = _dot(xv[...], wv_ref[...])
    z = w0_ref[...] + _dot(tw[...], w2_ref[...])
    ld_ref[...] = -jax.nn.sigmoid(z) * math.exp(-0.5)
    a_ref[...] = jax.nn.sigmoid(a0_ref[...] + _dot(ta[...], a2_ref[...]))
    g_ref[...] = _dot(sg[...], g2_ref[...])


def _rwkv_proj(x, mod, per_row, seq_len, norm_g, s0, rwp):
    m = x.shape[0]
    tm = min(ROW_TILE, m)
    tn = 256
    h_rows = tm if per_row else 8
    full = lambda shape: pl.BlockSpec(shape, lambda i, j: (0,) * len(shape))
    col = lambda rows: pl.BlockSpec((rows, tn), lambda i, j: (0, j))
    wspec = lambda which: pl.BlockSpec((None, D_MODEL, tn), lambda i, j: (which, 0, j))
    if per_row:
        s0_spec = pl.BlockSpec((tm, D_MODEL), lambda i, j: (i, 0))
    else:
        s0_spec = pl.BlockSpec((None, 1, D_MODEL), lambda i, j: ((i * tm) // seq_len, 0, 0))
    out_spec = pl.BlockSpec((tm, tn), lambda i, j: (i, j))
    out_sds = jax.ShapeDtypeStruct((m, D_MODEL), f32)
    outs = pl.pallas_call(
        functools.partial(_rwkv_proj_kernel, seq_len, tm, h_rows),
        grid=(m // tm, D_MODEL // tn),
        in_specs=[pl.BlockSpec((tm, D_MODEL), lambda i, j: (i, 0)),
                  full((1, D_MODEL)),
                  _mod_spec(per_row, tm, seq_len, 3),
                  _mod_spec(per_row, tm, seq_len, 4),
                  s0_spec,
                  full((6, D_MODEL)),
                  full((D_MODEL, LORA_PAD)), full((D_MODEL, LORA_PAD)), full((D_MODEL, 256)),
                  wspec(0), wspec(1), wspec(2),
                  col(LORA_PAD), col(LORA_PAD), col(256), col(1), col(1)],
        out_specs=[out_spec] * 6 + [pl.BlockSpec((h_rows, D_MODEL), lambda i, j: (i, 0))],
        out_shape=[out_sds] * 6 + [jax.ShapeDtypeStruct((m // tm * h_rows, D_MODEL), f32)],
        scratch_shapes=[pltpu.VMEM((tm + 8, D_MODEL), f32),
                        pltpu.VMEM((tm, D_MODEL), bf16), pltpu.VMEM((tm, D_MODEL), bf16),
                        pltpu.VMEM((tm, D_MODEL), bf16),
                        pltpu.VMEM((tm, LORA_PAD), bf16), pltpu.VMEM((tm, LORA_PAD), bf16),
                        pltpu.VMEM((tm, 256), bf16)],
        compiler_params=_cparams(2),
        name="rwkv_proj",
    )(x, norm_g.reshape(1, D_MODEL), mod, mod, s0, rwp["mu"], rwp["w1"], rwp["a1"], rwp["g1"],
      rwp["wrkv"], rwp["wrkv"], rwp["wrkv"], rwp["w2"], rwp["a2"], rwp["g2"], rwp["w0"], rwp["a0"])
    return outs


def _scan_pair(chunk, n_seq, passes, r, ld, k, v, a, g, kkp, kap, rkp, gnw, gnb, states):
    n_st = 2 * SLAB
    row = lax.broadcasted_iota(jnp.int32, (n_st, LANES), 0)
    lane = lax.broadcasted_iota(jnp.int32, (n_st, LANES), 1)
    own = (row // SLAB) == (lane // HEAD_DIM)
    blk = (row // chunk) == (lane // chunk)
    strict = blk & (lane < row)
    incl = blk & (lane <= row)

    def st(x):
        return jnp.where(own, jnp.concatenate([x, x], axis=0), 0.0)

    r64 = lax.broadcasted_iota(jnp.int32, (SLAB, SLAB), 0)
    c64 = lax.broadcasted_iota(jnp.int32, (SLAB, SLAB), 1)
    same = (r64 // chunk) == (c64 // chunk)
    tri = jnp.where(same & (c64 <= r64), 1.0, 0.0).astype(bf16)
    ones = jnp.where(same, 1.0, 0.0).astype(bf16)
    sums = _mm(jnp.concatenate([tri, ones], axis=0), ld, pb=3)
    cs, tot = sums[0:SLAB], sums[SLAB:n_st]
    cs_s = jnp.concatenate([cs, cs], axis=0)
    tot_s = jnp.concatenate([tot, tot], axis=0)
    ld_s = jnp.concatenate([ld, ld], axis=0)

    kk_raw = st(k * kkp)
    norm = jnp.sqrt(jnp.sum(kk_raw * kk_raw, axis=1, keepdims=True))
    kk = kk_raw / jnp.maximum(norm, 1e-12)
    a_s = st(a)
    k2 = st(k * (1.0 + (a - 1.0) * kap))
    b = kk * a_s
    r_s = st(r)
    v_s = st(v)
    bonus = jnp.sum(r_s * k2 * rkp, axis=1, keepdims=True) * v_s

    e_neg = jnp.exp(-cs_s)
    e_tail = jnp.exp(tot_s - cs_s)
    a_t = -kk * jnp.exp(cs_s - ld_s)
    r_t = r_s * jnp.exp(cs_s)
    k_t = k2 * e_neg
    b_t = b * e_neg
    k_h = k2 * e_tail
    b_h = b * e_tail

    lhs1 = jnp.concatenate([a_t, r_t], axis=0)
    x = _mm(lhs1, jnp.concatenate([k_t, b_t], axis=0), pa=passes, pb=passes, nt=True)
    a_k = jnp.where(strict, x[0:n_st, 0:n_st], 0.0)
    a_b = jnp.where(strict, x[0:n_st, n_st:], 0.0)
    r_k = jnp.where(incl, x[n_st:, 0:n_st], 0.0)
    r_b = jnp.where(incl, x[n_st:, n_st:], 0.0)

    if n_seq == 1:
        p1 = _mm(lhs1, states[0], pa=passes, pb=passes, nt=True)
    else:
        row_seq = (lax.broadcasted_iota(jnp.int32, (2 * n_st, 1), 0) % SLAB) // chunk
        p1 = jnp.zeros((2 * n_st, LANES), f32)
        for q in range(n_seq):
            p1 = p1 + jnp.where(row_seq == q, _mm(lhs1, states[q], pa=passes, pb=passes, nt=True), 0.0)

    u = p1[0:n_st] + _mm(a_k, v_s, pa=passes, pb=passes)
    powers = [a_b]
    for _ in range(int(math.log2(chunk)) - 1):
        powers.append(_mm(powers[-1], powers[-1], pa=passes, pb=passes))
    for p in reversed(powers):
        u = u + _mm(p, u, pa=passes, pb=passes)

    vu = jnp.concatenate([v_s, u], axis=0)
    y = p1[n_st:] + _mm(jnp.concatenate([r_k, r_b], axis=1), vu, pa=passes, pb=passes)

    mean = jnp.sum(y, axis=1, keepdims=True) * (1.0 / HEAD_DIM)
    dev = jnp.where(own, y - mean, 0.0)
    var = jnp.sum(dev * dev, axis=1, keepdims=True) * (1.0 / HEAD_DIM)
    out = jnp.where(own, dev * lax.rsqrt(var + GN_EPS) * gnw + gnb, 0.0) + bonus
    z = (out[0:SLAB] + out[SLAB:n_st]) * g

    vu_t = vu.T
    kb_h = jnp.concatenate([k_h, b_h], axis=0)
    p_tot = jnp.exp(tot)
    new_states = []
    if n_seq == 1:
        new_states.append(states[0] * p_tot[0:1, :] + _mm(vu_t, kb_h, pa=passes, pb=passes))
    else:
        col_seq = (lax.broadcasted_iota(jnp.int32, (1, 2 * n_st), 1) % SLAB) // chunk
        for q in range(n_seq):
            upd = _mm(jnp.where(col_seq == q, vu_t, 0.0), kb_h, pa=passes, pb=passes)
            new_states.append(states[q] * p_tot[q * chunk:q * chunk + 1, :] + upd)
    return z, new_states


def _scan_kernel(chunk, n_seq, n_pp, passes, has_state, *refs):
    if has_state:
        (r_ref, ld_ref, k_ref, v_ref, a_ref, g_ref, kk_ref, ka_ref, rk_ref, gw_ref, gb_ref, s_in_ref,
         z_ref, s_out_ref, s_scr) = refs
    else:
        (r_ref, ld_ref, k_ref, v_ref, a_ref, g_ref, kk_ref, ka_ref, rk_ref, gw_ref, gb_ref,
         z_ref, s_out_ref, s_scr) = refs
        s_in_ref = None
    c = pl.program_id(2)
    zero = jnp.zeros((HEAD_DIM, HEAD_DIM), f32)

    @pl.when(c == 0)
    def _():
        if has_state:
            for pp in range(n_pp):
                for q in range(n_seq):
                    s0 = s_in_ref[q, 2 * pp]
                    s1 = s_in_ref[q, 2 * pp + 1]
                    s_scr[pp * n_seq + q] = jnp.concatenate(
                        [jnp.concatenate([s0, zero], axis=1), jnp.concatenate([zero, s1], axis=1)], axis=0)
        else:
            s_scr[...] = jnp.zeros_like(s_scr)

    for pp in range(n_pp):
        sl = slice(pp * LANES, (pp + 1) * LANES)
        states = [s_scr[pp * n_seq + q] for q in range(n_seq)]
        z, new_states = _scan_pair(
            chunk, n_seq, passes, r_ref[:, sl], ld_ref[:, sl], k_ref[:, sl], v_ref[:, sl], a_ref[:, sl],
            g_ref[:, sl], kk_ref[:, sl], ka_ref[:, sl], rk_ref[:, sl], gw_ref[:, sl], gb_ref[:, sl], states)
        z_ref[:, sl] = z.astype(bf16)
        for q in range(n_seq):
            s_scr[pp * n_seq + q] = new_states[q]

    @pl.when(c == pl.num_programs(2) - 1)
    def _():
        for pp in range(n_pp):
            for q in range(n_seq):
                s = s_scr[pp * n_seq + q]
                s_out_ref[q, 2 * pp] = s[0:HEAD_DIM, 0:HEAD_DIM]
                s_out_ref[q, 2 * pp + 1] = s[HEAD_DIM:, HEAD_DIM:]


def _scan(proj, rwp, n_batch, seq_len, state_in, passes, n_pp):
    m = n_batch * seq_len
    if seq_len >= SLAB:
        chunk, n_seq = SLAB, 1
        n_chunks = seq_len // SLAB
        n_groups = n_batch
    else:
        chunk, n_seq = seq_len, SLAB // seq_len
        n_chunks = 1
        n_groups = n_batch // n_seq
    w = LANES * n_pp
    act = pl.BlockSpec((SLAB, w), lambda b, p, c: (b * n_chunks + c, p))
    vec = pl.BlockSpec((1, w), lambda b, p, c: (0, p))
    st_spec = pl.BlockSpec((n_seq, 2 * n_pp, HEAD_DIM, HEAD_DIM), lambda b, p, c: (b, p, 0, 0))
    has_state = state_in is not None
    in_specs = [act] * 6 + [vec] * 5 + ([st_spec] if has_state else [])
    args = list(proj) + [rwp["kk"], rwp["ka"], rwp["rk"], rwp["gn_w"], rwp["gn_b"]]
    if has_state:
        args.append(state_in)
    z, s_out = pl.pallas_call(
        functools.partial(_scan_kernel, chunk, n_seq, n_pp, passes, has_state),
        grid=(n_groups, N_PAIRS // n_pp, n_chunks),
        in_specs=in_specs,
        out_specs=[act, st_spec],
        out_shape=[jax.ShapeDtypeStruct((m, D_MODEL), bf16),
                   jax.ShapeDtypeStruct((n_batch, N_HEADS, HEAD_DIM, HEAD_DIM), f32)],
        scratch_shapes=[pltpu.VMEM((n_pp * n_seq, 2 * SLAB, LANES), f32)],
        compiler_params=_cparams(3),
        name="rwkv_scan",
    )(*args)
    return z, s_out


def _proj_res_kernel(z_ref, w_ref, b_ref, x_ref, gt_ref, o_ref):
    out = _dot(z_ref[...], w_ref[...]) + b_ref[...]
    o_ref[...] = x_ref[...] + gt_ref[...] * out


def _proj_res(z, w, bias, x, mod, per_row, seq_len):
    m = x.shape[0]
    tm = min(ROW_TILE, m)
    tn = 512
    if per_row:
        gate_spec = pl.BlockSpec((tm, tn), lambda i, j: (i, 5 * (D_MODEL // tn) + j))
    else:
        gate_spec = pl.BlockSpec((None, 1, tn),
                                 lambda i, j: ((i * tm) // seq_len, 0, 5 * (D_MODEL // tn) + j))
    return pl.pallas_call(
        _proj_res_kernel,
        grid=(m // tm, D_MODEL // tn),
        in_specs=[pl.BlockSpec((tm, D_MODEL), lambda i, j: (i, 0)),
                  pl.BlockSpec((D_MODEL, tn), lambda i, j: (0, j)),
                  pl.BlockSpec((1, tn), lambda i, j: (0, j)),
                  pl.BlockSpec((tm, tn), lambda i, j: (i, j)),
                  gate_spec],
        out_specs=pl.BlockSpec((tm, tn), lambda i, j: (i, j)),
        out_shape=jax.ShapeDtypeStruct((m, D_MODEL), f32),
        compiler_params=_cparams(2),
        name="proj_res",
    )(z, w, bias.reshape(1, D_MODEL), x, mod)


QKV_COLS = D_MODEL + 2 * N_KV_HEADS * LANES


def _qkv_kernel(n_rope, x_ref, ng_ref, sh_ref, sc_ref, w_ref, b_ref, cos_ref, sin_ref, o_ref, h_scr):
    j = pl.program_id(1)

    @pl.when(j == 0)
    def _():
        h_scr[...] = _rms_mod(x_ref[...], ng_ref[...], sh_ref[...], sc_ref[...]).astype(bf16)

    acc = _dot(h_scr[...], w_ref[...]) + b_ref[...]

    @pl.when(j < n_rope)
    def _():
        cos = cos_ref[...]
        sin = sin_ref[...]
        first = (lax.broadcasted_iota(jnp.int32, cos.shape, 1) % HEAD_DIM) < HEAD_DIM // 2
        for c in range(acc.shape[1] // LANES):
            xc = acc[:, c * LANES:(c + 1) * LANES]
            rot = jnp.where(first, pltpu.roll(xc, LANES - HEAD_DIM // 2, 1),
                            pltpu.roll(xc, HEAD_DIM // 2, 1))
            o_ref[:, c * LANES:(c + 1) * LANES] = xc * cos + rot * sin

    @pl.when(j >= n_rope)
    def _():
        o_ref[...] = acc


def _qkv(x, mod, per_row, seq_len, norm_g, w, b, cos, sin):
    m = x.shape[0]
    tm = min(ROW_TILE, m)
    tn = 512
    n_rope = (D_MODEL + N_KV_HEADS * LANES) // tn
    n_pos_blocks = cos.shape[0] // tm
    vec = pl.BlockSpec((1, D_MODEL), lambda i, j: (0, 0))
    tab = pl.BlockSpec((tm, LANES), lambda i, j: (i % n_pos_blocks, 0))
    return pl.pallas_call(
        functools.partial(_qkv_kernel, n_rope),
        grid=(m // tm, QKV_COLS // tn),
        in_specs=[pl.BlockSpec((tm, D_MODEL), lambda i, j: (i, 0)),
                  vec,
                  _mod_spec(per_row, tm, seq_len, 3),
                  _mod_spec(per_row, tm, seq_len, 4),
                  pl.BlockSpec((D_MODEL, tn), lambda i, j: (0, j)),
                  pl.BlockSpec((1, tn), lambda i, j: (0, j)),
                  tab, tab],
        out_specs=pl.BlockSpec((tm, tn), lambda i, j: (i, j)),
        out_shape=jax.ShapeDtypeStruct((m, QKV_COLS), f32),
        scratch_shapes=[pltpu.VMEM((tm, D_MODEL), bf16)],
        compiler_params=_cparams(2),
        name="swa_qkv",
    )(x, norm_g.reshape(1, D_MODEL), mod, mod, w, b.reshape(1, QKV_COLS), cos, sin)


def _rope_tables(positions):
    half = HEAD_DIM // 2
    inv_freq = ROPE_THETA ** (-jnp.arange(half, dtype=f32) / half)
    ang = positions.astype(f32)[:, None] * inv_freq[None, :]
    cos = jnp.tile(jnp.cos(ang), (1, LANES // half))
    sin = jnp.sin(ang)
    sin = jnp.tile(jnp.concatenate([-sin, sin], axis=1), (1, LANES // HEAD_DIM))
    return cos, sin


def _head_masks(rows):
    lane = lax.broadcasted_iota(jnp.int32, (rows, LANES), 1)
    return lane < HEAD_DIM, lane >= HEAD_DIM


def _attn_prompt_kernel(q_ref, kp_ref, kc_ref, vp_ref, vc_ref, sink_ref, o_ref):
    n = pl.program_id(1)
    blk = WINDOW
    m0, m1 = _head_masks(blk)
    m0k, m1k = _head_masks(2 * blk)
    qi = lax.broadcasted_iota(jnp.int32, (2 * blk, 2 * blk), 0) % blk
    sj = lax.broadcasted_iota(jnp.int32, (2 * blk, 2 * blk), 1)
    visible = (sj > qi) & (sj <= qi + blk) & ((n > 0) | (sj >= blk))
    top = lax.broadcasted_iota(jnp.int32, (2 * blk, 1), 0) < blk
    for c in range(N_KV_HEADS):
        sl = slice(c * LANES, (c + 1) * LANES)
        kd = jnp.concatenate([kp_ref[:, sl], kc_ref[:, sl]], axis=0).astype(bf16)
        vd = jnp.concatenate([vp_ref[:, sl], vc_ref[:, sl]], axis=0)
        vcat = jnp.concatenate([jnp.where(m0k, vd, 0.0), jnp.where(m1k, vd, 0.0)], axis=0).astype(bf16)
        for jj in range(GQA_GROUP // 2):
            pair = c * (GQA_GROUP // 2) + jj
            qp = q_ref[:, pair * LANES:(pair + 1) * LANES]
            qs = jnp.concatenate([jnp.where(m0, qp, 0.0), jnp.where(m1, qp, 0.0)], axis=0).astype(bf16)
            s = _dot_nt(qs, kd) * ATTN_SCALE
            s = jnp.where(visible, s, NEG_BIG)
            sk = jnp.where(top, sink_ref[2 * pair], sink_ref[2 * pair + 1])
            mx = jnp.maximum(jnp.max(s, axis=1, keepdims=True), sk)
            p = jnp.exp(s - mx)
            den = jnp.sum(p, axis=1, keepdims=True) + jnp.exp(sk - mx)
            p = (p / den).astype(bf16)
            pcat = jnp.concatenate([p[0:blk], p[blk:]], axis=1)
            o_ref[:, pair * LANES:(pair + 1) * LANES] = _dot(pcat, vcat).astype(bf16)


def _attn_prompt(qkv, sink, n_batch, seq_len):
    nb = seq_len // WINDOW
    kw = N_KV_HEADS * LANES
    k_blk = D_MODEL // kw
    cur = lambda off: pl.BlockSpec((WINDOW, kw), lambda b, n: (b * nb + n, k_blk + off))
    prev = lambda off: pl.BlockSpec((WINDOW, kw), lambda b, n: (b * nb + jnp.maximum(n - 1, 0), k_blk + off))
    return pl.pallas_call(
        _attn_prompt_kernel,
        grid=(n_batch, nb),
        in_specs=[pl.BlockSpec((WINDOW, D_MODEL), lambda b, n: (b * nb + n, 0)),
                  prev(0), cur(0), prev(1), cur(1),
                  pl.BlockSpec(memory_space=pltpu.SMEM)],
        out_specs=pl.BlockSpec((WINDOW, D_MODEL), lambda b, n: (b * nb + n, 0)),
        out_shape=jax.ShapeDtypeStruct((n_batch * seq_len, D_MODEL), bf16),
        compiler_params=_cparams(2),
        name="swa_prompt",
    )(qkv, qkv, qkv, qkv, qkv, sink)


SEQ_PER_GROUP = 4


def _attn_cached_kernel(n_tok, groups, q_ref, kn_ref, vn_ref, kc_ref, vc_ref, sink_ref, o_ref):
    rows = SEQ_PER_GROUP * n_tok
    n_st = GQA_GROUP * rows
    m0, m1 = _head_masks(rows)
    m0c, m1c = _head_masks(WINDOW)
    srow = lax.broadcasted_iota(jnp.int32, (n_st, 1), 0)
    row_seq = (srow % rows) // n_tok
    row_tok = srow % n_tok
    key_c = lax.broadcasted_iota(jnp.int32, (n_st, WINDOW), 1)
    vis_c = key_c > row_tok
    key_n = lax.broadcasted_iota(jnp.int32, (n_st, rows), 1)
    vis_n = ((key_n // n_tok) == row_seq) & ((key_n % n_tok) <= row_tok)
    for gi in range(groups):
        rs = slice(gi * rows, (gi + 1) * rows)
        for cp in range(N_KV_HEADS // 2):
            kc_pair = [kc_ref[gi * SEQ_PER_GROUP + b, :, cp * LANES:(cp + 1) * LANES]
                       for b in range(SEQ_PER_GROUP)]
            vc_pair = [vc_ref[gi * SEQ_PER_GROUP + b, :, cp * LANES:(cp + 1) * LANES]
                       for b in range(SEQ_PER_GROUP)]
            kc_sw = [pltpu.roll(t, HEAD_DIM, 1) for t in kc_pair]
            vc_sw = [pltpu.roll(t, HEAD_DIM, 1) for t in vc_pair]
            for ce in range(2):
                c = 2 * cp + ce
                keep = m0c if ce == 0 else m1c
                sl = slice(c * LANES, (c + 1) * LANES)
                kn = kn_ref[rs, sl].astype(bf16)
                vn = vn_ref[rs, sl].astype(bf16)
                pieces = []
                for jj in range(GQA_GROUP // 2):
                    pair = c * (GQA_GROUP // 2) + jj
                    qp = q_ref[rs, pair * LANES:(pair + 1) * LANES]
                    pieces += [jnp.where(m0, qp, 0.0), jnp.where(m1, qp, 0.0)]
                qs = jnp.concatenate(pieces, axis=0).astype(bf16)
                s_c = jnp.zeros((n_st, WINDOW), f32)
                for b in range(SEQ_PER_GROUP):
                    kx = jnp.where(keep, kc_pair[b], kc_sw[b]).astype(bf16)
                    s_c = jnp.where(row_seq == b, _dot_nt(qs, kx), s_c)
                s_c = jnp.where(vis_c, s_c * ATTN_SCALE, NEG_BIG)
                s_n = jnp.where(vis_n, _dot_nt(qs, kn) * ATTN_SCALE, NEG_BIG)
                sk = sink_ref[c]
                sk = sk[:, 0:1]
                mx = jnp.maximum(jnp.maximum(jnp.max(s_c, axis=1, keepdims=True),
                                             jnp.max(s_n, axis=1, keepdims=True)), sk)
                p_c = jnp.exp(s_c - mx)
                p_n = jnp.exp(s_n - mx)
                den = (jnp.sum(p_c, axis=1, keepdims=True) + jnp.sum(p_n, axis=1, keepdims=True)
                       + jnp.exp(sk - mx))
                p_c = (p_c / den).astype(bf16)
                p_n = (p_n / den).astype(bf16)
                o = _dot(p_n, vn)
                for b in range(SEQ_PER_GROUP):
                    vx = jnp.where(keep, vc_pair[b], vc_sw[b]).astype(bf16)
                    o = o + jnp.where(row_seq == b, _dot(p_c, vx), 0.0)
                for jj in range(GQA_GROUP // 2):
                    pair = c * (GQA_GROUP // 2) + jj
                    o0 = o[(2 * jj) * rows:(2 * jj + 1) * rows]
                    o1 = o[(2 * jj + 1) * rows:(2 * jj + 2) * rows]
                    o_ref[rs, pair * LANES:(pair + 1) * LANES] = jnp.where(m0, o0, o1).astype(bf16)


def _attn_cached(qkv, k_cache, v_cache, sink, n_batch, n_tok):
    groups = 2
    seqs = SEQ_PER_GROUP * groups
    rows = seqs * n_tok
    kw = N_KV_HEADS * LANES
    k_blk = D_MODEL // kw
    n_st = GQA_GROUP * SEQ_PER_GROUP * n_tok
    head = (jnp.arange(N_KV_HEADS)[:, None] * GQA_GROUP
            + (jnp.arange(n_st)[None, :] // (SEQ_PER_GROUP * n_tok)))
    sink_tab = jnp.broadcast_to(sink[head][:, :, None], (N_KV_HEADS, n_st, LANES))
    cache_spec = pl.BlockSpec((seqs, WINDOW, N_KV_HEADS * HEAD_DIM), lambda i: (i, 0, 0))
    return pl.pallas_call(
        functools.partial(_attn_cached_kernel, n_tok, groups),
        grid=(n_batch // seqs,),
        in_specs=[pl.BlockSpec((rows, D_MODEL), lambda i: (i, 0)),
                  pl.BlockSpec((rows, kw), lambda i: (i, k_blk)),
                  pl.BlockSpec((rows, kw), lambda i: (i, k_blk + 1)),
                  cache_spec, cache_spec,
                  pl.BlockSpec((N_KV_HEADS, n_st, LANES), lambda i: (0, 0, 0))],
        out_specs=pl.BlockSpec((rows, D_MODEL), lambda i: (i, 0)),
        out_shape=jax.ShapeDtypeStruct((n_batch * n_tok, D_MODEL), bf16),
        compiler_params=_cparams(1),
        name="swa_cached",
    )(qkv, qkv, qkv, k_cache, v_cache, sink_tab)


def _undup(t):
    return t.reshape(t.shape[0], N_KV_HEADS, 2, HEAD_DIM)[:, :, 0, :]


def _trunk(x, mod_all, per_row, n_batch, seq_len, pos0, wkv_in, shift_in, k_in, v_in, wts):
    scan_passes, scan_pairs = wts["scan_cfg"]
    mod = mod_all[0]
    x = _ffn(x, mod, per_row, seq_len, 0, wts["norm_g"][0, 0], wts["w_in"], wts["w_out"], 0, 0)
    if per_row:
        s0 = jnp.repeat(shift_in[0], seq_len, axis=0)
    else:
        s0 = jnp.zeros((n_batch, 1, D_MODEL), f32)
    *proj, h_tail = _rwkv_proj(x, mod, per_row, seq_len, wts["norm_g"][0, 1], s0, wts["rw"])
    z, wkv_out = _scan(proj, wts["rw"], n_batch, seq_len, None if wkv_in is None else wkv_in[0],
                       scan_passes, scan_pairs)
    if per_row:
        shift_out = h_tail.reshape(n_batch, seq_len, D_MODEL)[:, -1]
    else:
        shift_out = h_tail.reshape(n_batch, -1, 8, D_MODEL)[:, -1, -1]
    x = _proj_res(z, wts["rw"]["wo"], jnp.zeros((D_MODEL,), f32), x, mod, per_row, seq_len)
    x = _ffn(x, mod, per_row, seq_len, 6, wts["norm_g"][0, 2], wts["w_in"], wts["w_out"], 0, 1)
    mod = mod_all[1]
    x = _ffn(x, mod, per_row, seq_len, 0, wts["norm_g"][1, 0], wts["w_in"], wts["w_out"], 1, 0)
    if per_row:
        positions = jnp.tile(pos0 + jnp.arange(seq_len), n_batch)
    else:
        positions = pos0 + jnp.arange(seq_len)
    cos, sin = _rope_tables(positions)
    qkv = _qkv(x, mod, per_row, seq_len, wts["norm_g"][1, 1], wts["sw_wqkv"], wts["sw_bqkv"], cos, sin)
    kw = N_KV_HEADS * LANES
    if k_in is None:
        att = _attn_prompt(qkv, wts["sw_sink"], n_batch, seq_len)
        tail = qkv.reshape(n_batch, seq_len, QKV_COLS)[:, -WINDOW:, D_MODEL:]
        k_new = _undup(tail[..., :kw].reshape(n_batch * WINDOW, kw)).reshape(n_batch, WINDOW, N_KV_HEADS, HEAD_DIM)
        v_new = _undup(tail[..., kw:].reshape(n_batch * WINDOW, kw)).reshape(n_batch, WINDOW, N_KV_HEADS, HEAD_DIM)
    else:
        win = k_in.shape[2]
        att = _attn_cached(qkv, k_in[0].reshape(n_batch, win, N_KV_HEADS * HEAD_DIM),
                           v_in[0].reshape(n_batch, win, N_KV_HEADS * HEAD_DIM),
                           wts["sw_sink"], n_batch, seq_len)
        k_tok = _undup(qkv[:, D_MODEL:D_MODEL + kw]).reshape(n_batch, seq_len, N_KV_HEADS, HEAD_DIM)
        v_tok = _undup(qkv[:, D_MODEL + kw:]).reshape(n_batch, seq_len, N_KV_HEADS, HEAD_DIM)
        k_new = jnp.concatenate([k_in[0], k_tok], axis=1)[:, -win:]
        v_new = jnp.concatenate([v_in[0], v_tok], axis=1)[:, -win:]
    x = _proj_res(att, wts["sw_wo"], wts["sw_bo"], x, mod, per_row, seq_len)
    y = _ffn(x, mod, per_row, seq_len, 6, wts["norm_g"][1, 2], wts["w_in"], wts["w_out"], 1, 1,
             final_g=wts["final_g"])
    return y, wkv_out[None], shift_out[None], k_new[None], v_new[None]


def _dup_heads(w):
    lead = w.shape[:-1]
    w4 = w.reshape(lead + (N_KV_HEADS, 1, HEAD_DIM))
    return jnp.broadcast_to(w4, lead + (N_KV_HEADS, 2, HEAD_DIM)).reshape(lead + (N_KV_HEADS * LANES,))


def _pad_cols(w):
    return jnp.pad(w, ((0, 0), (0, LORA_PAD - w.shape[1])))


def _pad_rows(w):
    return jnp.pad(w, ((0, LORA_PAD - w.shape[0]), (0, 0)))


def kernel(x_prompt, x_sample, state_rwkv_wkv, state_rwkv_shift, cache_swa_k, cache_swa_v, c_prompt, c_sample, norm_g, w_ada, b_ada, w_ffn_in, w_ffn_out, rw_mu, rw_wrkv, rw_w0, rw_w1, rw_w2, rw_a0, rw_a1, rw_a2, rw_g1, rw_g2, rw_kk, rw_ka, rw_rk, rw_gn_w, rw_gn_b, rw_wo, sw_wqkv, sw_bqkv, sw_sink, sw_wo, sw_bo, final_g):
    n_p, seq_p, _ = x_prompt.shape
    n_s, seq_s, _ = x_sample.shape
    nq = N_HEADS * HEAD_DIM
    nkv = N_KV_HEADS * HEAD_DIM
    row = lambda t: t.reshape(1, D_MODEL)
    rw = dict(
        mu=rw_mu[0], wrkv=rw_wrkv[0].astype(bf16),
        w1=_pad_cols(rw_w1[0]).astype(bf16), a1=_pad_cols(rw_a1[0]).astype(bf16), g1=rw_g1[0].astype(bf16),
        w2=_pad_rows(rw_w2[0]).astype(bf16), a2=_pad_rows(rw_a2[0]).astype(bf16), g2=rw_g2[0].astype(bf16),
        w0=row(rw_w0[0]), a0=row(rw_a0[0]), kk=row(rw_kk[0]), ka=row(rw_ka[0]), rk=row(rw_rk[0]),
        gn_w=row(rw_gn_w[0]), gn_b=row(rw_gn_b[0]), wo=rw_wo[0].astype(bf16))
    wq = sw_wqkv[0]
    bq = sw_bqkv[0]
    wts = dict(
        norm_g=norm_g, final_g=final_g,
        w_in=w_ffn_in.astype(bf16), w_out=w_ffn_out.astype(bf16), rw=rw,
        sw_wqkv=jnp.concatenate([wq[:, :nq], _dup_heads(wq[:, nq:nq + nkv]), _dup_heads(wq[:, nq + nkv:])],
                                axis=1).astype(bf16),
        sw_bqkv=jnp.concatenate([bq[:nq], _dup_heads(bq[nq:nq + nkv]), _dup_heads(bq[nq + nkv:])]),
        sw_sink=sw_sink[0], sw_wo=sw_wo[0].astype(bf16), sw_bo=sw_bo[0])

    n_c = n_p + n_s
    pad = (-n_c) % 8
    c_all = jnp.concatenate([c_prompt, c_sample, jnp.zeros((pad, D_MODEL), f32)], axis=0)
    mod = _ada(c_all, w_ada, b_ada)
    mod_p = [mod[l, :n_p].reshape(n_p, 1, N_MOD * D_MODEL) for l in range(mod.shape[0])]
    mod_s = [jnp.repeat(mod[l, n_p:n_c], seq_s, axis=0) for l in range(mod.shape[0])]

    wts["scan_cfg"] = (3, 2)
    y_p, p_wkv, p_shift, p_k, p_v = _trunk(
        x_prompt.reshape(n_p * seq_p, D_MODEL), mod_p, False, n_p, seq_p, 0, None, None, None, None, wts)
    wts["scan_cfg"] = (3, 1)
    y_s, s_wkv, s_shift, s_k, s_v = _trunk(
        x_sample.reshape(n_s * seq_s, D_MODEL), mod_s, True, n_s, seq_s, PAST_LEN,
        state_rwkv_wkv, state_rwkv_shift, cache_swa_k, cache_swa_v, wts)
    return (y_p.reshape(n_p, seq_p, D_MODEL), y_s.reshape(n_s, seq_s, D_MODEL),
            p_wkv, p_shift, p_k, p_v, s_wkv, s_shift, s_k, s_v)
```

```python
import functools
import math

import jax
import jax.numpy as jnp
from jax import lax
from jax.experimental import pallas as pl
from jax.experimental.pallas import tpu as pltpu

f32 = jnp.float32
bf16 = jnp.bfloat16

D_MODEL = 2048
HEAD_DIM = 64
N_HEADS = D_MODEL // HEAD_DIM
N_KV_HEADS = 4
GQA_GROUP = N_HEADS // N_KV_HEADS
WINDOW = 128
ATTN_SCALE = HEAD_DIM ** -0.5
ROPE_THETA = 10000.0
D_FF = 5632
N_MOD = 9
RMS_EPS = 1e-6
GN_EPS = 64e-5
PAST_LEN = 8192
LANES = 128
N_PAIRS = D_MODEL // LANES
LORA_PAD = 128
SLAB = 64
VMEM_LIMIT = 56 * 1024 * 1024
NEG_BIG = -1e30
ROW_TILE = 512


def _cparams(n_axes):
    return pltpu.CompilerParams(dimension_semantics=("arbitrary",) * n_axes,
                                vmem_limit_bytes=VMEM_LIMIT)


def _dot(a, b):
    return jnp.dot(a, b, preferred_element_type=f32)


def _dot_nt(a, b):
    return lax.dot_general(a, b, (((1,), (1,)), ((), ())), preferred_element_type=f32)


def _split(x, n):
    if x.dtype == bf16:
        return [x]
    parts = []
    rem = x
    for i in range(n):
        p = rem.astype(bf16)
        parts.append(p)
        if i + 1 < n:
            rem = rem - p.astype(f32)
    return parts


def _mm(a, b, pa=1, pb=1, nt=False):
    a_parts = _split(a, pa)
    b_parts = _split(b, pb)
    order = max(len(a_parts), len(b_parts))
    acc = None
    for i, x in enumerate(a_parts):
        for j, y in enumerate(b_parts):
            if i + j >= order:
                continue
            t = _dot_nt(x, y) if nt else _dot(x, y)
            acc = t if acc is None else acc + t
    return acc


def _rms_mod(x, g, shift, scale):
    y = x * lax.rsqrt(jnp.mean(x * x, axis=-1, keepdims=True) + RMS_EPS) * g
    return y * (1.0 + scale) + shift


def _mod_spec(per_row, tm, seq_len, idx):
    if per_row:
        return pl.BlockSpec((tm, D_MODEL), lambda i, j: (i, idx))
    return pl.BlockSpec((None, 1, D_MODEL), lambda i, j: ((i * tm) // seq_len, 0, idx))


def _ada_kernel(c_ref, w_ref, b_ref, o_ref):
    c = c_ref[...]
    s = (c * jax.nn.sigmoid(c)).astype(bf16)
    o_ref[...] = _dot(s, w_ref[...].astype(bf16)) + b_ref[...]


def _ada(c_all, w_ada, b_ada):
    n_layers, _, n_out = w_ada.shape
    rows = c_all.shape[0]
    tn = 1024
    return pl.pallas_call(
        _ada_kernel,
        grid=(n_layers, n_out // tn),
        in_specs=[pl.BlockSpec((rows, D_MODEL), lambda l, j: (0, 0)),
                  pl.BlockSpec((None, D_MODEL, tn), lambda l, j: (l, 0, j)),
                  pl.BlockSpec((None, 1, tn), lambda l, j: (l, 0, j))],
        out_specs=pl.BlockSpec((None, rows, tn), lambda l, j: (l, 0, j)),
        out_shape=jax.ShapeDtypeStruct((n_layers, rows, n_out), f32),
        compiler_params=_cparams(2),
        name="ada_mod",
    )(c_all, w_ada, b_ada.reshape(n_layers, 1, n_out))


def _ffn_kernel(final_norm, x_ref, ng_ref, sh_ref, sc_ref, gt_ref, wg_ref, wu_ref, wo_ref, fg_ref,
                o_ref, h_scr, acc_scr):
    f = pl.program_id(1)

    @pl.when(f == 0)
    def _():
        h_scr[...] = _rms_mod(x_ref[...], ng_ref[...], sh_ref[...], sc_ref[...]).astype(bf16)
        acc_scr[...] = jnp.zeros_like(acc_scr)

    h = h_scr[...]
    gate = _dot(h, wg_ref[...])
    up = _dot(h, wu_ref[...])
    act = (gate * jax.nn.sigmoid(gate) * up).astype(bf16)
    acc_scr[...] += _dot(act, wo_ref[...])

    @pl.when(f == pl.num_programs(1) - 1)
    def _():
        y = x_ref[...] + 0.5 * gt_ref[...] * acc_scr[...]
        if final_norm:
            y = y * lax.rsqrt(jnp.mean(y * y, axis=-1, keepdims=True) + RMS_EPS) * fg_ref[...]
        o_ref[...] = y


def _ffn(x, mod, per_row, seq_len, mod_base, norm_g, w_in, w_out, layer, slot, final_g=None):
    m = x.shape[0]
    tm = min(ROW_TILE, m)
    tf = 512
    nf = D_FF // tf
    vec = pl.BlockSpec((1, D_MODEL), lambda i, j: (0, 0))
    fg = jnp.ones((1, D_MODEL), f32) if final_g is None else final_g.reshape(1, D_MODEL)
    return pl.pallas_call(
        functools.partial(_ffn_kernel, final_g is not None),
        grid=(m // tm, nf),
        in_specs=[pl.BlockSpec((tm, D_MODEL), lambda i, j: (i, 0)),
                  vec,
                  _mod_spec(per_row, tm, seq_len, mod_base),
                  _mod_spec(per_row, tm, seq_len, mod_base + 1),
                  _mod_spec(per_row, tm, seq_len, mod_base + 2),
                  pl.BlockSpec((None, None, D_MODEL, tf), lambda i, j: (layer, slot, 0, j)),
                  pl.BlockSpec((None, None, D_MODEL, tf), lambda i, j: (layer, slot, 0, j + nf)),
                  pl.BlockSpec((None, None, tf, D_MODEL), lambda i, j: (layer, slot, j, 0)),
                  vec],
        out_specs=pl.BlockSpec((tm, D_MODEL), lambda i, j: (i, 0)),
        out_shape=jax.ShapeDtypeStruct((m, D_MODEL), f32),
        scratch_shapes=[pltpu.VMEM((tm, D_MODEL), bf16), pltpu.VMEM((tm, D_MODEL), f32)],
        compiler_params=_cparams(2),
        name="ffn",
    )(x, norm_g.reshape(1, D_MODEL), mod, mod, mod, w_in, w_in, w_out, fg)


def _rwkv_proj_kernel(seq_len, tm, h_rows,
                      x_ref, ng_ref, sh_ref, sc_ref, s0_ref, mu_ref, w1_ref, a1_ref, g1_ref,
                      wr_ref, wk_ref, wv_ref, w2_ref, a2_ref, g2_ref, w0_ref, a0_ref,
                      r_ref, ld_ref, k_ref, v_ref, a_ref, g_ref, h_ref,
                      hs, xr, xk, xv, tw, ta, sg):
    i = pl.program_id(0)
    j = pl.program_id(1)

    @pl.when(j == 0)
    def _():
        @pl.when(i == 0)
        def _():
            hs[0:8, :] = jnp.zeros((8, D_MODEL), f32)

        @pl.when(i > 0)
        def _():
            hs[0:8, :] = hs[tm:tm + 8, :]

        h = _rms_mod(x_ref[...], ng_ref[...], sh_ref[...], sc_ref[...])
        hs[8:tm + 8, :] = h
        h_ref[...] = h[tm - h_rows:tm, :]
        row = i * tm + lax.broadcasted_iota(jnp.int32, (tm, 1), 0)
        prev = jnp.where(row % seq_len == 0, s0_ref[...], hs[7:tm + 7, :])
        xx = prev - h
        mu = mu_ref[...]
        xr[...] = (h + xx * mu[0:1, :]).astype(bf16)
        xk[...] = (h + xx * mu[2:3, :]).astype(bf16)
        xv[...] = (h + xx * mu[3:4, :]).astype(bf16)
        xw = (h + xx * mu[1:2, :]).astype(bf16)
        tw[...] = jnp.tanh(_dot(xw, w1_ref[...])).astype(bf16)
        xa = (h + xx * mu[4:5, :]).astype(bf16)
        ta[...] = _dot(xa, a1_ref[...]).astype(bf16)
        xg = (h + xx * mu[5:6, :]).astype(bf16)
        sg[...] = jax.nn.sigmoid(_dot(xg, g1_ref[...])).astype(bf16)

    r_ref[...] = _dot(xr[...], wr_ref[...])
    k_ref[...] = _dot(xk[...], wk_ref[...])
    v_ref[...] = _dot(xv[...], wv_ref[...])
    z = w0_ref[...] + _dot(tw[...], w2_ref[...])
    ld_ref[...] = -jax.nn.sigmoid(z) * math.exp(-0.5)
    a_ref[...] = jax.nn.sigmoid(a0_ref[...] + _dot(ta[...], a2_ref[...]))
    g_ref[...] = _dot(sg[...], g2_ref[...])


def _rwkv_proj(x, mod, per_row, seq_len, norm_g, s0, rwp):
    m = x.shape[0]
    tm = min(ROW_TILE, m)
    tn = 256
    h_rows = tm if per_row else 8
    full = lambda shape: pl.BlockSpec(shape, lambda i, j: (0,) * len(shape))
    col = lambda rows: pl.BlockSpec((rows, tn), lambda i, j: (0, j))
    wspec = lambda which: pl.BlockSpec((None, D_MODEL, tn), lambda i, j: (which, 0, j))
    if per_row:
        s0_spec = pl.BlockSpec((tm, D_MODEL), lambda i, j: (i, 0))
    else:
        s0_spec = pl.BlockSpec((None, 1, D_MODEL), lambda i, j: ((i * tm) // seq_len, 0, 0))
    out_spec = pl.BlockSpec((tm, tn), lambda i, j: (i, j))
    out_sds = jax.ShapeDtypeStruct((m, D_MODEL), f32)
    outs = pl.pallas_call(
        functools.partial(_rwkv_proj_kernel, seq_len, tm, h_rows),
        grid=(m // tm, D_MODEL // tn),
        in_specs=[pl.BlockSpec((tm, D_MODEL), lambda i, j: (i, 0)),
                  full((1, D_MODEL)),
                  _mod_spec(per_row, tm, seq_len, 3),
                  _mod_spec(per_row, tm, seq_len, 4),
                  s0_spec,
                  full((6, D_MODEL)),
                  full((D_MODEL, LORA_PAD)), full((D_MODEL, LORA_PAD)), full((D_MODEL, 256)),
                  wspec(0), wspec(1), wspec(2),
                  col(LORA_PAD), col(LORA_PAD), col(256), col(1), col(1)],
        out_specs=[out_spec] * 6 + [pl.BlockSpec((h_rows, D_MODEL), lambda i, j: (i, 0))],
        out_shape=[out_sds] * 6 + [jax.ShapeDtypeStruct((m // tm * h_rows, D_MODEL), f32)],
        scratch_shapes=[pltpu.VMEM((tm + 8, D_MODEL), f32),
                        pltpu.VMEM((tm, D_MODEL), bf16), pltpu.VMEM((tm, D_MODEL), bf16),
                        pltpu.VMEM((tm, D_MODEL), bf16),
                        pltpu.VMEM((tm, LORA_PAD), bf16), pltpu.VMEM((tm, LORA_PAD), bf16),
                        pltpu.VMEM((tm, 256), bf16)],
        compiler_params=_cparams(2),
        name="rwkv_proj",
    )(x, norm_g.reshape(1, D_MODEL), mod, mod, s0, rwp["mu"], rwp["w1"], rwp["a1"], rwp["g1"],
      rwp["wrkv"], rwp["wrkv"], rwp["wrkv"], rwp["w2"], rwp["a2"], rwp["g2"], rwp["w0"], rwp["a0"])
    return outs


def _scan_pairs(chunk, n_seq, passes, acts, params, states):
    px, ps, pv, pinv, pupd = passes
    n_p = len(acts)
    n_st = 2 * SLAB
    row = lax.broadcasted_iota(jnp.int32, (n_st, LANES), 0)
    lane = lax.broadcasted_iota(jnp.int32, (n_st, LANES), 1)
    own = (row // SLAB) == (lane // HEAD_DIM)
    blk = (row // chunk) == (lane // chunk)
    strict = blk & (lane < row)
    incl = blk & (lane <= row)
    eye = jnp.where(row == lane, 1.0, 0.0)
    twice = lambda t: jnp.concatenate([t, t], axis=0)
    st = lambda t: jnp.where(own, twice(t), 0.0)
    each = lambda fn, *lists: [fn(*args) for args in zip(*lists)]

    r, ld, k, v, a, g = [[act[i] for act in acts] for i in range(6)]
    kkp, kap, rkp, gnw, gnb = [[par[i] for par in params] for i in range(5)]

    r64 = lax.broadcasted_iota(jnp.int32, (SLAB, SLAB), 0)
    c64 = lax.broadcasted_iota(jnp.int32, (SLAB, SLAB), 1)
    same = (r64 // chunk) == (c64 // chunk)
    tri_ones = jnp.concatenate([jnp.where(same & (c64 <= r64), 1.0, 0.0),
                                jnp.where(same, 1.0, 0.0)], axis=0).astype(bf16)
    sums = each(lambda t: _mm(tri_ones, t, pb=3), ld)
    cs = each(lambda t: t[0:SLAB], sums)
    tot = each(lambda t: t[SLAB:n_st], sums)

    kk_raw = each(lambda t, p: st(t * p), k, kkp)
    kk = each(lambda t: t / jnp.maximum(jnp.sqrt(jnp.sum(t * t, axis=1, keepdims=True)), 1e-12), kk_raw)
    k2 = each(lambda kt, at, p: st(kt * (1.0 + (at - 1.0) * p)), k, a, kap)
    b = each(lambda t, at: t * twice(at), kk, a)
    r_s = each(st, r)
    v_s = each(st, v)
    bonus = each(lambda rt, kt, p, vt: jnp.sum(rt * kt * p, axis=1, keepdims=True) * vt, r_s, k2, rkp, v_s)

    e_neg = each(lambda c_: twice(jnp.exp(-c_)), cs)
    e_tail = each(lambda t_, c_: twice(jnp.exp(t_ - c_)), tot, cs)
    a_t = each(lambda t, c_, l_: -t * twice(jnp.exp(c_ - l_)), kk, cs, ld)
    r_t = each(lambda t, c_: t * twice(jnp.exp(c_)), r_s, cs)
    k_t = each(lambda t, e: t * e, k2, e_neg)
    b_t = each(lambda t, e: t * e, b, e_neg)
    k_h = each(lambda t, e: t * e, k2, e_tail)
    b_h = each(lambda t, e: t * e, b, e_tail)

    lhs1 = each(lambda x_, y_: jnp.concatenate([x_, y_], axis=0), a_t, r_t)
    rhs1 = each(lambda x_, y_: jnp.concatenate([x_, y_], axis=0), k_t, b_t)
    x = each(lambda l_, r_: _mm(l_, r_, pa=px, pb=px, nt=True), lhs1, rhs1)
    a_k = each(lambda t: jnp.where(strict, t[0:n_st, 0:n_st], 0.0), x)
    a_b = each(lambda t: jnp.where(strict, t[0:n_st, n_st:], 0.0), x)
    r_kb = each(lambda t: jnp.concatenate([jnp.where(incl, t[n_st:, 0:n_st], 0.0),
                                           jnp.where(incl, t[n_st:, n_st:], 0.0)], axis=1), x)

    if n_seq == 1:
        p1 = each(lambda l_, s_: _mm(l_, s_[0], pa=ps, pb=ps, nt=True), lhs1, states)
    else:
        row_seq = (lax.broadcasted_iota(jnp.int32, (2 * n_st, 1), 0) % SLAB) // chunk
        p1 = [jnp.zeros((2 * n_st, LANES), f32)] * n_p
        for q in range(n_seq):
            p1 = each(lambda acc, l_, s_: acc + jnp.where(row_seq == q, _mm(l_, s_[q], pa=ps, pb=ps, nt=True), 0.0),
                      p1, lhs1, states)
    rhs_u = each(lambda p_, ak, vt: p_[0:n_st] + _mm(ak, vt, pa=pv, pb=pv), p1, a_k, v_s)

    near = (row // 2) == (lane // 2)
    t_inv = each(lambda t: eye + jnp.where(near, t, 0.0), a_b)
    s = 2
    while s < chunk:
        off = ((row // (2 * s)) == (lane // (2 * s))) & ((row // s) != (lane // s))
        w = each(lambda ab, ti: _mm(jnp.where(off, ab, 0.0), ti, pa=pinv, pb=pinv), a_b, t_inv)
        t_inv = each(lambda ti, w_: ti + _mm(ti, w_, pa=pinv, pb=pinv), t_inv, w)
        s *= 2
    u = each(lambda ti, t: _mm(ti, t, pa=pinv, pb=pinv), t_inv, rhs_u)

    vu = each(lambda x_, y_: jnp.concatenate([x_, y_], axis=0), v_s, u)
    y = each(lambda p_, rk_, vu_: p_[n_st:] + _mm(rk_, vu_, pa=pv, pb=pv), p1, r_kb, vu)

    def finish(y_, w_, b_, bonus_, g_):
        mean = jnp.sum(y_, axis=1, keepdims=True) * (1.0 / HEAD_DIM)
        dev = jnp.where(own, y_ - mean, 0.0)
        var = jnp.sum(dev * dev, axis=1, keepdims=True) * (1.0 / HEAD_DIM)
        out = jnp.where(own, dev * lax.rsqrt(var + GN_EPS) * w_ + b_, 0.0) + bonus_
        return (out[0:SLAB] + out[SLAB:n_st]) * g_
    z = each(finish, y, gnw, gnb, bonus, g)

    vu_t = each(lambda t: t.T, vu)
    kb_h = each(lambda x_, y_: jnp.concatenate([x_, y_], axis=0), k_h, b_h)
    p_tot = each(jnp.exp, tot)
    if n_seq == 1:
        new_states = each(lambda s_, pt, vt, kb: [s_[0] * pt[0:1, :] + _mm(vt, kb, pa=pupd, pb=pupd)],
                          states, p_tot, vu_t, kb_h)
    else:
        col_seq = (lax.broadcasted_iota(jnp.int32, (1, 2 * n_st), 1) % SLAB) // chunk
        new_states = [[] for _ in range(n_p)]
        for q in range(n_seq):
            upd = each(lambda vt, kb: _mm(jnp.where(col_seq == q, vt, 0.0), kb, pa=pupd, pb=pupd), vu_t, kb_h)
            for p in range(n_p):
                new_states[p].append(states[p][q] * p_tot[p][q * chunk:q * chunk + 1, :] + upd[p])
    return z, new_states


def _scan_kernel(chunk, n_seq, n_pp, passes, has_state, *refs):
    if has_state:
        (r_ref, ld_ref, k_ref, v_ref, a_ref, g_ref, kk_ref, ka_ref, rk_ref, gw_ref, gb_ref, s_in_ref,
         z_ref, s_out_ref, s_scr) = refs
    else:
        (r_ref, ld_ref, k_ref, v_ref, a_ref, g_ref, kk_ref, ka_ref, rk_ref, gw_ref, gb_ref,
         z_ref, s_out_ref, s_scr) = refs
        s_in_ref = None
    c = pl.program_id(2)
    zero = jnp.zeros((HEAD_DIM, HEAD_DIM), f32)

    @pl.when(c == 0)
    def _():
        if has_state:
            for pp in range(n_pp):
                for q in range(n_seq):
                    s0 = s_in_ref[q, 2 * pp]
                    s1 = s_in_ref[q, 2 * pp + 1]
                    s_scr[pp * n_seq + q] = jnp.concatenate(
                        [jnp.concatenate([s0, zero], axis=1), jnp.concatenate([zero, s1], axis=1)], axis=0)
        else:
            s_scr[...] = jnp.zeros_like(s_scr)

    lanes = [slice(pp * LANES, (pp + 1) * LANES) for pp in range(n_pp)]
    acts = [tuple(ref[:, sl] for ref in (r_ref, ld_ref, k_ref, v_ref, a_ref, g_ref)) for sl in lanes]
    params = [tuple(ref[:, sl] for ref in (kk_ref, ka_ref, rk_ref, gw_ref, gb_ref)) for sl in lanes]
    states = [[s_scr[pp * n_seq + q] for q in range(n_seq)] for pp in range(n_pp)]
    z, new_states = _scan_pairs(chunk, n_seq, passes, acts, params, states)
    for pp in range(n_pp):
        z_ref[:, lanes[pp]] = z[pp].astype(bf16)
        for q in range(n_seq):
            s_scr[pp * n_seq + q] = new_states[pp][q]

    @pl.when(c == pl.num_programs(2) - 1)
    def _():
        for pp in range(n_pp):
            for q in range(n_seq):
                s = s_scr[pp * n_seq + q]
                s_out_ref[q, 2 * pp] = s[0:HEAD_DIM, 0:HEAD_DIM]
                s_out_ref[q, 2 * pp + 1] = s[HEAD_DIM:, HEAD_DIM:]


def _scan(proj, rwp, n_batch, seq_len, state_in, passes, n_pp):
    m = n_batch * seq_len
    if seq_len >= SLAB:
        chunk, n_seq = SLAB, 1
        n_chunks = seq_len // SLAB
        n_groups = n_batch
    else:
        chunk, n_seq = seq_len, SLAB // seq_len
        n_chunks = 1
        n_groups = n_batch // n_seq
    w = LANES * n_pp
    act = pl.BlockSpec((SLAB, w), lambda b, p, c: (b * n_chunks + c, p))
    vec = pl.BlockSpec((1, w), lambda b, p, c: (0, p))
    st_spec = pl.BlockSpec((n_seq, 2 * n_pp, HEAD_DIM, HEAD_DIM), lambda b, p, c: (b, p, 0, 0))
    has_state = state_in is not None
    in_specs = [act] * 6 + [vec] * 5 + ([st_spec] if has_state else [])
    args = list(proj) + [rwp["kk"], rwp["ka"], rwp["rk"], rwp["gn_w"], rwp["gn_b"]]
    if has_state:
        args.append(state_in)
    z, s_out = pl.pallas_call(
        functools.partial(_scan_kernel, chunk, n_seq, n_pp, passes, has_state),
        grid=(n_groups, N_PAIRS // n_pp, n_chunks),
        in_specs=in_specs,
        out_specs=[act, st_spec],
        out_shape=[jax.ShapeDtypeStruct((m, D_MODEL), bf16),
                   jax.ShapeDtypeStruct((n_batch, N_HEADS, HEAD_DIM, HEAD_DIM), f32)],
        scratch_shapes=[pltpu.VMEM((n_pp * n_seq, 2 * SLAB, LANES), f32)],
        compiler_params=_cparams(3),
        name="rwkv_scan",
    )(*args)
    return z, s_out


def _proj_res_kernel(z_ref, w_ref, b_ref, x_ref, gt_ref, o_ref):
    out = _dot(z_ref[...], w_ref[...]) + b_ref[...]
    o_ref[...] = x_ref[...] + gt_ref[...] * out


def _proj_res(z, w, bias, x, mod, per_row, seq_len):
    m = x.shape[0]
    tm = min(ROW_TILE, m)
    tn = 512
    if per_row:
        gate_spec = pl.BlockSpec((tm, tn), lambda i, j: (i, 5 * (D_MODEL // tn) + j))
    else:
        gate_spec = pl.BlockSpec((None, 1, tn),
                                 lambda i, j: ((i * tm) // seq_len, 0, 5 * (D_MODEL // tn) + j))
    return pl.pallas_call(
        _proj_res_kernel,
        grid=(m // tm, D_MODEL // tn),
        in_specs=[pl.BlockSpec((tm, D_MODEL), lambda i, j: (i, 0)),
                  pl.BlockSpec((D_MODEL, tn), lambda i, j: (0, j)),
                  pl.BlockSpec((1, tn), lambda i, j: (0, j)),
                  pl.BlockSpec((tm, tn), lambda i, j: (i, j)),
                  gate_spec],
        out_specs=pl.BlockSpec((tm, tn), lambda i, j: (i, j)),
        out_shape=jax.ShapeDtypeStruct((m, D_MODEL), f32),
        compiler_params=_cparams(2),
        name="proj_res",
    )(z, w, bias.reshape(1, D_MODEL), x, mod)


QKV_COLS = D_MODEL + 2 * N_KV_HEADS * LANES


def _qkv_kernel(n_rope, x_ref, ng_ref, sh_ref, sc_ref, w_ref, b_ref, cos_ref, sin_ref, o_ref, h_scr):
    j = pl.program_id(1)

    @pl.when(j == 0)
    def _():
        h_scr[...] = _rms_mod(x_ref[...], ng_ref[...], sh_ref[...], sc_ref[...]).astype(bf16)

    acc = _dot(h_scr[...], w_ref[...]) + b_ref[...]

    @pl.when(j < n_rope)
    def _():
        cos = cos_ref[...]
        sin = sin_ref[...]
        first = (lax.broadcasted_iota(jnp.int32, cos.shape, 1) % HEAD_DIM) < HEAD_DIM // 2
        for c in range(acc.shape[1] // LANES):
            xc = acc[:, c * LANES:(c + 1) * LANES]
            rot = jnp.where(first, pltpu.roll(xc, LANES - HEAD_DIM // 2, 1),
                            pltpu.roll(xc, HEAD_DIM // 2, 1))
            o_ref[:, c * LANES:(c + 1) * LANES] = xc * cos + rot * sin

    @pl.when(j >= n_rope)
    def _():
        o_ref[...] = acc


def _qkv(x, mod, per_row, seq_len, norm_g, w, b, cos, sin):
    m = x.shape[0]
    tm = min(ROW_TILE, m)
    tn = 512
    n_rope = (D_MODEL + N_KV_HEADS * LANES) // tn
    n_pos_blocks = cos.shape[0] // tm
    vec = pl.BlockSpec((1, D_MODEL), lambda i, j: (0, 0))
    tab = pl.BlockSpec((tm, LANES), lambda i, j: (i % n_pos_blocks, 0))
    return pl.pallas_call(
        functools.partial(_qkv_kernel, n_rope),
        grid=(m // tm, QKV_COLS // tn),
        in_specs=[pl.BlockSpec((tm, D_MODEL), lambda i, j: (i, 0)),
                  vec,
                  _mod_spec(per_row, tm, seq_len, 3),
                  _mod_spec(per_row, tm, seq_len, 4),
                  pl.BlockSpec((D_MODEL, tn), lambda i, j: (0, j)),
                  pl.BlockSpec((1, tn), lambda i, j: (0, j)),
                  tab, tab],
        out_specs=pl.BlockSpec((tm, tn), lambda i, j: (i, j)),
        out_shape=jax.ShapeDtypeStruct((m, QKV_COLS), f32),
        scratch_shapes=[pltpu.VMEM((tm, D_MODEL), bf16)],
        compiler_params=_cparams(2),
        name="swa_qkv",
    )(x, norm_g.reshape(1, D_MODEL), mod, mod, w, b.reshape(1, QKV_COLS), cos, sin)


def _rope_tables(positions):
    half = HEAD_DIM // 2
    inv_freq = ROPE_THETA ** (-jnp.arange(half, dtype=f32) / half)
    ang = positions.astype(f32)[:, None] * inv_freq[None, :]
    cos = jnp.tile(jnp.cos(ang), (1, LANES // half))
    sin = jnp.sin(ang)
    sin = jnp.tile(jnp.concatenate([-sin, sin], axis=1), (1, LANES // HEAD_DIM))
    return cos, sin


def _head_masks(rows):
    lane = lax.broadcasted_iota(jnp.int32, (rows, LANES), 1)
    return lane < HEAD_DIM, lane >= HEAD_DIM


def _attn_prompt_kernel(q_ref, kp_ref, kc_ref, vp_ref, vc_ref, sink_ref, o_ref):
    n = pl.program_id(1)
    blk = WINDOW
    m0, m1 = _head_masks(blk)
    m0k, m1k = _head_masks(2 * blk)
    qi = lax.broadcasted_iota(jnp.int32, (2 * blk, 2 * blk), 0) % blk
    sj = lax.broadcasted_iota(jnp.int32, (2 * blk, 2 * blk), 1)
    visible = (sj > qi) & (sj <= qi + blk) & ((n > 0) | (sj >= blk))
    top = lax.broadcasted_iota(jnp.int32, (2 * blk, 1), 0) < blk
    for c in range(N_KV_HEADS):
        sl = slice(c * LANES, (c + 1) * LANES)
        kd = jnp.concatenate([kp_ref[:, sl], kc_ref[:, sl]], axis=0).astype(bf16)
        vd = jnp.concatenate([vp_ref[:, sl], vc_ref[:, sl]], axis=0)
        vcat = jnp.concatenate([jnp.where(m0k, vd, 0.0), jnp.where(m1k, vd, 0.0)], axis=0).astype(bf16)
        for jj in range(GQA_GROUP // 2):
            pair = c * (GQA_GROUP // 2) + jj
            qp = q_ref[:, pair * LANES:(pair + 1) * LANES]
            qs = jnp.concatenate([jnp.where(m0, qp, 0.0), jnp.where(m1, qp, 0.0)], axis=0).astype(bf16)
            s = _dot_nt(qs, kd) * ATTN_SCALE
            s = jnp.where(visible, s, NEG_BIG)
            sk = jnp.where(top, sink_ref[2 * pair], sink_ref[2 * pair + 1])
            mx = jnp.maximum(jnp.max(s, axis=1, keepdims=True), sk)
            p = jnp.exp(s - mx)
            den = jnp.sum(p, axis=1, keepdims=True) + jnp.exp(sk - mx)
            p = (p / den).astype(bf16)
            pcat = jnp.concatenate([p[0:blk], p[blk:]], axis=1)
            o_ref[:, pair * LANES:(pair + 1) * LANES] = _dot(pcat, vcat).astype(bf16)


def _attn_prompt(qkv, sink, n_batch, seq_len):
    nb = seq_len // WINDOW
    kw = N_KV_HEADS * LANES
    k_blk = D_MODEL // kw
    cur = lambda off: pl.BlockSpec((WINDOW, kw), lambda b, n: (b * nb + n, k_blk + off))
    prev = lambda off: pl.BlockSpec((WINDOW, kw), lambda b, n: (b * nb + jnp.maximum(n - 1, 0), k_blk + off))
    return pl.pallas_call(
        _attn_prompt_kernel,
        grid=(n_batch, nb),
        in_specs=[pl.BlockSpec((WINDOW, D_MODEL), lambda b, n: (b * nb + n, 0)),
                  prev(0), cur(0), prev(1), cur(1),
                  pl.BlockSpec(memory_space=pltpu.SMEM)],
        out_specs=pl.BlockSpec((WINDOW, D_MODEL), lambda b, n: (b * nb + n, 0)),
        out_shape=jax.ShapeDtypeStruct((n_batch * seq_len, D_MODEL), bf16),
        compiler_params=_cparams(2),
        name="swa_prompt",
    )(qkv, qkv, qkv, qkv, qkv, sink)


SEQ_PER_GROUP = 4


def _attn_cached_kernel(n_tok, groups, q_ref, kn_ref, vn_ref, kc_ref, vc_ref, sink_ref, o_ref):
    rows = SEQ_PER_GROUP * n_tok
    n_st = GQA_GROUP * rows
    m0, m1 = _head_masks(rows)
    m0c, m1c = _head_masks(WINDOW)
    srow = lax.broadcasted_iota(jnp.int32, (n_st, 1), 0)
    row_seq = (srow % rows) // n_tok
    row_tok = srow % n_tok
    key_c = lax.broadcasted_iota(jnp.int32, (n_st, WINDOW), 1)
    vis_c = key_c > row_tok
    key_n = lax.broadcasted_iota(jnp.int32, (n_st, rows), 1)
    vis_n = ((key_n // n_tok) == row_seq) & ((key_n % n_tok) <= row_tok)
    for gi in range(groups):
        rs = slice(gi * rows, (gi + 1) * rows)
        for cp in range(N_KV_HEADS // 2):
            kc_pair = [kc_ref[gi * SEQ_PER_GROUP + b, :, cp * LANES:(cp + 1) * LANES]
                       for b in range(SEQ_PER_GROUP)]
            vc_pair = [vc_ref[gi * SEQ_PER_GROUP + b, :, cp * LANES:(cp + 1) * LANES]
                       for b in range(SEQ_PER_GROUP)]
            kc_sw = [pltpu.roll(t, HEAD_DIM, 1) for t in kc_pair]
            vc_sw = [pltpu.roll(t, HEAD_DIM, 1) for t in vc_pair]
            for ce in range(2):
                c = 2 * cp + ce
                keep = m0c if ce == 0 else m1c
                sl = slice(c * LANES, (c + 1) * LANES)
                kn = kn_ref[rs, sl].astype(bf16)
                vn = vn_ref[rs, sl].astype(bf16)
                pieces = []
                for jj in range(GQA_GROUP // 2):
                    pair = c * (GQA_GROUP // 2) + jj
                    qp = q_ref[rs, pair * LANES:(pair + 1) * LANES]
                    pieces += [jnp.where(m0, qp, 0.0), jnp.where(m1, qp, 0.0)]
                qs = jnp.concatenate(pieces, axis=0).astype(bf16)
                s_c = jnp.zeros((n_st, WINDOW), f32)
                for b in range(SEQ_PER_GROUP):
                    kx = jnp.where(keep, kc_pair[b], kc_sw[b]).astype(bf16)
                    s_c = jnp.where(row_seq == b, _dot_nt(qs, kx), s_c)
                s_c = jnp.where(vis_c, s_c * ATTN_SCALE, NEG_BIG)
                s_n = jnp.where(vis_n, _dot_nt(qs, kn) * ATTN_SCALE, NEG_BIG)
                sk = sink_ref[c]
                sk = sk[:, 0:1]
                mx = jnp.maximum(jnp.maximum(jnp.max(s_c, axis=1, keepdims=True),
                                             jnp.max(s_n, axis=1, keepdims=True)), sk)
                p_c = jnp.exp(s_c - mx)
                p_n = jnp.exp(s_n - mx)
                den = (jnp.sum(p_c, axis=1, keepdims=True) + jnp.sum(p_n, axis=1, keepdims=True)
                       + jnp.exp(sk - mx))
                p_c = (p_c / den).astype(bf16)
                p_n = (p_n / den).astype(bf16)
                o = _dot(p_n, vn)
                for b in range(SEQ_PER_GROUP):
                    vx = jnp.where(keep, vc_pair[b], vc_sw[b]).astype(bf16)
                    o = o + jnp.where(row_seq == b, _dot(p_c, vx), 0.0)
                for jj in range(GQA_GROUP // 2):
                    pair = c * (GQA_GROUP // 2) + jj
                    o0 = o[(2 * jj) * rows:(2 * jj + 1) * rows]
                    o1 = o[(2 * jj + 1) * rows:(2 * jj + 2) * rows]
                    o_ref[rs, pair * LANES:(pair + 1) * LANES] = jnp.where(m0, o0, o1).astype(bf16)


def _attn_cached(qkv, k_cache, v_cache, sink, n_batch, n_tok):
    groups = 2
    seqs = SEQ_PER_GROUP * groups
    rows = seqs * n_tok
    kw = N_KV_HEADS * LANES
    k_blk = D_MODEL // kw
    n_st = GQA_GROUP * SEQ_PER_GROUP * n_tok
    head = (jnp.arange(N_KV_HEADS)[:, None] * GQA_GROUP
            + (jnp.arange(n_st)[None, :] // (SEQ_PER_GROUP * n_tok)))
    sink_tab = jnp.broadcast_to(sink[head][:, :, None], (N_KV_HEADS, n_st, LANES))
    cache_spec = pl.BlockSpec((seqs, WINDOW, N_KV_HEADS * HEAD_DIM), lambda i: (i, 0, 0))
    return pl.pallas_call(
        functools.partial(_attn_cached_kernel, n_tok, groups),
        grid=(n_batch // seqs,),
        in_specs=[pl.BlockSpec((rows, D_MODEL), lambda i: (i, 0)),
                  pl.BlockSpec((rows, kw), lambda i: (i, k_blk)),
                  pl.BlockSpec((rows, kw), lambda i: (i, k_blk + 1)),
                  cache_spec, cache_spec,
                  pl.BlockSpec((N_KV_HEADS, n_st, LANES), lambda i: (0, 0, 0))],
        out_specs=pl.BlockSpec((rows, D_MODEL), lambda i: (i, 0)),
        out_shape=jax.ShapeDtypeStruct((n_batch * n_tok, D_MODEL), bf16),
        compiler_params=_cparams(1),
        name="swa_cached",
    )(qkv, qkv, qkv, k_cache, v_cache, sink_tab)


def _undup(t):
    return t.reshape(t.shape[0], N_KV_HEADS, 2, HEAD_DIM)[:, :, 0, :]


def _trunk(x, mod_all, per_row, n_batch, seq_len, pos0, wkv_in, shift_in, k_in, v_in, wts):
    scan_passes, scan_pairs = wts["scan_cfg"]
    mod = mod_all[0]
    x = _ffn(x, mod, per_row, seq_len, 0, wts["norm_g"][0, 0], wts["w_in"], wts["w_out"], 0, 0)
    if per_row:
        s0 = jnp.repeat(shift_in[0], seq_len, axis=0)
    else:
        s0 = jnp.zeros((n_batch, 1, D_MODEL), f32)
    *proj, h_tail = _rwkv_proj(x, mod, per_row, seq_len, wts["norm_g"][0, 1], s0, wts["rw"])
    z, wkv_out = _scan(proj, wts["rw"], n_batch, seq_len, None if wkv_in is None else wkv_in[0],
                       scan_passes, scan_pairs)
    if per_row:
        shift_out = h_tail.reshape(n_batch, seq_len, D_MODEL)[:, -1]
    else:
        shift_out = h_tail.reshape(n_batch, -1, 8, D_MODEL)[:, -1, -1]
    x = _proj_res(z, wts["rw"]["wo"], jnp.zeros((D_MODEL,), f32), x, mod, per_row, seq_len)
    x = _ffn(x, mod, per_row, seq_len, 6, wts["norm_g"][0, 2], wts["w_in"], wts["w_out"], 0, 1)
    mod = mod_all[1]
    x = _ffn(x, mod, per_row, seq_len, 0, wts["norm_g"][1, 0], wts["w_in"], wts["w_out"], 1, 0)
    if per_row:
        positions = jnp.tile(pos0 + jnp.arange(seq_len), n_batch)
    else:
        positions = pos0 + jnp.arange(seq_len)
    cos, sin = _rope_tables(positions)
    qkv = _qkv(x, mod, per_row, seq_len, wts["norm_g"][1, 1], wts["sw_wqkv"], wts["sw_bqkv"], cos, sin)
    kw = N_KV_HEADS * LANES
    if k_in is None:
        att = _attn_prompt(qkv, wts["sw_sink"], n_batch, seq_len)
        tail = qkv.reshape(n_batch, seq_len, QKV_COLS)[:, -WINDOW:, D_MODEL:]
        k_new = _undup(tail[..., :kw].reshape(n_batch * WINDOW, kw)).reshape(n_batch, WINDOW, N_KV_HEADS, HEAD_DIM)
        v_new = _undup(tail[..., kw:].reshape(n_batch * WINDOW, kw)).reshape(n_batch, WINDOW, N_KV_HEADS, HEAD_DIM)
    else:
        win = k_in.shape[2]
        att = _attn_cached(qkv, k_in[0].reshape(n_batch, win, N_KV_HEADS * HEAD_DIM),
                           v_in[0].reshape(n_batch, win, N_KV_HEADS * HEAD_DIM),
                           wts["sw_sink"], n_batch, seq_len)
        k_tok = _undup(qkv[:, D_MODEL:D_MODEL + kw]).reshape(n_batch, seq_len, N_KV_HEADS, HEAD_DIM)
        v_tok = _undup(qkv[:, D_MODEL + kw:]).reshape(n_batch, seq_len, N_KV_HEADS, HEAD_DIM)
        k_new = jnp.concatenate([k_in[0], k_tok], axis=1)[:, -win:]
        v_new = jnp.concatenate([v_in[0], v_tok], axis=1)[:, -win:]
    x = _proj_res(att, wts["sw_wo"], wts["sw_bo"], x, mod, per_row, seq_len)
    y = _ffn(x, mod, per_row, seq_len, 6, wts["norm_g"][1, 2], wts["w_in"], wts["w_out"], 1, 1,
             final_g=wts["final_g"])
    return y, wkv_out[None], shift_out[None], k_new[None], v_new[None]


def _dup_heads(w):
    lead = w.shape[:-1]
    w4 = w.reshape(lead + (N_KV_HEADS, 1, HEAD_DIM))
    return jnp.broadcast_to(w4, lead + (N_KV_HEADS, 2, HEAD_DIM)).reshape(lead + (N_KV_HEADS * LANES,))


def _pad_cols(w):
    return jnp.pad(w, ((0, 0), (0, LORA_PAD - w.shape[1])))


def _pad_rows(w):
    return jnp.pad(w, ((0, LORA_PAD - w.shape[0]), (0, 0)))


def kernel(x_prompt, x_sample, state_rwkv_wkv, state_rwkv_shift, cache_swa_k, cache_swa_v, c_prompt, c_sample, norm_g, w_ada, b_ada, w_ffn_in, w_ffn_out, rw_mu, rw_wrkv, rw_w0, rw_w1, rw_w2, rw_a0, rw_a1, rw_a2, rw_g1, rw_g2, rw_kk, rw_ka, rw_rk, rw_gn_w, rw_gn_b, rw_wo, sw_wqkv, sw_bqkv, sw_sink, sw_wo, sw_bo, final_g):
    n_p, seq_p, _ = x_prompt.shape
    n_s, seq_s, _ = x_sample.shape
    nq = N_HEADS * HEAD_DIM
    nkv = N_KV_HEADS * HEAD_DIM
    row = lambda t: t.reshape(1, D_MODEL)
    rw = dict(
        mu=rw_mu[0], wrkv=rw_wrkv[0].astype(bf16),
        w1=_pad_cols(rw_w1[0]).astype(bf16), a1=_pad_cols(rw_a1[0]).astype(bf16), g1=rw_g1[0].astype(bf16),
        w2=_pad_rows(rw_w2[0]).astype(bf16), a2=_pad_rows(rw_a2[0]).astype(bf16), g2=rw_g2[0].astype(bf16),
        w0=row(rw_w0[0]), a0=row(rw_a0[0]), kk=row(rw_kk[0]), ka=row(rw_ka[0]), rk=row(rw_rk[0]),
        gn_w=row(rw_gn_w[0]), gn_b=row(rw_gn_b[0]), wo=rw_wo[0].astype(bf16))
    wq = sw_wqkv[0]
    bq = sw_bqkv[0]
    wts = dict(
        norm_g=norm_g, final_g=final_g,
        w_in=w_ffn_in.astype(bf16), w_out=w_ffn_out.astype(bf16), rw=rw,
        sw_wqkv=jnp.concatenate([wq[:, :nq], _dup_heads(wq[:, nq:nq + nkv]), _dup_heads(wq[:, nq + nkv:])],
                                axis=1).astype(bf16),
        sw_bqkv=jnp.concatenate([bq[:nq], _dup_heads(bq[nq:nq + nkv]), _dup_heads(bq[nq + nkv:])]),
        sw_sink=sw_sink[0], sw_wo=sw_wo[0].astype(bf16), sw_bo=sw_bo[0])

    n_c = n_p + n_s
    pad = (-n_c) % 8
    c_all = jnp.concatenate([c_prompt, c_sample, jnp.zeros((pad, D_MODEL), f32)], axis=0)
    mod = _ada(c_all, w_ada, b_ada)
    mod_p = [mod[l, :n_p].reshape(n_p, 1, N_MOD * D_MODEL) for l in range(mod.shape[0])]
    mod_s = [jnp.repeat(mod[l, n_p:n_c], seq_s, axis=0) for l in range(mod.shape[0])]

    wts["scan_cfg"] = ((1, 1, 1, 1, 1), 8)
    y_p, p_wkv, p_shift, p_k, p_v = _trunk(
        x_prompt.reshape(n_p * seq_p, D_MODEL), mod_p, False, n_p, seq_p, 0, None, None, None, None, wts)
    wts["scan_cfg"] = ((1, 2, 1, 1, 2), 1)
    y_s, s_wkv, s_shift, s_k, s_v = _trunk(
        x_sample.reshape(n_s * seq_s, D_MODEL), mod_s, True, n_s, seq_s, PAST_LEN,
        state_rwkv_wkv, state_rwkv_shift, cache_swa_k, cache_swa_v, wts)
    return (y_p.reshape(n_p, seq_p, D_MODEL), y_s.reshape(n_s, seq_s, D_MODEL),
            p_wkv, p_shift, p_k, p_v, s_wkv, s_shift, s_k, s_v)
```

```python
import functools
import math

import jax
import jax.numpy as jnp
from jax import lax
from jax.experimental import pallas as pl
from jax.experimental.pallas import tpu as pltpu

f32 = jnp.float32
bf16 = jnp.bfloat16

D_MODEL = 2048
HEAD_DIM = 64
N_HEADS = D_MODEL // HEAD_DIM
N_KV_HEADS = 4
GQA_GROUP = N_HEADS // N_KV_HEADS
WINDOW = 128
ATTN_SCALE = HEAD_DIM ** -0.5
ROPE_THETA = 10000.0
D_FF = 5632
N_MOD = 9
RMS_EPS = 1e-6
GN_EPS = 64e-5
PAST_LEN = 8192
LANES = 128
N_PAIRS = D_MODEL // LANES
LORA_PAD = 128
SLAB = 64
VMEM_LIMIT = 56 * 1024 * 1024
NEG_BIG = -1e30
ROW_TILE = 512


def _cparams(n_axes):
    return pltpu.CompilerParams(dimension_semantics=("arbitrary",) * n_axes,
                                vmem_limit_bytes=VMEM_LIMIT)


def _dot(a, b):
    return jnp.dot(a, b, preferred_element_type=f32)


def _dot_nt(a, b):
    return lax.dot_general(a, b, (((1,), (1,)), ((), ())), preferred_element_type=f32)


def _split(x, n):
    if x.dtype == bf16:
        return [x]
    parts = []
    rem = x
    for i in range(n):
        p = rem.astype(bf16)
        parts.append(p)
        if i + 1 < n:
            rem = rem - p.astype(f32)
    return parts


def _mm(a, b, pa=1, pb=1, nt=False):
    a_parts = _split(a, pa)
    b_parts = _split(b, pb)
    order = max(len(a_parts), len(b_parts))
    acc = None
    for i, x in enumerate(a_parts):
        for j, y in enumerate(b_parts):
            if i + j >= order:
                continue
            t = _dot_nt(x, y) if nt else _dot(x, y)
            acc = t if acc is None else acc + t
    return acc


def _rms_mod(x, g, shift, scale):
    y = x * lax.rsqrt(jnp.mean(x * x, axis=-1, keepdims=True) + RMS_EPS) * g
    return y * (1.0 + scale) + shift


def _mod_spec(per_row, tm, seq_len, idx):
    if per_row:
        return pl.BlockSpec((tm, D_MODEL), lambda i, *_: (i, idx))
    return pl.BlockSpec((None, 1, D_MODEL), lambda i, *_: ((i * tm) // seq_len, 0, idx))


def _ada_kernel(c_ref, w_ref, b_ref, o_ref):
    c = c_ref[...]
    s = (c * jax.nn.sigmoid(c)).astype(bf16)
    o_ref[...] = _dot(s, w_ref[...].astype(bf16)) + b_ref[...]


def _ada(c_all, w_ada, b_ada):
    n_layers, _, n_out = w_ada.shape
    rows = c_all.shape[0]
    tn = 1024
    return pl.pallas_call(
        _ada_kernel,
        grid=(n_layers, n_out // tn),
        in_specs=[pl.BlockSpec((rows, D_MODEL), lambda l, j: (0, 0)),
                  pl.BlockSpec((None, D_MODEL, tn), lambda l, j: (l, 0, j)),
                  pl.BlockSpec((None, 1, tn), lambda l, j: (l, 0, j))],
        out_specs=pl.BlockSpec((None, rows, tn), lambda l, j: (l, 0, j)),
        out_shape=jax.ShapeDtypeStruct((n_layers, rows, n_out), f32),
        compiler_params=_cparams(2),
        name="ada_mod",
    )(c_all, w_ada, b_ada.reshape(n_layers, 1, n_out))


def _ffn_kernel(final_norm, emit_bf16, x_ref, ng_ref, sh_ref, sc_ref, gt_ref, wg_ref, wu_ref, wo_ref, fg_ref,
                *rest):
    if emit_bf16:
        o_ref, wg_out, wu_out, wo_out, h_scr, acc_scr = rest
    else:
        o_ref, h_scr, acc_scr = rest
    f = pl.program_id(1)

    @pl.when(f == 0)
    def _():
        h_scr[...] = _rms_mod(x_ref[...], ng_ref[...], sh_ref[...], sc_ref[...]).astype(bf16)
        acc_scr[...] = jnp.zeros_like(acc_scr)

    wg, wu, wo = wg_ref[...], wu_ref[...], wo_ref[...]
    if emit_bf16:
        wg, wu, wo = wg.astype(bf16), wu.astype(bf16), wo.astype(bf16)
        wg_out[...] = wg
        wu_out[...] = wu
        wo_out[...] = wo
    h = h_scr[...]
    gate = _dot(h, wg)
    up = _dot(h, wu)
    act = (gate * jax.nn.sigmoid(gate) * up).astype(bf16)
    acc_scr[...] += _dot(act, wo)

    @pl.when(f == pl.num_programs(1) - 1)
    def _():
        y = x_ref[...] + 0.5 * gt_ref[...] * acc_scr[...]
        if final_norm:
            y = y * lax.rsqrt(jnp.mean(y * y, axis=-1, keepdims=True) + RMS_EPS) * fg_ref[...]
        o_ref[...] = y


def _ffn(x, mod, per_row, seq_len, mod_base, norm_g, weights, final_g=None):
    m = x.shape[0]
    tm = min(ROW_TILE, m)
    emit = len(weights) == 4
    tf = 256 if emit else 512
    nf = D_FF // tf
    vec = pl.BlockSpec((1, D_MODEL), lambda i, j: (0, 0))
    fg = jnp.ones((1, D_MODEL), f32) if final_g is None else final_g.reshape(1, D_MODEL)
    w_in_spec = pl.BlockSpec((D_MODEL, tf), lambda i, j: (0, j))
    w_out_spec = pl.BlockSpec((tf, D_MODEL), lambda i, j: (j, 0))
    out_specs = [pl.BlockSpec((tm, D_MODEL), lambda i, j: (i, 0))]
    out_shape = [jax.ShapeDtypeStruct((m, D_MODEL), f32)]
    if emit:
        assert m == tm, "the weight copies are written once, by a single row tile"
        w_ffn_in, w_ffn_out, layer, slot = weights
        w_args = (w_ffn_in, w_ffn_in, w_ffn_out)
        w_specs = [pl.BlockSpec((None, None, D_MODEL, tf), lambda i, j: (layer, slot, 0, j)),
                   pl.BlockSpec((None, None, D_MODEL, tf), lambda i, j: (layer, slot, 0, j + nf)),
                   pl.BlockSpec((None, None, tf, D_MODEL), lambda i, j: (layer, slot, j, 0))]
        out_specs += [w_in_spec, w_in_spec, w_out_spec]
        out_shape += [jax.ShapeDtypeStruct((D_MODEL, D_FF), bf16), jax.ShapeDtypeStruct((D_MODEL, D_FF), bf16),
                      jax.ShapeDtypeStruct((D_FF, D_MODEL), bf16)]
    else:
        w_args = weights
        w_specs = [w_in_spec, w_in_spec, w_out_spec]
    outs = pl.pallas_call(
        functools.partial(_ffn_kernel, final_g is not None, emit),
        grid=(m // tm, nf),
        in_specs=[pl.BlockSpec((tm, D_MODEL), lambda i, j: (i, 0)),
                  vec,
                  _mod_spec(per_row, tm, seq_len, mod_base),
                  _mod_spec(per_row, tm, seq_len, mod_base + 1),
                  _mod_spec(per_row, tm, seq_len, mod_base + 2)] + w_specs + [vec],
        out_specs=out_specs,
        out_shape=out_shape,
        scratch_shapes=[pltpu.VMEM((tm, D_MODEL), bf16), pltpu.VMEM((tm, D_MODEL), f32)],
        compiler_params=_cparams(2),
        name="ffn",
    )(x, norm_g.reshape(1, D_MODEL), mod, mod, mod, *w_args, fg)
    return (outs[0], tuple(outs[1:])) if emit else (outs[0], weights)


def _rwkv_proj_kernel(seq_len, tm, h_rows,
                      x_ref, ng_ref, sh_ref, sc_ref, s0_ref, mu_ref, w1_ref, a1_ref, g1_ref,
                      wr_ref, wk_ref, wv_ref, w2_ref, a2_ref, g2_ref, w0_ref, a0_ref,
                      r_ref, ld_ref, k_ref, v_ref, a_ref, g_ref, h_ref,
                      hs, xr, xk, xv, tw, ta, sg):
    i = pl.program_id(0)
    j = pl.program_id(1)

    @pl.when(j == 0)
    def _():
        @pl.when(i == 0)
        def _():
            hs[0:8, :] = jnp.zeros((8, D_MODEL), f32)

        @pl.when(i > 0)
        def _():
            hs[0:8, :] = hs[tm:tm + 8, :]

        h = _rms_mod(x_ref[...], ng_ref[...], sh_ref[...], sc_ref[...])
        hs[8:tm + 8, :] = h
        h_ref[...] = h[tm - h_rows:tm, :]
        row = i * tm + lax.broadcasted_iota(jnp.int32, (tm, 1), 0)
        prev = jnp.where(row % seq_len == 0, s0_ref[...], hs[7:tm + 7, :])
        xx = prev - h
        mu = mu_ref[...]
        xr[...] = (h + xx * mu[0:1, :]).astype(bf16)
        xk[...] = (h + xx * mu[2:3, :]).astype(bf16)
        xv[...] = (h + xx * mu[3:4, :]).astype(bf16)
        xw = (h + xx * mu[1:2, :]).astype(bf16)
        tw[...] = jnp.tanh(_dot(xw, w1_ref[...])).astype(bf16)
        xa = (h + xx * mu[4:5, :]).astype(bf16)
        ta[...] = _dot(xa, a1_ref[...]).astype(bf16)
        xg = (h + xx * mu[5:6, :]).astype(bf16)
        sg[...] = jax.nn.sigmoid(_dot(xg, g1_ref[...])).astype(bf16)

    r_ref[...] = _dot(xr[...], wr_ref[...])
    k_ref[...] = _dot(xk[...], wk_ref[...])
    v_ref[...] = _dot(xv[...], wv_ref[...])
    z = w0_ref[...] + _dot(tw[...], w2_ref[...])
    ld_ref[...] = -jax.nn.sigmoid(z) * math.exp(-0.5)
    a_ref[...] = jax.nn.sigmoid(a0_ref[...] + _dot(ta[...], a2_ref[...]))
    g_ref[...] = _dot(sg[...], g2_ref[...])


def _rwkv_proj(x, mod, per_row, seq_len, norm_g, s0, rwp):
    m = x.shape[0]
    tm = min(ROW_TILE, m)
    tn = 256
    h_rows = tm if per_row else 8
    full = lambda shape: pl.BlockSpec(shape, lambda i, j: (0,) * len(shape))
    col = lambda rows: pl.BlockSpec((rows, tn), lambda i, j: (0, j))
    wspec = lambda which: pl.BlockSpec((None, D_MODEL, tn), lambda i, j: (which, 0, j))
    if per_row:
        s0_spec = pl.BlockSpec((tm, D_MODEL), lambda i, j: (i, 0))
    else:
        s0_spec = pl.BlockSpec((None, 1, D_MODEL), lambda i, j: ((i * tm) // seq_len, 0, 0))
    out_spec = pl.BlockSpec((tm, tn), lambda i, j: (i, j))
    out_sds = jax.ShapeDtypeStruct((m, D_MODEL), f32)
    outs = pl.pallas_call(
        functools.partial(_rwkv_proj_kernel, seq_len, tm, h_rows),
        grid=(m // tm, D_MODEL // tn),
        in_specs=[pl.BlockSpec((tm, D_MODEL), lambda i, j: (i, 0)),
                  full((1, D_MODEL)),
                  _mod_spec(per_row, tm, seq_len, 3),
                  _mod_spec(per_row, tm, seq_len, 4),
                  s0_spec,
                  full((6, D_MODEL)),
                  full((D_MODEL, LORA_PAD)), full((D_MODEL, LORA_PAD)), full((D_MODEL, 256)),
                  wspec(0), wspec(1), wspec(2),
                  col(LORA_PAD), col(LORA_PAD), col(256), col(1), col(1)],
        out_specs=[out_spec] * 6 + [pl.BlockSpec((h_rows, D_MODEL), lambda i, j: (i, 0))],
        out_shape=[out_sds] * 6 + [jax.ShapeDtypeStruct((m // tm * h_rows, D_MODEL), f32)],
        scratch_shapes=[pltpu.VMEM((tm + 8, D_MODEL), f32),
                        pltpu.VMEM((tm, D_MODEL), bf16), pltpu.VMEM((tm, D_MODEL), bf16),
                        pltpu.VMEM((tm, D_MODEL), bf16),
                        pltpu.VMEM((tm, LORA_PAD), bf16), pltpu.VMEM((tm, LORA_PAD), bf16),
                        pltpu.VMEM((tm, 256), bf16)],
        compiler_params=_cparams(2),
        name="rwkv_proj",
    )(x, norm_g.reshape(1, D_MODEL), mod, mod, s0, rwp["mu"], rwp["w1"], rwp["a1"], rwp["g1"],
      rwp["wrkv"], rwp["wrkv"], rwp["wrkv"], rwp["w2"], rwp["a2"], rwp["g2"], rwp["w0"], rwp["a0"])
    return outs


def _scan_pairs(chunk, n_seq, passes, acts, params, states):
    px, ps, pv, pinv, pupd = passes
    n_p = len(acts)
    n_st = 2 * SLAB
    row = lax.broadcasted_iota(jnp.int32, (n_st, LANES), 0)
    lane = lax.broadcasted_iota(jnp.int32, (n_st, LANES), 1)
    own = (row // SLAB) == (lane // HEAD_DIM)
    blk = (row // chunk) == (lane // chunk)
    strict = blk & (lane < row)
    incl = blk & (lane <= row)
    eye = jnp.where(row == lane, 1.0, 0.0)
    twice = lambda t: jnp.concatenate([t, t], axis=0)
    st = lambda t: jnp.where(own, twice(t), 0.0)
    each = lambda fn, *lists: [fn(*args) for args in zip(*lists)]

    r, ld, k, v, a, g = [[act[i] for act in acts] for i in range(6)]
    kkp, kap, rkp, gnw, gnb = [[par[i] for par in params] for i in range(5)]

    r64 = lax.broadcasted_iota(jnp.int32, (SLAB, SLAB), 0)
    c64 = lax.broadcasted_iota(jnp.int32, (SLAB, SLAB), 1)
    same = (r64 // chunk) == (c64 // chunk)
    tri_ones = jnp.concatenate([jnp.where(same & (c64 <= r64), 1.0, 0.0),
                                jnp.where(same, 1.0, 0.0)], axis=0).astype(bf16)
    sums = each(lambda t: _mm(tri_ones, t, pb=3), ld)
    cs = each(lambda t: t[0:SLAB], sums)
    tot = each(lambda t: t[SLAB:n_st], sums)

    kk_raw = each(lambda t, p: st(t * p), k, kkp)
    kk = each(lambda t: t / jnp.maximum(jnp.sqrt(jnp.sum(t * t, axis=1, keepdims=True)), 1e-12), kk_raw)
    k2 = each(lambda kt, at, p: st(kt * (1.0 + (at - 1.0) * p)), k, a, kap)
    b = each(lambda t, at: t * twice(at), kk, a)
    r_s = each(st, r)
    v_s = each(st, v)
    bonus = each(lambda rt, kt, p, vt: jnp.sum(rt * kt * p, axis=1, keepdims=True) * vt, r_s, k2, rkp, v_s)

    e_neg = each(lambda c_: twice(jnp.exp(-c_)), cs)
    e_tail = each(lambda t_, c_: twice(jnp.exp(t_ - c_)), tot, cs)
    a_t = each(lambda t, c_, l_: -t * twice(jnp.exp(c_ - l_)), kk, cs, ld)
    r_t = each(lambda t, c_: t * twice(jnp.exp(c_)), r_s, cs)
    k_t = each(lambda t, e: t * e, k2, e_neg)
    b_t = each(lambda t, e: t * e, b, e_neg)
    k_h = each(lambda t, e: t * e, k2, e_tail)
    b_h = each(lambda t, e: t * e, b, e_tail)

    lhs1 = each(lambda x_, y_: jnp.concatenate([x_, y_], axis=0), a_t, r_t)
    rhs1 = each(lambda x_, y_: jnp.concatenate([x_, y_], axis=0), k_t, b_t)
    x = each(lambda l_, r_: _mm(l_, r_, pa=px, pb=px, nt=True), lhs1, rhs1)
    a_k = each(lambda t: jnp.where(strict, t[0:n_st, 0:n_st], 0.0), x)
    a_b = each(lambda t: jnp.where(strict, t[0:n_st, n_st:], 0.0), x)
    r_kb = each(lambda t: jnp.concatenate([jnp.where(incl, t[n_st:, 0:n_st], 0.0),
                                           jnp.where(incl, t[n_st:, n_st:], 0.0)], axis=1), x)

    if n_seq == 1:
        p1 = each(lambda l_, s_: _mm(l_, s_[0], pa=ps, pb=ps, nt=True), lhs1, states)
    else:
        row_seq = (lax.broadcasted_iota(jnp.int32, (2 * n_st, 1), 0) % SLAB) // chunk
        p1 = [jnp.zeros((2 * n_st, LANES), f32)] * n_p
        for q in range(n_seq):
            p1 = each(lambda acc, l_, s_: acc + jnp.where(row_seq == q, _mm(l_, s_[q], pa=ps, pb=ps, nt=True), 0.0),
                      p1, lhs1, states)
    rhs_u = each(lambda p_, ak, vt: p_[0:n_st] + _mm(ak, vt, pa=pv, pb=pv), p1, a_k, v_s)

    near = (row // 2) == (lane // 2)
    t_inv = each(lambda t: eye + jnp.where(near, t, 0.0), a_b)
    s = 2
    while s < chunk:
        off = ((row // (2 * s)) == (lane // (2 * s))) & ((row // s) != (lane // s))
        w = each(lambda ab, ti: _mm(jnp.where(off, ab, 0.0), ti, pa=pinv, pb=pinv), a_b, t_inv)
        t_inv = each(lambda ti, w_: ti + _mm(ti, w_, pa=pinv, pb=pinv), t_inv, w)
        s *= 2
    u = each(lambda ti, t: _mm(ti, t, pa=pinv, pb=pinv), t_inv, rhs_u)

    vu = each(lambda x_, y_: jnp.concatenate([x_, y_], axis=0), v_s, u)
    y = each(lambda p_, rk_, vu_: p_[n_st:] + _mm(rk_, vu_, pa=pv, pb=pv), p1, r_kb, vu)

    def finish(y_, w_, b_, bonus_, g_):
        mean = jnp.sum(y_, axis=1, keepdims=True) * (1.0 / HEAD_DIM)
        dev = jnp.where(own, y_ - mean, 0.0)
        var = jnp.sum(dev * dev, axis=1, keepdims=True) * (1.0 / HEAD_DIM)
        out = jnp.where(own, dev * lax.rsqrt(var + GN_EPS) * w_ + b_, 0.0) + bonus_
        return (out[0:SLAB] + out[SLAB:n_st]) * g_
    z = each(finish, y, gnw, gnb, bonus, g)

    vu_t = each(lambda t: t.T, vu)
    kb_h = each(lambda x_, y_: jnp.concatenate([x_, y_], axis=0), k_h, b_h)
    p_tot = each(jnp.exp, tot)
    if n_seq == 1:
        new_states = each(lambda s_, pt, vt, kb: [s_[0] * pt[0:1, :] + _mm(vt, kb, pa=pupd, pb=pupd)],
                          states, p_tot, vu_t, kb_h)
    else:
        col_seq = (lax.broadcasted_iota(jnp.int32, (1, 2 * n_st), 1) % SLAB) // chunk
        new_states = [[] for _ in range(n_p)]
        for q in range(n_seq):
            upd = each(lambda vt, kb: _mm(jnp.where(col_seq == q, vt, 0.0), kb, pa=pupd, pb=pupd), vu_t, kb_h)
            for p in range(n_p):
                new_states[p].append(states[p][q] * p_tot[p][q * chunk:q * chunk + 1, :] + upd[p])
    return z, new_states


def _scan_kernel(chunk, n_seq, n_pp, passes, has_state, *refs):
    if has_state:
        (r_ref, ld_ref, k_ref, v_ref, a_ref, g_ref, kk_ref, ka_ref, rk_ref, gw_ref, gb_ref, s_in_ref,
         z_ref, s_out_ref, s_scr) = refs
    else:
        (r_ref, ld_ref, k_ref, v_ref, a_ref, g_ref, kk_ref, ka_ref, rk_ref, gw_ref, gb_ref,
         z_ref, s_out_ref, s_scr) = refs
        s_in_ref = None
    c = pl.program_id(2)
    zero = jnp.zeros((HEAD_DIM, HEAD_DIM), f32)

    @pl.when(c == 0)
    def _():
        if has_state:
            for pp in range(n_pp):
                for q in range(n_seq):
                    s0 = s_in_ref[q, 2 * pp]
                    s1 = s_in_ref[q, 2 * pp + 1]
                    s_scr[pp * n_seq + q] = jnp.concatenate(
                        [jnp.concatenate([s0, zero], axis=1), jnp.concatenate([zero, s1], axis=1)], axis=0)
        else:
            s_scr[...] = jnp.zeros_like(s_scr)

    lanes = [slice(pp * LANES, (pp + 1) * LANES) for pp in range(n_pp)]
    acts = [tuple(ref[:, sl] for ref in (r_ref, ld_ref, k_ref, v_ref, a_ref, g_ref)) for sl in lanes]
    params = [tuple(ref[:, sl] for ref in (kk_ref, ka_ref, rk_ref, gw_ref, gb_ref)) for sl in lanes]
    states = [[s_scr[pp * n_seq + q] for q in range(n_seq)] for pp in range(n_pp)]
    z, new_states = _scan_pairs(chunk, n_seq, passes, acts, params, states)
    for pp in range(n_pp):
        z_ref[:, lanes[pp]] = z[pp].astype(bf16)
        for q in range(n_seq):
            s_scr[pp * n_seq + q] = new_states[pp][q]

    @pl.when(c == pl.num_programs(2) - 1)
    def _():
        for pp in range(n_pp):
            for q in range(n_seq):
                s = s_scr[pp * n_seq + q]
                s_out_ref[q, 2 * pp] = s[0:HEAD_DIM, 0:HEAD_DIM]
                s_out_ref[q, 2 * pp + 1] = s[HEAD_DIM:, HEAD_DIM:]


def _scan(proj, rwp, n_batch, seq_len, state_in, passes, n_pp):
    m = n_batch * seq_len
    if seq_len >= SLAB:
        chunk, n_seq = SLAB, 1
        n_chunks = seq_len // SLAB
        n_groups = n_batch
    else:
        chunk, n_seq = seq_len, SLAB // seq_len
        n_chunks = 1
        n_groups = n_batch // n_seq
    w = LANES * n_pp
    act = pl.BlockSpec((SLAB, w), lambda b, p, c: (b * n_chunks + c, p))
    vec = pl.BlockSpec((1, w), lambda b, p, c: (0, p))
    st_spec = pl.BlockSpec((n_seq, 2 * n_pp, HEAD_DIM, HEAD_DIM), lambda b, p, c: (b, p, 0, 0))
    has_state = state_in is not None
    in_specs = [act] * 6 + [vec] * 5 + ([st_spec] if has_state else [])
    args = list(proj) + [rwp["kk"], rwp["ka"], rwp["rk"], rwp["gn_w"], rwp["gn_b"]]
    if has_state:
        args.append(state_in)
    z, s_out = pl.pallas_call(
        functools.partial(_scan_kernel, chunk, n_seq, n_pp, passes, has_state),
        grid=(n_groups, N_PAIRS // n_pp, n_chunks),
        in_specs=in_specs,
        out_specs=[act, st_spec],
        out_shape=[jax.ShapeDtypeStruct((m, D_MODEL), bf16),
                   jax.ShapeDtypeStruct((n_batch, N_HEADS, HEAD_DIM, HEAD_DIM), f32)],
        scratch_shapes=[pltpu.VMEM((n_pp * n_seq, 2 * SLAB, LANES), f32)],
        compiler_params=_cparams(3),
        name="rwkv_scan",
    )(*args)
    return z, s_out


def _proj_res_kernel(z_ref, w_ref, b_ref, x_ref, gt_ref, o_ref):
    out = _dot(z_ref[...], w_ref[...]) + b_ref[...]
    o_ref[...] = x_ref[...] + gt_ref[...] * out


def _proj_res(z, w, bias, x, mod, per_row, seq_len):
    m = x.shape[0]
    tm = min(ROW_TILE, m)
    rows = pl.BlockSpec((tm, D_MODEL), lambda i: (i, 0))
    return pl.pallas_call(
        _proj_res_kernel,
        grid=(m // tm,),
        in_specs=[rows,
                  pl.BlockSpec((D_MODEL, D_MODEL), lambda i: (0, 0), pipeline_mode=pl.Buffered(1)),
                  pl.BlockSpec((1, D_MODEL), lambda i: (0, 0)),
                  rows,
                  _mod_spec(per_row, tm, seq_len, 5)],
        out_specs=rows,
        out_shape=jax.ShapeDtypeStruct((m, D_MODEL), f32),
        compiler_params=_cparams(1),
        name="proj_res",
    )(z, w, bias.reshape(1, D_MODEL), x, mod)


QKV_COLS = D_MODEL + 2 * N_KV_HEADS * LANES


def _qkv_kernel(n_rope, x_ref, ng_ref, sh_ref, sc_ref, w_ref, b_ref, cos_ref, sin_ref, o_ref, h_scr):
    j = pl.program_id(1)

    @pl.when(j == 0)
    def _():
        h_scr[...] = _rms_mod(x_ref[...], ng_ref[...], sh_ref[...], sc_ref[...]).astype(bf16)

    acc = _dot(h_scr[...], w_ref[...]) + b_ref[...]

    @pl.when(j < n_rope)
    def _():
        cos = cos_ref[...]
        sin = sin_ref[...]
        first = (lax.broadcasted_iota(jnp.int32, cos.shape, 1) % HEAD_DIM) < HEAD_DIM // 2
        for c in range(acc.shape[1] // LANES):
            xc = acc[:, c * LANES:(c + 1) * LANES]
            rot = jnp.where(first, pltpu.roll(xc, LANES - HEAD_DIM // 2, 1),
                            pltpu.roll(xc, HEAD_DIM // 2, 1))
            o_ref[:, c * LANES:(c + 1) * LANES] = xc * cos + rot * sin

    @pl.when(j >= n_rope)
    def _():
        o_ref[...] = acc


def _qkv(x, mod, per_row, seq_len, norm_g, w, b, cos, sin):
    m = x.shape[0]
    tm = min(ROW_TILE, m)
    tn = 512
    n_rope = (D_MODEL + N_KV_HEADS * LANES) // tn
    n_pos_blocks = cos.shape[0] // tm
    vec = pl.BlockSpec((1, D_MODEL), lambda i, j: (0, 0))
    tab = pl.BlockSpec((tm, LANES), lambda i, j: (i % n_pos_blocks, 0))
    return pl.pallas_call(
        functools.partial(_qkv_kernel, n_rope),
        grid=(m // tm, QKV_COLS // tn),
        in_specs=[pl.BlockSpec((tm, D_MODEL), lambda i, j: (i, 0)),
                  vec,
                  _mod_spec(per_row, tm, seq_len, 3),
                  _mod_spec(per_row, tm, seq_len, 4),
                  pl.BlockSpec((D_MODEL, tn), lambda i, j: (0, j)),
                  pl.BlockSpec((1, tn), lambda i, j: (0, j)),
                  tab, tab],
        out_specs=pl.BlockSpec((tm, tn), lambda i, j: (i, j)),
        out_shape=jax.ShapeDtypeStruct((m, QKV_COLS), f32),
        scratch_shapes=[pltpu.VMEM((tm, D_MODEL), bf16)],
        compiler_params=_cparams(2),
        name="swa_qkv",
    )(x, norm_g.reshape(1, D_MODEL), mod, mod, w, b.reshape(1, QKV_COLS), cos, sin)


def _rope_tables(positions):
    half = HEAD_DIM // 2
    inv_freq = ROPE_THETA ** (-jnp.arange(half, dtype=f32) / half)
    ang = positions.astype(f32)[:, None] * inv_freq[None, :]
    cos = jnp.tile(jnp.cos(ang), (1, LANES // half))
    sin = jnp.sin(ang)
    sin = jnp.tile(jnp.concatenate([-sin, sin], axis=1), (1, LANES // HEAD_DIM))
    return cos, sin


def _head_masks(rows):
    lane = lax.broadcasted_iota(jnp.int32, (rows, LANES), 1)
    return lane < HEAD_DIM, lane >= HEAD_DIM


def _attn_prompt_kernel(q_ref, kp_ref, kc_ref, vp_ref, vc_ref, sink_ref, o_ref):
    n = pl.program_id(1)
    blk = WINDOW
    m0, m1 = _head_masks(blk)
    m0k, m1k = _head_masks(2 * blk)
    qi = lax.broadcasted_iota(jnp.int32, (2 * blk, 2 * blk), 0) % blk
    sj = lax.broadcasted_iota(jnp.int32, (2 * blk, 2 * blk), 1)
    visible = (sj > qi) & (sj <= qi + blk) & ((n > 0) | (sj >= blk))
    top = lax.broadcasted_iota(jnp.int32, (2 * blk, 1), 0) < blk
    for c in range(N_KV_HEADS):
        sl = slice(c * LANES, (c + 1) * LANES)
        kd = jnp.concatenate([kp_ref[:, sl], kc_ref[:, sl]], axis=0).astype(bf16)
        vd = jnp.concatenate([vp_ref[:, sl], vc_ref[:, sl]], axis=0)
        vcat = jnp.concatenate([jnp.where(m0k, vd, 0.0), jnp.where(m1k, vd, 0.0)], axis=0).astype(bf16)
        for jj in range(GQA_GROUP // 2):
            pair = c * (GQA_GROUP // 2) + jj
            qp = q_ref[:, pair * LANES:(pair + 1) * LANES]
            qs = jnp.concatenate([jnp.where(m0, qp, 0.0), jnp.where(m1, qp, 0.0)], axis=0).astype(bf16)
            s = _dot_nt(qs, kd) * ATTN_SCALE
            s = jnp.where(visible, s, NEG_BIG)
            sk = jnp.where(top, sink_ref[2 * pair], sink_ref[2 * pair + 1])
            mx = jnp.maximum(jnp.max(s, axis=1, keepdims=True), sk)
            p = jnp.exp(s - mx)
            den = jnp.sum(p, axis=1, keepdims=True) + jnp.exp(sk - mx)
            p = (p / den).astype(bf16)
            pcat = jnp.concatenate([p[0:blk], p[blk:]], axis=1)
            o_ref[:, pair * LANES:(pair + 1) * LANES] = _dot(pcat, vcat).astype(bf16)


def _attn_prompt(qkv, sink, n_batch, seq_len):
    nb = seq_len // WINDOW
    kw = N_KV_HEADS * LANES
    k_blk = D_MODEL // kw
    cur = lambda off: pl.BlockSpec((WINDOW, kw), lambda b, n: (b * nb + n, k_blk + off))
    prev = lambda off: pl.BlockSpec((WINDOW, kw), lambda b, n: (b * nb + jnp.maximum(n - 1, 0), k_blk + off))
    return pl.pallas_call(
        _attn_prompt_kernel,
        grid=(n_batch, nb),
        in_specs=[pl.BlockSpec((WINDOW, D_MODEL), lambda b, n: (b * nb + n, 0)),
                  prev(0), cur(0), prev(1), cur(1),
                  pl.BlockSpec(memory_space=pltpu.SMEM)],
        out_specs=pl.BlockSpec((WINDOW, D_MODEL), lambda b, n: (b * nb + n, 0)),
        out_shape=jax.ShapeDtypeStruct((n_batch * seq_len, D_MODEL), bf16),
        compiler_params=_cparams(2),
        name="swa_prompt",
    )(qkv, qkv, qkv, qkv, qkv, sink)


SEQ_PER_GROUP = 4


def _attn_cached_kernel(n_tok, groups, q_ref, kn_ref, vn_ref, kc_ref, vc_ref, sink_ref, o_ref):
    rows = SEQ_PER_GROUP * n_tok
    n_st = GQA_GROUP * rows
    m0, m1 = _head_masks(rows)
    m0c, m1c = _head_masks(WINDOW)
    srow = lax.broadcasted_iota(jnp.int32, (n_st, 1), 0)
    row_seq = (srow % rows) // n_tok
    row_tok = srow % n_tok
    key_c = lax.broadcasted_iota(jnp.int32, (n_st, WINDOW), 1)
    vis_c = key_c > row_tok
    key_n = lax.broadcasted_iota(jnp.int32, (n_st, rows), 1)
    vis_n = ((key_n // n_tok) == row_seq) & ((key_n % n_tok) <= row_tok)
    for gi in range(groups):
        rs = slice(gi * rows, (gi + 1) * rows)
        for cp in range(N_KV_HEADS // 2):
            kc_pair = [kc_ref[gi * SEQ_PER_GROUP + b, :, cp * LANES:(cp + 1) * LANES]
                       for b in range(SEQ_PER_GROUP)]
            vc_pair = [vc_ref[gi * SEQ_PER_GROUP + b, :, cp * LANES:(cp + 1) * LANES]
                       for b in range(SEQ_PER_GROUP)]
            kc_sw = [pltpu.roll(t, HEAD_DIM, 1) for t in kc_pair]
            vc_sw = [pltpu.roll(t, HEAD_DIM, 1) for t in vc_pair]
            for ce in range(2):
                c = 2 * cp + ce
                keep = m0c if ce == 0 else m1c
                sl = slice(c * LANES, (c + 1) * LANES)
                kn = kn_ref[rs, sl].astype(bf16)
                vn = vn_ref[rs, sl].astype(bf16)
                pieces = []
                for jj in range(GQA_GROUP // 2):
                    pair = c * (GQA_GROUP // 2) + jj
                    qp = q_ref[rs, pair * LANES:(pair + 1) * LANES]
                    pieces += [jnp.where(m0, qp, 0.0), jnp.where(m1, qp, 0.0)]
                qs = jnp.concatenate(pieces, axis=0).astype(bf16)
                s_c = jnp.zeros((n_st, WINDOW), f32)
                for b in range(SEQ_PER_GROUP):
                    kx = jnp.where(keep, kc_pair[b], kc_sw[b]).astype(bf16)
                    s_c = jnp.where(row_seq == b, _dot_nt(qs, kx), s_c)
                s_c = jnp.where(vis_c, s_c * ATTN_SCALE, NEG_BIG)
                s_n = jnp.where(vis_n, _dot_nt(qs, kn) * ATTN_SCALE, NEG_BIG)
                sk = sink_ref[c]
                sk = sk[:, 0:1]
                mx = jnp.maximum(jnp.maximum(jnp.max(s_c, axis=1, keepdims=True),
                                             jnp.max(s_n, axis=1, keepdims=True)), sk)
                p_c = jnp.exp(s_c - mx)
                p_n = jnp.exp(s_n - mx)
                den = (jnp.sum(p_c, axis=1, keepdims=True) + jnp.sum(p_n, axis=1, keepdims=True)
                       + jnp.exp(sk - mx))
                p_c = (p_c / den).astype(bf16)
                p_n = (p_n / den).astype(bf16)
                o = _dot(p_n, vn)
                for b in range(SEQ_PER_GROUP):
                    vx = jnp.where(keep, vc_pair[b], vc_sw[b]).astype(bf16)
                    o = o + jnp.where(row_seq == b, _dot(p_c, vx), 0.0)
                for jj in range(GQA_GROUP // 2):
                    pair = c * (GQA_GROUP // 2) + jj
                    o0 = o[(2 * jj) * rows:(2 * jj + 1) * rows]
                    o1 = o[(2 * jj + 1) * rows:(2 * jj + 2) * rows]
                    o_ref[rs, pair * LANES:(pair + 1) * LANES] = jnp.where(m0, o0, o1).astype(bf16)


def _attn_cached(qkv, k_cache, v_cache, sink, n_batch, n_tok):
    groups = 2
    seqs = SEQ_PER_GROUP * groups
    rows = seqs * n_tok
    kw = N_KV_HEADS * LANES
    k_blk = D_MODEL // kw
    n_st = GQA_GROUP * SEQ_PER_GROUP * n_tok
    head = (jnp.arange(N_KV_HEADS)[:, None] * GQA_GROUP
            + (jnp.arange(n_st)[None, :] // (SEQ_PER_GROUP * n_tok)))
    sink_tab = jnp.broadcast_to(sink[head][:, :, None], (N_KV_HEADS, n_st, LANES))
    cache_spec = pl.BlockSpec((seqs, WINDOW, N_KV_HEADS * HEAD_DIM), lambda i: (i, 0, 0))
    return pl.pallas_call(
        functools.partial(_attn_cached_kernel, n_tok, groups),
        grid=(n_batch // seqs,),
        in_specs=[pl.BlockSpec((rows, D_MODEL), lambda i: (i, 0)),
                  pl.BlockSpec((rows, kw), lambda i: (i, k_blk)),
                  pl.BlockSpec((rows, kw), lambda i: (i, k_blk + 1)),
                  cache_spec, cache_spec,
                  pl.BlockSpec((N_KV_HEADS, n_st, LANES), lambda i: (0, 0, 0))],
        out_specs=pl.BlockSpec((rows, D_MODEL), lambda i: (i, 0)),
        out_shape=jax.ShapeDtypeStruct((n_batch * n_tok, D_MODEL), bf16),
        compiler_params=_cparams(1),
        name="swa_cached",
    )(qkv, qkv, qkv, k_cache, v_cache, sink_tab)


def _undup(t):
    return t.reshape(t.shape[0], N_KV_HEADS, 2, HEAD_DIM)[:, :, 0, :]


def _trunk(x, mod_all, per_row, n_batch, seq_len, pos0, wkv_in, shift_in, k_in, v_in, wts):
    scan_passes, scan_pairs = wts["scan_cfg"]
    mod = mod_all[0]
    ffn_w = dict(wts["ffn"])
    x, ffn_w[0, 0] = _ffn(x, mod, per_row, seq_len, 0, wts["norm_g"][0, 0], ffn_w[0, 0])
    if per_row:
        s0 = jnp.repeat(shift_in[0], seq_len, axis=0)
    else:
        s0 = jnp.zeros((n_batch, 1, D_MODEL), f32)
    *proj, h_tail = _rwkv_proj(x, mod, per_row, seq_len, wts["norm_g"][0, 1], s0, wts["rw"])
    z, wkv_out = _scan(proj, wts["rw"], n_batch, seq_len, None if wkv_in is None else wkv_in[0],
                       scan_passes, scan_pairs)
    if per_row:
        shift_out = h_tail.reshape(n_batch, seq_len, D_MODEL)[:, -1]
    else:
        shift_out = h_tail.reshape(n_batch, -1, 8, D_MODEL)[:, -1, -1]
    x = _proj_res(z, wts["rw"]["wo"], jnp.zeros((D_MODEL,), f32), x, mod, per_row, seq_len)
    x, ffn_w[0, 1] = _ffn(x, mod, per_row, seq_len, 6, wts["norm_g"][0, 2], ffn_w[0, 1])
    mod = mod_all[1]
    x, ffn_w[1, 0] = _ffn(x, mod, per_row, seq_len, 0, wts["norm_g"][1, 0], ffn_w[1, 0])
    if per_row:
        positions = jnp.tile(pos0 + jnp.arange(seq_len), n_batch)
    else:
        positions = pos0 + jnp.arange(seq_len)
    cos, sin = _rope_tables(positions)
    qkv = _qkv(x, mod, per_row, seq_len, wts["norm_g"][1, 1], wts["sw_wqkv"], wts["sw_bqkv"], cos, sin)
    kw = N_KV_HEADS * LANES
    if k_in is None:
        att = _attn_prompt(qkv, wts["sw_sink"], n_batch, seq_len)
        tail = qkv.reshape(n_batch, seq_len, QKV_COLS)[:, -WINDOW:, D_MODEL:]
        k_new = _undup(tail[..., :kw].reshape(n_batch * WINDOW, kw)).reshape(n_batch, WINDOW, N_KV_HEADS, HEAD_DIM)
        v_new = _undup(tail[..., kw:].reshape(n_batch * WINDOW, kw)).reshape(n_batch, WINDOW, N_KV_HEADS, HEAD_DIM)
    else:
        win = k_in.shape[2]
        att = _attn_cached(qkv, k_in[0].reshape(n_batch, win, N_KV_HEADS * HEAD_DIM),
                           v_in[0].reshape(n_batch, win, N_KV_HEADS * HEAD_DIM),
                           wts["sw_sink"], n_batch, seq_len)
        k_tok = _undup(qkv[:, D_MODEL:D_MODEL + kw]).reshape(n_batch, seq_len, N_KV_HEADS, HEAD_DIM)
        v_tok = _undup(qkv[:, D_MODEL + kw:]).reshape(n_batch, seq_len, N_KV_HEADS, HEAD_DIM)
        k_new = jnp.concatenate([k_in[0], k_tok], axis=1)[:, -win:]
        v_new = jnp.concatenate([v_in[0], v_tok], axis=1)[:, -win:]
    x = _proj_res(att, wts["sw_wo"], wts["sw_bo"], x, mod, per_row, seq_len)
    y, ffn_w[1, 1] = _ffn(x, mod, per_row, seq_len, 6, wts["norm_g"][1, 2], ffn_w[1, 1], final_g=wts["final_g"])
    return (y, wkv_out[None], shift_out[None], k_new[None], v_new[None]), ffn_w


def _dup_heads(w):
    lead = w.shape[:-1]
    w4 = w.reshape(lead + (N_KV_HEADS, 1, HEAD_DIM))
    return jnp.broadcast_to(w4, lead + (N_KV_HEADS, 2, HEAD_DIM)).reshape(lead + (N_KV_HEADS * LANES,))


def _pad_cols(w):
    return jnp.pad(w, ((0, 0), (0, LORA_PAD - w.shape[1])))


def _pad_rows(w):
    return jnp.pad(w, ((0, LORA_PAD - w.shape[0]), (0, 0)))


def kernel(x_prompt, x_sample, state_rwkv_wkv, state_rwkv_shift, cache_swa_k, cache_swa_v, c_prompt, c_sample, norm_g, w_ada, b_ada, w_ffn_in, w_ffn_out, rw_mu, rw_wrkv, rw_w0, rw_w1, rw_w2, rw_a0, rw_a1, rw_a2, rw_g1, rw_g2, rw_kk, rw_ka, rw_rk, rw_gn_w, rw_gn_b, rw_wo, sw_wqkv, sw_bqkv, sw_sink, sw_wo, sw_bo, final_g):
    n_p, seq_p, _ = x_prompt.shape
    n_s, seq_s, _ = x_sample.shape
    nq = N_HEADS * HEAD_DIM
    nkv = N_KV_HEADS * HEAD_DIM
    row = lambda t: t.reshape(1, D_MODEL)
    rw = dict(
        mu=rw_mu[0], wrkv=rw_wrkv[0].astype(bf16),
        w1=_pad_cols(rw_w1[0]).astype(bf16), a1=_pad_cols(rw_a1[0]).astype(bf16), g1=rw_g1[0].astype(bf16),
        w2=_pad_rows(rw_w2[0]).astype(bf16), a2=_pad_rows(rw_a2[0]).astype(bf16), g2=rw_g2[0].astype(bf16),
        w0=row(rw_w0[0]), a0=row(rw_a0[0]), kk=row(rw_kk[0]), ka=row(rw_ka[0]), rk=row(rw_rk[0]),
        gn_w=row(rw_gn_w[0]), gn_b=row(rw_gn_b[0]), wo=rw_wo[0].astype(bf16))
    wq = sw_wqkv[0]
    bq = sw_bqkv[0]
    wts = dict(
        norm_g=norm_g, final_g=final_g,
        ffn={(l, s): (w_ffn_in, w_ffn_out, l, s) for l in range(2) for s in range(2)}, rw=rw,
        sw_wqkv=jnp.concatenate([wq[:, :nq], _dup_heads(wq[:, nq:nq + nkv]), _dup_heads(wq[:, nq + nkv:])],
                                axis=1).astype(bf16),
        sw_bqkv=jnp.concatenate([bq[:nq], _dup_heads(bq[nq:nq + nkv]), _dup_heads(bq[nq + nkv:])]),
        sw_sink=sw_sink[0], sw_wo=sw_wo[0].astype(bf16), sw_bo=sw_bo[0])

    n_c = n_p + n_s
    pad = (-n_c) % 8
    c_all = jnp.concatenate([c_prompt, c_sample, jnp.zeros((pad, D_MODEL), f32)], axis=0)
    mod = _ada(c_all, w_ada, b_ada)
    mod_p = [mod[l, :n_p].reshape(n_p, 1, N_MOD * D_MODEL) for l in range(mod.shape[0])]
    mod_s = [jnp.repeat(mod[l, n_p:n_c], seq_s, axis=0) for l in range(mod.shape[0])]

    wts["scan_cfg"] = ((1, 2, 1, 1, 2), 1)
    (y_s, s_wkv, s_shift, s_k, s_v), wts["ffn"] = _trunk(
        x_sample.reshape(n_s * seq_s, D_MODEL), mod_s, True, n_s, seq_s, PAST_LEN,
        state_rwkv_wkv, state_rwkv_shift, cache_swa_k, cache_swa_v, wts)
    wts["scan_cfg"] = ((1, 1, 1, 1, 1), 16)
    (y_p, p_wkv, p_shift, p_k, p_v), _ = _trunk(
        x_prompt.reshape(n_p * seq_p, D_MODEL), mod_p, False, n_p, seq_p, 0, None, None, None, None, wts)
    return (y_p.reshape(n_p, seq_p, D_MODEL), y_s.reshape(n_s, seq_s, D_MODEL),
            p_wkv, p_shift, p_k, p_v, s_wkv, s_shift, s_k, s_v)
```

```python
import functools
import math

import jax
import jax.numpy as jnp
from jax import lax
from jax.experimental import pallas as pl
from jax.experimental.pallas import tpu as pltpu

f32 = jnp.float32
bf16 = jnp.bfloat16

D_MODEL = 2048
HEAD_DIM = 64
N_HEADS = D_MODEL // HEAD_DIM
N_KV_HEADS = 4
GQA_GROUP = N_HEADS // N_KV_HEADS
WINDOW = 128
ATTN_SCALE = HEAD_DIM ** -0.5
ROPE_THETA = 10000.0
D_FF = 5632
N_MOD = 9
RMS_EPS = 1e-6
GN_EPS = 64e-5
PAST_LEN = 8192
LANES = 128
N_PAIRS = D_MODEL // LANES
LORA_PAD = 128
SLAB = 64
VMEM_LIMIT = 56 * 1024 * 1024
NEG_BIG = -1e30
ROW_TILE = 512


def _cparams(n_axes):
    return pltpu.CompilerParams(dimension_semantics=("arbitrary",) * n_axes,
                                vmem_limit_bytes=VMEM_LIMIT)


def _dot(a, b):
    return jnp.dot(a, b, preferred_element_type=f32)


def _dot_nt(a, b):
    return lax.dot_general(a, b, (((1,), (1,)), ((), ())), preferred_element_type=f32)


def _split(x, n):
    if x.dtype == bf16:
        return [x]
    parts = []
    rem = x
    for i in range(n):
        p = rem.astype(bf16)
        parts.append(p)
        if i + 1 < n:
            rem = rem - p.astype(f32)
    return parts


def _mm(a, b, pa=1, pb=1, nt=False):
    a_parts = _split(a, pa)
    b_parts = _split(b, pb)
    order = max(len(a_parts), len(b_parts))
    acc = None
    for i, x in enumerate(a_parts):
        for j, y in enumerate(b_parts):
            if i + j >= order:
                continue
            t = _dot_nt(x, y) if nt else _dot(x, y)
            acc = t if acc is None else acc + t
    return acc


def _rms_mod(x, g, shift, scale):
    y = x * lax.rsqrt(jnp.mean(x * x, axis=-1, keepdims=True) + RMS_EPS) * g
    return y * (1.0 + scale) + shift


def _mod_spec(tok_major, tm, seq_len, idx):
    if tok_major:
        return pl.BlockSpec((tm // seq_len, D_MODEL), lambda i, *_: (0, idx))
    return pl.BlockSpec((None, 1, D_MODEL), lambda i, *_: ((i * tm) // seq_len, 0, idx))


def _rows(v, tm):
    n = v.shape[0]
    return v if n in (1, tm) else jnp.concatenate([v] * (tm // n), axis=0)


def _ada_kernel(c_ref, w_ref, b_ref, o_ref):
    c = c_ref[...]
    s = (c * jax.nn.sigmoid(c)).astype(bf16)
    o_ref[...] = _dot(s, w_ref[...].astype(bf16)) + b_ref[...]


def _ada(c_all, w_ada, b_ada):
    n_layers, _, n_out = w_ada.shape
    rows = c_all.shape[0]
    tn = 1024
    return pl.pallas_call(
        _ada_kernel,
        grid=(n_layers, n_out // tn),
        in_specs=[pl.BlockSpec((rows, D_MODEL), lambda l, j: (0, 0)),
                  pl.BlockSpec((None, D_MODEL, tn), lambda l, j: (l, 0, j)),
                  pl.BlockSpec((None, 1, tn), lambda l, j: (l, 0, j))],
        out_specs=pl.BlockSpec((None, rows, tn), lambda l, j: (l, 0, j)),
        out_shape=jax.ShapeDtypeStruct((n_layers, rows, n_out), f32),
        compiler_params=_cparams(2),
        name="ada_mod",
    )(c_all, w_ada, b_ada.reshape(n_layers, 1, n_out))


def _ffn_kernel(final_norm, emit_bf16, x_ref, ng_ref, sh_ref, sc_ref, gt_ref, wg_ref, wu_ref, wo_ref, fg_ref,
                *rest):
    if emit_bf16:
        o_ref, wg_out, wu_out, wo_out, h_scr, acc_scr = rest
    else:
        o_ref, h_scr, acc_scr = rest
    f = pl.program_id(1)

    @pl.when(f == 0)
    def _():
        tm = x_ref.shape[0]
        h_scr[...] = _rms_mod(x_ref[...], ng_ref[...], _rows(sh_ref[...], tm), _rows(sc_ref[...], tm)).astype(bf16)
        acc_scr[...] = jnp.zeros_like(acc_scr)

    wg, wu, wo = wg_ref[...], wu_ref[...], wo_ref[...]
    if emit_bf16:
        wg, wu, wo = wg.astype(bf16), wu.astype(bf16), wo.astype(bf16)
        wg_out[...] = wg
        wu_out[...] = wu
        wo_out[...] = wo
    h = h_scr[...]
    gate = _dot(h, wg)
    up = _dot(h, wu)
    act = (gate * jax.nn.sigmoid(gate) * up).astype(bf16)
    acc_scr[...] += _dot(act, wo)

    @pl.when(f == pl.num_programs(1) - 1)
    def _():
        y = x_ref[...] + 0.5 * _rows(gt_ref[...], x_ref.shape[0]) * acc_scr[...]
        if final_norm:
            y = y * lax.rsqrt(jnp.mean(y * y, axis=-1, keepdims=True) + RMS_EPS) * fg_ref[...]
        o_ref[...] = y


def _ffn(x, mod, tok_major, seq_len, mod_base, norm_g, weights, final_g=None):
    m = x.shape[0]
    tm = min(ROW_TILE, m)
    emit = len(weights) == 4
    tf = 256 if emit else 512
    nf = D_FF // tf
    vec = pl.BlockSpec((1, D_MODEL), lambda i, j: (0, 0))
    fg = jnp.ones((1, D_MODEL), f32) if final_g is None else final_g.reshape(1, D_MODEL)
    w_in_spec = pl.BlockSpec((D_MODEL, tf), lambda i, j: (0, j))
    w_out_spec = pl.BlockSpec((tf, D_MODEL), lambda i, j: (j, 0))
    out_specs = [pl.BlockSpec((tm, D_MODEL), lambda i, j: (i, 0))]
    out_shape = [jax.ShapeDtypeStruct((m, D_MODEL), f32)]
    if emit:
        assert m == tm, "the weight copies are written once, by a single row tile"
        w_ffn_in, w_ffn_out, layer, slot = weights
        w_args = (w_ffn_in, w_ffn_in, w_ffn_out)
        w_specs = [pl.BlockSpec((None, None, D_MODEL, tf), lambda i, j: (layer, slot, 0, j)),
                   pl.BlockSpec((None, None, D_MODEL, tf), lambda i, j: (layer, slot, 0, j + nf)),
                   pl.BlockSpec((None, None, tf, D_MODEL), lambda i, j: (layer, slot, j, 0))]
        out_specs += [w_in_spec, w_in_spec, w_out_spec]
        out_shape += [jax.ShapeDtypeStruct((D_MODEL, D_FF), bf16), jax.ShapeDtypeStruct((D_MODEL, D_FF), bf16),
                      jax.ShapeDtypeStruct((D_FF, D_MODEL), bf16)]
    else:
        w_args = weights
        w_specs = [w_in_spec, w_in_spec, w_out_spec]
    outs = pl.pallas_call(
        functools.partial(_ffn_kernel, final_g is not None, emit),
        grid=(m // tm, nf),
        in_specs=[pl.BlockSpec((tm, D_MODEL), lambda i, j: (i, 0)),
                  vec,
                  _mod_spec(tok_major, tm, seq_len, mod_base),
                  _mod_spec(tok_major, tm, seq_len, mod_base + 1),
                  _mod_spec(tok_major, tm, seq_len, mod_base + 2)] + w_specs + [vec],
        out_specs=out_specs,
        out_shape=out_shape,
        scratch_shapes=[pltpu.VMEM((tm, D_MODEL), bf16), pltpu.VMEM((tm, D_MODEL), f32)],
        compiler_params=_cparams(2),
        name="ffn",
    )(x, norm_g.reshape(1, D_MODEL), mod, mod, mod, *w_args, fg)
    return (outs[0], tuple(outs[1:])) if emit else (outs[0], weights)


def _rwkv_proj_kernel(seq_len, tm, h_rows, tok_major,
                      x_ref, ng_ref, sh_ref, sc_ref, s0_ref, mu_ref, w1_ref, a1_ref, g1_ref,
                      wr_ref, wk_ref, wv_ref, w2_ref, a2_ref, g2_ref, w0_ref, a0_ref,
                      r_ref, ld_ref, k_ref, v_ref, a_ref, g_ref, h_ref,
                      hs, xr, xk, xv, tw, ta, sg):
    i = pl.program_id(0)
    j = pl.program_id(1)

    @pl.when(j == 0)
    def _():
        h = _rms_mod(x_ref[...], ng_ref[...], _rows(sh_ref[...], tm), _rows(sc_ref[...], tm))
        h_ref[...] = h[tm - h_rows:tm, :]
        if tok_major:
            n_seq = s0_ref.shape[0]
            prev = jnp.concatenate([s0_ref[...], h[0:tm - n_seq, :]], axis=0)
        else:
            @pl.when(i == 0)
            def _():
                hs[0:8, :] = jnp.zeros((8, D_MODEL), f32)

            @pl.when(i > 0)
            def _():
                hs[0:8, :] = hs[tm:tm + 8, :]

            hs[8:tm + 8, :] = h
            row = i * tm + lax.broadcasted_iota(jnp.int32, (tm, 1), 0)
            prev = jnp.where(row % seq_len == 0, s0_ref[...], hs[7:tm + 7, :])
        xx = prev - h
        mu = mu_ref[...]
        xr[...] = (h + xx * mu[0:1, :]).astype(bf16)
        xk[...] = (h + xx * mu[2:3, :]).astype(bf16)
        xv[...] = (h + xx * mu[3:4, :]).astype(bf16)
        xw = (h + xx * mu[1:2, :]).astype(bf16)
        tw[...] = jnp.tanh(_dot(xw, w1_ref[...])).astype(bf16)
        xa = (h + xx * mu[4:5, :]).astype(bf16)
        ta[...] = _dot(xa, a1_ref[...]).astype(bf16)
        xg = (h + xx * mu[5:6, :]).astype(bf16)
        sg[...] = jax.nn.sigmoid(_dot(xg, g1_ref[...])).astype(bf16)

    r_ref[...] = _dot(xr[...], wr_ref[...])
    k_ref[...] = _dot(xk[...], wk_ref[...])
    v_ref[...] = _dot(xv[...], wv_ref[...])
    z = w0_ref[...] + _dot(tw[...], w2_ref[...])
    ld_ref[...] = -jax.nn.sigmoid(z) * math.exp(-0.5)
    a_ref[...] = jax.nn.sigmoid(a0_ref[...] + _dot(ta[...], a2_ref[...]))
    g_ref[...] = _dot(sg[...], g2_ref[...])


def _rwkv_proj(x, mod, tok_major, seq_len, norm_g, s0, rwp):
    m = x.shape[0]
    tm = min(ROW_TILE, m)
    tn = 256
    h_rows = tm // seq_len if tok_major else 8
    full = lambda shape: pl.BlockSpec(shape, lambda i, j: (0,) * len(shape))
    col = lambda rows: pl.BlockSpec((rows, tn), lambda i, j: (0, j))
    wspec = lambda which: pl.BlockSpec((None, D_MODEL, tn), lambda i, j: (which, 0, j))
    if tok_major:
        assert m == tm
        s0_spec = pl.BlockSpec((tm // seq_len, D_MODEL), lambda i, j: (0, 0))
    else:
        s0_spec = pl.BlockSpec((None, 1, D_MODEL), lambda i, j: ((i * tm) // seq_len, 0, 0))
    out_spec = pl.BlockSpec((tm, tn), lambda i, j: (i, j))
    out_sds = jax.ShapeDtypeStruct((m, D_MODEL), f32)
    outs = pl.pallas_call(
        functools.partial(_rwkv_proj_kernel, seq_len, tm, h_rows, tok_major),
        grid=(m // tm, D_MODEL // tn),
        in_specs=[pl.BlockSpec((tm, D_MODEL), lambda i, j: (i, 0)),
                  full((1, D_MODEL)),
                  _mod_spec(tok_major, tm, seq_len, 3),
                  _mod_spec(tok_major, tm, seq_len, 4),
                  s0_spec,
                  full((6, D_MODEL)),
                  full((D_MODEL, LORA_PAD)), full((D_MODEL, LORA_PAD)), full((D_MODEL, 256)),
                  wspec(0), wspec(1), wspec(2),
                  col(LORA_PAD), col(LORA_PAD), col(256), col(1), col(1)],
        out_specs=[out_spec] * 6 + [pl.BlockSpec((h_rows, D_MODEL), lambda i, j: (i, 0))],
        out_shape=[out_sds] * 6 + [jax.ShapeDtypeStruct((m // tm * h_rows, D_MODEL), f32)],
        scratch_shapes=[pltpu.VMEM((tm + 8, D_MODEL), f32),
                        pltpu.VMEM((tm, D_MODEL), bf16), pltpu.VMEM((tm, D_MODEL), bf16),
                        pltpu.VMEM((tm, D_MODEL), bf16),
                        pltpu.VMEM((tm, LORA_PAD), bf16), pltpu.VMEM((tm, LORA_PAD), bf16),
                        pltpu.VMEM((tm, 256), bf16)],
        compiler_params=_cparams(2),
        name="rwkv_proj",
    )(x, norm_g.reshape(1, D_MODEL), mod, mod, s0, rwp["mu"], rwp["w1"], rwp["a1"], rwp["g1"],
      rwp["wrkv"], rwp["wrkv"], rwp["wrkv"], rwp["w2"], rwp["a2"], rwp["g2"], rwp["w0"], rwp["a0"])
    return outs


def _scan_pairs(chunk, n_seq, passes, acts, params, states):
    px, ps, pv, pinv, pupd = passes
    n_p = len(acts)
    n_st = 2 * SLAB
    row = lax.broadcasted_iota(jnp.int32, (n_st, LANES), 0)
    lane = lax.broadcasted_iota(jnp.int32, (n_st, LANES), 1)
    own = (row // SLAB) == (lane // HEAD_DIM)
    blk = (row // chunk) == (lane // chunk)
    strict = blk & (lane < row)
    incl = blk & (lane <= row)
    eye = jnp.where(row == lane, 1.0, 0.0)
    twice = lambda t: jnp.concatenate([t, t], axis=0)
    st = lambda t: jnp.where(own, twice(t), 0.0)
    each = lambda fn, *lists: [fn(*args) for args in zip(*lists)]

    r, ld, k, v, a, g = [[act[i] for act in acts] for i in range(6)]
    kkp, kap, rkp, gnw, gnb = [[par[i] for par in params] for i in range(5)]

    r64 = lax.broadcasted_iota(jnp.int32, (SLAB, SLAB), 0)
    c64 = lax.broadcasted_iota(jnp.int32, (SLAB, SLAB), 1)
    same = (r64 // chunk) == (c64 // chunk)
    tri_ones = jnp.concatenate([jnp.where(same & (c64 <= r64), 1.0, 0.0),
                                jnp.where(same, 1.0, 0.0)], axis=0).astype(bf16)
    sums = each(lambda t: _mm(tri_ones, t, pb=3), ld)
    cs = each(lambda t: t[0:SLAB], sums)
    tot = each(lambda t: t[SLAB:n_st], sums)

    kk_raw = each(lambda t, p: st(t * p), k, kkp)
    kk = each(lambda t: t / jnp.maximum(jnp.sqrt(jnp.sum(t * t, axis=1, keepdims=True)), 1e-12), kk_raw)
    k2 = each(lambda kt, at, p: st(kt * (1.0 + (at - 1.0) * p)), k, a, kap)
    b = each(lambda t, at: t * twice(at), kk, a)
    r_s = each(st, r)
    v_s = each(st, v)
    bonus = each(lambda rt, kt, p, vt: jnp.sum(rt * kt * p, axis=1, keepdims=True) * vt, r_s, k2, rkp, v_s)

    e_neg = each(lambda c_: twice(jnp.exp(-c_)), cs)
    e_tail = each(lambda t_, c_: twice(jnp.exp(t_ - c_)), tot, cs)
    a_t = each(lambda t, c_, l_: -t * twice(jnp.exp(c_ - l_)), kk, cs, ld)
    r_t = each(lambda t, c_: t * twice(jnp.exp(c_)), r_s, cs)
    k_t = each(lambda t, e: t * e, k2, e_neg)
    b_t = each(lambda t, e: t * e, b, e_neg)
    k_h = each(lambda t, e: t * e, k2, e_tail)
    b_h = each(lambda t, e: t * e, b, e_tail)

    lhs1 = each(lambda x_, y_: jnp.concatenate([x_, y_], axis=0), a_t, r_t)
    rhs1 = each(lambda x_, y_: jnp.concatenate([x_, y_], axis=0), k_t, b_t)
    x = each(lambda l_, r_: _mm(l_, r_, pa=px, pb=px, nt=True), lhs1, rhs1)
    a_k = each(lambda t: jnp.where(strict, t[0:n_st, 0:n_st], 0.0), x)
    a_b = each(lambda t: jnp.where(strict, t[0:n_st, n_st:], 0.0), x)
    r_kb = each(lambda t: jnp.concatenate([jnp.where(incl, t[n_st:, 0:n_st], 0.0),
                                           jnp.where(incl, t[n_st:, n_st:], 0.0)], axis=1), x)

    if n_seq == 1:
        p1 = each(lambda l_, s_: _mm(l_, s_[0], pa=ps, pb=ps, nt=True), lhs1, states)
    else:
        row_seq = (lax.broadcasted_iota(jnp.int32, (2 * n_st, 1), 0) % SLAB) // chunk
        p1 = [jnp.zeros((2 * n_st, LANES), f32)] * n_p
        for q in range(n_seq):
            p1 = each(lambda acc, l_, s_: acc + jnp.where(row_seq == q, _mm(l_, s_[q], pa=ps, pb=ps, nt=True), 0.0),
                      p1, lhs1, states)
    rhs_u = each(lambda p_, ak, vt: p_[0:n_st] + _mm(ak, vt, pa=pv, pb=pv), p1, a_k, v_s)

    near = (row // 2) == (lane // 2)
    t_inv = each(lambda t: eye + jnp.where(near, t, 0.0), a_b)
    s = 2
    while s < chunk:
        off = ((row // (2 * s)) == (lane // (2 * s))) & ((row // s) != (lane // s))
        w = each(lambda ab, ti: _mm(jnp.where(off, ab, 0.0), ti, pa=pinv, pb=pinv), a_b, t_inv)
        t_inv = each(lambda ti, w_: ti + _mm(ti, w_, pa=pinv, pb=pinv), t_inv, w)
        s *= 2
    u = each(lambda ti, t: _mm(ti, t, pa=pinv, pb=pinv), t_inv, rhs_u)

    vu = each(lambda x_, y_: jnp.concatenate([x_, y_], axis=0), v_s, u)
    y = each(lambda p_, rk_, vu_: p_[n_st:] + _mm(rk_, vu_, pa=pv, pb=pv), p1, r_kb, vu)

    def finish(y_, w_, b_, bonus_, g_):
        mean = jnp.sum(y_, axis=1, keepdims=True) * (1.0 / HEAD_DIM)
        dev = jnp.where(own, y_ - mean, 0.0)
        var = jnp.sum(dev * dev, axis=1, keepdims=True) * (1.0 / HEAD_DIM)
        out = jnp.where(own, dev * lax.rsqrt(var + GN_EPS) * w_ + b_, 0.0) + bonus_
        return (out[0:SLAB] + out[SLAB:n_st]) * g_
    z = each(finish, y, gnw, gnb, bonus, g)

    vu_t = each(lambda t: t.T, vu)
    kb_h = each(lambda x_, y_: jnp.concatenate([x_, y_], axis=0), k_h, b_h)
    p_tot = each(jnp.exp, tot)
    if n_seq == 1:
        new_states = each(lambda s_, pt, vt, kb: [s_[0] * pt[0:1, :] + _mm(vt, kb, pa=pupd, pb=pupd)],
                          states, p_tot, vu_t, kb_h)
    else:
        col_seq = (lax.broadcasted_iota(jnp.int32, (1, 2 * n_st), 1) % SLAB) // chunk
        new_states = [[] for _ in range(n_p)]
        for q in range(n_seq):
            upd = each(lambda vt, kb: _mm(jnp.where(col_seq == q, vt, 0.0), kb, pa=pupd, pb=pupd), vu_t, kb_h)
            for p in range(n_p):
                new_states[p].append(states[p][q] * p_tot[p][q * chunk:q * chunk + 1, :] + upd[p])
    return z, new_states


def _scan_kernel(chunk, n_seq, n_pp, passes, has_state, *refs):
    if has_state:
        (r_ref, ld_ref, k_ref, v_ref, a_ref, g_ref, kk_ref, ka_ref, rk_ref, gw_ref, gb_ref, s_in_ref,
         z_ref, s_out_ref, s_scr) = refs
    else:
        (r_ref, ld_ref, k_ref, v_ref, a_ref, g_ref, kk_ref, ka_ref, rk_ref, gw_ref, gb_ref,
         z_ref, s_out_ref, s_scr) = refs
        s_in_ref = None
    c = pl.program_id(2)
    zero = jnp.zeros((HEAD_DIM, HEAD_DIM), f32)

    @pl.when(c == 0)
    def _():
        if has_state:
            for pp in range(n_pp):
                for q in range(n_seq):
                    s0 = s_in_ref[q, 2 * pp]
                    s1 = s_in_ref[q, 2 * pp + 1]
                    s_scr[pp * n_seq + q] = jnp.concatenate(
                        [jnp.concatenate([s0, zero], axis=1), jnp.concatenate([zero, s1], axis=1)], axis=0)
        else:
            s_scr[...] = jnp.zeros_like(s_scr)

    lanes = [slice(pp * LANES, (pp + 1) * LANES) for pp in range(n_pp)]
    acts = [tuple(ref[:, sl] for ref in (r_ref, ld_ref, k_ref, v_ref, a_ref, g_ref)) for sl in lanes]
    params = [tuple(ref[:, sl] for ref in (kk_ref, ka_ref, rk_ref, gw_ref, gb_ref)) for sl in lanes]
    states = [[s_scr[pp * n_seq + q] for q in range(n_seq)] for pp in range(n_pp)]
    z, new_states = _scan_pairs(chunk, n_seq, passes, acts, params, states)
    for pp in range(n_pp):
        z_ref[:, lanes[pp]] = z[pp].astype(bf16)
        for q in range(n_seq):
            s_scr[pp * n_seq + q] = new_states[pp][q]

    @pl.when(c == pl.num_programs(2) - 1)
    def _():
        for pp in range(n_pp):
            for q in range(n_seq):
                s = s_scr[pp * n_seq + q]
                s_out_ref[q, 2 * pp] = s[0:HEAD_DIM, 0:HEAD_DIM]
                s_out_ref[q, 2 * pp + 1] = s[HEAD_DIM:, HEAD_DIM:]


def _scan(proj, rwp, n_batch, seq_len, state_in, passes, n_pp):
    m = n_batch * seq_len
    if seq_len >= SLAB:
        chunk, n_seq = SLAB, 1
        n_chunks = seq_len // SLAB
        n_groups = n_batch
    else:
        chunk, n_seq = seq_len, SLAB // seq_len
        n_chunks = 1
        n_groups = n_batch // n_seq
    w = LANES * n_pp
    act = pl.BlockSpec((SLAB, w), lambda b, p, c: (b * n_chunks + c, p))
    vec = pl.BlockSpec((1, w), lambda b, p, c: (0, p))
    st_spec = pl.BlockSpec((n_seq, 2 * n_pp, HEAD_DIM, HEAD_DIM), lambda b, p, c: (b, p, 0, 0))
    has_state = state_in is not None
    in_specs = [act] * 6 + [vec] * 5 + ([st_spec] if has_state else [])
    args = list(proj) + [rwp["kk"], rwp["ka"], rwp["rk"], rwp["gn_w"], rwp["gn_b"]]
    if has_state:
        args.append(state_in)
    z, s_out = pl.pallas_call(
        functools.partial(_scan_kernel, chunk, n_seq, n_pp, passes, has_state),
        grid=(n_groups, N_PAIRS // n_pp, n_chunks),
        in_specs=in_specs,
        out_specs=[act, st_spec],
        out_shape=[jax.ShapeDtypeStruct((m, D_MODEL), bf16),
                   jax.ShapeDtypeStruct((n_batch, N_HEADS, HEAD_DIM, HEAD_DIM), f32)],
        scratch_shapes=[pltpu.VMEM((n_pp * n_seq, 2 * SLAB, LANES), f32)],
        compiler_params=_cparams(3),
        name="rwkv_scan",
    )(*args)
    return z, s_out


def _scan_lanes_kernel(n_tok, r_ref, ld_ref, k_ref, v_ref, a_ref, g_ref, kk_ref, ka_ref, rk_ref, gw_ref, gb_ref,
                       s_in_ref, z_ref, s_out_ref, kk_s, w_s, b_s, k2_s, r_s, v_s, y_s):
    hd = HEAD_DIM
    colsum = lambda t: jnp.sum(t, axis=0, keepdims=True)
    for t in range(n_tok):
        r_t, k_t, v_t, a_t = r_ref[t].T, k_ref[t].T, v_ref[t].T, a_ref[t].T
        kk_raw = k_t * kk_ref[...]
        k2 = k_t * (1.0 + (a_t - 1.0) * ka_ref[...])
        w = jnp.exp(ld_ref[t].T)
        for e in range(2):
            sl = slice(e * hd, (e + 1) * hd)
            kr = kk_raw[sl]
            kk = kr / jnp.maximum(jnp.sqrt(colsum(kr * kr)), 1e-12)
            kk_s[e, t] = kk
            b_s[e, t] = kk * a_t[sl]
            k2_s[e, t] = k2[sl]
            w_s[e, t] = w[sl]
            r_s[e, t] = r_t[sl]
            v_s[e, t] = v_t[sl]

    for e in range(2):
        def per_v(vi, carry):
            s = s_in_ref[e, vi]
            for t in range(n_tok):
                s_kk = colsum(s * kk_s[e, t])
                s = s * w_s[e, t] - s_kk * b_s[e, t] + v_s[e, t, pl.ds(vi, 1), :] * k2_s[e, t]
                y_s[e, t, pl.ds(vi, 1), :] = colsum(s * r_s[e, t])
            s_out_ref[e, vi] = s
            return carry
        lax.fori_loop(0, hd, per_v, 0)

    for t in range(n_tok):
        outs = []
        for e in range(2):
            sl = slice(e * hd, (e + 1) * hd)
            y = y_s[e, t]
            dev = y - colsum(y) * (1.0 / hd)
            var = colsum(dev * dev) * (1.0 / hd)
            bonus = colsum(r_s[e, t] * k2_s[e, t] * rk_ref[sl, :]) * v_s[e, t]
            outs.append(dev * lax.rsqrt(var + GN_EPS) * gw_ref[sl, :] + gb_ref[sl, :] + bonus)
        z_ref[t] = (jnp.concatenate(outs, axis=0).T * g_ref[t]).astype(bf16)


def _scan_lanes(proj, rwp, n_batch, n_tok, state):
    acts = [p.reshape(n_tok, n_batch, D_MODEL) for p in proj]
    lane_bc = lambda v: jnp.broadcast_to(v.reshape(D_MODEL, 1), (D_MODEL, n_batch))
    params = [lane_bc(rwp[n]) for n in ("kk", "ka", "rk", "gn_w", "gn_b")]
    act = pl.BlockSpec((n_tok, n_batch, LANES), lambda p: (0, 0, p))
    par = pl.BlockSpec((LANES, n_batch), lambda p: (p, 0))
    st_spec = pl.BlockSpec((2, HEAD_DIM, HEAD_DIM, n_batch), lambda p: (p, 0, 0, 0))
    per_tok = pltpu.VMEM((2, n_tok, HEAD_DIM, n_batch), f32)
    z, s_out = pl.pallas_call(
        functools.partial(_scan_lanes_kernel, n_tok),
        grid=(N_PAIRS,),
        in_specs=[act] * 6 + [par] * 5 + [st_spec],
        out_specs=[act, st_spec],
        out_shape=[jax.ShapeDtypeStruct((n_tok, n_batch, D_MODEL), bf16),
                   jax.ShapeDtypeStruct(state.shape, f32)],
        scratch_shapes=[per_tok] * 7,
        compiler_params=_cparams(1),
        name="rwkv_decode_scan",
    )(*acts, *params, state)
    return z.reshape(n_tok * n_batch, D_MODEL), s_out


def _proj_res_kernel(z_ref, w_ref, b_ref, x_ref, gt_ref, o_ref):
    out = _dot(z_ref[...], w_ref[...]) + b_ref[...]
    o_ref[...] = x_ref[...] + _rows(gt_ref[...], x_ref.shape[0]) * out


def _proj_res(z, w, bias, x, mod, tok_major, seq_len):
    m = x.shape[0]
    tm = min(ROW_TILE, m)
    rows = pl.BlockSpec((tm, D_MODEL), lambda i: (i, 0))
    return pl.pallas_call(
        _proj_res_kernel,
        grid=(m // tm,),
        in_specs=[rows,
                  pl.BlockSpec((D_MODEL, D_MODEL), lambda i: (0, 0), pipeline_mode=pl.Buffered(1)),
                  pl.BlockSpec((1, D_MODEL), lambda i: (0, 0)),
                  rows,
                  _mod_spec(tok_major, tm, seq_len, 5)],
        out_specs=rows,
        out_shape=jax.ShapeDtypeStruct((m, D_MODEL), f32),
        compiler_params=_cparams(1),
        name="proj_res",
    )(z, w, bias.reshape(1, D_MODEL), x, mod)


QKV_COLS = D_MODEL + 2 * N_KV_HEADS * LANES


def _qkv_kernel(n_rope, x_ref, ng_ref, sh_ref, sc_ref, w_ref, b_ref, cos_ref, sin_ref, o_ref, h_scr):
    j = pl.program_id(1)

    @pl.when(j == 0)
    def _():
        tm = x_ref.shape[0]
        h_scr[...] = _rms_mod(x_ref[...], ng_ref[...], _rows(sh_ref[...], tm), _rows(sc_ref[...], tm)).astype(bf16)

    acc = _dot(h_scr[...], w_ref[...]) + b_ref[...]

    @pl.when(j < n_rope)
    def _():
        cos = cos_ref[...]
        sin = sin_ref[...]
        first = (lax.broadcasted_iota(jnp.int32, cos.shape, 1) % HEAD_DIM) < HEAD_DIM // 2
        for c in range(acc.shape[1] // LANES):
            xc = acc[:, c * LANES:(c + 1) * LANES]
            rot = jnp.where(first, pltpu.roll(xc, LANES - HEAD_DIM // 2, 1),
                            pltpu.roll(xc, HEAD_DIM // 2, 1))
            o_ref[:, c * LANES:(c + 1) * LANES] = xc * cos + rot * sin

    @pl.when(j >= n_rope)
    def _():
        o_ref[...] = acc


def _qkv(x, mod, tok_major, seq_len, norm_g, w, b, cos, sin):
    m = x.shape[0]
    tm = min(ROW_TILE, m)
    tn = 512
    n_rope = (D_MODEL + N_KV_HEADS * LANES) // tn
    n_pos_blocks = cos.shape[0] // tm
    vec = pl.BlockSpec((1, D_MODEL), lambda i, j: (0, 0))
    tab = pl.BlockSpec((tm, LANES), lambda i, j: (i % n_pos_blocks, 0))
    return pl.pallas_call(
        functools.partial(_qkv_kernel, n_rope),
        grid=(m // tm, QKV_COLS // tn),
        in_specs=[pl.BlockSpec((tm, D_MODEL), lambda i, j: (i, 0)),
                  vec,
                  _mod_spec(tok_major, tm, seq_len, 3),
                  _mod_spec(tok_major, tm, seq_len, 4),
                  pl.BlockSpec((D_MODEL, tn), lambda i, j: (0, j)),
                  pl.BlockSpec((1, tn), lambda i, j: (0, j)),
                  tab, tab],
        out_specs=pl.BlockSpec((tm, tn), lambda i, j: (i, j)),
        out_shape=jax.ShapeDtypeStruct((m, QKV_COLS), f32),
        scratch_shapes=[pltpu.VMEM((tm, D_MODEL), bf16)],
        compiler_params=_cparams(2),
        name="swa_qkv",
    )(x, norm_g.reshape(1, D_MODEL), mod, mod, w, b.reshape(1, QKV_COLS), cos, sin)


def _rope_tables(positions):
    half = HEAD_DIM // 2
    inv_freq = ROPE_THETA ** (-jnp.arange(half, dtype=f32) / half)
    ang = positions.astype(f32)[:, None] * inv_freq[None, :]
    cos = jnp.tile(jnp.cos(ang), (1, LANES // half))
    sin = jnp.sin(ang)
    sin = jnp.tile(jnp.concatenate([-sin, sin], axis=1), (1, LANES // HEAD_DIM))
    return cos, sin


def _head_masks(rows):
    lane = lax.broadcasted_iota(jnp.int32, (rows, LANES), 1)
    return lane < HEAD_DIM, lane >= HEAD_DIM


def _attn_prompt_kernel(q_ref, kp_ref, kc_ref, vp_ref, vc_ref, sink_ref, o_ref):
    n = pl.program_id(1)
    blk = WINDOW
    m0, m1 = _head_masks(blk)
    m0k, m1k = _head_masks(2 * blk)
    qi = lax.broadcasted_iota(jnp.int32, (2 * blk, 2 * blk), 0) % blk
    sj = lax.broadcasted_iota(jnp.int32, (2 * blk, 2 * blk), 1)
    visible = (sj > qi) & (sj <= qi + blk) & ((n > 0) | (sj >= blk))
    top = lax.broadcasted_iota(jnp.int32, (2 * blk, 1), 0) < blk
    for c in range(N_KV_HEADS):
        sl = slice(c * LANES, (c + 1) * LANES)
        kd = jnp.concatenate([kp_ref[:, sl], kc_ref[:, sl]], axis=0).astype(bf16)
        vd = jnp.concatenate([vp_ref[:, sl], vc_ref[:, sl]], axis=0)
        vcat = jnp.concatenate([jnp.where(m0k, vd, 0.0), jnp.where(m1k, vd, 0.0)], axis=0).astype(bf16)
        for jj in range(GQA_GROUP // 2):
            pair = c * (GQA_GROUP // 2) + jj
            qp = q_ref[:, pair * LANES:(pair + 1) * LANES]
            qs = jnp.concatenate([jnp.where(m0, qp, 0.0), jnp.where(m1, qp, 0.0)], axis=0).astype(bf16)
            s = _dot_nt(qs, kd) * ATTN_SCALE
            s = jnp.where(visible, s, NEG_BIG)
            sk = jnp.where(top, sink_ref[2 * pair], sink_ref[2 * pair + 1])
            mx = jnp.maximum(jnp.max(s, axis=1, keepdims=True), sk)
            p = jnp.exp(s - mx)
            den = jnp.sum(p, axis=1, keepdims=True) + jnp.exp(sk - mx)
            p = (p / den).astype(bf16)
            pcat = jnp.concatenate([p[0:blk], p[blk:]], axis=1)
            o_ref[:, pair * LANES:(pair + 1) * LANES] = _dot(pcat, vcat).astype(bf16)


def _attn_prompt(qkv, sink, n_batch, seq_len):
    nb = seq_len // WINDOW
    kw = N_KV_HEADS * LANES
    k_blk = D_MODEL // kw
    cur = lambda off: pl.BlockSpec((WINDOW, kw), lambda b, n: (b * nb + n, k_blk + off))
    prev = lambda off: pl.BlockSpec((WINDOW, kw), lambda b, n: (b * nb + jnp.maximum(n - 1, 0), k_blk + off))
    return pl.pallas_call(
        _attn_prompt_kernel,
        grid=(n_batch, nb),
        in_specs=[pl.BlockSpec((WINDOW, D_MODEL), lambda b, n: (b * nb + n, 0)),
                  prev(0), cur(0), prev(1), cur(1),
                  pl.BlockSpec(memory_space=pltpu.SMEM)],
        out_specs=pl.BlockSpec((WINDOW, D_MODEL), lambda b, n: (b * nb + n, 0)),
        out_shape=jax.ShapeDtypeStruct((n_batch * seq_len, D_MODEL), bf16),
        compiler_params=_cparams(2),
        name="swa_prompt",
    )(qkv, qkv, qkv, qkv, qkv, sink)


SEQ_PER_GROUP = 4


def _attn_cached_kernel(n_tok, groups, q_ref, kn_ref, vn_ref, kc_ref, vc_ref, sink_ref, o_ref):
    rows = SEQ_PER_GROUP * n_tok
    n_st = GQA_GROUP * rows
    m0, m1 = _head_masks(rows)
    m0c, m1c = _head_masks(WINDOW)
    srow = lax.broadcasted_iota(jnp.int32, (n_st, 1), 0)
    row_seq = (srow % rows) // n_tok
    row_tok = srow % n_tok
    key_c = lax.broadcasted_iota(jnp.int32, (n_st, WINDOW), 1)
    vis_c = key_c > row_tok
    key_n = lax.broadcasted_iota(jnp.int32, (n_st, rows), 1)
    vis_n = ((key_n // n_tok) == row_seq) & ((key_n % n_tok) <= row_tok)
    for gi in range(groups):
        rs = slice(gi * rows, (gi + 1) * rows)
        for cp in range(N_KV_HEADS // 2):
            kc_pair = [kc_ref[gi * SEQ_PER_GROUP + b, :, cp * LANES:(cp + 1) * LANES]
                       for b in range(SEQ_PER_GROUP)]
            vc_pair = [vc_ref[gi * SEQ_PER_GROUP + b, :, cp * LANES:(cp + 1) * LANES]
                       for b in range(SEQ_PER_GROUP)]
            kc_sw = [pltpu.roll(t, HEAD_DIM, 1) for t in kc_pair]
            vc_sw = [pltpu.roll(t, HEAD_DIM, 1) for t in vc_pair]
            for ce in range(2):
                c = 2 * cp + ce
                keep = m0c if ce == 0 else m1c
                sl = slice(c * LANES, (c + 1) * LANES)
                kn = kn_ref[rs, sl].astype(bf16)
                vn = vn_ref[rs, sl].astype(bf16)
                pieces = []
                for jj in range(GQA_GROUP // 2):
                    pair = c * (GQA_GROUP // 2) + jj
                    qp = q_ref[rs, pair * LANES:(pair + 1) * LANES]
                    pieces += [jnp.where(m0, qp, 0.0), jnp.where(m1, qp, 0.0)]
                qs = jnp.concatenate(pieces, axis=0).astype(bf16)
                s_c = jnp.zeros((n_st, WINDOW), f32)
                for b in range(SEQ_PER_GROUP):
                    kx = jnp.where(keep, kc_pair[b], kc_sw[b]).astype(bf16)
                    s_c = jnp.where(row_seq == b, _dot_nt(qs, kx), s_c)
                s_c = jnp.where(vis_c, s_c * ATTN_SCALE, NEG_BIG)
                s_n = jnp.where(vis_n, _dot_nt(qs, kn) * ATTN_SCALE, NEG_BIG)
                sk = sink_ref[c]
                sk = sk[:, 0:1]
                mx = jnp.maximum(jnp.maximum(jnp.max(s_c, axis=1, keepdims=True),
                                             jnp.max(s_n, axis=1, keepdims=True)), sk)
                p_c = jnp.exp(s_c - mx)
                p_n = jnp.exp(s_n - mx)
                den = (jnp.sum(p_c, axis=1, keepdims=True) + jnp.sum(p_n, axis=1, keepdims=True)
                       + jnp.exp(sk - mx))
                p_c = (p_c / den).astype(bf16)
                p_n = (p_n / den).astype(bf16)
                o = _dot(p_n, vn)
                for b in range(SEQ_PER_GROUP):
                    vx = jnp.where(keep, vc_pair[b], vc_sw[b]).astype(bf16)
                    o = o + jnp.where(row_seq == b, _dot(p_c, vx), 0.0)
                for jj in range(GQA_GROUP // 2):
                    pair = c * (GQA_GROUP // 2) + jj
                    o0 = o[(2 * jj) * rows:(2 * jj + 1) * rows]
                    o1 = o[(2 * jj + 1) * rows:(2 * jj + 2) * rows]
                    o_ref[rs, pair * LANES:(pair + 1) * LANES] = jnp.where(m0, o0, o1).astype(bf16)


def _attn_cached(qkv, k_cache, v_cache, sink, n_batch, n_tok):
    groups = 2
    seqs = SEQ_PER_GROUP * groups
    rows = seqs * n_tok
    kw = N_KV_HEADS * LANES
    k_blk = D_MODEL // kw
    n_st = GQA_GROUP * SEQ_PER_GROUP * n_tok
    head = (jnp.arange(N_KV_HEADS)[:, None] * GQA_GROUP
            + (jnp.arange(n_st)[None, :] // (SEQ_PER_GROUP * n_tok)))
    sink_tab = jnp.broadcast_to(sink[head][:, :, None], (N_KV_HEADS, n_st, LANES))
    cache_spec = pl.BlockSpec((seqs, WINDOW, N_KV_HEADS * HEAD_DIM), lambda i: (i, 0, 0))
    return pl.pallas_call(
        functools.partial(_attn_cached_kernel, n_tok, groups),
        grid=(n_batch // seqs,),
        in_specs=[pl.BlockSpec((rows, D_MODEL), lambda i: (i, 0)),
                  pl.BlockSpec((rows, kw), lambda i: (i, k_blk)),
                  pl.BlockSpec((rows, kw), lambda i: (i, k_blk + 1)),
                  cache_spec, cache_spec,
                  pl.BlockSpec((N_KV_HEADS, n_st, LANES), lambda i: (0, 0, 0))],
        out_specs=pl.BlockSpec((rows, D_MODEL), lambda i: (i, 0)),
        out_shape=jax.ShapeDtypeStruct((n_batch * n_tok, D_MODEL), bf16),
        compiler_params=_cparams(1),
        name="swa_cached",
    )(qkv, qkv, qkv, k_cache, v_cache, sink_tab)


def _undup(t):
    return t.reshape(t.shape[0], N_KV_HEADS, 2, HEAD_DIM)[:, :, 0, :]


def _trunk(x, mod_all, tok_major, n_batch, seq_len, pos0, wkv_in, shift_in, k_in, v_in, wts):
    to_seq_major = lambda t: t.reshape(seq_len, n_batch, -1).transpose(1, 0, 2).reshape(n_batch * seq_len, -1)
    to_tok_major = lambda t: t.reshape(n_batch, seq_len, -1).transpose(1, 0, 2).reshape(n_batch * seq_len, -1)
    mod = mod_all[0]
    ffn_w = dict(wts["ffn"])
    x, ffn_w[0, 0] = _ffn(x, mod, tok_major, seq_len, 0, wts["norm_g"][0, 0], ffn_w[0, 0])
    s0 = shift_in[0] if tok_major else jnp.zeros((n_batch, 1, D_MODEL), f32)
    *proj, h_tail = _rwkv_proj(x, mod, tok_major, seq_len, wts["norm_g"][0, 1], s0, wts["rw"])
    if tok_major:
        z, wkv_t = _scan_lanes(proj, wts["rw"], n_batch, seq_len, jnp.transpose(wkv_in[0], (1, 2, 3, 0)))
        wkv_out = jnp.transpose(wkv_t, (3, 0, 1, 2))
        shift_out = h_tail
    else:
        z, wkv_out = _scan(proj, wts["rw"], n_batch, seq_len, None, *wts["scan_cfg"])
        shift_out = h_tail.reshape(n_batch, -1, 8, D_MODEL)[:, -1, -1]
    x = _proj_res(z, wts["rw"]["wo"], jnp.zeros((D_MODEL,), f32), x, mod, tok_major, seq_len)
    x, ffn_w[0, 1] = _ffn(x, mod, tok_major, seq_len, 6, wts["norm_g"][0, 2], ffn_w[0, 1])
    mod = mod_all[1]
    x, ffn_w[1, 0] = _ffn(x, mod, tok_major, seq_len, 0, wts["norm_g"][1, 0], ffn_w[1, 0])
    positions = pos0 + jnp.arange(seq_len)
    if tok_major:
        positions = jnp.repeat(positions, n_batch)
    cos, sin = _rope_tables(positions)
    qkv = _qkv(x, mod, tok_major, seq_len, wts["norm_g"][1, 1], wts["sw_wqkv"], wts["sw_bqkv"], cos, sin)
    kw = N_KV_HEADS * LANES
    if k_in is None:
        att = _attn_prompt(qkv, wts["sw_sink"], n_batch, seq_len)
        tail = qkv.reshape(n_batch, seq_len, QKV_COLS)[:, -WINDOW:, D_MODEL:]
        k_new = _undup(tail[..., :kw].reshape(n_batch * WINDOW, kw)).reshape(n_batch, WINDOW, N_KV_HEADS, HEAD_DIM)
        v_new = _undup(tail[..., kw:].reshape(n_batch * WINDOW, kw)).reshape(n_batch, WINDOW, N_KV_HEADS, HEAD_DIM)
    else:
        win = k_in.shape[2]
        qkv = to_seq_major(qkv)
        att = _attn_cached(qkv, k_in[0].reshape(n_batch, win, N_KV_HEADS * HEAD_DIM),
                           v_in[0].reshape(n_batch, win, N_KV_HEADS * HEAD_DIM),
                           wts["sw_sink"], n_batch, seq_len)
        att = to_tok_major(att)
        k_tok = _undup(qkv[:, D_MODEL:D_MODEL + kw]).reshape(n_batch, seq_len, N_KV_HEADS, HEAD_DIM)
        v_tok = _undup(qkv[:, D_MODEL + kw:]).reshape(n_batch, seq_len, N_KV_HEADS, HEAD_DIM)
        k_new = jnp.concatenate([k_in[0], k_tok], axis=1)[:, -win:]
        v_new = jnp.concatenate([v_in[0], v_tok], axis=1)[:, -win:]
    x = _proj_res(att, wts["sw_wo"], wts["sw_bo"], x, mod, tok_major, seq_len)
    y, ffn_w[1, 1] = _ffn(x, mod, tok_major, seq_len, 6, wts["norm_g"][1, 2], ffn_w[1, 1], final_g=wts["final_g"])
    return (y, wkv_out[None], shift_out[None], k_new[None], v_new[None]), ffn_w


def _dup_heads(w):
    lead = w.shape[:-1]
    w4 = w.reshape(lead + (N_KV_HEADS, 1, HEAD_DIM))
    return jnp.broadcast_to(w4, lead + (N_KV_HEADS, 2, HEAD_DIM)).reshape(lead + (N_KV_HEADS * LANES,))


def _pad_cols(w):
    return jnp.pad(w, ((0, 0), (0, LORA_PAD - w.shape[1])))


def _pad_rows(w):
    return jnp.pad(w, ((0, LORA_PAD - w.shape[0]), (0, 0)))


def kernel(x_prompt, x_sample, state_rwkv_wkv, state_rwkv_shift, cache_swa_k, cache_swa_v, c_prompt, c_sample, norm_g, w_ada, b_ada, w_ffn_in, w_ffn_out, rw_mu, rw_wrkv, rw_w0, rw_w1, rw_w2, rw_a0, rw_a1, rw_a2, rw_g1, rw_g2, rw_kk, rw_ka, rw_rk, rw_gn_w, rw_gn_b, rw_wo, sw_wqkv, sw_bqkv, sw_sink, sw_wo, sw_bo, final_g):
    n_p, seq_p, _ = x_prompt.shape
    n_s, seq_s, _ = x_sample.shape
    nq = N_HEADS * HEAD_DIM
    nkv = N_KV_HEADS * HEAD_DIM
    row = lambda t: t.reshape(1, D_MODEL)
    rw = dict(
        mu=rw_mu[0], wrkv=rw_wrkv[0].astype(bf16),
        w1=_pad_cols(rw_w1[0]).astype(bf16), a1=_pad_cols(rw_a1[0]).astype(bf16), g1=rw_g1[0].astype(bf16),
        w2=_pad_rows(rw_w2[0]).astype(bf16), a2=_pad_rows(rw_a2[0]).astype(bf16), g2=rw_g2[0].astype(bf16),
        w0=row(rw_w0[0]), a0=row(rw_a0[0]), kk=row(rw_kk[0]), ka=row(rw_ka[0]), rk=row(rw_rk[0]),
        gn_w=row(rw_gn_w[0]), gn_b=row(rw_gn_b[0]), wo=rw_wo[0].astype(bf16))
    wq = sw_wqkv[0]
    bq = sw_bqkv[0]
    wts = dict(
        norm_g=norm_g, final_g=final_g,
        ffn={(l, s): (w_ffn_in, w_ffn_out, l, s) for l in range(2) for s in range(2)}, rw=rw,
        sw_wqkv=jnp.concatenate([wq[:, :nq], _dup_heads(wq[:, nq:nq + nkv]), _dup_heads(wq[:, nq + nkv:])],
                                axis=1).astype(bf16),
        sw_bqkv=jnp.concatenate([bq[:nq], _dup_heads(bq[nq:nq + nkv]), _dup_heads(bq[nq + nkv:])]),
        sw_sink=sw_sink[0], sw_wo=sw_wo[0].astype(bf16), sw_bo=sw_bo[0])

    n_c = n_p + n_s
    pad = (-n_c) % 8
    c_all = jnp.concatenate([c_sample, c_prompt, jnp.zeros((pad, D_MODEL), f32)], axis=0)
    mod = _ada(c_all, w_ada, b_ada)
    mod_s = [mod[l] for l in range(mod.shape[0])]
    mod_p = [mod[l, n_s:n_c].reshape(n_p, 1, N_MOD * D_MODEL) for l in range(mod.shape[0])]

    (y_s, s_wkv, s_shift, s_k, s_v), wts["ffn"] = _trunk(
        jnp.transpose(x_sample, (1, 0, 2)).reshape(n_s * seq_s, D_MODEL), mod_s, True, n_s, seq_s, PAST_LEN,
        state_rwkv_wkv, state_rwkv_shift, cache_swa_k, cache_swa_v, wts)
    wts["scan_cfg"] = ((1, 1, 1, 1, 1), 16)
    (y_p, p_wkv, p_shift, p_k, p_v), _ = _trunk(
        x_prompt.reshape(n_p * seq_p, D_MODEL), mod_p, False, n_p, seq_p, 0, None, None, None, None, wts)
    y_s = jnp.transpose(y_s.reshape(seq_s, n_s, D_MODEL), (1, 0, 2))
    return (y_p.reshape(n_p, seq_p, D_MODEL), y_s, p_wkv, p_shift, p_k, p_v, s_wkv, s_shift, s_k, s_v)
```

```python
import functools
import math

import jax
import jax.numpy as jnp
from jax import lax
from jax.experimental import pallas as pl
from jax.experimental.pallas import tpu as pltpu

f32 = jnp.float32
bf16 = jnp.bfloat16

D_MODEL = 2048
HEAD_DIM = 64
N_HEADS = D_MODEL // HEAD_DIM
N_KV_HEADS = 4
GQA_GROUP = N_HEADS // N_KV_HEADS
WINDOW = 128
ATTN_SCALE = HEAD_DIM ** -0.5
ROPE_THETA = 10000.0
D_FF = 5632
N_MOD = 9
RMS_EPS = 1e-6
GN_EPS = 64e-5
PAST_LEN = 8192
LANES = 128
MXU_COLS = 256
N_PAIRS = D_MODEL // LANES
LORA_PAD = 128
SLAB = 64
VMEM_LIMIT = 56 * 1024 * 1024
NEG_BIG = -1e30
ROW_TILE = 512


def _cparams(n_axes):
    return pltpu.CompilerParams(dimension_semantics=("arbitrary",) * n_axes,
                                vmem_limit_bytes=VMEM_LIMIT)


def _dot(a, b):
    return jnp.dot(a, b, preferred_element_type=f32)


def _dot_nt(a, b):
    return lax.dot_general(a, b, (((1,), (1,)), ((), ())), preferred_element_type=f32)


def _split(x, n):
    if x.dtype == bf16:
        return [x]
    parts = []
    rem = x
    for i in range(n):
        p = rem.astype(bf16)
        parts.append(p)
        if i + 1 < n:
            rem = rem - p.astype(f32)
    return parts


def _mm(a, b, pa=1, pb=1, nt=False):
    a_parts = _split(a, pa)
    b_parts = _split(b, pb)
    order = max(len(a_parts), len(b_parts))
    acc = None
    for i, x in enumerate(a_parts):
        for j, y in enumerate(b_parts):
            if i + j >= order:
                continue
            t = _dot_nt(x, y) if nt else _dot(x, y)
            acc = t if acc is None else acc + t
    return acc


def _rms_mod(x, g, shift, scale):
    y = x * lax.rsqrt(jnp.mean(x * x, axis=-1, keepdims=True) + RMS_EPS) * g
    return y * (1.0 + scale) + shift


def _mod_spec(tok_major, tm, seq_len, idx):
    if tok_major:
        return pl.BlockSpec((tm // seq_len, D_MODEL), lambda i, *_: (0, idx))
    return pl.BlockSpec((None, 1, D_MODEL), lambda i, *_: ((i * tm) // seq_len, 0, idx))


def _rows(v, tm):
    n = v.shape[0]
    return v if n in (1, tm) else jnp.concatenate([v] * (tm // n), axis=0)


def _ada_kernel(c_ref, w_ref, b_ref, o_ref):
    c = c_ref[...]
    s = (c * jax.nn.sigmoid(c)).astype(bf16)
    o_ref[...] = _dot(s, w_ref[...].astype(bf16)) + b_ref[...]


def _ada(c_all, w_ada, b_ada):
    n_layers, _, n_out = w_ada.shape
    rows = c_all.shape[0]
    tn = 1024
    return pl.pallas_call(
        _ada_kernel,
        grid=(n_layers, n_out // tn),
        in_specs=[pl.BlockSpec((rows, D_MODEL), lambda l, j: (0, 0)),
                  pl.BlockSpec((None, D_MODEL, tn), lambda l, j: (l, 0, j)),
                  pl.BlockSpec((None, 1, tn), lambda l, j: (l, 0, j))],
        out_specs=pl.BlockSpec((None, rows, tn), lambda l, j: (l, 0, j)),
        out_shape=jax.ShapeDtypeStruct((n_layers, rows, n_out), f32),
        compiler_params=_cparams(2),
        name="ada_mod",
    )(c_all, w_ada, b_ada.reshape(n_layers, 1, n_out))


def _ffn_kernel(final_norm, emit_bf16, x_ref, ng_ref, sh_ref, sc_ref, gt_ref, wg_ref, wu_ref, wo_ref, fg_ref,
                *rest):
    if emit_bf16:
        o_ref, wg_out, wu_out, wo_out, h_scr, acc_scr = rest
    else:
        o_ref, h_scr, acc_scr = rest
    f = pl.program_id(1)

    @pl.when(f == 0)
    def _():
        tm = x_ref.shape[0]
        h_scr[...] = _rms_mod(x_ref[...], ng_ref[...], _rows(sh_ref[...], tm), _rows(sc_ref[...], tm)).astype(bf16)
        acc_scr[...] = jnp.zeros_like(acc_scr)

    wg, wu, wo = wg_ref[...], wu_ref[...], wo_ref[...]
    if emit_bf16:
        wg, wu, wo = wg.astype(bf16), wu.astype(bf16), wo.astype(bf16)
        wg_out[...] = wg
        wu_out[...] = wu
        wo_out[...] = wo
    h = h_scr[...]
    gate = _dot(h, wg)
    up = _dot(h, wu)
    act = (gate * jax.nn.sigmoid(gate) * up).astype(bf16)
    acc_scr[...] += _dot(act, wo)

    @pl.when(f == pl.num_programs(1) - 1)
    def _():
        y = x_ref[...] + 0.5 * _rows(gt_ref[...], x_ref.shape[0]) * acc_scr[...]
        if final_norm:
            y = y * lax.rsqrt(jnp.mean(y * y, axis=-1, keepdims=True) + RMS_EPS) * fg_ref[...]
        o_ref[...] = y


def _ffn(x, mod, tok_major, seq_len, mod_base, norm_g, weights, final_g=None):
    m = x.shape[0]
    tm = min(ROW_TILE, m)
    emit = len(weights) == 4
    tf = 256 if emit else 512
    nf = D_FF // tf
    vec = pl.BlockSpec((1, D_MODEL), lambda i, j: (0, 0))
    fg = jnp.ones((1, D_MODEL), f32) if final_g is None else final_g.reshape(1, D_MODEL)
    w_in_spec = pl.BlockSpec((D_MODEL, tf), lambda i, j: (0, j))
    w_out_spec = pl.BlockSpec((tf, D_MODEL), lambda i, j: (j, 0))
    out_specs = [pl.BlockSpec((tm, D_MODEL), lambda i, j: (i, 0))]
    out_shape = [jax.ShapeDtypeStruct((m, D_MODEL), f32)]
    if emit:
        assert m == tm, "the weight copies are written once, by a single row tile"
        w_ffn_in, w_ffn_out, layer, slot = weights
        w_args = (w_ffn_in, w_ffn_in, w_ffn_out)
        w_specs = [pl.BlockSpec((None, None, D_MODEL, tf), lambda i, j: (layer, slot, 0, j)),
                   pl.BlockSpec((None, None, D_MODEL, tf), lambda i, j: (layer, slot, 0, j + nf)),
                   pl.BlockSpec((None, None, tf, D_MODEL), lambda i, j: (layer, slot, j, 0))]
        out_specs += [w_in_spec, w_in_spec, w_out_spec]
        out_shape += [jax.ShapeDtypeStruct((D_MODEL, D_FF), bf16), jax.ShapeDtypeStruct((D_MODEL, D_FF), bf16),
                      jax.ShapeDtypeStruct((D_FF, D_MODEL), bf16)]
    else:
        w_args = weights
        w_specs = [w_in_spec, w_in_spec, w_out_spec]
    outs = pl.pallas_call(
        functools.partial(_ffn_kernel, final_g is not None, emit),
        grid=(m // tm, nf),
        in_specs=[pl.BlockSpec((tm, D_MODEL), lambda i, j: (i, 0)),
                  vec,
                  _mod_spec(tok_major, tm, seq_len, mod_base),
                  _mod_spec(tok_major, tm, seq_len, mod_base + 1),
                  _mod_spec(tok_major, tm, seq_len, mod_base + 2)] + w_specs + [vec],
        out_specs=out_specs,
        out_shape=out_shape,
        scratch_shapes=[pltpu.VMEM((tm, D_MODEL), bf16), pltpu.VMEM((tm, D_MODEL), f32)],
        compiler_params=_cparams(2),
        name="ffn",
    )(x, norm_g.reshape(1, D_MODEL), mod, mod, mod, *w_args, fg)
    return (outs[0], tuple(outs[1:])) if emit else (outs[0], weights)


def _rwkv_proj_kernel(seq_len, tm, h_rows, tok_major,
                      x_ref, ng_ref, sh_ref, sc_ref, s0_ref, mu_ref, w1_ref, a1_ref, g1_ref,
                      wr_ref, wk_ref, wv_ref, w2_ref, a2_ref, g2_ref, w0_ref, a0_ref,
                      r_ref, ld_ref, k_ref, v_ref, a_ref, g_ref, h_ref,
                      hs, xr, xk, xv, tw, ta, sg):
    i = pl.program_id(0)
    j = pl.program_id(1)

    @pl.when(j == 0)
    def _():
        h = _rms_mod(x_ref[...], ng_ref[...], _rows(sh_ref[...], tm), _rows(sc_ref[...], tm))
        h_ref[...] = h[tm - h_rows:tm, :]
        if tok_major:
            n_seq = s0_ref.shape[0]
            prev = jnp.concatenate([s0_ref[...], h[0:tm - n_seq, :]], axis=0)
        else:
            @pl.when(i == 0)
            def _():
                hs[0:8, :] = jnp.zeros((8, D_MODEL), f32)

            @pl.when(i > 0)
            def _():
                hs[0:8, :] = hs[tm:tm + 8, :]

            hs[8:tm + 8, :] = h
            row = i * tm + lax.broadcasted_iota(jnp.int32, (tm, 1), 0)
            prev = jnp.where(row % seq_len == 0, s0_ref[...], hs[7:tm + 7, :])
        xx = prev - h
        mu = mu_ref[...]
        xr[...] = (h + xx * mu[0:1, :]).astype(bf16)
        xk[...] = (h + xx * mu[2:3, :]).astype(bf16)
        xv[...] = (h + xx * mu[3:4, :]).astype(bf16)
        xw = (h + xx * mu[1:2, :]).astype(bf16)
        tw[...] = jnp.tanh(_dot(xw, w1_ref[...])).astype(bf16)
        xa = (h + xx * mu[4:5, :]).astype(bf16)
        ta[...] = _dot(xa, a1_ref[...]).astype(bf16)
        xg = (h + xx * mu[5:6, :]).astype(bf16)
        sg[...] = jax.nn.sigmoid(_dot(xg, g1_ref[...])).astype(bf16)

    r_ref[...] = _dot(xr[...], wr_ref[...])
    k_ref[...] = _dot(xk[...], wk_ref[...])
    v_ref[...] = _dot(xv[...], wv_ref[...])
    z = w0_ref[...] + _dot(tw[...], w2_ref[...])
    ld_ref[...] = -jax.nn.sigmoid(z) * math.exp(-0.5)
    a_ref[...] = jax.nn.sigmoid(a0_ref[...] + _dot(ta[...], a2_ref[...]))
    g_ref[...] = _dot(sg[...], g2_ref[...])


def _rwkv_proj(x, mod, tok_major, seq_len, norm_g, s0, rwp):
    m = x.shape[0]
    tm = min(ROW_TILE, m)
    tn = 256
    h_rows = tm // seq_len if tok_major else 8
    full = lambda shape: pl.BlockSpec(shape, lambda i, j: (0,) * len(shape))
    col = lambda rows: pl.BlockSpec((rows, tn), lambda i, j: (0, j))
    wspec = lambda which: pl.BlockSpec((None, D_MODEL, tn), lambda i, j: (which, 0, j))
    if tok_major:
        assert m == tm
        s0_spec = pl.BlockSpec((tm // seq_len, D_MODEL), lambda i, j: (0, 0))
    else:
        s0_spec = pl.BlockSpec((None, 1, D_MODEL), lambda i, j: ((i * tm) // seq_len, 0, 0))
    out_spec = pl.BlockSpec((tm, tn), lambda i, j: (i, j))
    out_sds = jax.ShapeDtypeStruct((m, D_MODEL), f32)
    outs = pl.pallas_call(
        functools.partial(_rwkv_proj_kernel, seq_len, tm, h_rows, tok_major),
        grid=(m // tm, D_MODEL // tn),
        in_specs=[pl.BlockSpec((tm, D_MODEL), lambda i, j: (i, 0)),
                  full((1, D_MODEL)),
                  _mod_spec(tok_major, tm, seq_len, 3),
                  _mod_spec(tok_major, tm, seq_len, 4),
                  s0_spec,
                  full((6, D_MODEL)),
                  full((D_MODEL, LORA_PAD)), full((D_MODEL, LORA_PAD)), full((D_MODEL, 256)),
                  wspec(0), wspec(1), wspec(2),
                  col(LORA_PAD), col(LORA_PAD), col(256), col(1), col(1)],
        out_specs=[out_spec] * 6 + [pl.BlockSpec((h_rows, D_MODEL), lambda i, j: (i, 0))],
        out_shape=[out_sds] * 6 + [jax.ShapeDtypeStruct((m // tm * h_rows, D_MODEL), f32)],
        scratch_shapes=[pltpu.VMEM((tm + 8, D_MODEL), f32),
                        pltpu.VMEM((tm, D_MODEL), bf16), pltpu.VMEM((tm, D_MODEL), bf16),
                        pltpu.VMEM((tm, D_MODEL), bf16),
                        pltpu.VMEM((tm, LORA_PAD), bf16), pltpu.VMEM((tm, LORA_PAD), bf16),
                        pltpu.VMEM((tm, 256), bf16)],
        compiler_params=_cparams(2),
        name="rwkv_proj",
    )(x, norm_g.reshape(1, D_MODEL), mod, mod, s0, rwp["mu"], rwp["w1"], rwp["a1"], rwp["g1"],
      rwp["wrkv"], rwp["wrkv"], rwp["wrkv"], rwp["w2"], rwp["a2"], rwp["g2"], rwp["w0"], rwp["a0"])
    return outs


def _scan_pairs(chunk, n_seq, passes, acts, params, states):
    px, ps, pv, pinv, pupd = passes
    n_p = len(acts)
    n_st = 2 * SLAB
    row = lax.broadcasted_iota(jnp.int32, (n_st, LANES), 0)
    lane = lax.broadcasted_iota(jnp.int32, (n_st, LANES), 1)
    own = (row // SLAB) == (lane // HEAD_DIM)
    blk = (row // chunk) == (lane // chunk)
    strict = blk & (lane < row)
    incl = blk & (lane <= row)
    eye = jnp.where(row == lane, 1.0, 0.0)
    twice = lambda t: jnp.concatenate([t, t], axis=0)
    st = lambda t: jnp.where(own, twice(t), 0.0)
    each = lambda fn, *lists: [fn(*args) for args in zip(*lists)]

    r, ld, k, v, a, g = [[act[i] for act in acts] for i in range(6)]
    kkp, kap, rkp, gnw, gnb = [[par[i] for par in params] for i in range(5)]

    r64 = lax.broadcasted_iota(jnp.int32, (SLAB, SLAB), 0)
    c64 = lax.broadcasted_iota(jnp.int32, (SLAB, SLAB), 1)
    same = (r64 // chunk) == (c64 // chunk)
    tri_ones = jnp.concatenate([jnp.where(same & (c64 <= r64), 1.0, 0.0),
                                jnp.where(same, 1.0, 0.0)], axis=0).astype(bf16)
    sums = each(lambda t: _mm(tri_ones, t, pb=3), ld)
    cs = each(lambda t: t[0:SLAB], sums)
    tot = each(lambda t: t[SLAB:n_st], sums)

    kk_raw = each(lambda t, p: st(t * p), k, kkp)
    kk = each(lambda t: t * (1.0 / jnp.maximum(jnp.sqrt(jnp.sum(t * t, axis=1, keepdims=True)), 1e-12)), kk_raw)
    k2 = each(lambda kt, at, p: st(kt * (1.0 + (at - 1.0) * p)), k, a, kap)
    b = each(lambda t, at: t * twice(at), kk, a)
    r_s = each(st, r)
    v_s = each(st, v)
    bonus = each(lambda rt, kt, p, vt: jnp.sum(rt * kt * p, axis=1, keepdims=True) * vt, r_s, k2, rkp, v_s)

    e_neg = each(lambda c_: twice(jnp.exp(-c_)), cs)
    e_tail = each(lambda t_, c_: twice(jnp.exp(t_ - c_)), tot, cs)
    a_t = each(lambda t, c_, l_: -t * twice(jnp.exp(c_ - l_)), kk, cs, ld)
    r_t = each(lambda t, c_: t * twice(jnp.exp(c_)), r_s, cs)
    k_t = each(lambda t, e: t * e, k2, e_neg)
    b_t = each(lambda t, e: t * e, b, e_neg)
    k_h = each(lambda t, e: t * e, k2, e_tail)
    b_h = each(lambda t, e: t * e, b, e_tail)

    lhs1 = each(lambda x_, y_: jnp.concatenate([x_, y_], axis=0), a_t, r_t)
    rhs1 = each(lambda x_, y_: jnp.concatenate([x_, y_], axis=0), k_t, b_t)
    x = each(lambda l_, r_: _mm(l_, r_, pa=px, pb=px, nt=True), lhs1, rhs1)
    a_k = each(lambda t: jnp.where(strict, t[0:n_st, 0:n_st], 0.0), x)
    a_b = each(lambda t: jnp.where(strict, t[0:n_st, n_st:], 0.0), x)
    r_kb = each(lambda t: jnp.concatenate([jnp.where(incl, t[n_st:, 0:n_st], 0.0),
                                           jnp.where(incl, t[n_st:, n_st:], 0.0)], axis=1), x)

    if n_seq == 1:
        p1 = each(lambda l_, s_: _mm(l_, s_[0], pa=ps, pb=ps, nt=True), lhs1, states)
    else:
        row_seq = (lax.broadcasted_iota(jnp.int32, (2 * n_st, 1), 0) % SLAB) // chunk
        p1 = [jnp.zeros((2 * n_st, LANES), f32)] * n_p
        for q in range(n_seq):
            p1 = each(lambda acc, l_, s_: acc + jnp.where(row_seq == q, _mm(l_, s_[q], pa=ps, pb=ps, nt=True), 0.0),
                      p1, lhs1, states)
    rhs_u = each(lambda p_, ak, vt: p_[0:n_st] + _mm(ak, vt, pa=pv, pb=pv), p1, a_k, v_s)

    near = (row // 2) == (lane // 2)
    t_inv = each(lambda t: eye + jnp.where(near, t, 0.0), a_b)
    s = 2
    while s < chunk:
        off = ((row // (2 * s)) == (lane // (2 * s))) & ((row // s) != (lane // s))
        w = each(lambda ab, ti: _mm(jnp.where(off, ab, 0.0), ti, pa=pinv, pb=pinv), a_b, t_inv)
        t_inv = each(lambda ti, w_: ti + _mm(ti, w_, pa=pinv, pb=pinv), t_inv, w)
        s *= 2
    u = each(lambda ti, t: _mm(ti, t, pa=pinv, pb=pinv), t_inv, rhs_u)

    vu = each(lambda x_, y_: jnp.concatenate([x_, y_], axis=0), v_s, u)
    y = each(lambda p_, rk_, vu_: p_[n_st:] + _mm(rk_, vu_, pa=pv, pb=pv), p1, r_kb, vu)

    def finish(y_, w_, b_, bonus_, g_):
        mean = jnp.sum(y_, axis=1, keepdims=True) * (1.0 / HEAD_DIM)
        dev = jnp.where(own, y_ - mean, 0.0)
        var = jnp.sum(dev * dev, axis=1, keepdims=True) * (1.0 / HEAD_DIM)
        out = jnp.where(own, dev * lax.rsqrt(var + GN_EPS) * w_ + b_, 0.0) + bonus_
        return (out[0:SLAB] + out[SLAB:n_st]) * g_
    z = each(finish, y, gnw, gnb, bonus, g)

    vu_t = each(lambda t: t.T, vu)
    kb_h = each(lambda x_, y_: jnp.concatenate([x_, y_], axis=0), k_h, b_h)
    p_tot = each(jnp.exp, tot)
    if n_seq == 1:
        new_states = each(lambda s_, pt, vt, kb: [s_[0] * pt[0:1, :] + _mm(vt, kb, pa=pupd, pb=pupd)],
                          states, p_tot, vu_t, kb_h)
    else:
        col_seq = (lax.broadcasted_iota(jnp.int32, (1, 2 * n_st), 1) % SLAB) // chunk
        new_states = [[] for _ in range(n_p)]
        for q in range(n_seq):
            upd = each(lambda vt, kb: _mm(jnp.where(col_seq == q, vt, 0.0), kb, pa=pupd, pb=pupd), vu_t, kb_h)
            for p in range(n_p):
                new_states[p].append(states[p][q] * p_tot[p][q * chunk:q * chunk + 1, :] + upd[p])
    return z, new_states


def _scan_kernel(chunk, n_seq, n_pp, passes, has_state, *refs):
    if has_state:
        (r_ref, ld_ref, k_ref, v_ref, a_ref, g_ref, kk_ref, ka_ref, rk_ref, gw_ref, gb_ref, s_in_ref,
         z_ref, s_out_ref, s_scr) = refs
    else:
        (r_ref, ld_ref, k_ref, v_ref, a_ref, g_ref, kk_ref, ka_ref, rk_ref, gw_ref, gb_ref,
         z_ref, s_out_ref, s_scr) = refs
        s_in_ref = None
    c = pl.program_id(2)
    zero = jnp.zeros((HEAD_DIM, HEAD_DIM), f32)

    @pl.when(c == 0)
    def _():
        if has_state:
            for pp in range(n_pp):
                for q in range(n_seq):
                    s0 = s_in_ref[q, 2 * pp]
                    s1 = s_in_ref[q, 2 * pp + 1]
                    s_scr[pp * n_seq + q] = jnp.concatenate(
                        [jnp.concatenate([s0, zero], axis=1), jnp.concatenate([zero, s1], axis=1)], axis=0)
        else:
            s_scr[...] = jnp.zeros_like(s_scr)

    lanes = [slice(pp * LANES, (pp + 1) * LANES) for pp in range(n_pp)]
    acts = [tuple(ref[:, sl] for ref in (r_ref, ld_ref, k_ref, v_ref, a_ref, g_ref)) for sl in lanes]
    params = [tuple(ref[:, sl] for ref in (kk_ref, ka_ref, rk_ref, gw_ref, gb_ref)) for sl in lanes]
    states = [[s_scr[pp * n_seq + q] for q in range(n_seq)] for pp in range(n_pp)]
    z, new_states = _scan_pairs(chunk, n_seq, passes, acts, params, states)
    for pp in range(n_pp):
        z_ref[:, lanes[pp]] = z[pp].astype(bf16)
        for q in range(n_seq):
            s_scr[pp * n_seq + q] = new_states[pp][q]

    @pl.when(c == pl.num_programs(2) - 1)
    def _():
        for pp in range(n_pp):
            for q in range(n_seq):
                s = s_scr[pp * n_seq + q]
                s_out_ref[q, 2 * pp] = s[0:HEAD_DIM, 0:HEAD_DIM]
                s_out_ref[q, 2 * pp + 1] = s[HEAD_DIM:, HEAD_DIM:]


def _scan(proj, rwp, n_batch, seq_len, state_in, passes, n_pp):
    m = n_batch * seq_len
    if seq_len >= SLAB:
        chunk, n_seq = SLAB, 1
        n_chunks = seq_len // SLAB
        n_groups = n_batch
    else:
        chunk, n_seq = seq_len, SLAB // seq_len
        n_chunks = 1
        n_groups = n_batch // n_seq
    w = LANES * n_pp
    act = pl.BlockSpec((SLAB, w), lambda b, p, c: (b * n_chunks + c, p))
    vec = pl.BlockSpec((1, w), lambda b, p, c: (0, p))
    st_spec = pl.BlockSpec((n_seq, 2 * n_pp, HEAD_DIM, HEAD_DIM), lambda b, p, c: (b, p, 0, 0))
    has_state = state_in is not None
    in_specs = [act] * 6 + [vec] * 5 + ([st_spec] if has_state else [])
    args = list(proj) + [rwp["kk"], rwp["ka"], rwp["rk"], rwp["gn_w"], rwp["gn_b"]]
    if has_state:
        args.append(state_in)
    z, s_out = pl.pallas_call(
        functools.partial(_scan_kernel, chunk, n_seq, n_pp, passes, has_state),
        grid=(n_groups, N_PAIRS // n_pp, n_chunks),
        in_specs=in_specs,
        out_specs=[act, st_spec],
        out_shape=[jax.ShapeDtypeStruct((m, D_MODEL), bf16),
                   jax.ShapeDtypeStruct((n_batch, N_HEADS, HEAD_DIM, HEAD_DIM), f32)],
        scratch_shapes=[pltpu.VMEM((n_pp * n_seq, 2 * SLAB, LANES), f32)],
        compiler_params=_cparams(3),
        name="rwkv_scan",
    )(*args)
    return z, s_out


def _scan_lanes_kernel(n_tok, r_ref, ld_ref, k_ref, v_ref, a_ref, g_ref, kk_ref, ka_ref, rk_ref, gw_ref, gb_ref,
                       s_in_ref, z_ref, s_out_ref, kk_s, w_s, b_s, k2_s, r_s, v_s, y_s):
    hd = HEAD_DIM
    colsum = lambda t: jnp.sum(t, axis=0, keepdims=True)
    for t in range(n_tok):
        r_t, k_t, v_t, a_t = r_ref[t].T, k_ref[t].T, v_ref[t].T, a_ref[t].T
        kk_raw = k_t * kk_ref[...]
        k2 = k_t * (1.0 + (a_t - 1.0) * ka_ref[...])
        w = jnp.exp(ld_ref[t].T)
        for e in range(2):
            sl = slice(e * hd, (e + 1) * hd)
            kr = kk_raw[sl]
            kk = kr * (1.0 / jnp.maximum(jnp.sqrt(colsum(kr * kr)), 1e-12))
            kk_s[e, t] = kk
            b_s[e, t] = kk * a_t[sl]
            k2_s[e, t] = k2[sl]
            w_s[e, t] = w[sl]
            r_s[e, t] = r_t[sl]
            v_s[e, t] = v_t[sl]

    rows_per_iter = 2

    def per_rows(i, carry):
        chains = [(e, rows_per_iter * i + d) for e in range(2) for d in range(rows_per_iter)]
        s = [s_in_ref[e, vi] for e, vi in chains]
        for t in range(n_tok):
            for c, (e, vi) in enumerate(chains):
                s_kk = colsum(s[c] * kk_s[e, t])
                s[c] = s[c] * w_s[e, t] - s_kk * b_s[e, t] + v_s[e, t, pl.ds(vi, 1), :] * k2_s[e, t]
                y_s[e, t, pl.ds(vi, 1), :] = colsum(s[c] * r_s[e, t])
        for c, (e, vi) in enumerate(chains):
            s_out_ref[e, vi] = s[c]
        return carry
    lax.fori_loop(0, hd // rows_per_iter, per_rows, 0)

    for t in range(n_tok):
        outs = []
        for e in range(2):
            sl = slice(e * hd, (e + 1) * hd)
            y = y_s[e, t]
            dev = y - colsum(y) * (1.0 / hd)
            var = colsum(dev * dev) * (1.0 / hd)
            bonus = colsum(r_s[e, t] * k2_s[e, t] * rk_ref[sl, :]) * v_s[e, t]
            outs.append(dev * lax.rsqrt(var + GN_EPS) * gw_ref[sl, :] + gb_ref[sl, :] + bonus)
        z_ref[t] = (jnp.concatenate(outs, axis=0).T * g_ref[t]).astype(bf16)


def _scan_lanes(proj, rwp, n_batch, n_tok, state):
    acts = [p.reshape(n_tok, n_batch, D_MODEL) for p in proj]
    lane_bc = lambda v: jnp.broadcast_to(v.reshape(D_MODEL, 1), (D_MODEL, n_batch))
    params = [lane_bc(rwp[n]) for n in ("kk", "ka", "rk", "gn_w", "gn_b")]
    act = pl.BlockSpec((n_tok, n_batch, LANES), lambda p: (0, 0, p))
    par = pl.BlockSpec((LANES, n_batch), lambda p: (p, 0))
    st_spec = pl.BlockSpec((2, HEAD_DIM, HEAD_DIM, n_batch), lambda p: (p, 0, 0, 0))
    per_tok = pltpu.VMEM((2, n_tok, HEAD_DIM, n_batch), f32)
    z, s_out = pl.pallas_call(
        functools.partial(_scan_lanes_kernel, n_tok),
        grid=(N_PAIRS,),
        in_specs=[act] * 6 + [par] * 5 + [st_spec],
        out_specs=[act, st_spec],
        out_shape=[jax.ShapeDtypeStruct((n_tok, n_batch, D_MODEL), bf16),
                   jax.ShapeDtypeStruct(state.shape, f32)],
        scratch_shapes=[per_tok] * 7,
        compiler_params=_cparams(1),
        name="rwkv_decode_scan",
    )(*acts, *params, state)
    return z.reshape(n_tok * n_batch, D_MODEL), s_out


def _proj_res_kernel(z_ref, w_ref, b_ref, x_ref, gt_ref, o_ref):
    out = _dot(z_ref[...], w_ref[...]) + b_ref[...]
    o_ref[...] = x_ref[...] + _rows(gt_ref[...], x_ref.shape[0]) * out


def _proj_res(z, w, bias, x, mod, tok_major, seq_len):
    m = x.shape[0]
    tm = min(ROW_TILE, m)
    rows = pl.BlockSpec((tm, D_MODEL), lambda i: (i, 0))
    return pl.pallas_call(
        _proj_res_kernel,
        grid=(m // tm,),
        in_specs=[rows,
                  pl.BlockSpec((D_MODEL, D_MODEL), lambda i: (0, 0), pipeline_mode=pl.Buffered(1)),
                  pl.BlockSpec((1, D_MODEL), lambda i: (0, 0)),
                  rows,
                  _mod_spec(tok_major, tm, seq_len, 5)],
        out_specs=rows,
        out_shape=jax.ShapeDtypeStruct((m, D_MODEL), f32),
        compiler_params=_cparams(1),
        name="proj_res",
    )(z, w, bias.reshape(1, D_MODEL), x, mod)


QKV_COLS = D_MODEL + 2 * N_KV_HEADS * LANES


def _qkv_kernel(x_ref, ng_ref, sh_ref, sc_ref, w_ref, b_ref, cos_ref, sin_ref, o_ref, h_scr):
    @pl.when(pl.program_id(1) == 0)
    def _():
        tm = x_ref.shape[0]
        h_scr[...] = _rms_mod(x_ref[...], ng_ref[...], _rows(sh_ref[...], tm), _rows(sc_ref[...], tm)).astype(bf16)

    cos = cos_ref[...]
    sin = sin_ref[...]
    first = (lax.broadcasted_iota(jnp.int32, cos.shape, 1) % HEAD_DIM) < HEAD_DIM // 2
    h = h_scr[...]
    for c0 in range(0, o_ref.shape[1], MXU_COLS):
        acc = _dot(h, w_ref[:, c0:c0 + MXU_COLS]) + b_ref[:, c0:c0 + MXU_COLS]
        for c in range(0, MXU_COLS, LANES):
            xc = acc[:, c:c + LANES]
            rot = jnp.where(first, pltpu.roll(xc, LANES - HEAD_DIM // 2, 1),
                            pltpu.roll(xc, HEAD_DIM // 2, 1))
            o_ref[:, c0 + c:c0 + c + LANES] = xc * cos + rot * sin


def _qkv(x, mod, tok_major, seq_len, norm_g, w, b, cos, sin):
    m = x.shape[0]
    tm = min(ROW_TILE, m)
    tn = 512
    n_rope = (D_MODEL + N_KV_HEADS * LANES) // tn
    n_pos_blocks = cos.shape[0] // tm
    vec = pl.BlockSpec((1, D_MODEL), lambda i, j: (0, 0))
    tab = pl.BlockSpec((tm, LANES), lambda i, j: (i % n_pos_blocks, j // n_rope))
    return pl.pallas_call(
        _qkv_kernel,
        grid=(m // tm, QKV_COLS // tn),
        in_specs=[pl.BlockSpec((tm, D_MODEL), lambda i, j: (i, 0)),
                  vec,
                  _mod_spec(tok_major, tm, seq_len, 3),
                  _mod_spec(tok_major, tm, seq_len, 4),
                  pl.BlockSpec((D_MODEL, tn), lambda i, j: (0, j)),
                  pl.BlockSpec((1, tn), lambda i, j: (0, j)),
                  tab, tab],
        out_specs=pl.BlockSpec((tm, tn), lambda i, j: (i, j)),
        out_shape=jax.ShapeDtypeStruct((m, QKV_COLS), f32),
        scratch_shapes=[pltpu.VMEM((tm, D_MODEL), bf16)],
        compiler_params=_cparams(2),
        name="swa_qkv",
    )(x, norm_g.reshape(1, D_MODEL), mod, mod, w, b.reshape(1, QKV_COLS), cos, sin)


def _rope_tables(positions):
    half = HEAD_DIM // 2
    inv_freq = ROPE_THETA ** (-jnp.arange(half, dtype=f32) / half)
    ang = positions.astype(f32)[:, None] * inv_freq[None, :]
    cos = jnp.tile(jnp.cos(ang), (1, LANES // half))
    sin = jnp.sin(ang)
    sin = jnp.tile(jnp.concatenate([-sin, sin], axis=1), (1, LANES // HEAD_DIM))
    return (jnp.concatenate([cos, jnp.ones_like(cos)], axis=1),
            jnp.concatenate([sin, jnp.zeros_like(sin)], axis=1))


def _head_masks(rows):
    lane = lax.broadcasted_iota(jnp.int32, (rows, LANES), 1)
    return lane < HEAD_DIM, lane >= HEAD_DIM


def _attn_prompt_kernel(q_ref, kp_ref, kc_ref, vp_ref, vc_ref, sink_ref, o_ref):
    n = pl.program_id(1)
    blk = WINDOW
    m0, m1 = _head_masks(blk)
    m0k, m1k = _head_masks(2 * blk)
    qi = lax.broadcasted_iota(jnp.int32, (2 * blk, 2 * blk), 0) % blk
    sj = lax.broadcasted_iota(jnp.int32, (2 * blk, 2 * blk), 1)
    visible = (sj > qi) & (sj <= qi + blk) & ((n > 0) | (sj >= blk))
    top = lax.broadcasted_iota(jnp.int32, (2 * blk, 1), 0) < blk
    for c in range(N_KV_HEADS):
        sl = slice(c * LANES, (c + 1) * LANES)
        kd = jnp.concatenate([kp_ref[:, sl], kc_ref[:, sl]], axis=0).astype(bf16)
        vd = jnp.concatenate([vp_ref[:, sl], vc_ref[:, sl]], axis=0)
        vcat = jnp.concatenate([jnp.where(m0k, vd, 0.0), jnp.where(m1k, vd, 0.0)], axis=0).astype(bf16)
        for jj in range(GQA_GROUP // 2):
            pair = c * (GQA_GROUP // 2) + jj
            qp = q_ref[:, pair * LANES:(pair + 1) * LANES]
            qs = jnp.concatenate([jnp.where(m0, qp, 0.0), jnp.where(m1, qp, 0.0)], axis=0).astype(bf16)
            s = _dot_nt(qs, kd) * ATTN_SCALE
            s = jnp.where(visible, s, NEG_BIG)
            sk = jnp.where(top, sink_ref[2 * pair], sink_ref[2 * pair + 1])
            mx = jnp.maximum(jnp.max(s, axis=1, keepdims=True), sk)
            p = jnp.exp(s - mx)
            den = jnp.sum(p, axis=1, keepdims=True) + jnp.exp(sk - mx)
            p = (p * (1.0 / den)).astype(bf16)
            pcat = jnp.concatenate([p[0:blk], p[blk:]], axis=1)
            o_ref[:, pair * LANES:(pair + 1) * LANES] = _dot(pcat, vcat).astype(bf16)


def _attn_prompt(qkv, sink, n_batch, seq_len):
    nb = seq_len // WINDOW
    kw = N_KV_HEADS * LANES
    k_blk = D_MODEL // kw
    cur = lambda off: pl.BlockSpec((WINDOW, kw), lambda b, n: (b * nb + n, k_blk + off))
    prev = lambda off: pl.BlockSpec((WINDOW, kw), lambda b, n: (b * nb + jnp.maximum(n - 1, 0), k_blk + off))
    return pl.pallas_call(
        _attn_prompt_kernel,
        grid=(n_batch, nb),
        in_specs=[pl.BlockSpec((WINDOW, D_MODEL), lambda b, n: (b * nb + n, 0)),
                  prev(0), cur(0), prev(1), cur(1),
                  pl.BlockSpec(memory_space=pltpu.SMEM)],
        out_specs=pl.BlockSpec((WINDOW, D_MODEL), lambda b, n: (b * nb + n, 0)),
        out_shape=jax.ShapeDtypeStruct((n_batch * seq_len, D_MODEL), bf16),
        compiler_params=_cparams(2),
        name="swa_prompt",
    )(qkv, qkv, qkv, qkv, qkv, sink)


SEQ_PER_GROUP = 4


def _attn_cached_kernel(n_tok, groups, q_ref, kn_ref, vn_ref, kc_ref, vc_ref, sink_ref, o_ref):
    rows = SEQ_PER_GROUP * n_tok
    n_st = GQA_GROUP * rows
    m0, m1 = _head_masks(rows)
    m0c, m1c = _head_masks(WINDOW)
    srow = lax.broadcasted_iota(jnp.int32, (n_st, 1), 0)
    row_seq = (srow % rows) // n_tok
    row_tok = srow % n_tok
    key_c = lax.broadcasted_iota(jnp.int32, (n_st, WINDOW), 1)
    vis_c = key_c > row_tok
    key_n = lax.broadcasted_iota(jnp.int32, (n_st, rows), 1)
    vis_n = ((key_n // n_tok) == row_seq) & ((key_n % n_tok) <= row_tok)
    for gi in range(groups):
        rs = slice(gi * rows, (gi + 1) * rows)
        for cp in range(N_KV_HEADS // 2):
            kc_pair = [kc_ref[gi * SEQ_PER_GROUP + b, :, cp * LANES:(cp + 1) * LANES]
                       for b in range(SEQ_PER_GROUP)]
            vc_pair = [vc_ref[gi * SEQ_PER_GROUP + b, :, cp * LANES:(cp + 1) * LANES]
                       for b in range(SEQ_PER_GROUP)]
            kc_sw = [pltpu.roll(t, HEAD_DIM, 1) for t in kc_pair]
            vc_sw = [pltpu.roll(t, HEAD_DIM, 1) for t in vc_pair]
            for ce in range(2):
                c = 2 * cp + ce
                keep = m0c if ce == 0 else m1c
                sl = slice(c * LANES, (c + 1) * LANES)
                kn = kn_ref[rs, sl].astype(bf16)
                vn = vn_ref[rs, sl].astype(bf16)
                pieces = []
                for jj in range(GQA_GROUP // 2):
                    pair = c * (GQA_GROUP // 2) + jj
                    qp = q_ref[rs, pair * LANES:(pair + 1) * LANES]
                    pieces += [jnp.where(m0, qp, 0.0), jnp.where(m1, qp, 0.0)]
                qs = jnp.concatenate(pieces, axis=0).astype(bf16)
                s_c = jnp.zeros((n_st, WINDOW), f32)
                for b in range(SEQ_PER_GROUP):
                    kx = jnp.where(keep, kc_pair[b], kc_sw[b]).astype(bf16)
                    s_c = jnp.where(row_seq == b, _dot_nt(qs, kx), s_c)
                s_c = jnp.where(vis_c, s_c * ATTN_SCALE, NEG_BIG)
                s_n = jnp.where(vis_n, _dot_nt(qs, kn) * ATTN_SCALE, NEG_BIG)
                sk = sink_ref[c]
                sk = sk[:, 0:1]
                mx = jnp.maximum(jnp.maximum(jnp.max(s_c, axis=1, keepdims=True),
                                             jnp.max(s_n, axis=1, keepdims=True)), sk)
                p_c = jnp.exp(s_c - mx)
                p_n = jnp.exp(s_n - mx)
                den = (jnp.sum(p_c, axis=1, keepdims=True) + jnp.sum(p_n, axis=1, keepdims=True)
                       + jnp.exp(sk - mx))
                inv_den = 1.0 / den
                p_c = (p_c * inv_den).astype(bf16)
                p_n = (p_n * inv_den).astype(bf16)
                o = _dot(p_n, vn)
                for b in range(SEQ_PER_GROUP):
                    vx = jnp.where(keep, vc_pair[b], vc_sw[b]).astype(bf16)
                    o = o + jnp.where(row_seq == b, _dot(p_c, vx), 0.0)
                for jj in range(GQA_GROUP // 2):
                    pair = c * (GQA_GROUP // 2) + jj
                    o0 = o[(2 * jj) * rows:(2 * jj + 1) * rows]
                    o1 = o[(2 * jj + 1) * rows:(2 * jj + 2) * rows]
                    o_ref[rs, pair * LANES:(pair + 1) * LANES] = jnp.where(m0, o0, o1).astype(bf16)


def _attn_cached(qkv, k_cache, v_cache, sink, n_batch, n_tok):
    groups = 2
    seqs = SEQ_PER_GROUP * groups
    rows = seqs * n_tok
    kw = N_KV_HEADS * LANES
    k_blk = D_MODEL // kw
    n_st = GQA_GROUP * SEQ_PER_GROUP * n_tok
    head = (jnp.arange(N_KV_HEADS)[:, None] * GQA_GROUP
            + (jnp.arange(n_st)[None, :] // (SEQ_PER_GROUP * n_tok)))
    sink_tab = jnp.broadcast_to(sink[head][:, :, None], (N_KV_HEADS, n_st, LANES))
    cache_spec = pl.BlockSpec((seqs, WINDOW, N_KV_HEADS * HEAD_DIM), lambda i: (i, 0, 0))
    return pl.pallas_call(
        functools.partial(_attn_cached_kernel, n_tok, groups),
        grid=(n_batch // seqs,),
        in_specs=[pl.BlockSpec((rows, D_MODEL), lambda i: (i, 0)),
                  pl.BlockSpec((rows, kw), lambda i: (i, k_blk)),
                  pl.BlockSpec((rows, kw), lambda i: (i, k_blk + 1)),
                  cache_spec, cache_spec,
                  pl.BlockSpec((N_KV_HEADS, n_st, LANES), lambda i: (0, 0, 0))],
        out_specs=pl.BlockSpec((rows, D_MODEL), lambda i: (i, 0)),
        out_shape=jax.ShapeDtypeStruct((n_batch * n_tok, D_MODEL), bf16),
        compiler_params=_cparams(1),
        name="swa_cached",
    )(qkv, qkv, qkv, k_cache, v_cache, sink_tab)


def _undup(t):
    return t.reshape(t.shape[0], N_KV_HEADS, 2, HEAD_DIM)[:, :, 0, :]


def _trunk(x, mod_all, tok_major, n_batch, seq_len, pos0, wkv_in, shift_in, k_in, v_in, wts):
    to_seq_major = lambda t: t.reshape(seq_len, n_batch, -1).transpose(1, 0, 2).reshape(n_batch * seq_len, -1)
    to_tok_major = lambda t: t.reshape(n_batch, seq_len, -1).transpose(1, 0, 2).reshape(n_batch * seq_len, -1)
    mod = mod_all[0]
    ffn_w = dict(wts["ffn"])
    x, ffn_w[0, 0] = _ffn(x, mod, tok_major, seq_len, 0, wts["norm_g"][0, 0], ffn_w[0, 0])
    s0 = shift_in[0] if tok_major else jnp.zeros((n_batch, 1, D_MODEL), f32)
    *proj, h_tail = _rwkv_proj(x, mod, tok_major, seq_len, wts["norm_g"][0, 1], s0, wts["rw"])
    if tok_major:
        z, wkv_t = _scan_lanes(proj, wts["rw"], n_batch, seq_len, jnp.transpose(wkv_in[0], (1, 2, 3, 0)))
        wkv_out = jnp.transpose(wkv_t, (3, 0, 1, 2))
        shift_out = h_tail
    else:
        z, wkv_out = _scan(proj, wts["rw"], n_batch, seq_len, None, *wts["scan_cfg"])
        shift_out = h_tail.reshape(n_batch, -1, 8, D_MODEL)[:, -1, -1]
    x = _proj_res(z, wts["rw"]["wo"], jnp.zeros((D_MODEL,), f32), x, mod, tok_major, seq_len)
    x, ffn_w[0, 1] = _ffn(x, mod, tok_major, seq_len, 6, wts["norm_g"][0, 2], ffn_w[0, 1])
    mod = mod_all[1]
    x, ffn_w[1, 0] = _ffn(x, mod, tok_major, seq_len, 0, wts["norm_g"][1, 0], ffn_w[1, 0])
    positions = pos0 + jnp.arange(seq_len)
    if tok_major:
        positions = jnp.repeat(positions, n_batch)
    cos, sin = _rope_tables(positions)
    qkv = _qkv(x, mod, tok_major, seq_len, wts["norm_g"][1, 1], wts["sw_wqkv"], wts["sw_bqkv"], cos, sin)
    kw = N_KV_HEADS * LANES
    if k_in is None:
        att = _attn_prompt(qkv, wts["sw_sink"], n_batch, seq_len)
        tail = qkv.reshape(n_batch, seq_len, QKV_COLS)[:, -WINDOW:, D_MODEL:]
        k_new = _undup(tail[..., :kw].reshape(n_batch * WINDOW, kw)).reshape(n_batch, WINDOW, N_KV_HEADS, HEAD_DIM)
        v_new = _undup(tail[..., kw:].reshape(n_batch * WINDOW, kw)).reshape(n_batch, WINDOW, N_KV_HEADS, HEAD_DIM)
    else:
        win = k_in.shape[2]
        qkv = to_seq_major(qkv)
        att = _attn_cached(qkv, k_in[0].reshape(n_batch, win, N_KV_HEADS * HEAD_DIM),
                           v_in[0].reshape(n_batch, win, N_KV_HEADS * HEAD_DIM),
                           wts["sw_sink"], n_batch, seq_len)
        att = to_tok_major(att)
        k_tok = _undup(qkv[:, D_MODEL:D_MODEL + kw]).reshape(n_batch, seq_len, N_KV_HEADS, HEAD_DIM)
        v_tok = _undup(qkv[:, D_MODEL + kw:]).reshape(n_batch, seq_len, N_KV_HEADS, HEAD_DIM)
        k_new = jnp.concatenate([k_in[0], k_tok], axis=1)[:, -win:]
        v_new = jnp.concatenate([v_in[0], v_tok], axis=1)[:, -win:]
    x = _proj_res(att, wts["sw_wo"], wts["sw_bo"], x, mod, tok_major, seq_len)
    y, ffn_w[1, 1] = _ffn(x, mod, tok_major, seq_len, 6, wts["norm_g"][1, 2], ffn_w[1, 1], final_g=wts["final_g"])
    return (y, wkv_out[None], shift_out[None], k_new[None], v_new[None]), ffn_w


def _dup_heads(w):
    lead = w.shape[:-1]
    w4 = w.reshape(lead + (N_KV_HEADS, 1, HEAD_DIM))
    return jnp.broadcast_to(w4, lead + (N_KV_HEADS, 2, HEAD_DIM)).reshape(lead + (N_KV_HEADS * LANES,))


def _pad_cols(w):
    return jnp.pad(w, ((0, 0), (0, LORA_PAD - w.shape[1])))


def _pad_rows(w):
    return jnp.pad(w, ((0, LORA_PAD - w.shape[0]), (0, 0)))


def kernel(x_prompt, x_sample, state_rwkv_wkv, state_rwkv_shift, cache_swa_k, cache_swa_v, c_prompt, c_sample, norm_g, w_ada, b_ada, w_ffn_in, w_ffn_out, rw_mu, rw_wrkv, rw_w0, rw_w1, rw_w2, rw_a0, rw_a1, rw_a2, rw_g1, rw_g2, rw_kk, rw_ka, rw_rk, rw_gn_w, rw_gn_b, rw_wo, sw_wqkv, sw_bqkv, sw_sink, sw_wo, sw_bo, final_g):
    n_p, seq_p, _ = x_prompt.shape
    n_s, seq_s, _ = x_sample.shape
    nq = N_HEADS * HEAD_DIM
    nkv = N_KV_HEADS * HEAD_DIM
    row = lambda t: t.reshape(1, D_MODEL)
    rw = dict(
        mu=rw_mu[0], wrkv=rw_wrkv[0].astype(bf16),
        w1=_pad_cols(rw_w1[0]).astype(bf16), a1=_pad_cols(rw_a1[0]).astype(bf16), g1=rw_g1[0].astype(bf16),
        w2=_pad_rows(rw_w2[0]).astype(bf16), a2=_pad_rows(rw_a2[0]).astype(bf16), g2=rw_g2[0].astype(bf16),
        w0=row(rw_w0[0]), a0=row(rw_a0[0]), kk=row(rw_kk[0]), ka=row(rw_ka[0]), rk=row(rw_rk[0]),
        gn_w=row(rw_gn_w[0]), gn_b=row(rw_gn_b[0]), wo=rw_wo[0].astype(bf16))
    wq = sw_wqkv[0]
    bq = sw_bqkv[0]
    wts = dict(
        norm_g=norm_g, final_g=final_g,
        ffn={(l, s): (w_ffn_in, w_ffn_out, l, s) for l in range(2) for s in range(2)}, rw=rw,
        sw_wqkv=jnp.concatenate([wq[:, :nq], _dup_heads(wq[:, nq:nq + nkv]), _dup_heads(wq[:, nq + nkv:])],
                                axis=1).astype(bf16),
        sw_bqkv=jnp.concatenate([bq[:nq], _dup_heads(bq[nq:nq + nkv]), _dup_heads(bq[nq + nkv:])]),
        sw_sink=sw_sink[0], sw_wo=sw_wo[0].astype(bf16), sw_bo=sw_bo[0])

    n_c = n_p + n_s
    pad = (-n_c) % 8
    c_all = jnp.concatenate([c_sample, c_prompt, jnp.zeros((pad, D_MODEL), f32)], axis=0)
    mod = _ada(c_all, w_ada, b_ada)
    mod_s = [mod[l] for l in range(mod.shape[0])]
    mod_p = [mod[l, n_s:n_c].reshape(n_p, 1, N_MOD * D_MODEL) for l in range(mod.shape[0])]

    (y_s, s_wkv, s_shift, s_k, s_v), wts["ffn"] = _trunk(
        jnp.transpose(x_sample, (1, 0, 2)).reshape(n_s * seq_s, D_MODEL), mod_s, True, n_s, seq_s, PAST_LEN,
        state_rwkv_wkv, state_rwkv_shift, cache_swa_k, cache_swa_v, wts)
    wts["scan_cfg"] = ((1, 1, 1, 1, 1), 16)
    (y_p, p_wkv, p_shift, p_k, p_v), _ = _trunk(
        x_prompt.reshape(n_p * seq_p, D_MODEL), mod_p, False, n_p, seq_p, 0, None, None, None, None, wts)
    y_s = jnp.transpose(y_s.reshape(seq_s, n_s, D_MODEL), (1, 0, 2))
    return (y_p.reshape(n_p, seq_p, D_MODEL), y_s, p_wkv, p_shift, p_k, p_v, s_wkv, s_shift, s_k, s_v)
```

```python
import functools
import math

import jax
import jax.numpy as jnp
from jax import lax
from jax.experimental import pallas as pl
from jax.experimental.pallas import tpu as pltpu

f32 = jnp.float32
bf16 = jnp.bfloat16

D_MODEL = 2048
HEAD_DIM = 64
N_HEADS = D_MODEL // HEAD_DIM
N_KV_HEADS = 4
GQA_GROUP = N_HEADS // N_KV_HEADS
WINDOW = 128
ATTN_SCALE = HEAD_DIM ** -0.5
ROPE_THETA = 10000.0
D_FF = 5632
N_MOD = 9
RMS_EPS = 1e-6
GN_EPS = 64e-5
PAST_LEN = 8192
LANES = 128
MXU_COLS = 256
N_PAIRS = D_MODEL // LANES
LORA_PAD = 128
SLAB = 64
VMEM_LIMIT = 56 * 1024 * 1024
NEG_BIG = -1e30
ROW_TILE = 512


def _cparams(n_axes):
    return pltpu.CompilerParams(dimension_semantics=("arbitrary",) * n_axes,
                                vmem_limit_bytes=VMEM_LIMIT)


def _dot(a, b):
    return jnp.dot(a, b, preferred_element_type=f32)


def _dot_nt(a, b):
    return lax.dot_general(a, b, (((1,), (1,)), ((), ())), preferred_element_type=f32)


def _split(x, n):
    if x.dtype == bf16:
        return [x]
    parts = []
    rem = x
    for i in range(n):
        p = rem.astype(bf16)
        parts.append(p)
        if i + 1 < n:
            rem = rem - p.astype(f32)
    return parts


def _mm(a, b, pa=1, pb=1, nt=False):
    a_parts = _split(a, pa)
    b_parts = _split(b, pb)
    order = max(len(a_parts), len(b_parts))
    acc = None
    for i, x in enumerate(a_parts):
        for j, y in enumerate(b_parts):
            if i + j >= order:
                continue
            t = _dot_nt(x, y) if nt else _dot(x, y)
            acc = t if acc is None else acc + t
    return acc


def _rms_mod(x, g, shift, scale):
    return x * lax.rsqrt(jnp.mean(x * x, axis=-1, keepdims=True) + RMS_EPS) * (g * (1.0 + scale)) + shift


def _mod_spec(tok_major, tm, seq_len, idx):
    if tok_major:
        return pl.BlockSpec((tm // seq_len, D_MODEL), lambda i, *_: (0, idx))
    return pl.BlockSpec((None, 1, D_MODEL), lambda i, *_: ((i * tm) // seq_len, 0, idx))


def _rows(v, tm):
    n = v.shape[0]
    return v if n in (1, tm) else jnp.concatenate([v] * (tm // n), axis=0)


def _ada_kernel(c_ref, w_ref, b_ref, o_ref):
    c = c_ref[...]
    s = (c * jax.nn.sigmoid(c)).astype(bf16)
    o_ref[...] = _dot(s, w_ref[...].astype(bf16)) + b_ref[...]


def _ada(c_all, w_ada, b_ada):
    n_layers, _, n_out = w_ada.shape
    rows = c_all.shape[0]
    tn = 1024
    return pl.pallas_call(
        _ada_kernel,
        grid=(n_layers, n_out // tn),
        in_specs=[pl.BlockSpec((rows, D_MODEL), lambda l, j: (0, 0)),
                  pl.BlockSpec((None, D_MODEL, tn), lambda l, j: (l, 0, j)),
                  pl.BlockSpec((None, 1, tn), lambda l, j: (l, 0, j))],
        out_specs=pl.BlockSpec((None, rows, tn), lambda l, j: (l, 0, j)),
        out_shape=jax.ShapeDtypeStruct((n_layers, rows, n_out), f32),
        compiler_params=_cparams(2),
        name="ada_mod",
    )(c_all, w_ada, b_ada.reshape(n_layers, 1, n_out))


def _ffn_kernel(final_norm, emit_bf16, x_ref, ng_ref, sh_ref, sc_ref, gt_ref, wg_ref, wu_ref, wo_ref, fg_ref,
                *rest):
    if emit_bf16:
        o_ref, wg_out, wu_out, wo_out, h_scr, acc_scr = rest
    else:
        o_ref, h_scr, acc_scr = rest
    f = pl.program_id(1)

    @pl.when(f == 0)
    def _():
        tm = x_ref.shape[0]
        h_scr[...] = _rms_mod(x_ref[...], ng_ref[...], _rows(sh_ref[...], tm), _rows(sc_ref[...], tm)).astype(bf16)
        acc_scr[...] = jnp.zeros_like(acc_scr)

    wg, wu, wo = wg_ref[...], wu_ref[...], wo_ref[...]
    if emit_bf16:
        wg, wu, wo = wg.astype(bf16), wu.astype(bf16), wo.astype(bf16)
        wg_out[...] = wg
        wu_out[...] = wu
        wo_out[...] = wo
    h = h_scr[...]
    gate = _dot(h, wg)
    up = _dot(h, wu)
    act = (gate * jax.nn.sigmoid(gate) * up).astype(bf16)
    acc_scr[...] += _dot(act, wo)

    @pl.when(f == pl.num_programs(1) - 1)
    def _():
        y = x_ref[...] + 0.5 * _rows(gt_ref[...], x_ref.shape[0]) * acc_scr[...]
        if final_norm:
            y = y * lax.rsqrt(jnp.mean(y * y, axis=-1, keepdims=True) + RMS_EPS) * fg_ref[...]
        o_ref[...] = y


def _ffn(x, mod, tok_major, seq_len, mod_base, norm_g, weights, final_g=None):
    m = x.shape[0]
    tm = min(ROW_TILE, m)
    emit = len(weights) == 4
    tf = 256 if emit else 512
    nf = D_FF // tf
    vec = pl.BlockSpec((1, D_MODEL), lambda i, j: (0, 0))
    fg = jnp.ones((1, D_MODEL), f32) if final_g is None else final_g.reshape(1, D_MODEL)
    w_in_spec = pl.BlockSpec((D_MODEL, tf), lambda i, j: (0, j))
    w_out_spec = pl.BlockSpec((tf, D_MODEL), lambda i, j: (j, 0))
    out_specs = [pl.BlockSpec((tm, D_MODEL), lambda i, j: (i, 0))]
    out_shape = [jax.ShapeDtypeStruct((m, D_MODEL), f32)]
    if emit:
        assert m == tm, "the weight copies are written once, by a single row tile"
        w_ffn_in, w_ffn_out, layer, slot = weights
        w_args = (w_ffn_in, w_ffn_in, w_ffn_out)
        w_specs = [pl.BlockSpec((None, None, D_MODEL, tf), lambda i, j: (layer, slot, 0, j)),
                   pl.BlockSpec((None, None, D_MODEL, tf), lambda i, j: (layer, slot, 0, j + nf)),
                   pl.BlockSpec((None, None, tf, D_MODEL), lambda i, j: (layer, slot, j, 0))]
        out_specs += [w_in_spec, w_in_spec, w_out_spec]
        out_shape += [jax.ShapeDtypeStruct((D_MODEL, D_FF), bf16), jax.ShapeDtypeStruct((D_MODEL, D_FF), bf16),
                      jax.ShapeDtypeStruct((D_FF, D_MODEL), bf16)]
    else:
        w_args = weights
        w_specs = [w_in_spec, w_in_spec, w_out_spec]
    outs = pl.pallas_call(
        functools.partial(_ffn_kernel, final_g is not None, emit),
        grid=(m // tm, nf),
        in_specs=[pl.BlockSpec((tm, D_MODEL), lambda i, j: (i, 0)),
                  vec,
                  _mod_spec(tok_major, tm, seq_len, mod_base),
                  _mod_spec(tok_major, tm, seq_len, mod_base + 1),
                  _mod_spec(tok_major, tm, seq_len, mod_base + 2)] + w_specs + [vec],
        out_specs=out_specs,
        out_shape=out_shape,
        scratch_shapes=[pltpu.VMEM((tm, D_MODEL), bf16), pltpu.VMEM((tm, D_MODEL), f32)],
        compiler_params=_cparams(2),
        name="ffn",
    )(x, norm_g.reshape(1, D_MODEL), mod, mod, mod, *w_args, fg)
    return (outs[0], tuple(outs[1:])) if emit else (outs[0], weights)


def _rwkv_proj_kernel(seq_len, tm, h_rows, tok_major,
                      x_ref, ng_ref, sh_ref, sc_ref, s0_ref, mu_ref, w1_ref, a1_ref, g1_ref,
                      w_ref, w2_ref, a2_ref, g2_ref, w0_ref, a0_ref,
                      r_ref, ld_ref, k_ref, v_ref, a_ref, g_ref, h_ref,
                      hs, xr, xk, xv, tw, ta, sg):
    i = pl.program_id(0)
    j = pl.program_id(1)

    @pl.when(j == 0)
    def _():
        h = _rms_mod(x_ref[...], ng_ref[...], _rows(sh_ref[...], tm), _rows(sc_ref[...], tm))
        h_ref[...] = h[tm - h_rows:tm, :]
        if tok_major:
            n_seq = s0_ref.shape[0]
            prev = jnp.concatenate([s0_ref[...], h[0:tm - n_seq, :]], axis=0)
        else:
            @pl.when(i == 0)
            def _():
                hs[0:8, :] = jnp.zeros((8, D_MODEL), f32)

            @pl.when(i > 0)
            def _():
                hs[0:8, :] = hs[tm:tm + 8, :]

            hs[8:tm + 8, :] = h
            row = i * tm + lax.broadcasted_iota(jnp.int32, (tm, 1), 0)
            prev = jnp.where(row % seq_len == 0, s0_ref[...], hs[7:tm + 7, :])
        xx = prev - h
        mu = mu_ref[...]
        xr[...] = (h + xx * mu[0:1, :]).astype(bf16)
        xk[...] = (h + xx * mu[2:3, :]).astype(bf16)
        xv[...] = (h + xx * mu[3:4, :]).astype(bf16)
        xw = (h + xx * mu[1:2, :]).astype(bf16)
        tw[...] = jnp.tanh(_dot(xw, w1_ref[...])).astype(bf16)
        xa = (h + xx * mu[4:5, :]).astype(bf16)
        ta[...] = _dot(xa, a1_ref[...]).astype(bf16)
        xg = (h + xx * mu[5:6, :]).astype(bf16)
        sg[...] = jax.nn.sigmoid(_dot(xg, g1_ref[...])).astype(bf16)

    r_ref[...] = _dot(xr[...], w_ref[0, j])
    k_ref[...] = _dot(xk[...], w_ref[1, j])
    v_ref[...] = _dot(xv[...], w_ref[2, j])
    z = w0_ref[...] + _dot(tw[...], w2_ref[...])
    ld_ref[...] = -jax.nn.sigmoid(z) * math.exp(-0.5)
    a_ref[...] = jax.nn.sigmoid(a0_ref[...] + _dot(ta[...], a2_ref[...]))
    g_ref[...] = _dot(sg[...], g2_ref[...])


def _rwkv_proj(x, mod, tok_major, seq_len, norm_g, s0, rwp):
    m = x.shape[0]
    tm = min(ROW_TILE, m)
    tn = 256
    h_rows = tm // seq_len if tok_major else 8
    full = lambda shape: pl.BlockSpec(shape, lambda i, j: (0,) * len(shape))
    col = lambda rows: pl.BlockSpec((rows, tn), lambda i, j: (0, j))
    w_res = pl.BlockSpec((3, D_MODEL // tn, D_MODEL, tn), lambda i, j: (0, 0, 0, 0), pipeline_mode=pl.Buffered(1))
    if tok_major:
        assert m == tm
        s0_spec = pl.BlockSpec((tm // seq_len, D_MODEL), lambda i, j: (0, 0))
    else:
        s0_spec = pl.BlockSpec((None, 1, D_MODEL), lambda i, j: ((i * tm) // seq_len, 0, 0))
    out_spec = pl.BlockSpec((tm, tn), lambda i, j: (i, j))
    out_sds = jax.ShapeDtypeStruct((m, D_MODEL), f32)
    outs = pl.pallas_call(
        functools.partial(_rwkv_proj_kernel, seq_len, tm, h_rows, tok_major),
        grid=(m // tm, D_MODEL // tn),
        in_specs=[pl.BlockSpec((tm, D_MODEL), lambda i, j: (i, 0)),
                  full((1, D_MODEL)),
                  _mod_spec(tok_major, tm, seq_len, 3),
                  _mod_spec(tok_major, tm, seq_len, 4),
                  s0_spec,
                  full((6, D_MODEL)),
                  full((D_MODEL, LORA_PAD)), full((D_MODEL, LORA_PAD)), full((D_MODEL, 256)),
                  w_res,
                  col(LORA_PAD), col(LORA_PAD), col(256), col(1), col(1)],
        out_specs=[out_spec] * 6 + [pl.BlockSpec((h_rows, D_MODEL), lambda i, j: (i, 0))],
        out_shape=[out_sds] * 6 + [jax.ShapeDtypeStruct((m // tm * h_rows, D_MODEL), f32)],
        scratch_shapes=[pltpu.VMEM((tm + 8, D_MODEL), f32),
                        pltpu.VMEM((tm, D_MODEL), bf16), pltpu.VMEM((tm, D_MODEL), bf16),
                        pltpu.VMEM((tm, D_MODEL), bf16),
                        pltpu.VMEM((tm, LORA_PAD), bf16), pltpu.VMEM((tm, LORA_PAD), bf16),
                        pltpu.VMEM((tm, 256), bf16)],
        compiler_params=_cparams(2),
        name="rwkv_proj",
    )(x, norm_g.reshape(1, D_MODEL), mod, mod, s0, rwp["mu"], rwp["w1"], rwp["a1"], rwp["g1"],
      rwp["wrkv"].reshape(3, D_MODEL, D_MODEL // tn, tn).transpose(0, 2, 1, 3),
      rwp["w2"], rwp["a2"], rwp["g2"], rwp["w0"], rwp["a0"])
    return outs


def _scan_pairs(chunk, n_seq, passes, acts, params, states):
    px, ps, pv, pinv, pupd = passes
    n_p = len(acts)
    n_st = 2 * SLAB
    row = lax.broadcasted_iota(jnp.int32, (n_st, LANES), 0)
    lane = lax.broadcasted_iota(jnp.int32, (n_st, LANES), 1)
    own = (row // SLAB) == (lane // HEAD_DIM)
    blk = (row // chunk) == (lane // chunk)
    strict = blk & (lane < row)
    incl = blk & (lane <= row)
    eye = jnp.where(row == lane, 1.0, 0.0)
    twice = lambda t: jnp.concatenate([t, t], axis=0)
    st = lambda t: jnp.where(own, twice(t), 0.0)
    each = lambda fn, *lists: [fn(*args) for args in zip(*lists)]

    r, ld, k, v, a, g = [[act[i] for act in acts] for i in range(6)]
    kkp, kap, rkp, gnw, gnb = [[par[i] for par in params] for i in range(5)]

    r64 = lax.broadcasted_iota(jnp.int32, (SLAB, SLAB), 0)
    c64 = lax.broadcasted_iota(jnp.int32, (SLAB, SLAB), 1)
    same = (r64 // chunk) == (c64 // chunk)
    tri_ones = jnp.concatenate([jnp.where(same & (c64 <= r64), 1.0, 0.0),
                                jnp.where(same, 1.0, 0.0)], axis=0).astype(bf16)
    sums = each(lambda t: _mm(tri_ones, t, pb=3), ld)
    cs = each(lambda t: t[0:SLAB], sums)
    tot = each(lambda t: t[SLAB:n_st], sums)

    kk_raw = each(lambda t, p: st(t * p), k, kkp)
    kk = each(lambda t: t * (1.0 / jnp.maximum(jnp.sqrt(jnp.sum(t * t, axis=1, keepdims=True)), 1e-12)), kk_raw)
    k2 = each(lambda kt, at, p: st(kt * (1.0 + (at - 1.0) * p)), k, a, kap)
    b = each(lambda t, at: t * twice(at), kk, a)
    r_s = each(st, r)
    v_s = each(st, v)
    bonus = each(lambda rt, kt, p, vt: jnp.sum(rt * kt * p, axis=1, keepdims=True) * vt, r_s, k2, rkp, v_s)

    e_neg = each(lambda c_: twice(jnp.exp(-c_)), cs)
    e_tail = each(lambda t_, c_: twice(jnp.exp(t_ - c_)), tot, cs)
    a_t = each(lambda t, c_, l_: -t * twice(jnp.exp(c_ - l_)), kk, cs, ld)
    r_t = each(lambda t, c_: t * twice(jnp.exp(c_)), r_s, cs)
    k_t = each(lambda t, e: t * e, k2, e_neg)
    b_t = each(lambda t, e: t * e, b, e_neg)
    k_h = each(lambda t, e: t * e, k2, e_tail)
    b_h = each(lambda t, e: t * e, b, e_tail)

    lhs1 = each(lambda x_, y_: jnp.concatenate([x_, y_], axis=0), a_t, r_t)
    rhs1 = each(lambda x_, y_: jnp.concatenate([x_, y_], axis=0), k_t, b_t)
    x = each(lambda l_, r_: _mm(l_, r_, pa=px, pb=px, nt=True), lhs1, rhs1)
    a_k = each(lambda t: jnp.where(strict, t[0:n_st, 0:n_st], 0.0), x)
    a_b = each(lambda t: jnp.where(strict, t[0:n_st, n_st:], 0.0), x)
    r_kb = each(lambda t: jnp.concatenate([jnp.where(incl, t[n_st:, 0:n_st], 0.0),
                                           jnp.where(incl, t[n_st:, n_st:], 0.0)], axis=1), x)

    if n_seq == 1:
        p1 = each(lambda l_, s_: _mm(l_, s_[0], pa=ps, pb=ps, nt=True), lhs1, states)
    else:
        row_seq = (lax.broadcasted_iota(jnp.int32, (2 * n_st, 1), 0) % SLAB) // chunk
        p1 = [jnp.zeros((2 * n_st, LANES), f32)] * n_p
        for q in range(n_seq):
            p1 = each(lambda acc, l_, s_: acc + jnp.where(row_seq == q, _mm(l_, s_[q], pa=ps, pb=ps, nt=True), 0.0),
                      p1, lhs1, states)
    rhs_u = each(lambda p_, ak, vt: p_[0:n_st] + _mm(ak, vt, pa=pv, pb=pv), p1, a_k, v_s)

    near = (row // 2) == (lane // 2)
    t_inv = each(lambda t: eye + jnp.where(near, t, 0.0), a_b)
    s = 2
    while s < chunk:
        off = ((row // (2 * s)) == (lane // (2 * s))) & ((row // s) != (lane // s))
        w = each(lambda ab, ti: _mm(jnp.where(off, ab, 0.0), ti, pa=pinv, pb=pinv), a_b, t_inv)
        t_inv = each(lambda ti, w_: ti + _mm(ti, w_, pa=pinv, pb=pinv), t_inv, w)
        s *= 2
    u = each(lambda ti, t: _mm(ti, t, pa=pinv, pb=pinv), t_inv, rhs_u)

    vu = each(lambda x_, y_: jnp.concatenate([x_, y_], axis=0), v_s, u)
    y = each(lambda p_, rk_, vu_: p_[n_st:] + _mm(rk_, vu_, pa=pv, pb=pv), p1, r_kb, vu)

    def finish(y_, w_, b_, bonus_, g_):
        mean = jnp.sum(y_, axis=1, keepdims=True) * (1.0 / HEAD_DIM)
        dev = jnp.where(own, y_ - mean, 0.0)
        var = jnp.sum(dev * dev, axis=1, keepdims=True) * (1.0 / HEAD_DIM)
        out = jnp.where(own, dev * lax.rsqrt(var + GN_EPS) * w_ + b_, 0.0) + bonus_
        return (out[0:SLAB] + out[SLAB:n_st]) * g_
    z = each(finish, y, gnw, gnb, bonus, g)

    vu_t = each(lambda t: t.T, vu)
    kb_h = each(lambda x_, y_: jnp.concatenate([x_, y_], axis=0), k_h, b_h)
    p_tot = each(jnp.exp, tot)
    if n_seq == 1:
        new_states = each(lambda s_, pt, vt, kb: [s_[0] * pt[0:1, :] + _mm(vt, kb, pa=pupd, pb=pupd)],
                          states, p_tot, vu_t, kb_h)
    else:
        col_seq = (lax.broadcasted_iota(jnp.int32, (1, 2 * n_st), 1) % SLAB) // chunk
        new_states = [[] for _ in range(n_p)]
        for q in range(n_seq):
            upd = each(lambda vt, kb: _mm(jnp.where(col_seq == q, vt, 0.0), kb, pa=pupd, pb=pupd), vu_t, kb_h)
            for p in range(n_p):
                new_states[p].append(states[p][q] * p_tot[p][q * chunk:q * chunk + 1, :] + upd[p])
    return z, new_states


def _scan_kernel(n_seq, n_pp, passes, r_ref, ld_ref, k_ref, v_ref, a_ref, g_ref, kk_ref, ka_ref, rk_ref, gw_ref,
                 gb_ref, z_ref, s_out_ref, s_scr):
    c = pl.program_id(1)

    @pl.when(c == 0)
    def _():
        s_scr[...] = jnp.zeros_like(s_scr)

    chains = [(b, pp) for b in range(n_seq) for pp in range(n_pp)]
    lanes = [slice(pp * LANES, (pp + 1) * LANES) for _, pp in chains]
    acts = [tuple(ref[b, :, sl] for ref in (r_ref, ld_ref, k_ref, v_ref, a_ref, g_ref))
            for (b, _), sl in zip(chains, lanes)]
    params = [tuple(ref[:, sl] for ref in (kk_ref, ka_ref, rk_ref, gw_ref, gb_ref)) for sl in lanes]
    states = [[s_scr[n]] for n in range(len(chains))]
    z, new_states = _scan_pairs(SLAB, 1, passes, acts, params, states)
    for n, (b, _) in enumerate(chains):
        z_ref[b, :, lanes[n]] = z[n].astype(bf16)
        s_scr[n] = new_states[n][0]

    @pl.when(c == pl.num_programs(1) - 1)
    def _():
        for n, (b, pp) in enumerate(chains):
            s = s_scr[n]
            s_out_ref[b, 2 * pp] = s[0:HEAD_DIM, 0:HEAD_DIM]
            s_out_ref[b, 2 * pp + 1] = s[HEAD_DIM:, HEAD_DIM:]


def _scan(proj, rwp, n_batch, seq_len, passes, n_pp):
    w = LANES * n_pp
    act = pl.BlockSpec((n_batch, SLAB, w), lambda p, c: (0, c, p))
    vec = pl.BlockSpec((1, w), lambda p, c: (0, p))
    st_spec = pl.BlockSpec((n_batch, 2 * n_pp, HEAD_DIM, HEAD_DIM), lambda p, c: (0, p, 0, 0))
    z, s_out = pl.pallas_call(
        functools.partial(_scan_kernel, n_batch, n_pp, passes),
        grid=(N_PAIRS // n_pp, seq_len // SLAB),
        in_specs=[act] * 6 + [vec] * 5,
        out_specs=[act, st_spec],
        out_shape=[jax.ShapeDtypeStruct((n_batch, seq_len, D_MODEL), bf16),
                   jax.ShapeDtypeStruct((n_batch, N_HEADS, HEAD_DIM, HEAD_DIM), f32)],
        scratch_shapes=[pltpu.VMEM((n_batch * n_pp, 2 * SLAB, LANES), f32)],
        compiler_params=_cparams(2),
        name="rwkv_scan",
    )(*[p.reshape(n_batch, seq_len, D_MODEL) for p in proj],
      rwp["kk"], rwp["ka"], rwp["rk"], rwp["gn_w"], rwp["gn_b"])
    return z.reshape(n_batch * seq_len, D_MODEL), s_out


def _scan_lanes_kernel(n_tok, r_ref, ld_ref, k_ref, v_ref, a_ref, g_ref, kk_ref, ka_ref, rk_ref, gw_ref, gb_ref,
                       s_in_ref, z_ref, s_out_ref, kk_s, w_s, b_s, k2_s, r_s, v_s, y_s):
    hd = HEAD_DIM
    colsum = lambda t: jnp.sum(t, axis=0, keepdims=True)
    for t in range(n_tok):
        r_t, k_t, v_t, a_t = r_ref[t].T, k_ref[t].T, v_ref[t].T, a_ref[t].T
        kk_raw = k_t * kk_ref[...]
        k2 = k_t * (1.0 + (a_t - 1.0) * ka_ref[...])
        w = jnp.exp(ld_ref[t].T)
        for e in range(2):
            sl = slice(e * hd, (e + 1) * hd)
            kr = kk_raw[sl]
            kk = kr * (1.0 / jnp.maximum(jnp.sqrt(colsum(kr * kr)), 1e-12))
            kk_s[e, t] = kk
            b_s[e, t] = kk * a_t[sl]
            k2_s[e, t] = k2[sl]
            w_s[e, t] = w[sl]
            r_s[e, t] = r_t[sl]
            v_s[e, t] = v_t[sl]

    rows_per_iter = 2

    def per_rows(i, carry):
        chains = [(e, rows_per_iter * i + d) for e in range(2) for d in range(rows_per_iter)]
        s = [s_in_ref[e, vi] for e, vi in chains]
        for t in range(n_tok):
            for c, (e, vi) in enumerate(chains):
                s_kk = colsum(s[c] * kk_s[e, t])
                s[c] = s[c] * w_s[e, t] - s_kk * b_s[e, t] + v_s[e, t, pl.ds(vi, 1), :] * k2_s[e, t]
                y_s[e, t, pl.ds(vi, 1), :] = colsum(s[c] * r_s[e, t])
        for c, (e, vi) in enumerate(chains):
            s_out_ref[e, vi] = s[c]
        return carry
    lax.fori_loop(0, hd // rows_per_iter, per_rows, 0)

    for t in range(n_tok):
        outs = []
        for e in range(2):
            sl = slice(e * hd, (e + 1) * hd)
            y = y_s[e, t]
            dev = y - colsum(y) * (1.0 / hd)
            var = colsum(dev * dev) * (1.0 / hd)
            bonus = colsum(r_s[e, t] * k2_s[e, t] * rk_ref[sl, :]) * v_s[e, t]
            outs.append(dev * lax.rsqrt(var + GN_EPS) * gw_ref[sl, :] + gb_ref[sl, :] + bonus)
        z_ref[t] = (jnp.concatenate(outs, axis=0).T * g_ref[t]).astype(bf16)


def _scan_lanes(proj, rwp, n_batch, n_tok, state):
    acts = [p.reshape(n_tok, n_batch, D_MODEL) for p in proj]
    lane_bc = lambda v: jnp.broadcast_to(v.reshape(D_MODEL, 1), (D_MODEL, n_batch))
    params = [lane_bc(rwp[n]) for n in ("kk", "ka", "rk", "gn_w", "gn_b")]
    act = pl.BlockSpec((n_tok, n_batch, LANES), lambda p: (0, 0, p))
    par = pl.BlockSpec((LANES, n_batch), lambda p: (p, 0))
    st_spec = pl.BlockSpec((2, HEAD_DIM, HEAD_DIM, n_batch), lambda p: (p, 0, 0, 0))
    per_tok = pltpu.VMEM((2, n_tok, HEAD_DIM, n_batch), f32)
    z, s_out = pl.pallas_call(
        functools.partial(_scan_lanes_kernel, n_tok),
        grid=(N_PAIRS,),
        in_specs=[act] * 6 + [par] * 5 + [st_spec],
        out_specs=[act, st_spec],
        out_shape=[jax.ShapeDtypeStruct((n_tok, n_batch, D_MODEL), bf16),
                   jax.ShapeDtypeStruct(state.shape, f32)],
        scratch_shapes=[per_tok] * 7,
        compiler_params=_cparams(1),
        name="rwkv_decode_scan",
    )(*acts, *params, state)
    return z.reshape(n_tok * n_batch, D_MODEL), s_out


def _proj_res_kernel(z_ref, w_ref, b_ref, x_ref, gt_ref, o_ref):
    out = _dot(z_ref[...], w_ref[...]) + b_ref[...]
    o_ref[...] = x_ref[...] + _rows(gt_ref[...], x_ref.shape[0]) * out


def _proj_res(z, w, bias, x, mod, tok_major, seq_len):
    m = x.shape[0]
    tm = min(ROW_TILE, m)
    rows = pl.BlockSpec((tm, D_MODEL), lambda i: (i, 0))
    return pl.pallas_call(
        _proj_res_kernel,
        grid=(m // tm,),
        in_specs=[rows,
                  pl.BlockSpec((D_MODEL, D_MODEL), lambda i: (0, 0), pipeline_mode=pl.Buffered(1)),
                  pl.BlockSpec((1, D_MODEL), lambda i: (0, 0)),
                  rows,
                  _mod_spec(tok_major, tm, seq_len, 5)],
        out_specs=rows,
        out_shape=jax.ShapeDtypeStruct((m, D_MODEL), f32),
        compiler_params=_cparams(1),
        name="proj_res",
    )(z, w, bias.reshape(1, D_MODEL), x, mod)


QKV_COLS = D_MODEL + 2 * N_KV_HEADS * LANES


def _qkv_kernel(x_ref, ng_ref, sh_ref, sc_ref, w_ref, b_ref, cos_ref, sin_ref, o_ref, h_scr):
    @pl.when(pl.program_id(1) == 0)
    def _():
        tm = x_ref.shape[0]
        h_scr[...] = _rms_mod(x_ref[...], ng_ref[...], _rows(sh_ref[...], tm), _rows(sc_ref[...], tm)).astype(bf16)

    cos = cos_ref[...]
    sin = sin_ref[...]
    first = (lax.broadcasted_iota(jnp.int32, cos.shape, 1) % HEAD_DIM) < HEAD_DIM // 2
    h = h_scr[...]
    for c0 in range(0, o_ref.shape[1], MXU_COLS):
        acc = _dot(h, w_ref[:, c0:c0 + MXU_COLS]) + b_ref[:, c0:c0 + MXU_COLS]
        for c in range(0, MXU_COLS, LANES):
            xc = acc[:, c:c + LANES]
            rot = jnp.where(first, pltpu.roll(xc, LANES - HEAD_DIM // 2, 1),
                            pltpu.roll(xc, HEAD_DIM // 2, 1))
            o_ref[:, c0 + c:c0 + c + LANES] = xc * cos + rot * sin


def _qkv(x, mod, tok_major, seq_len, norm_g, w, b, cos, sin):
    m = x.shape[0]
    tm = min(ROW_TILE, m)
    tn = 512
    n_rope = (D_MODEL + N_KV_HEADS * LANES) // tn
    n_pos_blocks = cos.shape[0] // tm
    vec = pl.BlockSpec((1, D_MODEL), lambda i, j: (0, 0))
    tab = pl.BlockSpec((tm, LANES), lambda i, j: (i % n_pos_blocks, j // n_rope))
    return pl.pallas_call(
        _qkv_kernel,
        grid=(m // tm, QKV_COLS // tn),
        in_specs=[pl.BlockSpec((tm, D_MODEL), lambda i, j: (i, 0)),
                  vec,
                  _mod_spec(tok_major, tm, seq_len, 3),
                  _mod_spec(tok_major, tm, seq_len, 4),
                  pl.BlockSpec((D_MODEL, tn), lambda i, j: (0, j)),
                  pl.BlockSpec((1, tn), lambda i, j: (0, j)),
                  tab, tab],
        out_specs=pl.BlockSpec((tm, tn), lambda i, j: (i, j)),
        out_shape=jax.ShapeDtypeStruct((m, QKV_COLS), f32),
        scratch_shapes=[pltpu.VMEM((tm, D_MODEL), bf16)],
        compiler_params=_cparams(2),
        name="swa_qkv",
    )(x, norm_g.reshape(1, D_MODEL), mod, mod, w, b.reshape(1, QKV_COLS), cos, sin)


def _rope_tables(positions):
    half = HEAD_DIM // 2
    inv_freq = ROPE_THETA ** (-jnp.arange(half, dtype=f32) / half)
    ang = positions.astype(f32)[:, None] * inv_freq[None, :]
    cos = jnp.tile(jnp.cos(ang), (1, LANES // half))
    sin = jnp.sin(ang)
    sin = jnp.tile(jnp.concatenate([-sin, sin], axis=1), (1, LANES // HEAD_DIM))
    return (jnp.concatenate([cos, jnp.ones_like(cos)], axis=1),
            jnp.concatenate([sin, jnp.zeros_like(sin)], axis=1))


def _head_masks(rows):
    lane = lax.broadcasted_iota(jnp.int32, (rows, LANES), 1)
    return lane < HEAD_DIM, lane >= HEAD_DIM


def _attn_prompt_kernel(q_ref, kp_ref, kc_ref, vp_ref, vc_ref, sink_ref, o_ref):
    n = pl.program_id(1)
    blk = WINDOW
    m0, m1 = _head_masks(blk)
    m0k, m1k = _head_masks(2 * blk)
    qi = lax.broadcasted_iota(jnp.int32, (2 * blk, 2 * blk), 0) % blk
    sj = lax.broadcasted_iota(jnp.int32, (2 * blk, 2 * blk), 1)
    visible = (sj > qi) & (sj <= qi + blk) & ((n > 0) | (sj >= blk))
    top = lax.broadcasted_iota(jnp.int32, (2 * blk, 1), 0) < blk
    for c in range(N_KV_HEADS):
        sl = slice(c * LANES, (c + 1) * LANES)
        kd = jnp.concatenate([kp_ref[:, sl], kc_ref[:, sl]], axis=0).astype(bf16)
        vd = jnp.concatenate([vp_ref[:, sl], vc_ref[:, sl]], axis=0)
        vcat = jnp.concatenate([jnp.where(m0k, vd, 0.0), jnp.where(m1k, vd, 0.0)], axis=0).astype(bf16)
        for jj in range(GQA_GROUP // 2):
            pair = c * (GQA_GROUP // 2) + jj
            qp = q_ref[:, pair * LANES:(pair + 1) * LANES]
            qp = qp * ATTN_SCALE
            qs = jnp.concatenate([jnp.where(m0, qp, 0.0), jnp.where(m1, qp, 0.0)], axis=0).astype(bf16)
            s = _dot_nt(qs, kd)
            s = jnp.where(visible, s, NEG_BIG)
            sk = jnp.where(top, sink_ref[2 * pair], sink_ref[2 * pair + 1])
            mx = jnp.maximum(jnp.max(s, axis=1, keepdims=True), sk)
            p = jnp.exp(s - mx)
            den = jnp.sum(p, axis=1, keepdims=True) + jnp.exp(sk - mx)
            p = (p * (1.0 / den)).astype(bf16)
            pcat = jnp.concatenate([p[0:blk], p[blk:]], axis=1)
            o_ref[:, pair * LANES:(pair + 1) * LANES] = _dot(pcat, vcat).astype(bf16)


def _attn_prompt(qkv, sink, n_batch, seq_len):
    nb = seq_len // WINDOW
    kw = N_KV_HEADS * LANES
    k_blk = D_MODEL // kw
    cur = lambda off: pl.BlockSpec((WINDOW, kw), lambda b, n: (b * nb + n, k_blk + off))
    prev = lambda off: pl.BlockSpec((WINDOW, kw), lambda b, n: (b * nb + jnp.maximum(n - 1, 0), k_blk + off))
    return pl.pallas_call(
        _attn_prompt_kernel,
        grid=(n_batch, nb),
        in_specs=[pl.BlockSpec((WINDOW, D_MODEL), lambda b, n: (b * nb + n, 0)),
                  prev(0), cur(0), prev(1), cur(1),
                  pl.BlockSpec(memory_space=pltpu.SMEM)],
        out_specs=pl.BlockSpec((WINDOW, D_MODEL), lambda b, n: (b * nb + n, 0)),
        out_shape=jax.ShapeDtypeStruct((n_batch * seq_len, D_MODEL), bf16),
        compiler_params=_cparams(2),
        name="swa_prompt",
    )(qkv, qkv, qkv, qkv, qkv, sink)


SEQ_PER_GROUP = 4


def _attn_cached_kernel(n_tok, groups, q_ref, kn_ref, vn_ref, kc_ref, vc_ref, sink_ref, o_ref):
    rows = SEQ_PER_GROUP * n_tok
    n_st = GQA_GROUP * rows
    m0, m1 = _head_masks(rows)
    m0c, m1c = _head_masks(WINDOW)
    srow = lax.broadcasted_iota(jnp.int32, (n_st, 1), 0)
    row_seq = (srow % rows) // n_tok
    row_tok = srow % n_tok
    key_c = lax.broadcasted_iota(jnp.int32, (n_st, WINDOW), 1)
    vis_c = key_c > row_tok
    key_n = lax.broadcasted_iota(jnp.int32, (n_st, rows), 1)
    vis_n = ((key_n // n_tok) == row_seq) & ((key_n % n_tok) <= row_tok)
    for gi in range(groups):
        rs = slice(gi * rows, (gi + 1) * rows)
        for cp in range(N_KV_HEADS // 2):
            kc_pair = [kc_ref[gi * SEQ_PER_GROUP + b, :, cp * LANES:(cp + 1) * LANES]
                       for b in range(SEQ_PER_GROUP)]
            vc_pair = [vc_ref[gi * SEQ_PER_GROUP + b, :, cp * LANES:(cp + 1) * LANES]
                       for b in range(SEQ_PER_GROUP)]
            kc_sw = [pltpu.roll(t, HEAD_DIM, 1) for t in kc_pair]
            vc_sw = [pltpu.roll(t, HEAD_DIM, 1) for t in vc_pair]
            for ce in range(2):
                c = 2 * cp + ce
                keep = m0c if ce == 0 else m1c
                sl = slice(c * LANES, (c + 1) * LANES)
                kn = kn_ref[rs, sl].astype(bf16)
                vn = vn_ref[rs, sl].astype(bf16)
                pieces = []
                for jj in range(GQA_GROUP // 2):
                    pair = c * (GQA_GROUP // 2) + jj
                    qp = q_ref[rs, pair * LANES:(pair + 1) * LANES]
                    pieces += [jnp.where(m0, qp, 0.0), jnp.where(m1, qp, 0.0)]
                qs = jnp.concatenate(pieces, axis=0).astype(bf16)
                s_c = jnp.zeros((n_st, WINDOW), f32)
                for b in range(SEQ_PER_GROUP):
                    kx = jnp.where(keep, kc_pair[b], kc_sw[b]).astype(bf16)
                    s_c = jnp.where(row_seq == b, _dot_nt(qs, kx), s_c)
                s_c = jnp.where(vis_c, s_c * ATTN_SCALE, NEG_BIG)
                s_n = jnp.where(vis_n, _dot_nt(qs, kn) * ATTN_SCALE, NEG_BIG)
                sk = sink_ref[c]
                sk = sk[:, 0:1]
                mx = jnp.maximum(jnp.maximum(jnp.max(s_c, axis=1, keepdims=True),
                                             jnp.max(s_n, axis=1, keepdims=True)), sk)
                p_c = jnp.exp(s_c - mx)
                p_n = jnp.exp(s_n - mx)
                den = (jnp.sum(p_c, axis=1, keepdims=True) + jnp.sum(p_n, axis=1, keepdims=True)
                       + jnp.exp(sk - mx))
                inv_den = 1.0 / den
                p_c = (p_c * inv_den).astype(bf16)
                p_n = (p_n * inv_den).astype(bf16)
                o = _dot(p_n, vn)
                for b in range(SEQ_PER_GROUP):
                    vx = jnp.where(keep, vc_pair[b], vc_sw[b]).astype(bf16)
                    o = o + jnp.where(row_seq == b, _dot(p_c, vx), 0.0)
                for jj in range(GQA_GROUP // 2):
                    pair = c * (GQA_GROUP // 2) + jj
                    o0 = o[(2 * jj) * rows:(2 * jj + 1) * rows]
                    o1 = o[(2 * jj + 1) * rows:(2 * jj + 2) * rows]
                    o_ref[rs, pair * LANES:(pair + 1) * LANES] = jnp.where(m0, o0, o1).astype(bf16)


def _attn_cached(qkv, k_cache, v_cache, sink, n_batch, n_tok):
    groups = 2
    seqs = SEQ_PER_GROUP * groups
    rows = seqs * n_tok
    kw = N_KV_HEADS * LANES
    k_blk = D_MODEL // kw
    n_st = GQA_GROUP * SEQ_PER_GROUP * n_tok
    head = (jnp.arange(N_KV_HEADS)[:, None] * GQA_GROUP
            + (jnp.arange(n_st)[None, :] // (SEQ_PER_GROUP * n_tok)))
    sink_tab = jnp.broadcast_to(sink[head][:, :, None], (N_KV_HEADS, n_st, LANES))
    cache_spec = pl.BlockSpec((seqs, WINDOW, N_KV_HEADS * HEAD_DIM), lambda i: (i, 0, 0))
    return pl.pallas_call(
        functools.partial(_attn_cached_kernel, n_tok, groups),
        grid=(n_batch // seqs,),
        in_specs=[pl.BlockSpec((rows, D_MODEL), lambda i: (i, 0)),
                  pl.BlockSpec((rows, kw), lambda i: (i, k_blk)),
                  pl.BlockSpec((rows, kw), lambda i: (i, k_blk + 1)),
                  cache_spec, cache_spec,
                  pl.BlockSpec((N_KV_HEADS, n_st, LANES), lambda i: (0, 0, 0))],
        out_specs=pl.BlockSpec((rows, D_MODEL), lambda i: (i, 0)),
        out_shape=jax.ShapeDtypeStruct((n_batch * n_tok, D_MODEL), bf16),
        compiler_params=_cparams(1),
        name="swa_cached",
    )(qkv, qkv, qkv, k_cache, v_cache, sink_tab)


def _undup(t):
    return t.reshape(t.shape[0], N_KV_HEADS, 2, HEAD_DIM)[:, :, 0, :]


def _trunk(x, mod_all, tok_major, n_batch, seq_len, pos0, wkv_in, shift_in, k_in, v_in, wts):
    to_seq_major = lambda t: t.reshape(seq_len, n_batch, -1).transpose(1, 0, 2).reshape(n_batch * seq_len, -1)
    to_tok_major = lambda t: t.reshape(n_batch, seq_len, -1).transpose(1, 0, 2).reshape(n_batch * seq_len, -1)
    mod = mod_all[0]
    ffn_w = dict(wts["ffn"])
    x, ffn_w[0, 0] = _ffn(x, mod, tok_major, seq_len, 0, wts["norm_g"][0, 0], ffn_w[0, 0])
    s0 = shift_in[0] if tok_major else jnp.zeros((n_batch, 1, D_MODEL), f32)
    *proj, h_tail = _rwkv_proj(x, mod, tok_major, seq_len, wts["norm_g"][0, 1], s0, wts["rw"])
    if tok_major:
        z, wkv_t = _scan_lanes(proj, wts["rw"], n_batch, seq_len, jnp.transpose(wkv_in[0], (1, 2, 3, 0)))
        wkv_out = jnp.transpose(wkv_t, (3, 0, 1, 2))
        shift_out = h_tail
    else:
        z, wkv_out = _scan(proj, wts["rw"], n_batch, seq_len, *wts["scan_cfg"])
        shift_out = h_tail.reshape(n_batch, -1, 8, D_MODEL)[:, -1, -1]
    x = _proj_res(z, wts["rw"]["wo"], jnp.zeros((D_MODEL,), f32), x, mod, tok_major, seq_len)
    x, ffn_w[0, 1] = _ffn(x, mod, tok_major, seq_len, 6, wts["norm_g"][0, 2], ffn_w[0, 1])
    mod = mod_all[1]
    x, ffn_w[1, 0] = _ffn(x, mod, tok_major, seq_len, 0, wts["norm_g"][1, 0], ffn_w[1, 0])
    positions = pos0 + jnp.arange(seq_len)
    if tok_major:
        positions = jnp.repeat(positions, n_batch)
    cos, sin = _rope_tables(positions)
    qkv = _qkv(x, mod, tok_major, seq_len, wts["norm_g"][1, 1], wts["sw_wqkv"], wts["sw_bqkv"], cos, sin)
    kw = N_KV_HEADS * LANES
    if k_in is None:
        att = _attn_prompt(qkv, wts["sw_sink"], n_batch, seq_len)
        tail = qkv.reshape(n_batch, seq_len, QKV_COLS)[:, -WINDOW:, D_MODEL:]
        k_new = _undup(tail[..., :kw].reshape(n_batch * WINDOW, kw)).reshape(n_batch, WINDOW, N_KV_HEADS, HEAD_DIM)
        v_new = _undup(tail[..., kw:].reshape(n_batch * WINDOW, kw)).reshape(n_batch, WINDOW, N_KV_HEADS, HEAD_DIM)
    else:
        win = k_in.shape[2]
        qkv = to_seq_major(qkv)
        att = _attn_cached(qkv, k_in[0].reshape(n_batch, win, N_KV_HEADS * HEAD_DIM),
                           v_in[0].reshape(n_batch, win, N_KV_HEADS * HEAD_DIM),
                           wts["sw_sink"], n_batch, seq_len)
        att = to_tok_major(att)
        k_tok = _undup(qkv[:, D_MODEL:D_MODEL + kw]).reshape(n_batch, seq_len, N_KV_HEADS, HEAD_DIM)
        v_tok = _undup(qkv[:, D_MODEL + kw:]).reshape(n_batch, seq_len, N_KV_HEADS, HEAD_DIM)
        k_new = jnp.concatenate([k_in[0], k_tok], axis=1)[:, -win:]
        v_new = jnp.concatenate([v_in[0], v_tok], axis=1)[:, -win:]
    x = _proj_res(att, wts["sw_wo"], wts["sw_bo"], x, mod, tok_major, seq_len)
    y, ffn_w[1, 1] = _ffn(x, mod, tok_major, seq_len, 6, wts["norm_g"][1, 2], ffn_w[1, 1], final_g=wts["final_g"])
    return (y, wkv_out[None], shift_out[None], k_new[None], v_new[None]), ffn_w


def _dup_heads(w):
    lead = w.shape[:-1]
    w4 = w.reshape(lead + (N_KV_HEADS, 1, HEAD_DIM))
    return jnp.broadcast_to(w4, lead + (N_KV_HEADS, 2, HEAD_DIM)).reshape(lead + (N_KV_HEADS * LANES,))


def _pad_cols(w):
    return jnp.pad(w, ((0, 0), (0, LORA_PAD - w.shape[1])))


def _pad_rows(w):
    return jnp.pad(w, ((0, LORA_PAD - w.shape[0]), (0, 0)))


def kernel(x_prompt, x_sample, state_rwkv_wkv, state_rwkv_shift, cache_swa_k, cache_swa_v, c_prompt, c_sample, norm_g, w_ada, b_ada, w_ffn_in, w_ffn_out, rw_mu, rw_wrkv, rw_w0, rw_w1, rw_w2, rw_a0, rw_a1, rw_a2, rw_g1, rw_g2, rw_kk, rw_ka, rw_rk, rw_gn_w, rw_gn_b, rw_wo, sw_wqkv, sw_bqkv, sw_sink, sw_wo, sw_bo, final_g):
    n_p, seq_p, _ = x_prompt.shape
    n_s, seq_s, _ = x_sample.shape
    nq = N_HEADS * HEAD_DIM
    nkv = N_KV_HEADS * HEAD_DIM
    row = lambda t: t.reshape(1, D_MODEL)
    rw = dict(
        mu=rw_mu[0], wrkv=rw_wrkv[0].astype(bf16),
        w1=_pad_cols(rw_w1[0]).astype(bf16), a1=_pad_cols(rw_a1[0]).astype(bf16), g1=rw_g1[0].astype(bf16),
        w2=_pad_rows(rw_w2[0]).astype(bf16), a2=_pad_rows(rw_a2[0]).astype(bf16), g2=rw_g2[0].astype(bf16),
        w0=row(rw_w0[0]), a0=row(rw_a0[0]), kk=row(rw_kk[0]), ka=row(rw_ka[0]), rk=row(rw_rk[0]),
        gn_w=row(rw_gn_w[0]), gn_b=row(rw_gn_b[0]), wo=rw_wo[0].astype(bf16))
    wq = sw_wqkv[0]
    bq = sw_bqkv[0]
    wts = dict(
        norm_g=norm_g, final_g=final_g,
        ffn={(l, s): (w_ffn_in, w_ffn_out, l, s) for l in range(2) for s in range(2)}, rw=rw,
        sw_wqkv=jnp.concatenate([wq[:, :nq], _dup_heads(wq[:, nq:nq + nkv]), _dup_heads(wq[:, nq + nkv:])],
                                axis=1).astype(bf16),
        sw_bqkv=jnp.concatenate([bq[:nq], _dup_heads(bq[nq:nq + nkv]), _dup_heads(bq[nq + nkv:])]),
        sw_sink=sw_sink[0], sw_wo=sw_wo[0].astype(bf16), sw_bo=sw_bo[0])

    n_c = n_p + n_s
    pad = (-n_c) % 8
    c_all = jnp.concatenate([c_sample, c_prompt, jnp.zeros((pad, D_MODEL), f32)], axis=0)
    mod = _ada(c_all, w_ada, b_ada)
    mod_s = [mod[l] for l in range(mod.shape[0])]
    mod_p = [mod[l, n_s:n_c].reshape(n_p, 1, N_MOD * D_MODEL) for l in range(mod.shape[0])]

    (y_s, s_wkv, s_shift, s_k, s_v), wts["ffn"] = _trunk(
        jnp.transpose(x_sample, (1, 0, 2)).reshape(n_s * seq_s, D_MODEL), mod_s, True, n_s, seq_s, PAST_LEN,
        state_rwkv_wkv, state_rwkv_shift, cache_swa_k, cache_swa_v, wts)
    wts["scan_cfg"] = ((1, 1, 1, 1, 1), 8)
    (y_p, p_wkv, p_shift, p_k, p_v), _ = _trunk(
        x_prompt.reshape(n_p * seq_p, D_MODEL), mod_p, False, n_p, seq_p, 0, None, None, None, None, wts)
    y_s = jnp.transpose(y_s.reshape(seq_s, n_s, D_MODEL), (1, 0, 2))
    return (y_p.reshape(n_p, seq_p, D_MODEL), y_s, p_wkv, p_shift, p_k, p_v, s_wkv, s_shift, s_k, s_v)
```

```python
import functools
import math

import jax
import jax.numpy as jnp
from jax import lax
from jax.experimental import pallas as pl
from jax.experimental.pallas import tpu as pltpu

f32 = jnp.float32
bf16 = jnp.bfloat16

D_MODEL = 2048
HEAD_DIM = 64
N_HEADS = D_MODEL // HEAD_DIM
N_KV_HEADS = 4
GQA_GROUP = N_HEADS // N_KV_HEADS
WINDOW = 128
ATTN_SCALE = HEAD_DIM ** -0.5
ROPE_THETA = 10000.0
D_FF = 5632
N_MOD = 9
RMS_EPS = 1e-6
GN_EPS = 64e-5
PAST_LEN = 8192
LANES = 128
MXU_COLS = 256
N_PAIRS = D_MODEL // LANES
LORA_PAD = 128
SLAB = 64
SCAN_CHAINS = 16
VMEM_LIMIT = 56 * 1024 * 1024
NEG_BIG = -1e30
ROW_TILE = 512


def _cparams(n_axes):
    return pltpu.CompilerParams(dimension_semantics=("arbitrary",) * n_axes,
                                vmem_limit_bytes=VMEM_LIMIT)


def _dot(a, b):
    return jnp.dot(a, b, preferred_element_type=f32)


def _dot_nt(a, b):
    return lax.dot_general(a, b, (((1,), (1,)), ((), ())), preferred_element_type=f32)


def _split(x, n):
    if x.dtype == bf16:
        return [x]
    parts = []
    rem = x
    for i in range(n):
        p = rem.astype(bf16)
        parts.append(p)
        if i + 1 < n:
            rem = rem - p.astype(f32)
    return parts


def _mm(a, b, pa=1, pb=1, nt=False):
    a_parts = _split(a, pa)
    b_parts = _split(b, pb)
    order = max(len(a_parts), len(b_parts))
    acc = None
    for i, x in enumerate(a_parts):
        for j, y in enumerate(b_parts):
            if i + j >= order:
                continue
            t = _dot_nt(x, y) if nt else _dot(x, y)
            acc = t if acc is None else acc + t
    return acc


def _rms_mod(x, g, shift, scale):
    return x * lax.rsqrt(jnp.mean(x * x, axis=-1, keepdims=True) + RMS_EPS) * (g * (1.0 + scale)) + shift


def _mod_spec(tok_major, tm, seq_len, idx):
    if tok_major:
        return pl.BlockSpec((tm // seq_len, D_MODEL), lambda i, *_: (0, idx))
    return pl.BlockSpec((None, 1, D_MODEL), lambda i, *_: ((i * tm) // seq_len, 0, idx))


def _rows(v, tm):
    n = v.shape[0]
    return v if n in (1, tm) else jnp.concatenate([v] * (tm // n), axis=0)


def _ada_kernel(c_ref, w_ref, b_ref, o_ref):
    c = c_ref[...]
    s = (c * jax.nn.sigmoid(c)).astype(bf16)
    o_ref[...] = _dot(s, w_ref[...].astype(bf16)) + b_ref[...]


def _ada(c_all, w_ada, b_ada):
    n_layers, _, n_out = w_ada.shape
    rows = c_all.shape[0]
    tn = 1024
    return pl.pallas_call(
        _ada_kernel,
        grid=(n_layers, n_out // tn),
        in_specs=[pl.BlockSpec((rows, D_MODEL), lambda l, j: (0, 0)),
                  pl.BlockSpec((None, D_MODEL, tn), lambda l, j: (l, 0, j)),
                  pl.BlockSpec((None, 1, tn), lambda l, j: (l, 0, j))],
        out_specs=pl.BlockSpec((None, rows, tn), lambda l, j: (l, 0, j)),
        out_shape=jax.ShapeDtypeStruct((n_layers, rows, n_out), f32),
        compiler_params=_cparams(2),
        name="ada_mod",
    )(c_all, w_ada, b_ada.reshape(n_layers, 1, n_out))


def _ffn_kernel(final_norm, emit_bf16, x_ref, ng_ref, sh_ref, sc_ref, gt_ref, wg_ref, wu_ref, wo_ref, fg_ref,
                *rest):
    if emit_bf16:
        o_ref, wg_out, wu_out, wo_out, h_scr, acc_scr = rest
    else:
        o_ref, h_scr, acc_scr = rest
    f = pl.program_id(1)

    @pl.when(f == 0)
    def _():
        tm = x_ref.shape[0]
        h_scr[...] = _rms_mod(x_ref[...], ng_ref[...], _rows(sh_ref[...], tm), _rows(sc_ref[...], tm)).astype(bf16)
        acc_scr[...] = jnp.zeros_like(acc_scr)

    wg, wu, wo = wg_ref[...], wu_ref[...], wo_ref[...]
    if emit_bf16:
        wg, wu, wo = wg.astype(bf16), wu.astype(bf16), wo.astype(bf16)
        wg_out[...] = wg
        wu_out[...] = wu
        wo_out[...] = wo
    h = h_scr[...]
    gate = _dot(h, wg)
    up = _dot(h, wu)
    act = (gate * jax.nn.sigmoid(gate) * up).astype(bf16)
    acc_scr[...] += _dot(act, wo)

    @pl.when(f == pl.num_programs(1) - 1)
    def _():
        y = x_ref[...] + 0.5 * _rows(gt_ref[...], x_ref.shape[0]) * acc_scr[...]
        if final_norm:
            y = y * lax.rsqrt(jnp.mean(y * y, axis=-1, keepdims=True) + RMS_EPS) * fg_ref[...]
        o_ref[...] = y


def _ffn(x, mod, tok_major, seq_len, mod_base, norm_g, weights, final_g=None):
    m = x.shape[0]
    tm = min(ROW_TILE, m)
    emit = len(weights) == 4
    tf = 256 if emit else 512
    nf = D_FF // tf
    vec = pl.BlockSpec((1, D_MODEL), lambda i, j: (0, 0))
    fg = jnp.ones((1, D_MODEL), f32) if final_g is None else final_g.reshape(1, D_MODEL)
    w_in_spec = pl.BlockSpec((D_MODEL, tf), lambda i, j: (0, j))
    w_out_spec = pl.BlockSpec((tf, D_MODEL), lambda i, j: (j, 0))
    out_specs = [pl.BlockSpec((tm, D_MODEL), lambda i, j: (i, 0))]
    out_shape = [jax.ShapeDtypeStruct((m, D_MODEL), f32)]
    if emit:
        assert m == tm, "the weight copies are written once, by a single row tile"
        w_ffn_in, w_ffn_out, layer, slot = weights
        w_args = (w_ffn_in, w_ffn_in, w_ffn_out)
        w_specs = [pl.BlockSpec((None, None, D_MODEL, tf), lambda i, j: (layer, slot, 0, j)),
                   pl.BlockSpec((None, None, D_MODEL, tf), lambda i, j: (layer, slot, 0, j + nf)),
                   pl.BlockSpec((None, None, tf, D_MODEL), lambda i, j: (layer, slot, j, 0))]
        out_specs += [w_in_spec, w_in_spec, w_out_spec]
        out_shape += [jax.ShapeDtypeStruct((D_MODEL, D_FF), bf16), jax.ShapeDtypeStruct((D_MODEL, D_FF), bf16),
                      jax.ShapeDtypeStruct((D_FF, D_MODEL), bf16)]
    else:
        w_args = weights
        w_specs = [w_in_spec, w_in_spec, w_out_spec]
    outs = pl.pallas_call(
        functools.partial(_ffn_kernel, final_g is not None, emit),
        grid=(m // tm, nf),
        in_specs=[pl.BlockSpec((tm, D_MODEL), lambda i, j: (i, 0)),
                  vec,
                  _mod_spec(tok_major, tm, seq_len, mod_base),
                  _mod_spec(tok_major, tm, seq_len, mod_base + 1),
                  _mod_spec(tok_major, tm, seq_len, mod_base + 2)] + w_specs + [vec],
        out_specs=out_specs,
        out_shape=out_shape,
        scratch_shapes=[pltpu.VMEM((tm, D_MODEL), bf16), pltpu.VMEM((tm, D_MODEL), f32)],
        compiler_params=_cparams(2),
        name="ffn",
    )(x, norm_g.reshape(1, D_MODEL), mod, mod, mod, *w_args, fg)
    return (outs[0], tuple(outs[1:])) if emit else (outs[0], weights)


def _rwkv_proj_kernel(seq_len, tm, h_rows, tok_major,
                      x_ref, ng_ref, sh_ref, sc_ref, s0_ref, mu_ref, w1_ref, a1_ref, g1_ref,
                      w_ref, w2_ref, a2_ref, g2_ref, w0_ref, a0_ref,
                      r_ref, ld_ref, k_ref, v_ref, a_ref, g_ref, h_ref,
                      hs, xr, xk, xv, tw, ta, sg):
    i = pl.program_id(0)
    j = pl.program_id(1)

    @pl.when(j == 0)
    def _():
        h = _rms_mod(x_ref[...], ng_ref[...], _rows(sh_ref[...], tm), _rows(sc_ref[...], tm))
        h_ref[...] = h[tm - h_rows:tm, :]
        if tok_major:
            n_seq = s0_ref.shape[0]
            prev = jnp.concatenate([s0_ref[...], h[0:tm - n_seq, :]], axis=0)
        else:
            @pl.when(i == 0)
            def _():
                hs[0:8, :] = jnp.zeros((8, D_MODEL), f32)

            @pl.when(i > 0)
            def _():
                hs[0:8, :] = hs[tm:tm + 8, :]

            hs[8:tm + 8, :] = h
            row = i * tm + lax.broadcasted_iota(jnp.int32, (tm, 1), 0)
            prev = jnp.where(row % seq_len == 0, s0_ref[...], hs[7:tm + 7, :])
        xx = prev - h
        mu = mu_ref[...]
        xr[...] = (h + xx * mu[0:1, :]).astype(bf16)
        xk[...] = (h + xx * mu[2:3, :]).astype(bf16)
        xv[...] = (h + xx * mu[3:4, :]).astype(bf16)
        xw = (h + xx * mu[1:2, :]).astype(bf16)
        tw[...] = jnp.tanh(_dot(xw, w1_ref[...])).astype(bf16)
        xa = (h + xx * mu[4:5, :]).astype(bf16)
        ta[...] = _dot(xa, a1_ref[...]).astype(bf16)
        xg = (h + xx * mu[5:6, :]).astype(bf16)
        sg[...] = jax.nn.sigmoid(_dot(xg, g1_ref[...])).astype(bf16)

    r_ref[...] = _dot(xr[...], w_ref[0, j])
    k_ref[...] = _dot(xk[...], w_ref[1, j])
    v_ref[...] = _dot(xv[...], w_ref[2, j])
    z = w0_ref[...] + _dot(tw[...], w2_ref[...])
    ld_ref[...] = -jax.nn.sigmoid(z) * math.exp(-0.5)
    a_ref[...] = jax.nn.sigmoid(a0_ref[...] + _dot(ta[...], a2_ref[...]))
    g_ref[...] = _dot(sg[...], g2_ref[...])


def _rwkv_proj(x, mod, tok_major, seq_len, norm_g, s0, rwp):
    m = x.shape[0]
    tm = min(ROW_TILE, m)
    tn = 256
    h_rows = tm // seq_len if tok_major else 8
    full = lambda shape: pl.BlockSpec(shape, lambda i, j: (0,) * len(shape))
    col = lambda rows: pl.BlockSpec((rows, tn), lambda i, j: (0, j))
    w_res = pl.BlockSpec((3, D_MODEL // tn, D_MODEL, tn), lambda i, j: (0, 0, 0, 0), pipeline_mode=pl.Buffered(1))
    if tok_major:
        assert m == tm
        s0_spec = pl.BlockSpec((tm // seq_len, D_MODEL), lambda i, j: (0, 0))
    else:
        s0_spec = pl.BlockSpec((None, 1, D_MODEL), lambda i, j: ((i * tm) // seq_len, 0, 0))
    out_spec = pl.BlockSpec((tm, tn), lambda i, j: (i, j))
    out_sds = jax.ShapeDtypeStruct((m, D_MODEL), f32)
    outs = pl.pallas_call(
        functools.partial(_rwkv_proj_kernel, seq_len, tm, h_rows, tok_major),
        grid=(m // tm, D_MODEL // tn),
        in_specs=[pl.BlockSpec((tm, D_MODEL), lambda i, j: (i, 0)),
                  full((1, D_MODEL)),
                  _mod_spec(tok_major, tm, seq_len, 3),
                  _mod_spec(tok_major, tm, seq_len, 4),
                  s0_spec,
                  full((6, D_MODEL)),
                  full((D_MODEL, LORA_PAD)), full((D_MODEL, LORA_PAD)), full((D_MODEL, 256)),
                  w_res,
                  col(LORA_PAD), col(LORA_PAD), col(256), col(1), col(1)],
        out_specs=[out_spec] * 6 + [pl.BlockSpec((h_rows, D_MODEL), lambda i, j: (i, 0))],
        out_shape=[out_sds] * 6 + [jax.ShapeDtypeStruct((m // tm * h_rows, D_MODEL), f32)],
        scratch_shapes=[pltpu.VMEM((tm + 8, D_MODEL), f32),
                        pltpu.VMEM((tm, D_MODEL), bf16), pltpu.VMEM((tm, D_MODEL), bf16),
                        pltpu.VMEM((tm, D_MODEL), bf16),
                        pltpu.VMEM((tm, LORA_PAD), bf16), pltpu.VMEM((tm, LORA_PAD), bf16),
                        pltpu.VMEM((tm, 256), bf16)],
        compiler_params=_cparams(2),
        name="rwkv_proj",
    )(x, norm_g.reshape(1, D_MODEL), mod, mod, s0, rwp["mu"], rwp["w1"], rwp["a1"], rwp["g1"],
      rwp["wrkv"].reshape(3, D_MODEL, D_MODEL // tn, tn).transpose(0, 2, 1, 3),
      rwp["w2"], rwp["a2"], rwp["g2"], rwp["w0"], rwp["a0"])
    return outs


def _scan_chains(acts, params, states):
    n_st = 2 * SLAB
    row = lax.broadcasted_iota(jnp.int32, (n_st, LANES), 0)
    lane = lax.broadcasted_iota(jnp.int32, (n_st, LANES), 1)
    own = (row // SLAB) == (lane // HEAD_DIM)
    blk = (row // SLAB) == (lane // SLAB)
    strict = blk & (lane < row)
    incl = blk & (lane <= row)
    eye = jnp.where(row == lane, 1.0, 0.0)
    twice = lambda t: jnp.concatenate([t, t], axis=0)
    st = lambda t: jnp.where(own, twice(t), 0.0)
    each = lambda fn, *lists: [fn(*args) for args in zip(*lists)]

    r, ld, k, v, a, g = [[act[i] for act in acts] for i in range(6)]
    kkp, kap, rkp, gnw, gnb = [[par[i] for par in params] for i in range(5)]

    r64 = lax.broadcasted_iota(jnp.int32, (SLAB, SLAB), 0)
    c64 = lax.broadcasted_iota(jnp.int32, (SLAB, SLAB), 1)
    tri_ones = jnp.concatenate([jnp.where(c64 <= r64, 1.0, 0.0),
                                jnp.ones((SLAB, SLAB), f32)], axis=0).astype(bf16)
    sums = each(lambda t: _mm(tri_ones, t, pb=3), ld)
    cs = each(lambda t: t[0:SLAB], sums)
    tot = each(lambda t: t[SLAB:n_st], sums)

    kk_raw = each(lambda t, p: st(t * p), k, kkp)
    kk = each(lambda t: t * (1.0 / jnp.maximum(jnp.sqrt(jnp.sum(t * t, axis=1, keepdims=True)), 1e-12)), kk_raw)
    k2 = each(lambda kt, at, p: st(kt * (1.0 + (at - 1.0) * p)), k, a, kap)
    b = each(lambda t, at: t * twice(at), kk, a)
    r_s = each(st, r)
    v_s = each(st, v)
    bonus = each(lambda rt, kt, p, vt: jnp.sum(rt * kt * p, axis=1, keepdims=True) * vt, r_s, k2, rkp, v_s)

    e_neg = each(lambda c_: twice(jnp.exp(-c_)), cs)
    e_tail = each(lambda t_, c_: twice(jnp.exp(t_ - c_)), tot, cs)
    a_t = each(lambda t, c_, l_: -t * twice(jnp.exp(c_ - l_)), kk, cs, ld)
    r_t = each(lambda t, c_: t * twice(jnp.exp(c_)), r_s, cs)
    k_t = each(lambda t, e: t * e, k2, e_neg)
    b_t = each(lambda t, e: t * e, b, e_neg)
    k_h = each(lambda t, e: t * e, k2, e_tail)
    b_h = each(lambda t, e: t * e, b, e_tail)

    lhs1 = each(lambda x_, y_: jnp.concatenate([x_, y_], axis=0), a_t, r_t)
    rhs1 = each(lambda x_, y_: jnp.concatenate([x_, y_], axis=0), k_t, b_t)
    x = each(lambda l_, r_: _mm(l_, r_, nt=True), lhs1, rhs1)
    a_k = each(lambda t: jnp.where(strict, t[0:n_st, 0:n_st], 0.0), x)
    a_b = each(lambda t: jnp.where(strict, t[0:n_st, n_st:], 0.0), x)
    r_kb = each(lambda t: jnp.concatenate([jnp.where(incl, t[n_st:, 0:n_st], 0.0),
                                           jnp.where(incl, t[n_st:, n_st:], 0.0)], axis=1), x)

    p1 = each(lambda l_, s_: _mm(l_, s_, nt=True), lhs1, states)
    rhs_u = each(lambda p_, ak, vt: p_[0:n_st] + _mm(ak, vt), p1, a_k, v_s)

    near = (row // 2) == (lane // 2)
    t_inv = each(lambda t: eye + jnp.where(near, t, 0.0), a_b)
    s = 2
    while s < SLAB:
        off = ((row // (2 * s)) == (lane // (2 * s))) & ((row // s) != (lane // s))
        w = each(lambda ab, ti: _mm(jnp.where(off, ab, 0.0), ti), a_b, t_inv)
        t_inv = each(lambda ti, w_: ti + _mm(ti, w_), t_inv, w)
        s *= 2
    u = each(_mm, t_inv, rhs_u)

    vu = each(lambda x_, y_: jnp.concatenate([x_, y_], axis=0), v_s, u)
    y = each(lambda p_, rk_, vu_: p_[n_st:] + _mm(rk_, vu_), p1, r_kb, vu)

    def finish(y_, w_, b_, bonus_, g_):
        mean = jnp.sum(y_, axis=1, keepdims=True) * (1.0 / HEAD_DIM)
        dev = jnp.where(own, y_ - mean, 0.0)
        var = jnp.sum(dev * dev, axis=1, keepdims=True) * (1.0 / HEAD_DIM)
        out = jnp.where(own, dev * lax.rsqrt(var + GN_EPS) * w_ + b_, 0.0) + bonus_
        return (out[0:SLAB] + out[SLAB:n_st]) * g_
    z = each(finish, y, gnw, gnb, bonus, g)

    vu_t = each(lambda t: t.T, vu)
    kb_h = each(lambda x_, y_: jnp.concatenate([x_, y_], axis=0), k_h, b_h)
    new_states = each(lambda s_, t_, vt, kb: s_ * jnp.exp(t_[0:1, :]) + _mm(vt, kb), states, tot, vu_t, kb_h)
    return z, new_states


def _scan_kernel(n_seq, n_pp, r_ref, ld_ref, k_ref, v_ref, a_ref, g_ref, kk_ref, ka_ref, rk_ref, gw_ref,
                 gb_ref, z_ref, s_out_ref, s_scr):
    c = pl.program_id(1)

    @pl.when(c == 0)
    def _():
        s_scr[...] = jnp.zeros_like(s_scr)

    chains = [(b, pp) for b in range(n_seq) for pp in range(n_pp)]
    lanes = [slice(pp * LANES, (pp + 1) * LANES) for _, pp in chains]
    acts = [tuple(ref[b, :, sl] for ref in (r_ref, ld_ref, k_ref, v_ref, a_ref, g_ref))
            for (b, _), sl in zip(chains, lanes)]
    params = [tuple(ref[:, sl] for ref in (kk_ref, ka_ref, rk_ref, gw_ref, gb_ref)) for sl in lanes]
    states = [s_scr[n] for n in range(len(chains))]
    z, new_states = _scan_chains(acts, params, states)
    for n, (b, _) in enumerate(chains):
        z_ref[b, :, lanes[n]] = z[n].astype(bf16)
        s_scr[n] = new_states[n]

    @pl.when(c == pl.num_programs(1) - 1)
    def _():
        for n, (b, pp) in enumerate(chains):
            s = s_scr[n]
            s_out_ref[b, 2 * pp] = s[0:HEAD_DIM, 0:HEAD_DIM]
            s_out_ref[b, 2 * pp + 1] = s[HEAD_DIM:, HEAD_DIM:]


def _scan(proj, rwp, n_batch, seq_len):
    n_pp = SCAN_CHAINS // n_batch
    w = LANES * n_pp
    act = pl.BlockSpec((n_batch, SLAB, w), lambda p, c: (0, c, p))
    vec = pl.BlockSpec((1, w), lambda p, c: (0, p))
    st_spec = pl.BlockSpec((n_batch, 2 * n_pp, HEAD_DIM, HEAD_DIM), lambda p, c: (0, p, 0, 0))
    z, s_out = pl.pallas_call(
        functools.partial(_scan_kernel, n_batch, n_pp),
        grid=(N_PAIRS // n_pp, seq_len // SLAB),
        in_specs=[act] * 6 + [vec] * 5,
        out_specs=[act, st_spec],
        out_shape=[jax.ShapeDtypeStruct((n_batch, seq_len, D_MODEL), bf16),
                   jax.ShapeDtypeStruct((n_batch, N_HEADS, HEAD_DIM, HEAD_DIM), f32)],
        scratch_shapes=[pltpu.VMEM((n_batch * n_pp, 2 * SLAB, LANES), f32)],
        compiler_params=_cparams(2),
        name="rwkv_scan",
    )(*[p.reshape(n_batch, seq_len, D_MODEL) for p in proj],
      rwp["kk"], rwp["ka"], rwp["rk"], rwp["gn_w"], rwp["gn_b"])
    return z.reshape(n_batch * seq_len, D_MODEL), s_out


def _scan_lanes_kernel(n_tok, r_ref, ld_ref, k_ref, v_ref, a_ref, g_ref, kk_ref, ka_ref, rk_ref, gw_ref, gb_ref,
                       s_in_ref, z_ref, s_out_ref, kk_s, w_s, b_s, k2_s, r_s, v_s, y_s):
    hd = HEAD_DIM
    colsum = lambda t: jnp.sum(t, axis=0, keepdims=True)
    for t in range(n_tok):
        r_t, k_t, v_t, a_t = r_ref[t].T, k_ref[t].T, v_ref[t].T, a_ref[t].T
        kk_raw = k_t * kk_ref[...]
        k2 = k_t * (1.0 + (a_t - 1.0) * ka_ref[...])
        w = jnp.exp(ld_ref[t].T)
        for e in range(2):
            sl = slice(e * hd, (e + 1) * hd)
            kr = kk_raw[sl]
            kk = kr * (1.0 / jnp.maximum(jnp.sqrt(colsum(kr * kr)), 1e-12))
            kk_s[e, t] = kk
            b_s[e, t] = kk * a_t[sl]
            k2_s[e, t] = k2[sl]
            w_s[e, t] = w[sl]
            r_s[e, t] = r_t[sl]
            v_s[e, t] = v_t[sl]

    rows_per_iter = 2

    def per_rows(i, carry):
        chains = [(e, rows_per_iter * i + d) for e in range(2) for d in range(rows_per_iter)]
        s = [s_in_ref[e, vi] for e, vi in chains]
        for t in range(n_tok):
            for c, (e, vi) in enumerate(chains):
                s_kk = colsum(s[c] * kk_s[e, t])
                s[c] = s[c] * w_s[e, t] - s_kk * b_s[e, t] + v_s[e, t, pl.ds(vi, 1), :] * k2_s[e, t]
                y_s[e, t, pl.ds(vi, 1), :] = colsum(s[c] * r_s[e, t])
        for c, (e, vi) in enumerate(chains):
            s_out_ref[e, vi] = s[c]
        return carry
    lax.fori_loop(0, hd // rows_per_iter, per_rows, 0)

    for t in range(n_tok):
        outs = []
        for e in range(2):
            sl = slice(e * hd, (e + 1) * hd)
            y = y_s[e, t]
            dev = y - colsum(y) * (1.0 / hd)
            var = colsum(dev * dev) * (1.0 / hd)
            bonus = colsum(r_s[e, t] * k2_s[e, t] * rk_ref[sl, :]) * v_s[e, t]
            outs.append(dev * lax.rsqrt(var + GN_EPS) * gw_ref[sl, :] + gb_ref[sl, :] + bonus)
        z_ref[t] = (jnp.concatenate(outs, axis=0).T * g_ref[t]).astype(bf16)


def _scan_lanes(proj, rwp, n_batch, n_tok, state):
    acts = [p.reshape(n_tok, n_batch, D_MODEL) for p in proj]
    lane_bc = lambda v: jnp.broadcast_to(v.reshape(D_MODEL, 1), (D_MODEL, n_batch))
    params = [lane_bc(rwp[n]) for n in ("kk", "ka", "rk", "gn_w", "gn_b")]
    act = pl.BlockSpec((n_tok, n_batch, LANES), lambda p: (0, 0, p))
    par = pl.BlockSpec((LANES, n_batch), lambda p: (p, 0))
    st_spec = pl.BlockSpec((2, HEAD_DIM, HEAD_DIM, n_batch), lambda p: (p, 0, 0, 0))
    per_tok = pltpu.VMEM((2, n_tok, HEAD_DIM, n_batch), f32)
    z, s_out = pl.pallas_call(
        functools.partial(_scan_lanes_kernel, n_tok),
        grid=(N_PAIRS,),
        in_specs=[act] * 6 + [par] * 5 + [st_spec],
        out_specs=[act, st_spec],
        out_shape=[jax.ShapeDtypeStruct((n_tok, n_batch, D_MODEL), bf16),
                   jax.ShapeDtypeStruct(state.shape, f32)],
        scratch_shapes=[per_tok] * 7,
        compiler_params=_cparams(1),
        name="rwkv_decode_scan",
    )(*acts, *params, state)
    return z.reshape(n_tok * n_batch, D_MODEL), s_out


def _proj_res_kernel(z_ref, w_ref, b_ref, x_ref, gt_ref, o_ref):
    out = _dot(z_ref[...], w_ref[...]) + b_ref[...]
    o_ref[...] = x_ref[...] + _rows(gt_ref[...], x_ref.shape[0]) * out


def _proj_res(z, w, bias, x, mod, tok_major, seq_len):
    m = x.shape[0]
    tm = min(ROW_TILE, m)
    rows = pl.BlockSpec((tm, D_MODEL), lambda i: (i, 0))
    return pl.pallas_call(
        _proj_res_kernel,
        grid=(m // tm,),
        in_specs=[rows,
                  pl.BlockSpec((D_MODEL, D_MODEL), lambda i: (0, 0), pipeline_mode=pl.Buffered(1)),
                  pl.BlockSpec((1, D_MODEL), lambda i: (0, 0)),
                  rows,
                  _mod_spec(tok_major, tm, seq_len, 5)],
        out_specs=rows,
        out_shape=jax.ShapeDtypeStruct((m, D_MODEL), f32),
        compiler_params=_cparams(1),
        name="proj_res",
    )(z, w, bias.reshape(1, D_MODEL), x, mod)


QKV_COLS = D_MODEL + 2 * N_KV_HEADS * LANES


def _qkv_kernel(x_ref, ng_ref, sh_ref, sc_ref, w_ref, b_ref, cos_ref, sin_ref, o_ref, h_scr):
    @pl.when(pl.program_id(1) == 0)
    def _():
        tm = x_ref.shape[0]
        h_scr[...] = _rms_mod(x_ref[...], ng_ref[...], _rows(sh_ref[...], tm), _rows(sc_ref[...], tm)).astype(bf16)

    cos = cos_ref[...]
    sin = sin_ref[...]
    first = (lax.broadcasted_iota(jnp.int32, cos.shape, 1) % HEAD_DIM) < HEAD_DIM // 2
    h = h_scr[...]
    for c0 in range(0, o_ref.shape[1], MXU_COLS):
        acc = _dot(h, w_ref[:, c0:c0 + MXU_COLS]) + b_ref[:, c0:c0 + MXU_COLS]
        for c in range(0, MXU_COLS, LANES):
            xc = acc[:, c:c + LANES]
            rot = jnp.where(first, pltpu.roll(xc, LANES - HEAD_DIM // 2, 1),
                            pltpu.roll(xc, HEAD_DIM // 2, 1))
            o_ref[:, c0 + c:c0 + c + LANES] = xc * cos + rot * sin


def _qkv(x, mod, tok_major, seq_len, norm_g, w, b, cos, sin):
    m = x.shape[0]
    tm = min(ROW_TILE, m)
    tn = 512
    n_rope = (D_MODEL + N_KV_HEADS * LANES) // tn
    n_pos_blocks = cos.shape[0] // tm
    vec = pl.BlockSpec((1, D_MODEL), lambda i, j: (0, 0))
    tab = pl.BlockSpec((tm, LANES), lambda i, j: (i % n_pos_blocks, j // n_rope))
    return pl.pallas_call(
        _qkv_kernel,
        grid=(m // tm, QKV_COLS // tn),
        in_specs=[pl.BlockSpec((tm, D_MODEL), lambda i, j: (i, 0)),
                  vec,
                  _mod_spec(tok_major, tm, seq_len, 3),
                  _mod_spec(tok_major, tm, seq_len, 4),
                  pl.BlockSpec((D_MODEL, tn), lambda i, j: (0, j)),
                  pl.BlockSpec((1, tn), lambda i, j: (0, j)),
                  tab, tab],
        out_specs=pl.BlockSpec((tm, tn), lambda i, j: (i, j)),
        out_shape=jax.ShapeDtypeStruct((m, QKV_COLS), f32),
        scratch_shapes=[pltpu.VMEM((tm, D_MODEL), bf16)],
        compiler_params=_cparams(2),
        name="swa_qkv",
    )(x, norm_g.reshape(1, D_MODEL), mod, mod, w, b.reshape(1, QKV_COLS), cos, sin)


def _rope_tables(positions):
    half = HEAD_DIM // 2
    inv_freq = ROPE_THETA ** (-jnp.arange(half, dtype=f32) / half)
    ang = positions.astype(f32)[:, None] * inv_freq[None, :]
    cos = jnp.tile(jnp.cos(ang), (1, LANES // half))
    sin = jnp.sin(ang)
    sin = jnp.tile(jnp.concatenate([-sin, sin], axis=1), (1, LANES // HEAD_DIM))
    return (jnp.concatenate([cos, jnp.ones_like(cos)], axis=1),
            jnp.concatenate([sin, jnp.zeros_like(sin)], axis=1))


def _head_masks(rows):
    lane = lax.broadcasted_iota(jnp.int32, (rows, LANES), 1)
    return lane < HEAD_DIM, lane >= HEAD_DIM


def _attn_prompt_kernel(q_ref, kp_ref, kc_ref, vp_ref, vc_ref, sink_ref, o_ref):
    n = pl.program_id(1)
    blk = WINDOW
    m0, m1 = _head_masks(blk)
    m0k, m1k = _head_masks(2 * blk)
    qi = lax.broadcasted_iota(jnp.int32, (2 * blk, 2 * blk), 0) % blk
    sj = lax.broadcasted_iota(jnp.int32, (2 * blk, 2 * blk), 1)
    visible = (sj > qi) & (sj <= qi + blk) & ((n > 0) | (sj >= blk))
    top = lax.broadcasted_iota(jnp.int32, (2 * blk, 1), 0) < blk
    for c in range(N_KV_HEADS):
        sl = slice(c * LANES, (c + 1) * LANES)
        kd = jnp.concatenate([kp_ref[:, sl], kc_ref[:, sl]], axis=0).astype(bf16)
        vd = jnp.concatenate([vp_ref[:, sl], vc_ref[:, sl]], axis=0)
        vcat = jnp.concatenate([jnp.where(m0k, vd, 0.0), jnp.where(m1k, vd, 0.0)], axis=0).astype(bf16)
        for jj in range(GQA_GROUP // 2):
            pair = c * (GQA_GROUP // 2) + jj
            qp = q_ref[:, pair * LANES:(pair + 1) * LANES]
            qp = qp * ATTN_SCALE
            qs = jnp.concatenate([jnp.where(m0, qp, 0.0), jnp.where(m1, qp, 0.0)], axis=0).astype(bf16)
            s = _dot_nt(qs, kd)
            s = jnp.where(visible, s, NEG_BIG)
            sk = jnp.where(top, sink_ref[2 * pair], sink_ref[2 * pair + 1])
            mx = jnp.maximum(jnp.max(s, axis=1, keepdims=True), sk)
            p = jnp.exp(s - mx)
            den = jnp.sum(p, axis=1, keepdims=True) + jnp.exp(sk - mx)
            p = (p * (1.0 / den)).astype(bf16)
            pcat = jnp.concatenate([p[0:blk], p[blk:]], axis=1)
            o_ref[:, pair * LANES:(pair + 1) * LANES] = _dot(pcat, vcat).astype(bf16)


def _attn_prompt(qkv, sink, n_batch, seq_len):
    nb = seq_len // WINDOW
    kw = N_KV_HEADS * LANES
    k_blk = D_MODEL // kw
    cur = lambda off: pl.BlockSpec((WINDOW, kw), lambda b, n: (b * nb + n, k_blk + off))
    prev = lambda off: pl.BlockSpec((WINDOW, kw), lambda b, n: (b * nb + jnp.maximum(n - 1, 0), k_blk + off))
    return pl.pallas_call(
        _attn_prompt_kernel,
        grid=(n_batch, nb),
        in_specs=[pl.BlockSpec((WINDOW, D_MODEL), lambda b, n: (b * nb + n, 0)),
                  prev(0), cur(0), prev(1), cur(1),
                  pl.BlockSpec(memory_space=pltpu.SMEM)],
        out_specs=pl.BlockSpec((WINDOW, D_MODEL), lambda b, n: (b * nb + n, 0)),
        out_shape=jax.ShapeDtypeStruct((n_batch * seq_len, D_MODEL), bf16),
        compiler_params=_cparams(2),
        name="swa_prompt",
    )(qkv, qkv, qkv, qkv, qkv, sink)


SEQ_PER_GROUP = 4


def _attn_cached_kernel(n_tok, groups, q_ref, kn_ref, vn_ref, kc_ref, vc_ref, sink_ref, o_ref):
    rows = SEQ_PER_GROUP * n_tok
    n_st = GQA_GROUP * rows
    m0, m1 = _head_masks(rows)
    srow = lax.broadcasted_iota(jnp.int32, (n_st, 1), 0)
    row_seq = (srow % rows) // n_tok
    row_tok = srow % n_tok
    key_c = lax.broadcasted_iota(jnp.int32, (n_st, WINDOW), 1)
    vis_c = key_c > row_tok
    key_n = lax.broadcasted_iota(jnp.int32, (n_st, rows), 1)
    vis_n = ((key_n // n_tok) == row_seq) & ((key_n % n_tok) <= row_tok)
    for gi in range(groups):
        rs = slice(gi * rows, (gi + 1) * rows)
        for cp in range(N_KV_HEADS // 2):
            for ce in range(2):
                c = 2 * cp + ce
                sl = slice(c * LANES, (c + 1) * LANES)
                kt = [kc_ref[gi * SEQ_PER_GROUP + b, c * HEAD_DIM:(c + 1) * HEAD_DIM, :] for b in range(SEQ_PER_GROUP)]
                vt = [vc_ref[gi * SEQ_PER_GROUP + b, c * HEAD_DIM:(c + 1) * HEAD_DIM, :] for b in range(SEQ_PER_GROUP)]
                kn = kn_ref[rs, sl].astype(bf16)
                vn = vn_ref[rs, sl].astype(bf16)
                pieces = []
                for jj in range(GQA_GROUP // 2):
                    pair = c * (GQA_GROUP // 2) + jj
                    qp = q_ref[rs, pair * LANES:(pair + 1) * LANES]
                    pieces += [jnp.where(m0, qp, 0.0), jnp.where(m1, qp, 0.0)]
                qs = jnp.concatenate(pieces, axis=0).astype(bf16)
                s_c = jnp.zeros((n_st, WINDOW), f32)
                for b in range(SEQ_PER_GROUP):
                    kx = jnp.concatenate([kt[b], kt[b]], axis=0).astype(bf16)
                    s_c = jnp.where(row_seq == b, _dot(qs, kx), s_c)
                s_c = jnp.where(vis_c, s_c * ATTN_SCALE, NEG_BIG)
                s_n = jnp.where(vis_n, _dot_nt(qs, kn) * ATTN_SCALE, NEG_BIG)
                sk = sink_ref[c]
                sk = sk[:, 0:1]
                mx = jnp.maximum(jnp.maximum(jnp.max(s_c, axis=1, keepdims=True),
                                             jnp.max(s_n, axis=1, keepdims=True)), sk)
                p_c = jnp.exp(s_c - mx)
                p_n = jnp.exp(s_n - mx)
                den = (jnp.sum(p_c, axis=1, keepdims=True) + jnp.sum(p_n, axis=1, keepdims=True)
                       + jnp.exp(sk - mx))
                inv_den = 1.0 / den
                p_c = (p_c * inv_den).astype(bf16)
                p_n = (p_n * inv_den).astype(bf16)
                o = _dot(p_n, vn)
                for b in range(SEQ_PER_GROUP):
                    vx = jnp.concatenate([vt[b], vt[b]], axis=0).astype(bf16)
                    o = o + jnp.where(row_seq == b, _dot_nt(p_c, vx), 0.0)
                for jj in range(GQA_GROUP // 2):
                    pair = c * (GQA_GROUP // 2) + jj
                    o0 = o[(2 * jj) * rows:(2 * jj + 1) * rows]
                    o1 = o[(2 * jj + 1) * rows:(2 * jj + 2) * rows]
                    o_ref[rs, pair * LANES:(pair + 1) * LANES] = jnp.where(m0, o0, o1).astype(bf16)


def _attn_cached(qkv, k_cache, v_cache, sink, n_batch, n_tok):
    groups = 2
    seqs = SEQ_PER_GROUP * groups
    rows = seqs * n_tok
    kw = N_KV_HEADS * LANES
    k_blk = D_MODEL // kw
    n_st = GQA_GROUP * SEQ_PER_GROUP * n_tok
    head = (jnp.arange(N_KV_HEADS)[:, None] * GQA_GROUP
            + (jnp.arange(n_st)[None, :] // (SEQ_PER_GROUP * n_tok)))
    sink_tab = jnp.broadcast_to(sink[head][:, :, None], (N_KV_HEADS, n_st, LANES))
    cache_spec = pl.BlockSpec((seqs, N_KV_HEADS * HEAD_DIM, WINDOW), lambda i: (i, 0, 0))
    return pl.pallas_call(
        functools.partial(_attn_cached_kernel, n_tok, groups),
        grid=(n_batch // seqs,),
        in_specs=[pl.BlockSpec((rows, D_MODEL), lambda i: (i, 0)),
                  pl.BlockSpec((rows, kw), lambda i: (i, k_blk)),
                  pl.BlockSpec((rows, kw), lambda i: (i, k_blk + 1)),
                  cache_spec, cache_spec,
                  pl.BlockSpec((N_KV_HEADS, n_st, LANES), lambda i: (0, 0, 0))],
        out_specs=pl.BlockSpec((rows, D_MODEL), lambda i: (i, 0)),
        out_shape=jax.ShapeDtypeStruct((n_batch * n_tok, D_MODEL), bf16),
        compiler_params=_cparams(1),
        name="swa_cached",
    )(qkv, qkv, qkv, k_cache, v_cache, sink_tab)


def _undup(t):
    return t.reshape(t.shape[0], N_KV_HEADS, 2, HEAD_DIM)[:, :, 0, :]


def _trunk(x, mod_all, tok_major, n_batch, seq_len, pos0, wkv_in, shift_in, k_in, v_in, wts):
    to_seq_major = lambda t: t.reshape(seq_len, n_batch, -1).transpose(1, 0, 2).reshape(n_batch * seq_len, -1)
    to_tok_major = lambda t: t.reshape(n_batch, seq_len, -1).transpose(1, 0, 2).reshape(n_batch * seq_len, -1)
    mod = mod_all[0]
    ffn_w = dict(wts["ffn"])
    x, ffn_w[0, 0] = _ffn(x, mod, tok_major, seq_len, 0, wts["norm_g"][0, 0], ffn_w[0, 0])
    s0 = shift_in[0] if tok_major else jnp.zeros((n_batch, 1, D_MODEL), f32)
    *proj, h_tail = _rwkv_proj(x, mod, tok_major, seq_len, wts["norm_g"][0, 1], s0, wts["rw"])
    if tok_major:
        z, wkv_t = _scan_lanes(proj, wts["rw"], n_batch, seq_len, jnp.transpose(wkv_in[0], (1, 2, 3, 0)))
        wkv_out = jnp.transpose(wkv_t, (3, 0, 1, 2))
        shift_out = h_tail
    else:
        z, wkv_out = _scan(proj, wts["rw"], n_batch, seq_len)
        shift_out = h_tail.reshape(n_batch, -1, 8, D_MODEL)[:, -1, -1]
    x = _proj_res(z, wts["rw"]["wo"], jnp.zeros((D_MODEL,), f32), x, mod, tok_major, seq_len)
    x, ffn_w[0, 1] = _ffn(x, mod, tok_major, seq_len, 6, wts["norm_g"][0, 2], ffn_w[0, 1])
    mod = mod_all[1]
    x, ffn_w[1, 0] = _ffn(x, mod, tok_major, seq_len, 0, wts["norm_g"][1, 0], ffn_w[1, 0])
    positions = pos0 + jnp.arange(seq_len)
    if tok_major:
        positions = jnp.repeat(positions, n_batch)
    cos, sin = _rope_tables(positions)
    qkv = _qkv(x, mod, tok_major, seq_len, wts["norm_g"][1, 1], wts["sw_wqkv"], wts["sw_bqkv"], cos, sin)
    kw = N_KV_HEADS * LANES
    if k_in is None:
        att = _attn_prompt(qkv, wts["sw_sink"], n_batch, seq_len)
        tail = qkv.reshape(n_batch, seq_len, QKV_COLS)[:, -WINDOW:, D_MODEL:]
        k_new = _undup(tail[..., :kw].reshape(n_batch * WINDOW, kw)).reshape(n_batch, WINDOW, N_KV_HEADS, HEAD_DIM)
        v_new = _undup(tail[..., kw:].reshape(n_batch * WINDOW, kw)).reshape(n_batch, WINDOW, N_KV_HEADS, HEAD_DIM)
    else:
        win = k_in.shape[2]
        qkv = to_seq_major(qkv)
        cache_t = lambda t: jnp.transpose(t[0], (0, 2, 3, 1)).reshape(n_batch, N_KV_HEADS * HEAD_DIM, win)
        att = _attn_cached(qkv, cache_t(k_in), cache_t(v_in), wts["sw_sink"], n_batch, seq_len)
        att = to_tok_major(att)
        k_tok = _undup(qkv[:, D_MODEL:D_MODEL + kw]).reshape(n_batch, seq_len, N_KV_HEADS, HEAD_DIM)
        v_tok = _undup(qkv[:, D_MODEL + kw:]).reshape(n_batch, seq_len, N_KV_HEADS, HEAD_DIM)
        k_new = jnp.concatenate([k_in[0], k_tok], axis=1)[:, -win:]
        v_new = jnp.concatenate([v_in[0], v_tok], axis=1)[:, -win:]
    x = _proj_res(att, wts["sw_wo"], wts["sw_bo"], x, mod, tok_major, seq_len)
    y, ffn_w[1, 1] = _ffn(x, mod, tok_major, seq_len, 6, wts["norm_g"][1, 2], ffn_w[1, 1], final_g=wts["final_g"])
    return (y, wkv_out[None], shift_out[None], k_new[None], v_new[None]), ffn_w


def _dup_heads(w):
    lead = w.shape[:-1]
    w4 = w.reshape(lead + (N_KV_HEADS, 1, HEAD_DIM))
    return jnp.broadcast_to(w4, lead + (N_KV_HEADS, 2, HEAD_DIM)).reshape(lead + (N_KV_HEADS * LANES,))


def _pad_cols(w):
    return jnp.pad(w, ((0, 0), (0, LORA_PAD - w.shape[1])))


def _pad_rows(w):
    return jnp.pad(w, ((0, LORA_PAD - w.shape[0]), (0, 0)))


def kernel(x_prompt, x_sample, state_rwkv_wkv, state_rwkv_shift, cache_swa_k, cache_swa_v, c_prompt, c_sample, norm_g, w_ada, b_ada, w_ffn_in, w_ffn_out, rw_mu, rw_wrkv, rw_w0, rw_w1, rw_w2, rw_a0, rw_a1, rw_a2, rw_g1, rw_g2, rw_kk, rw_ka, rw_rk, rw_gn_w, rw_gn_b, rw_wo, sw_wqkv, sw_bqkv, sw_sink, sw_wo, sw_bo, final_g):
    n_p, seq_p, _ = x_prompt.shape
    n_s, seq_s, _ = x_sample.shape
    nq = N_HEADS * HEAD_DIM
    nkv = N_KV_HEADS * HEAD_DIM
    row = lambda t: t.reshape(1, D_MODEL)
    rw = dict(
        mu=rw_mu[0], wrkv=rw_wrkv[0].astype(bf16),
        w1=_pad_cols(rw_w1[0]).astype(bf16), a1=_pad_cols(rw_a1[0]).astype(bf16), g1=rw_g1[0].astype(bf16),
        w2=_pad_rows(rw_w2[0]).astype(bf16), a2=_pad_rows(rw_a2[0]).astype(bf16), g2=rw_g2[0].astype(bf16),
        w0=row(rw_w0[0]), a0=row(rw_a0[0]), kk=row(rw_kk[0]), ka=row(rw_ka[0]), rk=row(rw_rk[0]),
        gn_w=row(rw_gn_w[0]), gn_b=row(rw_gn_b[0]), wo=rw_wo[0].astype(bf16))
    wq = sw_wqkv[0]
    bq = sw_bqkv[0]
    wts = dict(
        norm_g=norm_g, final_g=final_g,
        ffn={(l, s): (w_ffn_in, w_ffn_out, l, s) for l in range(2) for s in range(2)}, rw=rw,
        sw_wqkv=jnp.concatenate([wq[:, :nq], _dup_heads(wq[:, nq:nq + nkv]), _dup_heads(wq[:, nq + nkv:])],
                                axis=1).astype(bf16),
        sw_bqkv=jnp.concatenate([bq[:nq], _dup_heads(bq[nq:nq + nkv]), _dup_heads(bq[nq + nkv:])]),
        sw_sink=sw_sink[0], sw_wo=sw_wo[0].astype(bf16), sw_bo=sw_bo[0])

    n_c = n_p + n_s
    pad = (-n_c) % 8
    c_all = jnp.concatenate([c_sample, c_prompt, jnp.zeros((pad, D_MODEL), f32)], axis=0)
    mod = _ada(c_all, w_ada, b_ada)
    mod_s = [mod[l] for l in range(mod.shape[0])]
    mod_p = [mod[l, n_s:n_c].reshape(n_p, 1, N_MOD * D_MODEL) for l in range(mod.shape[0])]

    (y_s, s_wkv, s_shift, s_k, s_v), wts["ffn"] = _trunk(
        jnp.transpose(x_sample, (1, 0, 2)).reshape(n_s * seq_s, D_MODEL), mod_s, True, n_s, seq_s, PAST_LEN,
        state_rwkv_wkv, state_rwkv_shift, cache_swa_k, cache_swa_v, wts)
    (y_p, p_wkv, p_shift, p_k, p_v), _ = _trunk(
        x_prompt.reshape(n_p * seq_p, D_MODEL), mod_p, False, n_p, seq_p, 0, None, None, None, None, wts)
    y_s = jnp.transpose(y_s.reshape(seq_s, n_s, D_MODEL), (1, 0, 2))
    return (y_p.reshape(n_p, seq_p, D_MODEL), y_s, p_wkv, p_shift, p_k, p_v, s_wkv, s_shift, s_k, s_v)
```

```python
import functools
import math

import jax
import jax.numpy as jnp
from jax import lax
from jax.experimental import pallas as pl
from jax.experimental.pallas import tpu as pltpu

f32 = jnp.float32
bf16 = jnp.bfloat16

D_MODEL = 2048
HEAD_DIM = 64
N_HEADS = D_MODEL // HEAD_DIM
N_KV_HEADS = 4
GQA_GROUP = N_HEADS // N_KV_HEADS
WINDOW = 128
ATTN_SCALE = HEAD_DIM ** -0.5
ROPE_THETA = 10000.0
D_FF = 5632
N_MOD = 9
RMS_EPS = 1e-6
GN_EPS = 64e-5
PAST_LEN = 8192
LANES = 128
MXU_COLS = 256
N_PAIRS = D_MODEL // LANES
LORA_PAD = 128
SLAB = 64
SCAN_CHAINS = 16
VMEM_LIMIT = 56 * 1024 * 1024
NEG_BIG = -1e30
ROW_TILE = 512


def _cparams(n_axes):
    return pltpu.CompilerParams(dimension_semantics=("arbitrary",) * n_axes,
                                vmem_limit_bytes=VMEM_LIMIT)


def _dot(a, b):
    return jnp.dot(a, b, preferred_element_type=f32)


def _dot_nt(a, b):
    return lax.dot_general(a, b, (((1,), (1,)), ((), ())), preferred_element_type=f32)


def _split(x, n):
    if x.dtype == bf16:
        return [x]
    parts = []
    rem = x
    for i in range(n):
        p = rem.astype(bf16)
        parts.append(p)
        if i + 1 < n:
            rem = rem - p.astype(f32)
    return parts


def _mm(a, b, pa=1, pb=1, nt=False):
    a_parts = _split(a, pa)
    b_parts = _split(b, pb)
    order = max(len(a_parts), len(b_parts))
    acc = None
    for i, x in enumerate(a_parts):
        for j, y in enumerate(b_parts):
            if i + j >= order:
                continue
            t = _dot_nt(x, y) if nt else _dot(x, y)
            acc = t if acc is None else acc + t
    return acc


def _rms_mod(x, g, shift, scale):
    return x * lax.rsqrt(jnp.mean(x * x, axis=-1, keepdims=True) + RMS_EPS) * (g * (1.0 + scale)) + shift


def _mod_spec(tok_major, tm, seq_len, idx):
    if tok_major:
        return pl.BlockSpec((tm // seq_len, D_MODEL), lambda i, *_: (0, idx))
    return pl.BlockSpec((None, 1, D_MODEL), lambda i, *_: ((i * tm) // seq_len, 0, idx))


def _rows(v, tm):
    n = v.shape[0]
    return v if n in (1, tm) else jnp.concatenate([v] * (tm // n), axis=0)


def _ada_kernel(c_ref, w_ref, b_ref, o_ref):
    c = c_ref[...]
    s = (c * jax.nn.sigmoid(c)).astype(bf16)
    o_ref[...] = _dot(s, w_ref[...].astype(bf16)) + b_ref[...]


def _ada(c_all, w_ada, b_ada):
    n_layers, _, n_out = w_ada.shape
    rows = c_all.shape[0]
    tn = 1024
    return pl.pallas_call(
        _ada_kernel,
        grid=(n_layers, n_out // tn),
        in_specs=[pl.BlockSpec((rows, D_MODEL), lambda l, j: (0, 0)),
                  pl.BlockSpec((None, D_MODEL, tn), lambda l, j: (l, 0, j)),
                  pl.BlockSpec((None, 1, tn), lambda l, j: (l, 0, j))],
        out_specs=pl.BlockSpec((None, rows, tn), lambda l, j: (l, 0, j)),
        out_shape=jax.ShapeDtypeStruct((n_layers, rows, n_out), f32),
        compiler_params=_cparams(2),
        name="ada_mod",
    )(c_all, w_ada, b_ada.reshape(n_layers, 1, n_out))


def _ffn_kernel(final_norm, emit_bf16, tiled_gate, x_ref, ng_ref, sh_ref, sc_ref, gt_ref, wg_ref, wu_ref, wo_ref,
                fg_ref, o_ref, *rest):
    rest = list(rest)
    if emit_bf16:
        wg_out, wu_out, wo_out = rest[:3]
        del rest[:3]
    h_scr = rest.pop(0)
    gt_scr = rest.pop(0) if tiled_gate else None
    f = pl.program_id(1)
    tm = x_ref.shape[0]

    @pl.when(f == 0)
    def _():
        h_scr[...] = _rms_mod(x_ref[...], ng_ref[...], _rows(sh_ref[...], tm), _rows(sc_ref[...], tm)).astype(bf16)
        o_ref[...] = x_ref[...]
        if tiled_gate:
            gt_scr[...] = 0.5 * _rows(gt_ref[...], tm)

    wg, wu, wo = wg_ref[...], wu_ref[...], wo_ref[...]
    if emit_bf16:
        wg, wu, wo = wg.astype(bf16), wu.astype(bf16), wo.astype(bf16)
        wg_out[...] = wg
        wu_out[...] = wu
        wo_out[...] = wo
    h = h_scr[...]
    gate = _dot(h, wg)
    up = _dot(h, wu)
    act = (gate * jax.nn.sigmoid(gate) * up).astype(bf16)
    half_gate = gt_scr[...] if tiled_gate else 0.5 * gt_ref[...]
    o_ref[...] += half_gate * _dot(act, wo)

    if final_norm:
        @pl.when(f == pl.num_programs(1) - 1)
        def _():
            y = o_ref[...]
            o_ref[...] = y * lax.rsqrt(jnp.mean(y * y, axis=-1, keepdims=True) + RMS_EPS) * fg_ref[...]


def _ffn(x, mod, tok_major, seq_len, mod_base, norm_g, weights, final_g=None):
    m = x.shape[0]
    tm = min(ROW_TILE, m)
    emit = len(weights) == 4
    tf = 256 if emit else 512
    nf = D_FF // tf
    vec = pl.BlockSpec((1, D_MODEL), lambda i, j: (0, 0))
    fg = jnp.ones((1, D_MODEL), f32) if final_g is None else final_g.reshape(1, D_MODEL)
    w_in_spec = pl.BlockSpec((D_MODEL, tf), lambda i, j: (0, j))
    w_out_spec = pl.BlockSpec((tf, D_MODEL), lambda i, j: (j, 0))
    out_specs = [pl.BlockSpec((tm, D_MODEL), lambda i, j: (i, 0))]
    out_shape = [jax.ShapeDtypeStruct((m, D_MODEL), f32)]
    if emit:
        assert m == tm, "the weight copies are written once, by a single row tile"
        w_ffn_in, w_ffn_out, layer, slot = weights
        w_args = (w_ffn_in, w_ffn_in, w_ffn_out)
        w_specs = [pl.BlockSpec((None, None, D_MODEL, tf), lambda i, j: (layer, slot, 0, j)),
                   pl.BlockSpec((None, None, D_MODEL, tf), lambda i, j: (layer, slot, 0, j + nf)),
                   pl.BlockSpec((None, None, tf, D_MODEL), lambda i, j: (layer, slot, j, 0))]
        out_specs += [w_in_spec, w_in_spec, w_out_spec]
        out_shape += [jax.ShapeDtypeStruct((D_MODEL, D_FF), bf16), jax.ShapeDtypeStruct((D_MODEL, D_FF), bf16),
                      jax.ShapeDtypeStruct((D_FF, D_MODEL), bf16)]
    else:
        w_args = weights
        w_specs = [w_in_spec, w_in_spec, w_out_spec]
    outs = pl.pallas_call(
        functools.partial(_ffn_kernel, final_g is not None, emit, tok_major),
        grid=(m // tm, nf),
        in_specs=[pl.BlockSpec((tm, D_MODEL), lambda i, j: (i, 0)),
                  vec,
                  _mod_spec(tok_major, tm, seq_len, mod_base),
                  _mod_spec(tok_major, tm, seq_len, mod_base + 1),
                  _mod_spec(tok_major, tm, seq_len, mod_base + 2)] + w_specs + [vec],
        out_specs=out_specs,
        out_shape=out_shape,
        scratch_shapes=[pltpu.VMEM((tm, D_MODEL), bf16)] + ([pltpu.VMEM((tm, D_MODEL), f32)] if tok_major else []),
        compiler_params=_cparams(2),
        name="ffn",
    )(x, norm_g.reshape(1, D_MODEL), mod, mod, mod, *w_args, fg)
    return (outs[0], tuple(outs[1:])) if emit else (outs[0], weights)


def _rwkv_proj_kernel(seq_len, tm, h_rows, tok_major,
                      x_ref, ng_ref, sh_ref, sc_ref, s0_ref, mu_ref, w1_ref, a1_ref, g1_ref,
                      w_ref, w2_ref, a2_ref, g2_ref, w0_ref, a0_ref,
                      r_ref, ld_ref, k_ref, v_ref, a_ref, g_ref, h_ref,
                      hs, xr, xk, xv, tw, ta, sg):
    i = pl.program_id(0)
    j = pl.program_id(1)

    @pl.when(j == 0)
    def _():
        h = _rms_mod(x_ref[...], ng_ref[...], _rows(sh_ref[...], tm), _rows(sc_ref[...], tm))
        h_ref[...] = h[tm - h_rows:tm, :]
        if tok_major:
            n_seq = s0_ref.shape[0]
            prev = jnp.concatenate([s0_ref[...], h[0:tm - n_seq, :]], axis=0)
        else:
            @pl.when(i == 0)
            def _():
                hs[0:8, :] = jnp.zeros((8, D_MODEL), f32)

            @pl.when(i > 0)
            def _():
                hs[0:8, :] = hs[tm:tm + 8, :]

            hs[8:tm + 8, :] = h
            row = i * tm + lax.broadcasted_iota(jnp.int32, (tm, 1), 0)
            prev = jnp.where(row % seq_len == 0, s0_ref[...], hs[7:tm + 7, :])
        xx = prev - h
        mu = mu_ref[...]
        xr[...] = (h + xx * mu[0:1, :]).astype(bf16)
        xk[...] = (h + xx * mu[2:3, :]).astype(bf16)
        xv[...] = (h + xx * mu[3:4, :]).astype(bf16)
        xw = (h + xx * mu[1:2, :]).astype(bf16)
        tw[...] = jnp.tanh(_dot(xw, w1_ref[...])).astype(bf16)
        xa = (h + xx * mu[4:5, :]).astype(bf16)
        ta[...] = _dot(xa, a1_ref[...]).astype(bf16)
        xg = (h + xx * mu[5:6, :]).astype(bf16)
        sg[...] = jax.nn.sigmoid(_dot(xg, g1_ref[...])).astype(bf16)

    r_ref[...] = _dot(xr[...], w_ref[0, j])
    k_ref[...] = _dot(xk[...], w_ref[1, j])
    v_ref[...] = _dot(xv[...], w_ref[2, j])
    z = w0_ref[...] + _dot(tw[...], w2_ref[...])
    ld_ref[...] = -jax.nn.sigmoid(z) * math.exp(-0.5)
    a_ref[...] = jax.nn.sigmoid(a0_ref[...] + _dot(ta[...], a2_ref[...]))
    g_ref[...] = _dot(sg[...], g2_ref[...])


def _rwkv_proj(x, mod, tok_major, seq_len, norm_g, s0, rwp):
    m = x.shape[0]
    tm = min(ROW_TILE, m)
    tn = 256
    h_rows = tm // seq_len if tok_major else 8
    full = lambda shape: pl.BlockSpec(shape, lambda i, j: (0,) * len(shape))
    col = lambda rows: pl.BlockSpec((rows, tn), lambda i, j: (0, j))
    w_res = pl.BlockSpec((3, D_MODEL // tn, D_MODEL, tn), lambda i, j: (0, 0, 0, 0), pipeline_mode=pl.Buffered(1))
    if tok_major:
        assert m == tm
        s0_spec = pl.BlockSpec((tm // seq_len, D_MODEL), lambda i, j: (0, 0))
    else:
        s0_spec = pl.BlockSpec((None, 1, D_MODEL), lambda i, j: ((i * tm) // seq_len, 0, 0))
    out_spec = pl.BlockSpec((tm, tn), lambda i, j: (i, j))
    out_sds = jax.ShapeDtypeStruct((m, D_MODEL), f32)
    outs = pl.pallas_call(
        functools.partial(_rwkv_proj_kernel, seq_len, tm, h_rows, tok_major),
        grid=(m // tm, D_MODEL // tn),
        in_specs=[pl.BlockSpec((tm, D_MODEL), lambda i, j: (i, 0)),
                  full((1, D_MODEL)),
                  _mod_spec(tok_major, tm, seq_len, 3),
                  _mod_spec(tok_major, tm, seq_len, 4),
                  s0_spec,
                  full((6, D_MODEL)),
                  full((D_MODEL, LORA_PAD)), full((D_MODEL, LORA_PAD)), full((D_MODEL, 256)),
                  w_res,
                  col(LORA_PAD), col(LORA_PAD), col(256), col(1), col(1)],
        out_specs=[out_spec] * 6 + [pl.BlockSpec((h_rows, D_MODEL), lambda i, j: (i, 0))],
        out_shape=[out_sds] * 6 + [jax.ShapeDtypeStruct((m // tm * h_rows, D_MODEL), f32)],
        scratch_shapes=[pltpu.VMEM((tm + 8, D_MODEL), f32),
                        pltpu.VMEM((tm, D_MODEL), bf16), pltpu.VMEM((tm, D_MODEL), bf16),
                        pltpu.VMEM((tm, D_MODEL), bf16),
                        pltpu.VMEM((tm, LORA_PAD), bf16), pltpu.VMEM((tm, LORA_PAD), bf16),
                        pltpu.VMEM((tm, 256), bf16)],
        compiler_params=_cparams(2),
        name="rwkv_proj",
    )(x, norm_g.reshape(1, D_MODEL), mod, mod, s0, rwp["mu"], rwp["w1"], rwp["a1"], rwp["g1"],
      rwp["wrkv"].reshape(3, D_MODEL, D_MODEL // tn, tn).transpose(0, 2, 1, 3),
      rwp["w2"], rwp["a2"], rwp["g2"], rwp["w0"], rwp["a0"])
    return outs


def _scan_chains(acts, params, states):
    n_st = 2 * SLAB
    row = lax.broadcasted_iota(jnp.int32, (n_st, LANES), 0)
    lane = lax.broadcasted_iota(jnp.int32, (n_st, LANES), 1)
    own = (row // SLAB) == (lane // HEAD_DIM)
    blk = (row // SLAB) == (lane // SLAB)
    strict = blk & (lane < row)
    incl = blk & (lane <= row)
    eye = jnp.where(row == lane, 1.0, 0.0)
    twice = lambda t: jnp.concatenate([t, t], axis=0)
    st = lambda t: jnp.where(own, twice(t), 0.0)
    each = lambda fn, *lists: [fn(*args) for args in zip(*lists)]

    r, ld, k, v, a, g = [[act[i] for act in acts] for i in range(6)]
    kkp, kap, rkp, gnw, gnb = [[par[i] for par in params] for i in range(5)]

    r64 = lax.broadcasted_iota(jnp.int32, (SLAB, SLAB), 0)
    c64 = lax.broadcasted_iota(jnp.int32, (SLAB, SLAB), 1)
    tri_ones = jnp.concatenate([jnp.where(c64 <= r64, 1.0, 0.0),
                                jnp.ones((SLAB, SLAB), f32)], axis=0).astype(bf16)
    sums = each(lambda t: _mm(tri_ones, t, pb=3), ld)
    cs = each(lambda t: t[0:SLAB], sums)
    tot = each(lambda t: t[SLAB:n_st], sums)

    kk_raw = each(lambda t, p: st(t * p), k, kkp)
    kk = each(lambda t: t * (1.0 / jnp.maximum(jnp.sqrt(jnp.sum(t * t, axis=1, keepdims=True)), 1e-12)), kk_raw)
    k2 = each(lambda kt, at, p: st(kt * (1.0 + (at - 1.0) * p)), k, a, kap)
    b = each(lambda t, at: t * twice(at), kk, a)
    r_s = each(st, r)
    v_s = each(st, v)
    bonus = each(lambda rt, kt, p, vt: jnp.sum(rt * kt * p, axis=1, keepdims=True) * vt, r_s, k2, rkp, v_s)

    e_neg = each(lambda c_: twice(jnp.exp(-c_)), cs)
    e_tail = each(lambda t_, c_: twice(jnp.exp(t_ - c_)), tot, cs)
    a_t = each(lambda t, c_, l_: -t * twice(jnp.exp(c_ - l_)), kk, cs, ld)
    r_t = each(lambda t, c_: t * twice(jnp.exp(c_)), r_s, cs)
    k_t = each(lambda t, e: t * e, k2, e_neg)
    b_t = each(lambda t, e: t * e, b, e_neg)
    k_h = each(lambda t, e: t * e, k2, e_tail)
    b_h = each(lambda t, e: t * e, b, e_tail)

    lhs1 = each(lambda x_, y_: jnp.concatenate([x_, y_], axis=0), a_t, r_t)
    rhs1 = each(lambda x_, y_: jnp.concatenate([x_, y_], axis=0), k_t, b_t)
    x = each(lambda l_, r_: _mm(l_, r_, nt=True), lhs1, rhs1)
    a_k = each(lambda t: jnp.where(strict, t[0:n_st, 0:n_st], 0.0), x)
    a_b = each(lambda t: jnp.where(strict, t[0:n_st, n_st:], 0.0), x)
    r_kb = each(lambda t: jnp.concatenate([jnp.where(incl, t[n_st:, 0:n_st], 0.0),
                                           jnp.where(incl, t[n_st:, n_st:], 0.0)], axis=1), x)

    p1 = each(lambda l_, s_: _mm(l_, s_, nt=True), lhs1, states)
    rhs_u = each(lambda p_, ak, vt: p_[0:n_st] + _mm(ak, vt), p1, a_k, v_s)

    near = (row // 2) == (lane // 2)
    t_inv = each(lambda t: eye + jnp.where(near, t, 0.0), a_b)
    s = 2
    while s < SLAB:
        off = ((row // (2 * s)) == (lane // (2 * s))) & ((row // s) != (lane // s))
        w = each(lambda ab, ti: _mm(jnp.where(off, ab, 0.0), ti), a_b, t_inv)
        t_inv = each(lambda ti, w_: ti + _mm(ti, w_), t_inv, w)
        s *= 2
    u = each(_mm, t_inv, rhs_u)

    vu = each(lambda x_, y_: jnp.concatenate([x_, y_], axis=0), v_s, u)
    y = each(lambda p_, rk_, vu_: p_[n_st:] + _mm(rk_, vu_), p1, r_kb, vu)

    def finish(y_, w_, b_, bonus_, g_):
        mean = jnp.sum(y_, axis=1, keepdims=True) * (1.0 / HEAD_DIM)
        dev = jnp.where(own, y_ - mean, 0.0)
        var = jnp.sum(dev * dev, axis=1, keepdims=True) * (1.0 / HEAD_DIM)
        out = jnp.where(own, dev * lax.rsqrt(var + GN_EPS) * w_ + b_, 0.0) + bonus_
        return (out[0:SLAB] + out[SLAB:n_st]) * g_
    z = each(finish, y, gnw, gnb, bonus, g)

    vu_t = each(lambda t: t.T, vu)
    kb_h = each(lambda x_, y_: jnp.concatenate([x_, y_], axis=0), k_h, b_h)
    new_states = each(lambda s_, t_, vt, kb: s_ * jnp.exp(t_[0:1, :]) + _mm(vt, kb), states, tot, vu_t, kb_h)
    return z, new_states


def _scan_kernel(n_seq, n_pp, r_ref, ld_ref, k_ref, v_ref, a_ref, g_ref, kk_ref, ka_ref, rk_ref, gw_ref,
                 gb_ref, z_ref, s_out_ref, s_scr):
    c = pl.program_id(1)

    @pl.when(c == 0)
    def _():
        s_scr[...] = jnp.zeros_like(s_scr)

    chains = [(b, pp) for b in range(n_seq) for pp in range(n_pp)]
    lanes = [slice(pp * LANES, (pp + 1) * LANES) for _, pp in chains]
    acts = [tuple(ref[b, :, sl] for ref in (r_ref, ld_ref, k_ref, v_ref, a_ref, g_ref))
            for (b, _), sl in zip(chains, lanes)]
    params = [tuple(ref[:, sl] for ref in (kk_ref, ka_ref, rk_ref, gw_ref, gb_ref)) for sl in lanes]
    states = [s_scr[n] for n in range(len(chains))]
    z, new_states = _scan_chains(acts, params, states)
    for n, (b, _) in enumerate(chains):
        z_ref[b, :, lanes[n]] = z[n].astype(bf16)
        s_scr[n] = new_states[n]

    @pl.when(c == pl.num_programs(1) - 1)
    def _():
        for n, (b, pp) in enumerate(chains):
            s = s_scr[n]
            s_out_ref[b, 2 * pp] = s[0:HEAD_DIM, 0:HEAD_DIM]
            s_out_ref[b, 2 * pp + 1] = s[HEAD_DIM:, HEAD_DIM:]


def _scan(proj, rwp, n_batch, seq_len):
    n_pp = SCAN_CHAINS // n_batch
    w = LANES * n_pp
    act = pl.BlockSpec((n_batch, SLAB, w), lambda p, c: (0, c, p))
    vec = pl.BlockSpec((1, w), lambda p, c: (0, p))
    st_spec = pl.BlockSpec((n_batch, 2 * n_pp, HEAD_DIM, HEAD_DIM), lambda p, c: (0, p, 0, 0))
    z, s_out = pl.pallas_call(
        functools.partial(_scan_kernel, n_batch, n_pp),
        grid=(N_PAIRS // n_pp, seq_len // SLAB),
        in_specs=[act] * 6 + [vec] * 5,
        out_specs=[act, st_spec],
        out_shape=[jax.ShapeDtypeStruct((n_batch, seq_len, D_MODEL), bf16),
                   jax.ShapeDtypeStruct((n_batch, N_HEADS, HEAD_DIM, HEAD_DIM), f32)],
        scratch_shapes=[pltpu.VMEM((n_batch * n_pp, 2 * SLAB, LANES), f32)],
        compiler_params=_cparams(2),
        name="rwkv_scan",
    )(*[p.reshape(n_batch, seq_len, D_MODEL) for p in proj],
      rwp["kk"], rwp["ka"], rwp["rk"], rwp["gn_w"], rwp["gn_b"])
    return z.reshape(n_batch * seq_len, D_MODEL), s_out


def _scan_lanes_kernel(n_tok, r_ref, ld_ref, k_ref, v_ref, a_ref, g_ref, kk_ref, ka_ref, rk_ref, gw_ref, gb_ref,
                       s_in_ref, z_ref, s_out_ref, kk_s, w_s, b_s, k2_s, r_s, v_s, y_s):
    hd = HEAD_DIM
    colsum = lambda t: jnp.sum(t, axis=0, keepdims=True)
    for t in range(n_tok):
        r_t, k_t, v_t, a_t = r_ref[t].T, k_ref[t].T, v_ref[t].T, a_ref[t].T
        kk_raw = k_t * kk_ref[...]
        k2 = k_t * (1.0 + (a_t - 1.0) * ka_ref[...])
        w = jnp.exp(ld_ref[t].T)
        for e in range(2):
            sl = slice(e * hd, (e + 1) * hd)
            kr = kk_raw[sl]
            kk = kr * (1.0 / jnp.maximum(jnp.sqrt(colsum(kr * kr)), 1e-12))
            kk_s[e, t] = kk
            b_s[e, t] = kk * a_t[sl]
            k2_s[e, t] = k2[sl]
            w_s[e, t] = w[sl]
            r_s[e, t] = r_t[sl]
            v_s[e, t] = v_t[sl]

    rows_per_iter = 2

    def per_rows(i, carry):
        chains = [(e, rows_per_iter * i + d) for e in range(2) for d in range(rows_per_iter)]
        s = [s_in_ref[e, vi] for e, vi in chains]
        for t in range(n_tok):
            for c, (e, vi) in enumerate(chains):
                s_kk = colsum(s[c] * kk_s[e, t])
                s[c] = s[c] * w_s[e, t] - s_kk * b_s[e, t] + v_s[e, t, pl.ds(vi, 1), :] * k2_s[e, t]
                y_s[e, t, pl.ds(vi, 1), :] = colsum(s[c] * r_s[e, t])
        for c, (e, vi) in enumerate(chains):
            s_out_ref[e, vi] = s[c]
        return carry
    lax.fori_loop(0, hd // rows_per_iter, per_rows, 0)

    for t in range(n_tok):
        outs = []
        for e in range(2):
            sl = slice(e * hd, (e + 1) * hd)
            y = y_s[e, t]
            dev = y - colsum(y) * (1.0 / hd)
            var = colsum(dev * dev) * (1.0 / hd)
            bonus = colsum(r_s[e, t] * k2_s[e, t] * rk_ref[sl, :]) * v_s[e, t]
            outs.append(dev * lax.rsqrt(var + GN_EPS) * gw_ref[sl, :] + gb_ref[sl, :] + bonus)
        z_ref[t] = (jnp.concatenate(outs, axis=0).T * g_ref[t]).astype(bf16)


def _scan_lanes(proj, rwp, n_batch, n_tok, state):
    acts = [p.reshape(n_tok, n_batch, D_MODEL) for p in proj]
    lane_bc = lambda v: jnp.broadcast_to(v.reshape(D_MODEL, 1), (D_MODEL, n_batch))
    params = [lane_bc(rwp[n]) for n in ("kk", "ka", "rk", "gn_w", "gn_b")]
    act = pl.BlockSpec((n_tok, n_batch, LANES), lambda p: (0, 0, p))
    par = pl.BlockSpec((LANES, n_batch), lambda p: (p, 0))
    st_spec = pl.BlockSpec((2, HEAD_DIM, HEAD_DIM, n_batch), lambda p: (p, 0, 0, 0))
    per_tok = pltpu.VMEM((2, n_tok, HEAD_DIM, n_batch), f32)
    z, s_out = pl.pallas_call(
        functools.partial(_scan_lanes_kernel, n_tok),
        grid=(N_PAIRS,),
        in_specs=[act] * 6 + [par] * 5 + [st_spec],
        out_specs=[act, st_spec],
        out_shape=[jax.ShapeDtypeStruct((n_tok, n_batch, D_MODEL), bf16),
                   jax.ShapeDtypeStruct(state.shape, f32)],
        scratch_shapes=[per_tok] * 7,
        compiler_params=_cparams(1),
        name="rwkv_decode_scan",
    )(*acts, *params, state)
    return z.reshape(n_tok * n_batch, D_MODEL), s_out


def _proj_res_kernel(z_ref, w_ref, b_ref, x_ref, gt_ref, o_ref):
    out = _dot(z_ref[...], w_ref[...]) + b_ref[...]
    o_ref[...] = x_ref[...] + _rows(gt_ref[...], x_ref.shape[0]) * out


def _proj_res(z, w, bias, x, mod, tok_major, seq_len):
    m = x.shape[0]
    tm = min(ROW_TILE, m)
    rows = pl.BlockSpec((tm, D_MODEL), lambda i: (i, 0))
    return pl.pallas_call(
        _proj_res_kernel,
        grid=(m // tm,),
        in_specs=[rows,
                  pl.BlockSpec((D_MODEL, D_MODEL), lambda i: (0, 0), pipeline_mode=pl.Buffered(1)),
                  pl.BlockSpec((1, D_MODEL), lambda i: (0, 0)),
                  rows,
                  _mod_spec(tok_major, tm, seq_len, 5)],
        out_specs=rows,
        out_shape=jax.ShapeDtypeStruct((m, D_MODEL), f32),
        compiler_params=_cparams(1),
        name="proj_res",
    )(z, w, bias.reshape(1, D_MODEL), x, mod)


QKV_COLS = D_MODEL + 2 * N_KV_HEADS * LANES


def _qkv_kernel(x_ref, ng_ref, sh_ref, sc_ref, w_ref, b_ref, cos_ref, sin_ref, o_ref, h_scr):
    @pl.when(pl.program_id(1) == 0)
    def _():
        tm = x_ref.shape[0]
        h_scr[...] = _rms_mod(x_ref[...], ng_ref[...], _rows(sh_ref[...], tm), _rows(sc_ref[...], tm)).astype(bf16)

    cos = cos_ref[...]
    sin = sin_ref[...]
    first = (lax.broadcasted_iota(jnp.int32, cos.shape, 1) % HEAD_DIM) < HEAD_DIM // 2
    h = h_scr[...]
    for c0 in range(0, o_ref.shape[1], MXU_COLS):
        acc = _dot(h, w_ref[:, c0:c0 + MXU_COLS]) + b_ref[:, c0:c0 + MXU_COLS]
        for c in range(0, MXU_COLS, LANES):
            xc = acc[:, c:c + LANES]
            rot = jnp.where(first, pltpu.roll(xc, LANES - HEAD_DIM // 2, 1),
                            pltpu.roll(xc, HEAD_DIM // 2, 1))
            o_ref[:, c0 + c:c0 + c + LANES] = xc * cos + rot * sin


def _qkv(x, mod, tok_major, seq_len, norm_g, w, b, cos, sin):
    m = x.shape[0]
    tm = min(ROW_TILE, m)
    tn = 512
    n_rope = (D_MODEL + N_KV_HEADS * LANES) // tn
    n_pos_blocks = cos.shape[0] // tm
    vec = pl.BlockSpec((1, D_MODEL), lambda i, j: (0, 0))
    tab = pl.BlockSpec((tm, LANES), lambda i, j: (i % n_pos_blocks, j // n_rope))
    return pl.pallas_call(
        _qkv_kernel,
        grid=(m // tm, QKV_COLS // tn),
        in_specs=[pl.BlockSpec((tm, D_MODEL), lambda i, j: (i, 0)),
                  vec,
                  _mod_spec(tok_major, tm, seq_len, 3),
                  _mod_spec(tok_major, tm, seq_len, 4),
                  pl.BlockSpec((D_MODEL, tn), lambda i, j: (0, j)),
                  pl.BlockSpec((1, tn), lambda i, j: (0, j)),
                  tab, tab],
        out_specs=pl.BlockSpec((tm, tn), lambda i, j: (i, j)),
        out_shape=jax.ShapeDtypeStruct((m, QKV_COLS), f32),
        scratch_shapes=[pltpu.VMEM((tm, D_MODEL), bf16)],
        compiler_params=_cparams(2),
        name="swa_qkv",
    )(x, norm_g.reshape(1, D_MODEL), mod, mod, w, b.reshape(1, QKV_COLS), cos, sin)


def _rope_tables(positions):
    half = HEAD_DIM // 2
    inv_freq = ROPE_THETA ** (-jnp.arange(half, dtype=f32) / half)
    ang = positions.astype(f32)[:, None] * inv_freq[None, :]
    cos = jnp.tile(jnp.cos(ang), (1, LANES // half))
    sin = jnp.sin(ang)
    sin = jnp.tile(jnp.concatenate([-sin, sin], axis=1), (1, LANES // HEAD_DIM))
    return (jnp.concatenate([cos, jnp.ones_like(cos)], axis=1),
            jnp.concatenate([sin, jnp.zeros_like(sin)], axis=1))


def _head_masks(rows):
    lane = lax.broadcasted_iota(jnp.int32, (rows, LANES), 1)
    return lane < HEAD_DIM, lane >= HEAD_DIM


def _attn_prompt_kernel(q_ref, kp_ref, kc_ref, vp_ref, vc_ref, sink_ref, o_ref):
    n = pl.program_id(1)
    blk = WINDOW
    m0, m1 = _head_masks(blk)
    m0k, m1k = _head_masks(2 * blk)
    qi = lax.broadcasted_iota(jnp.int32, (2 * blk, 2 * blk), 0) % blk
    sj = lax.broadcasted_iota(jnp.int32, (2 * blk, 2 * blk), 1)
    visible = (sj > qi) & (sj <= qi + blk) & ((n > 0) | (sj >= blk))
    top = lax.broadcasted_iota(jnp.int32, (2 * blk, 1), 0) < blk
    for c in range(N_KV_HEADS):
        sl = slice(c * LANES, (c + 1) * LANES)
        kd = jnp.concatenate([kp_ref[:, sl], kc_ref[:, sl]], axis=0).astype(bf16)
        vd = jnp.concatenate([vp_ref[:, sl], vc_ref[:, sl]], axis=0)
        vcat = jnp.concatenate([jnp.where(m0k, vd, 0.0), jnp.where(m1k, vd, 0.0)], axis=0).astype(bf16)
        pairs = [c * (GQA_GROUP // 2) + jj for jj in range(GQA_GROUP // 2)]
        each = lambda fn, *lists: [fn(*args) for args in zip(*lists)]
        qp = [q_ref[:, pair * LANES:(pair + 1) * LANES] * ATTN_SCALE for pair in pairs]
        qs = each(lambda t: jnp.concatenate([jnp.where(m0, t, 0.0), jnp.where(m1, t, 0.0)], axis=0).astype(bf16), qp)
        s = each(lambda t: jnp.where(visible, _dot_nt(t, kd), NEG_BIG), qs)
        sk = [jnp.where(top, sink_ref[2 * pair], sink_ref[2 * pair + 1]) for pair in pairs]
        mx = each(lambda t, k_: jnp.maximum(jnp.max(t, axis=1, keepdims=True), k_), s, sk)
        p = each(lambda t, m_: jnp.exp(t - m_), s, mx)
        den = each(lambda t, k_, m_: jnp.sum(t, axis=1, keepdims=True) + jnp.exp(k_ - m_), p, sk, mx)
        p = each(lambda t, d_: (t * (1.0 / d_)).astype(bf16), p, den)
        pcat = each(lambda t: jnp.concatenate([t[0:blk], t[blk:]], axis=1), p)
        for pair, t in zip(pairs, pcat):
            o_ref[:, pair * LANES:(pair + 1) * LANES] = _dot(t, vcat).astype(bf16)


def _attn_prompt(qkv, sink, n_batch, seq_len):
    nb = seq_len // WINDOW
    kw = N_KV_HEADS * LANES
    k_blk = D_MODEL // kw
    cur = lambda off: pl.BlockSpec((WINDOW, kw), lambda b, n: (b * nb + n, k_blk + off))
    prev = lambda off: pl.BlockSpec((WINDOW, kw), lambda b, n: (b * nb + jnp.maximum(n - 1, 0), k_blk + off))
    return pl.pallas_call(
        _attn_prompt_kernel,
        grid=(n_batch, nb),
        in_specs=[pl.BlockSpec((WINDOW, D_MODEL), lambda b, n: (b * nb + n, 0)),
                  prev(0), cur(0), prev(1), cur(1),
                  pl.BlockSpec(memory_space=pltpu.SMEM)],
        out_specs=pl.BlockSpec((WINDOW, D_MODEL), lambda b, n: (b * nb + n, 0)),
        out_shape=jax.ShapeDtypeStruct((n_batch * seq_len, D_MODEL), bf16),
        compiler_params=_cparams(2),
        name="swa_prompt",
    )(qkv, qkv, qkv, qkv, qkv, sink)


SEQ_PER_GROUP = 4


def _attn_cached_kernel(n_tok, groups, q_ref, kn_ref, vn_ref, kc_ref, vc_ref, sink_ref, o_ref):
    rows = SEQ_PER_GROUP * n_tok
    n_st = GQA_GROUP * rows
    m0, m1 = _head_masks(rows)
    srow = lax.broadcasted_iota(jnp.int32, (n_st, 1), 0)
    row_seq = (srow % rows) // n_tok
    row_tok = srow % n_tok
    key_c = lax.broadcasted_iota(jnp.int32, (n_st, WINDOW), 1)
    vis_c = key_c > row_tok
    key_n = lax.broadcasted_iota(jnp.int32, (n_st, rows), 1)
    vis_n = ((key_n // n_tok) == row_seq) & ((key_n % n_tok) <= row_tok)
    for gi in range(groups):
        rs = slice(gi * rows, (gi + 1) * rows)
        for cp in range(N_KV_HEADS // 2):
            for ce in range(2):
                c = 2 * cp + ce
                sl = slice(c * LANES, (c + 1) * LANES)
                kt = [kc_ref[gi * SEQ_PER_GROUP + b, c * HEAD_DIM:(c + 1) * HEAD_DIM, :] for b in range(SEQ_PER_GROUP)]
                vt = [vc_ref[gi * SEQ_PER_GROUP + b, c * HEAD_DIM:(c + 1) * HEAD_DIM, :] for b in range(SEQ_PER_GROUP)]
                kn = kn_ref[rs, sl].astype(bf16)
                vn = vn_ref[rs, sl].astype(bf16)
                pieces = []
                for jj in range(GQA_GROUP // 2):
                    pair = c * (GQA_GROUP // 2) + jj
                    qp = q_ref[rs, pair * LANES:(pair + 1) * LANES]
                    pieces += [jnp.where(m0, qp, 0.0), jnp.where(m1, qp, 0.0)]
                qs = jnp.concatenate(pieces, axis=0).astype(bf16)
                s_c = jnp.zeros((n_st, WINDOW), f32)
                for b in range(SEQ_PER_GROUP):
                    kx = jnp.concatenate([kt[b], kt[b]], axis=0).astype(bf16)
                    s_c = jnp.where(row_seq == b, _dot(qs, kx), s_c)
                s_c = jnp.where(vis_c, s_c * ATTN_SCALE, NEG_BIG)
                s_n = jnp.where(vis_n, _dot_nt(qs, kn) * ATTN_SCALE, NEG_BIG)
                sk = sink_ref[c]
                sk = sk[:, 0:1]
                mx = jnp.maximum(jnp.maximum(jnp.max(s_c, axis=1, keepdims=True),
                                             jnp.max(s_n, axis=1, keepdims=True)), sk)
                p_c = jnp.exp(s_c - mx)
                p_n = jnp.exp(s_n - mx)
                den = (jnp.sum(p_c, axis=1, keepdims=True) + jnp.sum(p_n, axis=1, keepdims=True)
                       + jnp.exp(sk - mx))
                inv_den = 1.0 / den
                p_c = (p_c * inv_den).astype(bf16)
                p_n = (p_n * inv_den).astype(bf16)
                o = _dot(p_n, vn)
                for b in range(SEQ_PER_GROUP):
                    vx = jnp.concatenate([vt[b], vt[b]], axis=0).astype(bf16)
                    o = o + jnp.where(row_seq == b, _dot_nt(p_c, vx), 0.0)
                for jj in range(GQA_GROUP // 2):
                    pair = c * (GQA_GROUP // 2) + jj
                    o0 = o[(2 * jj) * rows:(2 * jj + 1) * rows]
                    o1 = o[(2 * jj + 1) * rows:(2 * jj + 2) * rows]
                    o_ref[rs, pair * LANES:(pair + 1) * LANES] = jnp.where(m0, o0, o1).astype(bf16)


def _attn_cached(qkv, k_cache, v_cache, sink, n_batch, n_tok):
    groups = 2
    seqs = SEQ_PER_GROUP * groups
    rows = seqs * n_tok
    kw = N_KV_HEADS * LANES
    k_blk = D_MODEL // kw
    n_st = GQA_GROUP * SEQ_PER_GROUP * n_tok
    head = (jnp.arange(N_KV_HEADS)[:, None] * GQA_GROUP
            + (jnp.arange(n_st)[None, :] // (SEQ_PER_GROUP * n_tok)))
    sink_tab = jnp.broadcast_to(sink[head][:, :, None], (N_KV_HEADS, n_st, LANES))
    cache_spec = pl.BlockSpec((seqs, N_KV_HEADS * HEAD_DIM, WINDOW), lambda i: (i, 0, 0))
    return pl.pallas_call(
        functools.partial(_attn_cached_kernel, n_tok, groups),
        grid=(n_batch // seqs,),
        in_specs=[pl.BlockSpec((rows, D_MODEL), lambda i: (i, 0)),
                  pl.BlockSpec((rows, kw), lambda i: (i, k_blk)),
                  pl.BlockSpec((rows, kw), lambda i: (i, k_blk + 1)),
                  cache_spec, cache_spec,
                  pl.BlockSpec((N_KV_HEADS, n_st, LANES), lambda i: (0, 0, 0))],
        out_specs=pl.BlockSpec((rows, D_MODEL), lambda i: (i, 0)),
        out_shape=jax.ShapeDtypeStruct((n_batch * n_tok, D_MODEL), bf16),
        compiler_params=_cparams(1),
        name="swa_cached",
    )(qkv, qkv, qkv, k_cache, v_cache, sink_tab)


def _undup(t):
    return t.reshape(t.shape[0], N_KV_HEADS, 2, HEAD_DIM)[:, :, 0, :]


def _trunk(x, mod_all, tok_major, n_batch, seq_len, pos0, wkv_in, shift_in, k_in, v_in, wts):
    to_seq_major = lambda t: t.reshape(seq_len, n_batch, -1).transpose(1, 0, 2).reshape(n_batch * seq_len, -1)
    to_tok_major = lambda t: t.reshape(n_batch, seq_len, -1).transpose(1, 0, 2).reshape(n_batch * seq_len, -1)
    mod = mod_all[0]
    ffn_w = dict(wts["ffn"])
    x, ffn_w[0, 0] = _ffn(x, mod, tok_major, seq_len, 0, wts["norm_g"][0, 0], ffn_w[0, 0])
    s0 = shift_in[0] if tok_major else jnp.zeros((n_batch, 1, D_MODEL), f32)
    *proj, h_tail = _rwkv_proj(x, mod, tok_major, seq_len, wts["norm_g"][0, 1], s0, wts["rw"])
    if tok_major:
        z, wkv_t = _scan_lanes(proj, wts["rw"], n_batch, seq_len, jnp.transpose(wkv_in[0], (1, 2, 3, 0)))
        wkv_out = jnp.transpose(wkv_t, (3, 0, 1, 2))
        shift_out = h_tail
    else:
        z, wkv_out = _scan(proj, wts["rw"], n_batch, seq_len)
        shift_out = h_tail.reshape(n_batch, -1, 8, D_MODEL)[:, -1, -1]
    x = _proj_res(z, wts["rw"]["wo"], jnp.zeros((D_MODEL,), f32), x, mod, tok_major, seq_len)
    x, ffn_w[0, 1] = _ffn(x, mod, tok_major, seq_len, 6, wts["norm_g"][0, 2], ffn_w[0, 1])
    mod = mod_all[1]
    x, ffn_w[1, 0] = _ffn(x, mod, tok_major, seq_len, 0, wts["norm_g"][1, 0], ffn_w[1, 0])
    positions = pos0 + jnp.arange(seq_len)
    if tok_major:
        positions = jnp.repeat(positions, n_batch)
    cos, sin = _rope_tables(positions)
    qkv = _qkv(x, mod, tok_major, seq_len, wts["norm_g"][1, 1], wts["sw_wqkv"], wts["sw_bqkv"], cos, sin)
    kw = N_KV_HEADS * LANES
    if k_in is None:
        att = _attn_prompt(qkv, wts["sw_sink"], n_batch, seq_len)
        tail = qkv.reshape(n_batch, seq_len, QKV_COLS)[:, -WINDOW:, D_MODEL:]
        k_new = _undup(tail[..., :kw].reshape(n_batch * WINDOW, kw)).reshape(n_batch, WINDOW, N_KV_HEADS, HEAD_DIM)
        v_new = _undup(tail[..., kw:].reshape(n_batch * WINDOW, kw)).reshape(n_batch, WINDOW, N_KV_HEADS, HEAD_DIM)
    else:
        win = k_in.shape[2]
        qkv = to_seq_major(qkv)
        cache_t = lambda t: jnp.transpose(t[0], (0, 2, 3, 1)).reshape(n_batch, N_KV_HEADS * HEAD_DIM, win)
        att = _attn_cached(qkv, cache_t(k_in), cache_t(v_in), wts["sw_sink"], n_batch, seq_len)
        att = to_tok_major(att)
        k_tok = _undup(qkv[:, D_MODEL:D_MODEL + kw]).reshape(n_batch, seq_len, N_KV_HEADS, HEAD_DIM)
        v_tok = _undup(qkv[:, D_MODEL + kw:]).reshape(n_batch, seq_len, N_KV_HEADS, HEAD_DIM)
        k_new = jnp.concatenate([k_in[0], k_tok], axis=1)[:, -win:]
        v_new = jnp.concatenate([v_in[0], v_tok], axis=1)[:, -win:]
    x = _proj_res(att, wts["sw_wo"], wts["sw_bo"], x, mod, tok_major, seq_len)
    y, ffn_w[1, 1] = _ffn(x, mod, tok_major, seq_len, 6, wts["norm_g"][1, 2], ffn_w[1, 1], final_g=wts["final_g"])
    return (y, wkv_out[None], shift_out[None], k_new[None], v_new[None]), ffn_w


def _dup_heads(w):
    lead = w.shape[:-1]
    w4 = w.reshape(lead + (N_KV_HEADS, 1, HEAD_DIM))
    return jnp.broadcast_to(w4, lead + (N_KV_HEADS, 2, HEAD_DIM)).reshape(lead + (N_KV_HEADS * LANES,))


def _pad_cols(w):
    return jnp.pad(w, ((0, 0), (0, LORA_PAD - w.shape[1])))


def _pad_rows(w):
    return jnp.pad(w, ((0, LORA_PAD - w.shape[0]), (0, 0)))


def kernel(x_prompt, x_sample, state_rwkv_wkv, state_rwkv_shift, cache_swa_k, cache_swa_v, c_prompt, c_sample, norm_g, w_ada, b_ada, w_ffn_in, w_ffn_out, rw_mu, rw_wrkv, rw_w0, rw_w1, rw_w2, rw_a0, rw_a1, rw_a2, rw_g1, rw_g2, rw_kk, rw_ka, rw_rk, rw_gn_w, rw_gn_b, rw_wo, sw_wqkv, sw_bqkv, sw_sink, sw_wo, sw_bo, final_g):
    n_p, seq_p, _ = x_prompt.shape
    n_s, seq_s, _ = x_sample.shape
    nq = N_HEADS * HEAD_DIM
    nkv = N_KV_HEADS * HEAD_DIM
    row = lambda t: t.reshape(1, D_MODEL)
    rw = dict(
        mu=rw_mu[0], wrkv=rw_wrkv[0].astype(bf16),
        w1=_pad_cols(rw_w1[0]).astype(bf16), a1=_pad_cols(rw_a1[0]).astype(bf16), g1=rw_g1[0].astype(bf16),
        w2=_pad_rows(rw_w2[0]).astype(bf16), a2=_pad_rows(rw_a2[0]).astype(bf16), g2=rw_g2[0].astype(bf16),
        w0=row(rw_w0[0]), a0=row(rw_a0[0]), kk=row(rw_kk[0]), ka=row(rw_ka[0]), rk=row(rw_rk[0]),
        gn_w=row(rw_gn_w[0]), gn_b=row(rw_gn_b[0]), wo=rw_wo[0].astype(bf16))
    wq = sw_wqkv[0]
    bq = sw_bqkv[0]
    wts = dict(
        norm_g=norm_g, final_g=final_g,
        ffn={(l, s): (w_ffn_in, w_ffn_out, l, s) for l in range(2) for s in range(2)}, rw=rw,
        sw_wqkv=jnp.concatenate([wq[:, :nq], _dup_heads(wq[:, nq:nq + nkv]), _dup_heads(wq[:, nq + nkv:])],
                                axis=1).astype(bf16),
        sw_bqkv=jnp.concatenate([bq[:nq], _dup_heads(bq[nq:nq + nkv]), _dup_heads(bq[nq + nkv:])]),
        sw_sink=sw_sink[0], sw_wo=sw_wo[0].astype(bf16), sw_bo=sw_bo[0])

    n_c = n_p + n_s
    pad = (-n_c) % 8
    c_all = jnp.concatenate([c_sample, c_prompt, jnp.zeros((pad, D_MODEL), f32)], axis=0)
    mod = _ada(c_all, w_ada, b_ada)
    mod_s = [mod[l] for l in range(mod.shape[0])]
    mod_p = [mod[l, n_s:n_c].reshape(n_p, 1, N_MOD * D_MODEL) for l in range(mod.shape[0])]

    (y_s, s_wkv, s_shift, s_k, s_v), wts["ffn"] = _trunk(
        jnp.transpose(x_sample, (1, 0, 2)).reshape(n_s * seq_s, D_MODEL), mod_s, True, n_s, seq_s, PAST_LEN,
        state_rwkv_wkv, state_rwkv_shift, cache_swa_k, cache_swa_v, wts)
    (y_p, p_wkv, p_shift, p_k, p_v), _ = _trunk(
        x_prompt.reshape(n_p * seq_p, D_MODEL), mod_p, False, n_p, seq_p, 0, None, None, None, None, wts)
    y_s = jnp.transpose(y_s.reshape(seq_s, n_s, D_MODEL), (1, 0, 2))
    return (y_p.reshape(n_p, seq_p, D_MODEL), y_s, p_wkv, p_shift, p_k, p_v, s_wkv, s_shift, s_k, s_v)
```

```python
import functools
import math

import jax
import jax.numpy as jnp
from jax import lax
from jax.experimental import pallas as pl
from jax.experimental.pallas import tpu as pltpu

f32 = jnp.float32
bf16 = jnp.bfloat16

D_MODEL = 2048
HEAD_DIM = 64
N_HEADS = D_MODEL // HEAD_DIM
N_KV_HEADS = 4
GQA_GROUP = N_HEADS // N_KV_HEADS
WINDOW = 128
ATTN_SCALE = HEAD_DIM ** -0.5
ROPE_THETA = 10000.0
D_FF = 5632
FFN_TILE = 512
N_MOD = 9
RMS_EPS = 1e-6
GN_EPS = 64e-5
PAST_LEN = 8192
LANES = 128
MXU_COLS = 256
N_PAIRS = D_MODEL // LANES
LORA_PAD = 128
SLAB = 64
SCAN_CHAINS = 16
VMEM_LIMIT = 56 * 1024 * 1024
NEG_BIG = -1e30
ROW_TILE = 512


def _cparams(n_axes):
    return pltpu.CompilerParams(dimension_semantics=("arbitrary",) * n_axes,
                                vmem_limit_bytes=VMEM_LIMIT)


def _dot(a, b):
    return jnp.dot(a, b, preferred_element_type=f32)


def _dot_nt(a, b):
    return lax.dot_general(a, b, (((1,), (1,)), ((), ())), preferred_element_type=f32)


def _split(x, n):
    if x.dtype == bf16:
        return [x]
    parts = []
    rem = x
    for i in range(n):
        p = rem.astype(bf16)
        parts.append(p)
        if i + 1 < n:
            rem = rem - p.astype(f32)
    return parts


def _mm(a, b, pa=1, pb=1, nt=False):
    a_parts = _split(a, pa)
    b_parts = _split(b, pb)
    order = max(len(a_parts), len(b_parts))
    acc = None
    for i, x in enumerate(a_parts):
        for j, y in enumerate(b_parts):
            if i + j >= order:
                continue
            t = _dot_nt(x, y) if nt else _dot(x, y)
            acc = t if acc is None else acc + t
    return acc


def _rms_mod(x, g, shift, scale):
    return x * lax.rsqrt(jnp.mean(x * x, axis=-1, keepdims=True) + RMS_EPS) * (g * (1.0 + scale)) + shift


def _mod_spec(tok_major, tm, seq_len, idx):
    if tok_major:
        return pl.BlockSpec((tm // seq_len, D_MODEL), lambda i, *_: (0, idx))
    return pl.BlockSpec((None, 1, D_MODEL), lambda i, *_: ((i * tm) // seq_len, 0, idx))


def _rows(v, tm):
    n = v.shape[0]
    return v if n in (1, tm) else jnp.concatenate([v] * (tm // n), axis=0)


def _ada_kernel(c_ref, w_ref, b_ref, o_ref):
    c = c_ref[...]
    s = (c * jax.nn.sigmoid(c)).astype(bf16)
    o_ref[...] = _dot(s, w_ref[...].astype(bf16)) + b_ref[...]


def _ada(c_all, w_ada, b_ada):
    n_layers, _, n_out = w_ada.shape
    rows = c_all.shape[0]
    tn = 1024
    return pl.pallas_call(
        _ada_kernel,
        grid=(n_layers, n_out // tn),
        in_specs=[pl.BlockSpec((rows, D_MODEL), lambda l, j: (0, 0)),
                  pl.BlockSpec((None, D_MODEL, tn), lambda l, j: (l, 0, j)),
                  pl.BlockSpec((None, 1, tn), lambda l, j: (l, 0, j))],
        out_specs=pl.BlockSpec((None, rows, tn), lambda l, j: (l, 0, j)),
        out_shape=jax.ShapeDtypeStruct((n_layers, rows, n_out), f32),
        compiler_params=_cparams(2),
        name="ada_mod",
    )(c_all, w_ada, b_ada.reshape(n_layers, 1, n_out))


def _ffn_kernel(final_norm, emit_bf16, tiled_gate, x_ref, ng_ref, sh_ref, sc_ref, gt_ref, wg_ref, wu_ref, wo_ref,
                fg_ref, o_ref, *rest):
    rest = list(rest)
    if emit_bf16:
        wg_out, wu_out, wo_out = rest[:3]
        del rest[:3]
    h_scr = rest.pop(0)
    gt_scr = rest.pop(0) if tiled_gate else None
    f = pl.program_id(1)
    tm = x_ref.shape[0]

    @pl.when(f == 0)
    def _():
        h_scr[...] = _rms_mod(x_ref[...], ng_ref[...], _rows(sh_ref[...], tm), _rows(sc_ref[...], tm)).astype(bf16)
        o_ref[...] = x_ref[...]
        if tiled_gate:
            gt_scr[...] = 0.5 * _rows(gt_ref[...], tm)

    wg, wu, wo = wg_ref[...], wu_ref[...], wo_ref[...]
    if emit_bf16:
        wg, wu, wo = wg.astype(bf16), wu.astype(bf16), wo.astype(bf16)
        wg_out[...] = wg
        wu_out[...] = wu
        wo_out[...] = wo
    h = h_scr[...]
    gate = _dot(h, wg)
    up = _dot(h, wu)
    act = (gate * jax.nn.sigmoid(gate) * up).astype(bf16)
    half_gate = gt_scr[...] if tiled_gate else 0.5 * gt_ref[...]
    o_ref[...] += half_gate * _dot(act, wo)

    if final_norm:
        @pl.when(f == pl.num_programs(1) - 1)
        def _():
            y = o_ref[...]
            o_ref[...] = y * lax.rsqrt(jnp.mean(y * y, axis=-1, keepdims=True) + RMS_EPS) * fg_ref[...]


def _ffn(x, mod, tok_major, seq_len, mod_base, norm_g, weights, final_g=None):
    m = x.shape[0]
    tm = min(ROW_TILE, m)
    emit = len(weights) == 4
    tf = FFN_TILE // 2 if emit else FFN_TILE
    nf = D_FF // tf
    vec = pl.BlockSpec((1, D_MODEL), lambda i, j: (0, 0))
    fg = jnp.ones((1, D_MODEL), f32) if final_g is None else final_g.reshape(1, D_MODEL)
    per_tile = FFN_TILE // tf
    w_in_spec = pl.BlockSpec((None, D_MODEL, tf), lambda i, j: (j // per_tile, 0, j % per_tile))
    w_out_spec = pl.BlockSpec((tf, D_MODEL), lambda i, j: (j, 0))
    out_specs = [pl.BlockSpec((tm, D_MODEL), lambda i, j: (i, 0))]
    out_shape = [jax.ShapeDtypeStruct((m, D_MODEL), f32)]
    if emit:
        assert m == tm, "the weight copies are written once, by a single row tile"
        w_ffn_in, w_ffn_out, layer, slot = weights
        w_args = (w_ffn_in, w_ffn_in, w_ffn_out)
        w_specs = [pl.BlockSpec((None, None, D_MODEL, tf), lambda i, j: (layer, slot, 0, j)),
                   pl.BlockSpec((None, None, D_MODEL, tf), lambda i, j: (layer, slot, 0, j + nf)),
                   pl.BlockSpec((None, None, tf, D_MODEL), lambda i, j: (layer, slot, j, 0))]
        out_specs += [w_in_spec, w_in_spec, w_out_spec]
        tiled = jax.ShapeDtypeStruct((D_FF // FFN_TILE, D_MODEL, FFN_TILE), bf16)
        out_shape += [tiled, tiled, jax.ShapeDtypeStruct((D_FF, D_MODEL), bf16)]
    else:
        w_args = weights
        w_specs = [w_in_spec, w_in_spec, w_out_spec]
    outs = pl.pallas_call(
        functools.partial(_ffn_kernel, final_g is not None, emit, tok_major),
        grid=(m // tm, nf),
        in_specs=[pl.BlockSpec((tm, D_MODEL), lambda i, j: (i, 0)),
                  vec,
                  _mod_spec(tok_major, tm, seq_len, mod_base),
                  _mod_spec(tok_major, tm, seq_len, mod_base + 1),
                  _mod_spec(tok_major, tm, seq_len, mod_base + 2)] + w_specs + [vec],
        out_specs=out_specs,
        out_shape=out_shape,
        scratch_shapes=[pltpu.VMEM((tm, D_MODEL), bf16)] + ([pltpu.VMEM((tm, D_MODEL), f32)] if tok_major else []),
        compiler_params=_cparams(2),
        name="ffn",
    )(x, norm_g.reshape(1, D_MODEL), mod, mod, mod, *w_args, fg)
    return (outs[0], tuple(outs[1:])) if emit else (outs[0], weights)


def _rwkv_proj_kernel(seq_len, tm, h_rows, tok_major,
                      x_ref, ng_ref, sh_ref, sc_ref, s0_ref, mu_ref, w1_ref, a1_ref, g1_ref,
                      w_ref, w2_ref, a2_ref, g2_ref, w0_ref, a0_ref,
                      r_ref, ld_ref, k_ref, v_ref, a_ref, g_ref, h_ref,
                      hs, xr, xk, xv, tw, ta, sg):
    i = pl.program_id(0)
    j = pl.program_id(1)

    @pl.when(j == 0)
    def _():
        h = _rms_mod(x_ref[...], ng_ref[...], _rows(sh_ref[...], tm), _rows(sc_ref[...], tm))
        h_ref[...] = h[tm - h_rows:tm, :]
        if tok_major:
            n_seq = s0_ref.shape[0]
            prev = jnp.concatenate([s0_ref[...], h[0:tm - n_seq, :]], axis=0)
        else:
            @pl.when(i == 0)
            def _():
                hs[0:8, :] = jnp.zeros((8, D_MODEL), f32)

            @pl.when(i > 0)
            def _():
                hs[0:8, :] = hs[tm:tm + 8, :]

            hs[8:tm + 8, :] = h
            row = i * tm + lax.broadcasted_iota(jnp.int32, (tm, 1), 0)
            prev = jnp.where(row % seq_len == 0, s0_ref[...], hs[7:tm + 7, :])
        xx = prev - h
        mu = mu_ref[...]
        xr[...] = (h + xx * mu[0:1, :]).astype(bf16)
        xk[...] = (h + xx * mu[2:3, :]).astype(bf16)
        xv[...] = (h + xx * mu[3:4, :]).astype(bf16)
        xw = (h + xx * mu[1:2, :]).astype(bf16)
        tw[...] = jnp.tanh(_dot(xw, w1_ref[...])).astype(bf16)
        xa = (h + xx * mu[4:5, :]).astype(bf16)
        ta[...] = _dot(xa, a1_ref[...]).astype(bf16)
        xg = (h + xx * mu[5:6, :]).astype(bf16)
        sg[...] = jax.nn.sigmoid(_dot(xg, g1_ref[...])).astype(bf16)

    r_ref[...] = _dot(xr[...], w_ref[0, j])
    k_ref[...] = _dot(xk[...], w_ref[1, j])
    v_ref[...] = _dot(xv[...], w_ref[2, j])
    z = w0_ref[...] + _dot(tw[...], w2_ref[...])
    ld_ref[...] = -jax.nn.sigmoid(z) * math.exp(-0.5)
    a_ref[...] = jax.nn.sigmoid(a0_ref[...] + _dot(ta[...], a2_ref[...]))
    g_ref[...] = _dot(sg[...], g2_ref[...])


def _rwkv_proj(x, mod, tok_major, seq_len, norm_g, s0, rwp):
    m = x.shape[0]
    tm = min(ROW_TILE, m)
    tn = 256
    h_rows = tm // seq_len if tok_major else 8
    full = lambda shape: pl.BlockSpec(shape, lambda i, j: (0,) * len(shape))
    col = lambda rows: pl.BlockSpec((rows, tn), lambda i, j: (0, j))
    w_res = pl.BlockSpec((3, D_MODEL // tn, D_MODEL, tn), lambda i, j: (0, 0, 0, 0), pipeline_mode=pl.Buffered(1))
    if tok_major:
        assert m == tm
        s0_spec = pl.BlockSpec((tm // seq_len, D_MODEL), lambda i, j: (0, 0))
    else:
        s0_spec = pl.BlockSpec((None, 1, D_MODEL), lambda i, j: ((i * tm) // seq_len, 0, 0))
    out_spec = pl.BlockSpec((tm, tn), lambda i, j: (i, j))
    out_sds = jax.ShapeDtypeStruct((m, D_MODEL), f32)
    outs = pl.pallas_call(
        functools.partial(_rwkv_proj_kernel, seq_len, tm, h_rows, tok_major),
        grid=(m // tm, D_MODEL // tn),
        in_specs=[pl.BlockSpec((tm, D_MODEL), lambda i, j: (i, 0)),
                  full((1, D_MODEL)),
                  _mod_spec(tok_major, tm, seq_len, 3),
                  _mod_spec(tok_major, tm, seq_len, 4),
                  s0_spec,
                  full((6, D_MODEL)),
                  full((D_MODEL, LORA_PAD)), full((D_MODEL, LORA_PAD)), full((D_MODEL, 256)),
                  w_res,
                  col(LORA_PAD), col(LORA_PAD), col(256), col(1), col(1)],
        out_specs=[out_spec] * 6 + [pl.BlockSpec((h_rows, D_MODEL), lambda i, j: (i, 0))],
        out_shape=[out_sds] * 6 + [jax.ShapeDtypeStruct((m // tm * h_rows, D_MODEL), f32)],
        scratch_shapes=[pltpu.VMEM((tm + 8, D_MODEL), f32),
                        pltpu.VMEM((tm, D_MODEL), bf16), pltpu.VMEM((tm, D_MODEL), bf16),
                        pltpu.VMEM((tm, D_MODEL), bf16),
                        pltpu.VMEM((tm, LORA_PAD), bf16), pltpu.VMEM((tm, LORA_PAD), bf16),
                        pltpu.VMEM((tm, 256), bf16)],
        compiler_params=_cparams(2),
        name="rwkv_proj",
    )(x, norm_g.reshape(1, D_MODEL), mod, mod, s0, rwp["mu"], rwp["w1"], rwp["a1"], rwp["g1"],
      rwp["wrkv"].reshape(3, D_MODEL, D_MODEL // tn, tn).transpose(0, 2, 1, 3),
      rwp["w2"], rwp["a2"], rwp["g2"], rwp["w0"], rwp["a0"])
    return outs


def _scan_chains(acts, params, states):
    n_st = 2 * SLAB
    row = lax.broadcasted_iota(jnp.int32, (n_st, LANES), 0)
    lane = lax.broadcasted_iota(jnp.int32, (n_st, LANES), 1)
    own = (row // SLAB) == (lane // HEAD_DIM)
    blk = (row // SLAB) == (lane // SLAB)
    strict = blk & (lane < row)
    incl = blk & (lane <= row)
    eye = jnp.where(row == lane, 1.0, 0.0)
    twice = lambda t: jnp.concatenate([t, t], axis=0)
    st = lambda t: jnp.where(own, twice(t), 0.0)
    each = lambda fn, *lists: [fn(*args) for args in zip(*lists)]

    r, ld, k, v, a, g = [[act[i] for act in acts] for i in range(6)]
    kkp, kap, rkp, gnw, gnb = [[par[i] for par in params] for i in range(5)]

    r64 = lax.broadcasted_iota(jnp.int32, (SLAB, SLAB), 0)
    c64 = lax.broadcasted_iota(jnp.int32, (SLAB, SLAB), 1)
    tri_ones = jnp.concatenate([jnp.where(c64 <= r64, 1.0, 0.0),
                                jnp.ones((SLAB, SLAB), f32)], axis=0).astype(bf16)
    sums = each(lambda t: _mm(tri_ones, t, pb=3), ld)
    cs = each(lambda t: t[0:SLAB], sums)
    tot = each(lambda t: t[SLAB:n_st], sums)

    kk_raw = each(lambda t, p: st(t * p), k, kkp)
    kk = each(lambda t: t * (1.0 / jnp.maximum(jnp.sqrt(jnp.sum(t * t, axis=1, keepdims=True)), 1e-12)), kk_raw)
    k2 = each(lambda kt, at, p: st(kt * (1.0 + (at - 1.0) * p)), k, a, kap)
    b = each(lambda t, at: t * twice(at), kk, a)
    r_s = each(st, r)
    v_s = each(st, v)
    bonus = each(lambda rt, kt, p, vt: jnp.sum(rt * kt * p, axis=1, keepdims=True) * vt, r_s, k2, rkp, v_s)

    e_neg = each(lambda c_: twice(jnp.exp(-c_)), cs)
    e_tail = each(lambda t_, c_: twice(jnp.exp(t_ - c_)), tot, cs)
    a_t = each(lambda t, c_, l_: -t * twice(jnp.exp(c_ - l_)), kk, cs, ld)
    r_t = each(lambda t, c_: t * twice(jnp.exp(c_)), r_s, cs)
    k_t = each(lambda t, e: t * e, k2, e_neg)
    b_t = each(lambda t, e: t * e, b, e_neg)
    k_h = each(lambda t, e: t * e, k2, e_tail)
    b_h = each(lambda t, e: t * e, b, e_tail)

    lhs1 = each(lambda x_, y_: jnp.concatenate([x_, y_], axis=0), a_t, r_t)
    rhs1 = each(lambda x_, y_: jnp.concatenate([x_, y_], axis=0), k_t, b_t)
    x = each(lambda l_, r_: _mm(l_, r_, nt=True), lhs1, rhs1)
    a_k = each(lambda t: jnp.where(strict, t[0:n_st, 0:n_st], 0.0), x)
    a_b = each(lambda t: jnp.where(strict, t[0:n_st, n_st:], 0.0), x)
    r_kb = each(lambda t: jnp.concatenate([jnp.where(incl, t[n_st:, 0:n_st], 0.0),
                                           jnp.where(incl, t[n_st:, n_st:], 0.0)], axis=1), x)

    p1 = each(lambda l_, s_: _mm(l_, s_, nt=True), lhs1, states)
    rhs_u = each(lambda p_, ak, vt: p_[0:n_st] + _mm(ak, vt), p1, a_k, v_s)

    near = (row // 2) == (lane // 2)
    t_inv = each(lambda t: eye + jnp.where(near, t, 0.0), a_b)
    s = 2
    while s < SLAB:
        off = ((row // (2 * s)) == (lane // (2 * s))) & ((row // s) != (lane // s))
        w = each(lambda ab, ti: _mm(jnp.where(off, ab, 0.0), ti), a_b, t_inv)
        t_inv = each(lambda ti, w_: ti + _mm(ti, w_), t_inv, w)
        s *= 2
    u = each(_mm, t_inv, rhs_u)

    vu = each(lambda x_, y_: jnp.concatenate([x_, y_], axis=0), v_s, u)
    y = each(lambda p_, rk_, vu_: p_[n_st:] + _mm(rk_, vu_), p1, r_kb, vu)

    def finish(y_, w_, b_, bonus_, g_):
        mean = jnp.sum(y_, axis=1, keepdims=True) * (1.0 / HEAD_DIM)
        dev = jnp.where(own, y_ - mean, 0.0)
        var = jnp.sum(dev * dev, axis=1, keepdims=True) * (1.0 / HEAD_DIM)
        out = jnp.where(own, dev * lax.rsqrt(var + GN_EPS) * w_ + b_, 0.0) + bonus_
        return (out[0:SLAB] + out[SLAB:n_st]) * g_
    z = each(finish, y, gnw, gnb, bonus, g)

    vu_t = each(lambda t: t.T, vu)
    kb_h = each(lambda x_, y_: jnp.concatenate([x_, y_], axis=0), k_h, b_h)
    new_states = each(lambda s_, t_, vt, kb: s_ * jnp.exp(t_[0:1, :]) + _mm(vt, kb), states, tot, vu_t, kb_h)
    return z, new_states


def _scan_kernel(n_seq, n_pp, r_ref, ld_ref, k_ref, v_ref, a_ref, g_ref, kk_ref, ka_ref, rk_ref, gw_ref,
                 gb_ref, z_ref, s_out_ref, s_scr):
    c = pl.program_id(1)

    @pl.when(c == 0)
    def _():
        s_scr[...] = jnp.zeros_like(s_scr)

    chains = [(b, pp) for b in range(n_seq) for pp in range(n_pp)]
    lanes = [slice(pp * LANES, (pp + 1) * LANES) for _, pp in chains]
    acts = [tuple(ref[b, :, sl] for ref in (r_ref, ld_ref, k_ref, v_ref, a_ref, g_ref))
            for (b, _), sl in zip(chains, lanes)]
    params = [tuple(ref[:, sl] for ref in (kk_ref, ka_ref, rk_ref, gw_ref, gb_ref)) for sl in lanes]
    states = [s_scr[n] for n in range(len(chains))]
    z, new_states = _scan_chains(acts, params, states)
    for n, (b, _) in enumerate(chains):
        z_ref[b, :, lanes[n]] = z[n].astype(bf16)
        s_scr[n] = new_states[n]

    @pl.when(c == pl.num_programs(1) - 1)
    def _():
        for n, (b, pp) in enumerate(chains):
            s = s_scr[n]
            s_out_ref[b, 2 * pp] = s[0:HEAD_DIM, 0:HEAD_DIM]
            s_out_ref[b, 2 * pp + 1] = s[HEAD_DIM:, HEAD_DIM:]


def _scan(proj, rwp, n_batch, seq_len):
    n_pp = SCAN_CHAINS // n_batch
    w = LANES * n_pp
    act = pl.BlockSpec((n_batch, SLAB, w), lambda p, c: (0, c, p))
    vec = pl.BlockSpec((1, w), lambda p, c: (0, p))
    st_spec = pl.BlockSpec((n_batch, 2 * n_pp, HEAD_DIM, HEAD_DIM), lambda p, c: (0, p, 0, 0))
    z, s_out = pl.pallas_call(
        functools.partial(_scan_kernel, n_batch, n_pp),
        grid=(N_PAIRS // n_pp, seq_len // SLAB),
        in_specs=[act] * 6 + [vec] * 5,
        out_specs=[act, st_spec],
        out_shape=[jax.ShapeDtypeStruct((n_batch, seq_len, D_MODEL), bf16),
                   jax.ShapeDtypeStruct((n_batch, N_HEADS, HEAD_DIM, HEAD_DIM), f32)],
        scratch_shapes=[pltpu.VMEM((n_batch * n_pp, 2 * SLAB, LANES), f32)],
        compiler_params=_cparams(2),
        name="rwkv_scan",
    )(*[p.reshape(n_batch, seq_len, D_MODEL) for p in proj],
      rwp["kk"], rwp["ka"], rwp["rk"], rwp["gn_w"], rwp["gn_b"])
    return z.reshape(n_batch * seq_len, D_MODEL), s_out


def _scan_lanes_kernel(n_tok, r_ref, ld_ref, k_ref, v_ref, a_ref, g_ref, kk_ref, ka_ref, rk_ref, gw_ref, gb_ref,
                       s_in_ref, z_ref, s_out_ref, kk_s, w_s, b_s, k2_s, r_s, v_s, y_s):
    hd = HEAD_DIM
    colsum = lambda t: jnp.sum(t, axis=0, keepdims=True)
    for t in range(n_tok):
        r_t, k_t, v_t, a_t = r_ref[t].T, k_ref[t].T, v_ref[t].T, a_ref[t].T
        kk_raw = k_t * kk_ref[...]
        k2 = k_t * (1.0 + (a_t - 1.0) * ka_ref[...])
        w = jnp.exp(ld_ref[t].T)
        for e in range(2):
            sl = slice(e * hd, (e + 1) * hd)
            kr = kk_raw[sl]
            kk = kr * (1.0 / jnp.maximum(jnp.sqrt(colsum(kr * kr)), 1e-12))
            kk_s[e, t] = kk
            b_s[e, t] = kk * a_t[sl]
            k2_s[e, t] = k2[sl]
            w_s[e, t] = w[sl]
            r_s[e, t] = r_t[sl]
            v_s[e, t] = v_t[sl]

    rows_per_iter = 2

    def per_rows(i, carry):
        chains = [(e, rows_per_iter * i + d) for e in range(2) for d in range(rows_per_iter)]
        s = [s_in_ref[e, vi] for e, vi in chains]
        for t in range(n_tok):
            for c, (e, vi) in enumerate(chains):
                s_kk = colsum(s[c] * kk_s[e, t])
                s[c] = s[c] * w_s[e, t] - s_kk * b_s[e, t] + v_s[e, t, pl.ds(vi, 1), :] * k2_s[e, t]
                y_s[e, t, pl.ds(vi, 1), :] = colsum(s[c] * r_s[e, t])
        for c, (e, vi) in enumerate(chains):
            s_out_ref[e, vi] = s[c]
        return carry
    lax.fori_loop(0, hd // rows_per_iter, per_rows, 0)

    for t in range(n_tok):
        outs = []
        for e in range(2):
            sl = slice(e * hd, (e + 1) * hd)
            y = y_s[e, t]
            dev = y - colsum(y) * (1.0 / hd)
            var = colsum(dev * dev) * (1.0 / hd)
            bonus = colsum(r_s[e, t] * k2_s[e, t] * rk_ref[sl, :]) * v_s[e, t]
            outs.append(dev * lax.rsqrt(var + GN_EPS) * gw_ref[sl, :] + gb_ref[sl, :] + bonus)
        z_ref[t] = (jnp.concatenate(outs, axis=0).T * g_ref[t]).astype(bf16)


def _scan_lanes(proj, rwp, n_batch, n_tok, state):
    acts = [p.reshape(n_tok, n_batch, D_MODEL) for p in proj]
    lane_bc = lambda v: jnp.broadcast_to(v.reshape(D_MODEL, 1), (D_MODEL, n_batch))
    params = [lane_bc(rwp[n]) for n in ("kk", "ka", "rk", "gn_w", "gn_b")]
    act = pl.BlockSpec((n_tok, n_batch, LANES), lambda p: (0, 0, p))
    par = pl.BlockSpec((LANES, n_batch), lambda p: (p, 0))
    st_spec = pl.BlockSpec((2, HEAD_DIM, HEAD_DIM, n_batch), lambda p: (p, 0, 0, 0))
    per_tok = pltpu.VMEM((2, n_tok, HEAD_DIM, n_batch), f32)
    z, s_out = pl.pallas_call(
        functools.partial(_scan_lanes_kernel, n_tok),
        grid=(N_PAIRS,),
        in_specs=[act] * 6 + [par] * 5 + [st_spec],
        out_specs=[act, st_spec],
        out_shape=[jax.ShapeDtypeStruct((n_tok, n_batch, D_MODEL), bf16),
                   jax.ShapeDtypeStruct(state.shape, f32)],
        scratch_shapes=[per_tok] * 7,
        compiler_params=_cparams(1),
        name="rwkv_decode_scan",
    )(*acts, *params, state)
    return z.reshape(n_tok * n_batch, D_MODEL), s_out


def _proj_res_kernel(z_ref, w_ref, b_ref, x_ref, gt_ref, o_ref):
    out = _dot(z_ref[...], w_ref[...]) + b_ref[...]
    o_ref[...] = x_ref[...] + _rows(gt_ref[...], x_ref.shape[0]) * out


def _proj_res(z, w, bias, x, mod, tok_major, seq_len):
    m = x.shape[0]
    tm = min(ROW_TILE, m)
    rows = pl.BlockSpec((tm, D_MODEL), lambda i: (i, 0))
    return pl.pallas_call(
        _proj_res_kernel,
        grid=(m // tm,),
        in_specs=[rows,
                  pl.BlockSpec((D_MODEL, D_MODEL), lambda i: (0, 0), pipeline_mode=pl.Buffered(1)),
                  pl.BlockSpec((1, D_MODEL), lambda i: (0, 0)),
                  rows,
                  _mod_spec(tok_major, tm, seq_len, 5)],
        out_specs=rows,
        out_shape=jax.ShapeDtypeStruct((m, D_MODEL), f32),
        compiler_params=_cparams(1),
        name="proj_res",
    )(z, w, bias.reshape(1, D_MODEL), x, mod)


QKV_COLS = D_MODEL + 2 * N_KV_HEADS * LANES


def _qkv_kernel(x_ref, ng_ref, sh_ref, sc_ref, w_ref, b_ref, cos_ref, sin_ref, o_ref, h_scr):
    @pl.when(pl.program_id(1) == 0)
    def _():
        tm = x_ref.shape[0]
        h_scr[...] = _rms_mod(x_ref[...], ng_ref[...], _rows(sh_ref[...], tm), _rows(sc_ref[...], tm)).astype(bf16)

    cos = cos_ref[...]
    sin = sin_ref[...]
    first = (lax.broadcasted_iota(jnp.int32, cos.shape, 1) % HEAD_DIM) < HEAD_DIM // 2
    h = h_scr[...]
    for c0 in range(0, o_ref.shape[1], MXU_COLS):
        acc = _dot(h, w_ref[:, c0:c0 + MXU_COLS]) + b_ref[:, c0:c0 + MXU_COLS]
        for c in range(0, MXU_COLS, LANES):
            xc = acc[:, c:c + LANES]
            rot = jnp.where(first, pltpu.roll(xc, LANES - HEAD_DIM // 2, 1),
                            pltpu.roll(xc, HEAD_DIM // 2, 1))
            o_ref[:, c0 + c:c0 + c + LANES] = xc * cos + rot * sin


def _qkv(x, mod, tok_major, seq_len, norm_g, w, b, cos, sin):
    m = x.shape[0]
    tm = min(ROW_TILE, m)
    tn = 512
    n_rope = (D_MODEL + N_KV_HEADS * LANES) // tn
    n_pos_blocks = cos.shape[0] // tm
    vec = pl.BlockSpec((1, D_MODEL), lambda i, j: (0, 0))
    tab = pl.BlockSpec((tm, LANES), lambda i, j: (i % n_pos_blocks, j // n_rope))
    return pl.pallas_call(
        _qkv_kernel,
        grid=(m // tm, QKV_COLS // tn),
        in_specs=[pl.BlockSpec((tm, D_MODEL), lambda i, j: (i, 0)),
                  vec,
                  _mod_spec(tok_major, tm, seq_len, 3),
                  _mod_spec(tok_major, tm, seq_len, 4),
                  pl.BlockSpec((D_MODEL, tn), lambda i, j: (0, j)),
                  pl.BlockSpec((1, tn), lambda i, j: (0, j)),
                  tab, tab],
        out_specs=pl.BlockSpec((tm, tn), lambda i, j: (i, j)),
        out_shape=jax.ShapeDtypeStruct((m, QKV_COLS), f32),
        scratch_shapes=[pltpu.VMEM((tm, D_MODEL), bf16)],
        compiler_params=_cparams(2),
        name="swa_qkv",
    )(x, norm_g.reshape(1, D_MODEL), mod, mod, w, b.reshape(1, QKV_COLS), cos, sin)


def _rope_tables(positions):
    half = HEAD_DIM // 2
    inv_freq = ROPE_THETA ** (-jnp.arange(half, dtype=f32) / half)
    ang = positions.astype(f32)[:, None] * inv_freq[None, :]
    cos = jnp.tile(jnp.cos(ang), (1, LANES // half))
    sin = jnp.sin(ang)
    sin = jnp.tile(jnp.concatenate([-sin, sin], axis=1), (1, LANES // HEAD_DIM))
    return (jnp.concatenate([cos, jnp.ones_like(cos)], axis=1),
            jnp.concatenate([sin, jnp.zeros_like(sin)], axis=1))


def _head_masks(rows):
    lane = lax.broadcasted_iota(jnp.int32, (rows, LANES), 1)
    return lane < HEAD_DIM, lane >= HEAD_DIM


def _attn_prompt_kernel(q_ref, kp_ref, kc_ref, vp_ref, vc_ref, sink_ref, o_ref):
    n = pl.program_id(1)
    blk = WINDOW
    m0, m1 = _head_masks(blk)
    m0k, m1k = _head_masks(2 * blk)
    qi = lax.broadcasted_iota(jnp.int32, (2 * blk, 2 * blk), 0) % blk
    sj = lax.broadcasted_iota(jnp.int32, (2 * blk, 2 * blk), 1)
    visible = (sj > qi) & (sj <= qi + blk) & ((n > 0) | (sj >= blk))
    top = lax.broadcasted_iota(jnp.int32, (2 * blk, 1), 0) < blk
    for c in range(N_KV_HEADS):
        sl = slice(c * LANES, (c + 1) * LANES)
        kd = jnp.concatenate([kp_ref[:, sl], kc_ref[:, sl]], axis=0).astype(bf16)
        vd = jnp.concatenate([vp_ref[:, sl], vc_ref[:, sl]], axis=0)
        vcat = jnp.concatenate([jnp.where(m0k, vd, 0.0), jnp.where(m1k, vd, 0.0)], axis=0).astype(bf16)
        pairs = [c * (GQA_GROUP // 2) + jj for jj in range(GQA_GROUP // 2)]
        each = lambda fn, *lists: [fn(*args) for args in zip(*lists)]
        qp = [q_ref[:, pair * LANES:(pair + 1) * LANES] * ATTN_SCALE for pair in pairs]
        qs = each(lambda t: jnp.concatenate([jnp.where(m0, t, 0.0), jnp.where(m1, t, 0.0)], axis=0).astype(bf16), qp)
        s = each(lambda t: jnp.where(visible, _dot_nt(t, kd), NEG_BIG), qs)
        sk = [jnp.where(top, sink_ref[2 * pair], sink_ref[2 * pair + 1]) for pair in pairs]
        mx = each(lambda t, k_: jnp.maximum(jnp.max(t, axis=1, keepdims=True), k_), s, sk)
        p = each(lambda t, m_: jnp.exp(t - m_), s, mx)
        den = each(lambda t, k_, m_: jnp.sum(t, axis=1, keepdims=True) + jnp.exp(k_ - m_), p, sk, mx)
        p = each(lambda t, d_: (t * (1.0 / d_)).astype(bf16), p, den)
        pcat = each(lambda t: jnp.concatenate([t[0:blk], t[blk:]], axis=1), p)
        for pair, t in zip(pairs, pcat):
            o_ref[:, pair * LANES:(pair + 1) * LANES] = _dot(t, vcat).astype(bf16)


def _attn_prompt(qkv, sink, n_batch, seq_len):
    nb = seq_len // WINDOW
    kw = N_KV_HEADS * LANES
    k_blk = D_MODEL // kw
    cur = lambda off: pl.BlockSpec((WINDOW, kw), lambda b, n: (b * nb + n, k_blk + off))
    prev = lambda off: pl.BlockSpec((WINDOW, kw), lambda b, n: (b * nb + jnp.maximum(n - 1, 0), k_blk + off))
    return pl.pallas_call(
        _attn_prompt_kernel,
        grid=(n_batch, nb),
        in_specs=[pl.BlockSpec((WINDOW, D_MODEL), lambda b, n: (b * nb + n, 0)),
                  prev(0), cur(0), prev(1), cur(1),
                  pl.BlockSpec(memory_space=pltpu.SMEM)],
        out_specs=pl.BlockSpec((WINDOW, D_MODEL), lambda b, n: (b * nb + n, 0)),
        out_shape=jax.ShapeDtypeStruct((n_batch * seq_len, D_MODEL), bf16),
        compiler_params=_cparams(2),
        name="swa_prompt",
    )(qkv, qkv, qkv, qkv, qkv, sink)


SEQ_PER_GROUP = 4


def _attn_cached_kernel(n_tok, groups, q_ref, kn_ref, vn_ref, kc_ref, vc_ref, sink_ref, o_ref):
    rows = SEQ_PER_GROUP * n_tok
    n_st = GQA_GROUP * rows
    m0, m1 = _head_masks(rows)
    srow = lax.broadcasted_iota(jnp.int32, (n_st, 1), 0)
    row_seq = (srow % rows) // n_tok
    row_tok = srow % n_tok
    key_c = lax.broadcasted_iota(jnp.int32, (n_st, WINDOW), 1)
    vis_c = key_c > row_tok
    key_n = lax.broadcasted_iota(jnp.int32, (n_st, rows), 1)
    vis_n = ((key_n // n_tok) == row_seq) & ((key_n % n_tok) <= row_tok)
    for gi in range(groups):
        rs = slice(gi * rows, (gi + 1) * rows)
        for cp in range(N_KV_HEADS // 2):
            for ce in range(2):
                c = 2 * cp + ce
                sl = slice(c * LANES, (c + 1) * LANES)
                kt = [kc_ref[gi * SEQ_PER_GROUP + b, c * HEAD_DIM:(c + 1) * HEAD_DIM, :] for b in range(SEQ_PER_GROUP)]
                vt = [vc_ref[gi * SEQ_PER_GROUP + b, c * HEAD_DIM:(c + 1) * HEAD_DIM, :] for b in range(SEQ_PER_GROUP)]
                kn = kn_ref[rs, sl].astype(bf16)
                vn = vn_ref[rs, sl].astype(bf16)
                pieces = []
                for jj in range(GQA_GROUP // 2):
                    pair = c * (GQA_GROUP // 2) + jj
                    qp = q_ref[rs, pair * LANES:(pair + 1) * LANES]
                    pieces += [jnp.where(m0, qp, 0.0), jnp.where(m1, qp, 0.0)]
                qs = jnp.concatenate(pieces, axis=0).astype(bf16)
                s_c = jnp.zeros((n_st, WINDOW), f32)
                for b in range(SEQ_PER_GROUP):
                    kx = jnp.concatenate([kt[b], kt[b]], axis=0).astype(bf16)
                    s_c = jnp.where(row_seq == b, _dot(qs, kx), s_c)
                s_c = jnp.where(vis_c, s_c * ATTN_SCALE, NEG_BIG)
                s_n = jnp.where(vis_n, _dot_nt(qs, kn) * ATTN_SCALE, NEG_BIG)
                sk = sink_ref[c]
                sk = sk[:, 0:1]
                mx = jnp.maximum(jnp.maximum(jnp.max(s_c, axis=1, keepdims=True),
                                             jnp.max(s_n, axis=1, keepdims=True)), sk)
                p_c = jnp.exp(s_c - mx)
                p_n = jnp.exp(s_n - mx)
                den = (jnp.sum(p_c, axis=1, keepdims=True) + jnp.sum(p_n, axis=1, keepdims=True)
                       + jnp.exp(sk - mx))
                inv_den = 1.0 / den
                p_c = (p_c * inv_den).astype(bf16)
                p_n = (p_n * inv_den).astype(bf16)
                o = _dot(p_n, vn)
                for b in range(SEQ_PER_GROUP):
                    vx = jnp.concatenate([vt[b], vt[b]], axis=0).astype(bf16)
                    o = o + jnp.where(row_seq == b, _dot_nt(p_c, vx), 0.0)
                for jj in range(GQA_GROUP // 2):
                    pair = c * (GQA_GROUP // 2) + jj
                    o0 = o[(2 * jj) * rows:(2 * jj + 1) * rows]
                    o1 = o[(2 * jj + 1) * rows:(2 * jj + 2) * rows]
                    o_ref[rs, pair * LANES:(pair + 1) * LANES] = jnp.where(m0, o0, o1).astype(bf16)


def _attn_cached(qkv, k_cache, v_cache, sink, n_batch, n_tok):
    groups = 2
    seqs = SEQ_PER_GROUP * groups
    rows = seqs * n_tok
    kw = N_KV_HEADS * LANES
    k_blk = D_MODEL // kw
    n_st = GQA_GROUP * SEQ_PER_GROUP * n_tok
    head = (jnp.arange(N_KV_HEADS)[:, None] * GQA_GROUP
            + (jnp.arange(n_st)[None, :] // (SEQ_PER_GROUP * n_tok)))
    sink_tab = jnp.broadcast_to(sink[head][:, :, None], (N_KV_HEADS, n_st, LANES))
    cache_spec = pl.BlockSpec((seqs, N_KV_HEADS * HEAD_DIM, WINDOW), lambda i: (i, 0, 0))
    return pl.pallas_call(
        functools.partial(_attn_cached_kernel, n_tok, groups),
        grid=(n_batch // seqs,),
        in_specs=[pl.BlockSpec((rows, D_MODEL), lambda i: (i, 0)),
                  pl.BlockSpec((rows, kw), lambda i: (i, k_blk)),
                  pl.BlockSpec((rows, kw), lambda i: (i, k_blk + 1)),
                  cache_spec, cache_spec,
                  pl.BlockSpec((N_KV_HEADS, n_st, LANES), lambda i: (0, 0, 0))],
        out_specs=pl.BlockSpec((rows, D_MODEL), lambda i: (i, 0)),
        out_shape=jax.ShapeDtypeStruct((n_batch * n_tok, D_MODEL), bf16),
        compiler_params=_cparams(1),
        name="swa_cached",
    )(qkv, qkv, qkv, k_cache, v_cache, sink_tab)


def _undup(t):
    return t.reshape(t.shape[0], N_KV_HEADS, 2, HEAD_DIM)[:, :, 0, :]


def _trunk(x, mod_all, tok_major, n_batch, seq_len, pos0, wkv_in, shift_in, k_in, v_in, wts):
    to_seq_major = lambda t: t.reshape(seq_len, n_batch, -1).transpose(1, 0, 2).reshape(n_batch * seq_len, -1)
    to_tok_major = lambda t: t.reshape(n_batch, seq_len, -1).transpose(1, 0, 2).reshape(n_batch * seq_len, -1)
    mod = mod_all[0]
    ffn_w = dict(wts["ffn"])
    x, ffn_w[0, 0] = _ffn(x, mod, tok_major, seq_len, 0, wts["norm_g"][0, 0], ffn_w[0, 0])
    s0 = shift_in[0] if tok_major else jnp.zeros((n_batch, 1, D_MODEL), f32)
    *proj, h_tail = _rwkv_proj(x, mod, tok_major, seq_len, wts["norm_g"][0, 1], s0, wts["rw"])
    if tok_major:
        z, wkv_t = _scan_lanes(proj, wts["rw"], n_batch, seq_len, jnp.transpose(wkv_in[0], (1, 2, 3, 0)))
        wkv_out = jnp.transpose(wkv_t, (3, 0, 1, 2))
        shift_out = h_tail
    else:
        z, wkv_out = _scan(proj, wts["rw"], n_batch, seq_len)
        shift_out = h_tail.reshape(n_batch, -1, 8, D_MODEL)[:, -1, -1]
    x = _proj_res(z, wts["rw"]["wo"], jnp.zeros((D_MODEL,), f32), x, mod, tok_major, seq_len)
    x, ffn_w[0, 1] = _ffn(x, mod, tok_major, seq_len, 6, wts["norm_g"][0, 2], ffn_w[0, 1])
    mod = mod_all[1]
    x, ffn_w[1, 0] = _ffn(x, mod, tok_major, seq_len, 0, wts["norm_g"][1, 0], ffn_w[1, 0])
    positions = pos0 + jnp.arange(seq_len)
    if tok_major:
        positions = jnp.repeat(positions, n_batch)
    cos, sin = _rope_tables(positions)
    qkv = _qkv(x, mod, tok_major, seq_len, wts["norm_g"][1, 1], wts["sw_wqkv"], wts["sw_bqkv"], cos, sin)
    kw = N_KV_HEADS * LANES
    if k_in is None:
        att = _attn_prompt(qkv, wts["sw_sink"], n_batch, seq_len)
        tail = qkv.reshape(n_batch, seq_len, QKV_COLS)[:, -WINDOW:, D_MODEL:]
        k_new = _undup(tail[..., :kw].reshape(n_batch * WINDOW, kw)).reshape(n_batch, WINDOW, N_KV_HEADS, HEAD_DIM)
        v_new = _undup(tail[..., kw:].reshape(n_batch * WINDOW, kw)).reshape(n_batch, WINDOW, N_KV_HEADS, HEAD_DIM)
    else:
        win = k_in.shape[2]
        qkv = to_seq_major(qkv)
        cache_t = lambda t: jnp.transpose(t[0], (0, 2, 3, 1)).reshape(n_batch, N_KV_HEADS * HEAD_DIM, win)
        att = _attn_cached(qkv, cache_t(k_in), cache_t(v_in), wts["sw_sink"], n_batch, seq_len)
        att = to_tok_major(att)
        k_tok = _undup(qkv[:, D_MODEL:D_MODEL + kw]).reshape(n_batch, seq_len, N_KV_HEADS, HEAD_DIM)
        v_tok = _undup(qkv[:, D_MODEL + kw:]).reshape(n_batch, seq_len, N_KV_HEADS, HEAD_DIM)
        k_new = jnp.concatenate([k_in[0], k_tok], axis=1)[:, -win:]
        v_new = jnp.concatenate([v_in[0], v_tok], axis=1)[:, -win:]
    x = _proj_res(att, wts["sw_wo"], wts["sw_bo"], x, mod, tok_major, seq_len)
    y, ffn_w[1, 1] = _ffn(x, mod, tok_major, seq_len, 6, wts["norm_g"][1, 2], ffn_w[1, 1], final_g=wts["final_g"])
    return (y, wkv_out[None], shift_out[None], k_new[None], v_new[None]), ffn_w


def _dup_heads(w):
    lead = w.shape[:-1]
    w4 = w.reshape(lead + (N_KV_HEADS, 1, HEAD_DIM))
    return jnp.broadcast_to(w4, lead + (N_KV_HEADS, 2, HEAD_DIM)).reshape(lead + (N_KV_HEADS * LANES,))


def _pad_cols(w):
    return jnp.pad(w, ((0, 0), (0, LORA_PAD - w.shape[1])))


def _pad_rows(w):
    return jnp.pad(w, ((0, LORA_PAD - w.shape[0]), (0, 0)))


def kernel(x_prompt, x_sample, state_rwkv_wkv, state_rwkv_shift, cache_swa_k, cache_swa_v, c_prompt, c_sample, norm_g, w_ada, b_ada, w_ffn_in, w_ffn_out, rw_mu, rw_wrkv, rw_w0, rw_w1, rw_w2, rw_a0, rw_a1, rw_a2, rw_g1, rw_g2, rw_kk, rw_ka, rw_rk, rw_gn_w, rw_gn_b, rw_wo, sw_wqkv, sw_bqkv, sw_sink, sw_wo, sw_bo, final_g):
    n_p, seq_p, _ = x_prompt.shape
    n_s, seq_s, _ = x_sample.shape
    nq = N_HEADS * HEAD_DIM
    nkv = N_KV_HEADS * HEAD_DIM
    row = lambda t: t.reshape(1, D_MODEL)
    rw = dict(
        mu=rw_mu[0], wrkv=rw_wrkv[0].astype(bf16),
        w1=_pad_cols(rw_w1[0]).astype(bf16), a1=_pad_cols(rw_a1[0]).astype(bf16), g1=rw_g1[0].astype(bf16),
        w2=_pad_rows(rw_w2[0]).astype(bf16), a2=_pad_rows(rw_a2[0]).astype(bf16), g2=rw_g2[0].astype(bf16),
        w0=row(rw_w0[0]), a0=row(rw_a0[0]), kk=row(rw_kk[0]), ka=row(rw_ka[0]), rk=row(rw_rk[0]),
        gn_w=row(rw_gn_w[0]), gn_b=row(rw_gn_b[0]), wo=rw_wo[0].astype(bf16))
    wq = sw_wqkv[0]
    bq = sw_bqkv[0]
    wts = dict(
        norm_g=norm_g, final_g=final_g,
        ffn={(l, s): (w_ffn_in, w_ffn_out, l, s) for l in range(2) for s in range(2)}, rw=rw,
        sw_wqkv=jnp.concatenate([wq[:, :nq], _dup_heads(wq[:, nq:nq + nkv]), _dup_heads(wq[:, nq + nkv:])],
                                axis=1).astype(bf16),
        sw_bqkv=jnp.concatenate([bq[:nq], _dup_heads(bq[nq:nq + nkv]), _dup_heads(bq[nq + nkv:])]),
        sw_sink=sw_sink[0], sw_wo=sw_wo[0].astype(bf16), sw_bo=sw_bo[0])

    n_c = n_p + n_s
    pad = (-n_c) % 8
    c_all = jnp.concatenate([c_sample, c_prompt, jnp.zeros((pad, D_MODEL), f32)], axis=0)
    mod = _ada(c_all, w_ada, b_ada)
    mod_s = [mod[l] for l in range(mod.shape[0])]
    mod_p = [mod[l, n_s:n_c].reshape(n_p, 1, N_MOD * D_MODEL) for l in range(mod.shape[0])]

    (y_s, s_wkv, s_shift, s_k, s_v), wts["ffn"] = _trunk(
        jnp.transpose(x_sample, (1, 0, 2)).reshape(n_s * seq_s, D_MODEL), mod_s, True, n_s, seq_s, PAST_LEN,
        state_rwkv_wkv, state_rwkv_shift, cache_swa_k, cache_swa_v, wts)
    (y_p, p_wkv, p_shift, p_k, p_v), _ = _trunk(
        x_prompt.reshape(n_p * seq_p, D_MODEL), mod_p, False, n_p, seq_p, 0, None, None, None, None, wts)
    y_s = jnp.transpose(y_s.reshape(seq_s, n_s, D_MODEL), (1, 0, 2))
    return (y_p.reshape(n_p, seq_p, D_MODEL), y_s, p_wkv, p_shift, p_k, p_v, s_wkv, s_shift, s_k, s_v)
```

```python
import functools
import math

import jax
import jax.numpy as jnp
from jax import lax
from jax.experimental import pallas as pl
from jax.experimental.pallas import tpu as pltpu

f32 = jnp.float32
bf16 = jnp.bfloat16

D_MODEL = 2048
HEAD_DIM = 64
N_HEADS = D_MODEL // HEAD_DIM
N_KV_HEADS = 4
GQA_GROUP = N_HEADS // N_KV_HEADS
WINDOW = 128
ATTN_SCALE = HEAD_DIM ** -0.5
ROPE_THETA = 10000.0
D_FF = 5632
N_MOD = 9
RMS_EPS = 1e-6
GN_EPS = 64e-5
PAST_LEN = 8192
LANES = 128
MXU_COLS = 256
N_PAIRS = D_MODEL // LANES
LORA_PAD = 128
SLAB = 64
SCAN_CHAINS = 16
VMEM_LIMIT = 56 * 1024 * 1024
NEG_BIG = -1e30
ROW_TILE = 512


def _cparams(n_axes):
    return pltpu.CompilerParams(dimension_semantics=("arbitrary",) * n_axes,
                                vmem_limit_bytes=VMEM_LIMIT)


def _dot(a, b):
    return jnp.dot(a, b, preferred_element_type=f32)


def _dot_nt(a, b):
    return lax.dot_general(a, b, (((1,), (1,)), ((), ())), preferred_element_type=f32)


def _split(x, n):
    if x.dtype == bf16:
        return [x]
    parts = []
    rem = x
    for i in range(n):
        p = rem.astype(bf16)
        parts.append(p)
        if i + 1 < n:
            rem = rem - p.astype(f32)
    return parts


def _mm(a, b, pa=1, pb=1, nt=False):
    a_parts = _split(a, pa)
    b_parts = _split(b, pb)
    order = max(len(a_parts), len(b_parts))
    acc = None
    for i, x in enumerate(a_parts):
        for j, y in enumerate(b_parts):
            if i + j >= order:
                continue
            t = _dot_nt(x, y) if nt else _dot(x, y)
            acc = t if acc is None else acc + t
    return acc


def _rms_mod(x, g, shift, scale):
    return x * lax.rsqrt(jnp.mean(x * x, axis=-1, keepdims=True) + RMS_EPS) * (g * (1.0 + scale)) + shift


def _mod_spec(tok_major, tm, seq_len, idx):
    if tok_major:
        return pl.BlockSpec((tm // seq_len, D_MODEL), lambda i, *_: (0, idx))
    return pl.BlockSpec((None, 1, D_MODEL), lambda i, *_: ((i * tm) // seq_len, 0, idx))


def _rows(v, tm):
    n = v.shape[0]
    return v if n in (1, tm) else jnp.concatenate([v] * (tm // n), axis=0)


def _ada_kernel(c_ref, w_ref, b_ref, *o_refs):
    c = c_ref[...]
    s = (c * jax.nn.sigmoid(c)).astype(bf16)
    for layer, o_ref in enumerate(o_refs):
        o_ref[...] = _dot(s, w_ref[layer].astype(bf16)) + b_ref[layer]


def _ada(c_all, w_ada, b_ada):
    n_layers, _, n_out = w_ada.shape
    rows = c_all.shape[0]
    tn = 1024
    return pl.pallas_call(
        _ada_kernel,
        grid=(n_out // tn,),
        in_specs=[pl.BlockSpec((rows, D_MODEL), lambda j: (0, 0)),
                  pl.BlockSpec((n_layers, D_MODEL, tn), lambda j: (0, 0, j)),
                  pl.BlockSpec((n_layers, 1, tn), lambda j: (0, 0, j))],
        out_specs=[pl.BlockSpec((rows, tn), lambda j: (0, j))] * n_layers,
        out_shape=[jax.ShapeDtypeStruct((rows, n_out), f32)] * n_layers,
        compiler_params=_cparams(1),
        name="ada_mod",
    )(c_all, w_ada, b_ada.reshape(n_layers, 1, n_out))


def _ffn_kernel(final_norm, emit_bf16, tiled_gate, x_ref, ng_ref, sh_ref, sc_ref, gt_ref, wg_ref, wu_ref, wo_ref,
                fg_ref, o_ref, *rest):
    rest = list(rest)
    if emit_bf16:
        wg_out, wu_out, wo_out = rest[:3]
        del rest[:3]
    h_scr = rest.pop(0)
    gt_scr = rest.pop(0) if tiled_gate else None
    f = pl.program_id(1)
    tm = x_ref.shape[0]

    @pl.when(f == 0)
    def _():
        h_scr[...] = _rms_mod(x_ref[...], ng_ref[...], _rows(sh_ref[...], tm), _rows(sc_ref[...], tm)).astype(bf16)
        o_ref[...] = x_ref[...]
        if tiled_gate:
            gt_scr[...] = 0.5 * _rows(gt_ref[...], tm)

    wg, wu, wo = wg_ref[...], wu_ref[...], wo_ref[...]
    if emit_bf16:
        wg, wu, wo = wg.astype(bf16), wu.astype(bf16), wo.astype(bf16)
        wg_out[...] = wg
        wu_out[...] = wu
        wo_out[...] = wo
    h = h_scr[...]
    gate = _dot(h, wg)
    up = _dot(h, wu)
    act = (gate * jax.nn.sigmoid(gate) * up).astype(bf16)
    half_gate = gt_scr[...] if tiled_gate else 0.5 * gt_ref[...]
    o_ref[...] += half_gate * _dot(act, wo)

    if final_norm:
        @pl.when(f == pl.num_programs(1) - 1)
        def _():
            y = o_ref[...]
            o_ref[...] = y * lax.rsqrt(jnp.mean(y * y, axis=-1, keepdims=True) + RMS_EPS) * fg_ref[...]


def _ffn(x, mod, tok_major, seq_len, mod_base, norm_g, weights, final_g=None):
    m = x.shape[0]
    tm = min(ROW_TILE, m)
    emit = len(weights) == 4
    tf = 256 if emit else 512
    nf = D_FF // tf
    vec = pl.BlockSpec((1, D_MODEL), lambda i, j: (0, 0))
    fg = jnp.ones((1, D_MODEL), f32) if final_g is None else final_g.reshape(1, D_MODEL)
    w_in_spec = pl.BlockSpec((D_MODEL, tf), lambda i, j: (0, j))
    w_out_spec = pl.BlockSpec((tf, D_MODEL), lambda i, j: (j, 0))
    out_specs = [pl.BlockSpec((tm, D_MODEL), lambda i, j: (i, 0))]
    out_shape = [jax.ShapeDtypeStruct((m, D_MODEL), f32)]
    if emit:
        assert m == tm, "the weight copies are written once, by a single row tile"
        w_ffn_in, w_ffn_out, layer, slot = weights
        w_args = (w_ffn_in, w_ffn_in, w_ffn_out)
        w_specs = [pl.BlockSpec((None, None, D_MODEL, tf), lambda i, j: (layer, slot, 0, j)),
                   pl.BlockSpec((None, None, D_MODEL, tf), lambda i, j: (layer, slot, 0, j + nf)),
                   pl.BlockSpec((None, None, tf, D_MODEL), lambda i, j: (layer, slot, j, 0))]
        out_specs += [w_in_spec, w_in_spec, w_out_spec]
        out_shape += [jax.ShapeDtypeStruct((D_MODEL, D_FF), bf16), jax.ShapeDtypeStruct((D_MODEL, D_FF), bf16),
                      jax.ShapeDtypeStruct((D_FF, D_MODEL), bf16)]
    else:
        w_args = weights
        w_specs = [w_in_spec, w_in_spec, w_out_spec]
    outs = pl.pallas_call(
        functools.partial(_ffn_kernel, final_g is not None, emit, tok_major),
        grid=(m // tm, nf),
        in_specs=[pl.BlockSpec((tm, D_MODEL), lambda i, j: (i, 0)),
                  vec,
                  _mod_spec(tok_major, tm, seq_len, mod_base),
                  _mod_spec(tok_major, tm, seq_len, mod_base + 1),
                  _mod_spec(tok_major, tm, seq_len, mod_base + 2)] + w_specs + [vec],
        out_specs=out_specs,
        out_shape=out_shape,
        scratch_shapes=[pltpu.VMEM((tm, D_MODEL), bf16)] + ([pltpu.VMEM((tm, D_MODEL), f32)] if tok_major else []),
        compiler_params=_cparams(2),
        name="ffn",
    )(x, norm_g.reshape(1, D_MODEL), mod, mod, mod, *w_args, fg)
    return (outs[0], tuple(outs[1:])) if emit else (outs[0], weights)


def _rwkv_proj_kernel(seq_len, tm, h_rows, tok_major,
                      x_ref, ng_ref, sh_ref, sc_ref, s0_ref, mu_ref, w1_ref, a1_ref, g1_ref,
                      w_ref, w2_ref, a2_ref, g2_ref, w0_ref, a0_ref,
                      r_ref, ld_ref, k_ref, v_ref, a_ref, g_ref, h_ref,
                      hs, xr, xk, xv, tw, ta, sg):
    i = pl.program_id(0)
    j = pl.program_id(1)

    @pl.when(j == 0)
    def _():
        h = _rms_mod(x_ref[...], ng_ref[...], _rows(sh_ref[...], tm), _rows(sc_ref[...], tm))
        h_ref[...] = h[tm - h_rows:tm, :]
        if tok_major:
            n_seq = s0_ref.shape[0]
            prev = jnp.concatenate([s0_ref[...], h[0:tm - n_seq, :]], axis=0)
        else:
            @pl.when(i == 0)
            def _():
                hs[0:8, :] = jnp.zeros((8, D_MODEL), f32)

            @pl.when(i > 0)
            def _():
                hs[0:8, :] = hs[tm:tm + 8, :]

            hs[8:tm + 8, :] = h
            row = i * tm + lax.broadcasted_iota(jnp.int32, (tm, 1), 0)
            prev = jnp.where(row % seq_len == 0, s0_ref[...], hs[7:tm + 7, :])
        xx = prev - h
        mu = mu_ref[...]
        xr[...] = (h + xx * mu[0:1, :]).astype(bf16)
        xk[...] = (h + xx * mu[2:3, :]).astype(bf16)
        xv[...] = (h + xx * mu[3:4, :]).astype(bf16)
        xw = (h + xx * mu[1:2, :]).astype(bf16)
        tw[...] = jnp.tanh(_dot(xw, w1_ref[...])).astype(bf16)
        xa = (h + xx * mu[4:5, :]).astype(bf16)
        ta[...] = _dot(xa, a1_ref[...]).astype(bf16)
        xg = (h + xx * mu[5:6, :]).astype(bf16)
        sg[...] = jax.nn.sigmoid(_dot(xg, g1_ref[...])).astype(bf16)

    cols = pl.ds(pl.multiple_of(j * r_ref.shape[1], r_ref.shape[1]), r_ref.shape[1])
    r_ref[...] = _dot(xr[...], w_ref[0, :, cols])
    k_ref[...] = _dot(xk[...], w_ref[1, :, cols])
    v_ref[...] = _dot(xv[...], w_ref[2, :, cols])
    z = w0_ref[...] + _dot(tw[...], w2_ref[...])
    ld_ref[...] = -jax.nn.sigmoid(z) * math.exp(-0.5)
    a_ref[...] = jax.nn.sigmoid(a0_ref[...] + _dot(ta[...], a2_ref[...]))
    g_ref[...] = _dot(sg[...], g2_ref[...])


def _rwkv_proj(x, mod, tok_major, seq_len, norm_g, s0, rwp):
    m = x.shape[0]
    tm = min(ROW_TILE, m)
    tn = 256
    h_rows = tm // seq_len if tok_major else 8
    full = lambda shape: pl.BlockSpec(shape, lambda i, j: (0,) * len(shape))
    col = lambda rows: pl.BlockSpec((rows, tn), lambda i, j: (0, j))
    w_res = pl.BlockSpec((3, D_MODEL, D_MODEL), lambda i, j: (0, 0, 0), pipeline_mode=pl.Buffered(1))
    if tok_major:
        assert m == tm
        s0_spec = pl.BlockSpec((tm // seq_len, D_MODEL), lambda i, j: (0, 0))
    else:
        s0_spec = pl.BlockSpec((None, 1, D_MODEL), lambda i, j: ((i * tm) // seq_len, 0, 0))
    out_spec = pl.BlockSpec((tm, tn), lambda i, j: (i, j))
    out_sds = jax.ShapeDtypeStruct((m, D_MODEL), f32)
    outs = pl.pallas_call(
        functools.partial(_rwkv_proj_kernel, seq_len, tm, h_rows, tok_major),
        grid=(m // tm, D_MODEL // tn),
        in_specs=[pl.BlockSpec((tm, D_MODEL), lambda i, j: (i, 0)),
                  full((1, D_MODEL)),
                  _mod_spec(tok_major, tm, seq_len, 3),
                  _mod_spec(tok_major, tm, seq_len, 4),
                  s0_spec,
                  full((6, D_MODEL)),
                  full((D_MODEL, LORA_PAD)), full((D_MODEL, LORA_PAD)), full((D_MODEL, 256)),
                  w_res,
                  col(LORA_PAD), col(LORA_PAD), col(256), col(1), col(1)],
        out_specs=[out_spec] * 6 + [pl.BlockSpec((h_rows, D_MODEL), lambda i, j: (i, 0))],
        out_shape=[out_sds] * 6 + [jax.ShapeDtypeStruct((m // tm * h_rows, D_MODEL), f32)],
        scratch_shapes=[pltpu.VMEM((tm + 8, D_MODEL), f32),
                        pltpu.VMEM((tm, D_MODEL), bf16), pltpu.VMEM((tm, D_MODEL), bf16),
                        pltpu.VMEM((tm, D_MODEL), bf16),
                        pltpu.VMEM((tm, LORA_PAD), bf16), pltpu.VMEM((tm, LORA_PAD), bf16),
                        pltpu.VMEM((tm, 256), bf16)],
        compiler_params=_cparams(2),
        name="rwkv_proj",
    )(x, norm_g.reshape(1, D_MODEL), mod, mod, s0, rwp["mu"], rwp["w1"], rwp["a1"], rwp["g1"],
      rwp["wrkv"], rwp["w2"], rwp["a2"], rwp["g2"], rwp["w0"], rwp["a0"])
    return outs


def _scan_chains(acts, params, states):
    n_st = 2 * SLAB
    row = lax.broadcasted_iota(jnp.int32, (n_st, LANES), 0)
    lane = lax.broadcasted_iota(jnp.int32, (n_st, LANES), 1)
    own = (row // SLAB) == (lane // HEAD_DIM)
    blk = (row // SLAB) == (lane // SLAB)
    strict = blk & (lane < row)
    incl = blk & (lane <= row)
    eye = jnp.where(row == lane, 1.0, 0.0)
    twice = lambda t: jnp.concatenate([t, t], axis=0)
    st = lambda t: jnp.where(own, twice(t), 0.0)
    each = lambda fn, *lists: [fn(*args) for args in zip(*lists)]

    r, ld, k, v, a, g = [[act[i] for act in acts] for i in range(6)]
    kkp, kap, rkp, gnw, gnb = [[par[i] for par in params] for i in range(5)]

    r64 = lax.broadcasted_iota(jnp.int32, (SLAB, SLAB), 0)
    c64 = lax.broadcasted_iota(jnp.int32, (SLAB, SLAB), 1)
    tri_ones = jnp.concatenate([jnp.where(c64 <= r64, 1.0, 0.0),
                                jnp.ones((SLAB, SLAB), f32)], axis=0).astype(bf16)
    sums = each(lambda t: _mm(tri_ones, t, pb=3), ld)
    cs = each(lambda t: t[0:SLAB], sums)
    tot = each(lambda t: t[SLAB:n_st], sums)

    kk_raw = each(lambda t, p: st(t * p), k, kkp)
    kk = each(lambda t: t * (1.0 / jnp.maximum(jnp.sqrt(jnp.sum(t * t, axis=1, keepdims=True)), 1e-12)), kk_raw)
    k2 = each(lambda kt, at, p: st(kt * (1.0 + (at - 1.0) * p)), k, a, kap)
    b = each(lambda t, at: t * twice(at), kk, a)
    r_s = each(st, r)
    v_s = each(st, v)
    bonus = each(lambda rt, kt, p, vt: jnp.sum(rt * kt * p, axis=1, keepdims=True) * vt, r_s, k2, rkp, v_s)

    e_neg = each(lambda c_: twice(jnp.exp(-c_)), cs)
    e_tail = each(lambda t_, c_: twice(jnp.exp(t_ - c_)), tot, cs)
    a_t = each(lambda t, c_, l_: -t * twice(jnp.exp(c_ - l_)), kk, cs, ld)
    r_t = each(lambda t, c_: t * twice(jnp.exp(c_)), r_s, cs)
    k_t = each(lambda t, e: t * e, k2, e_neg)
    b_t = each(lambda t, e: t * e, b, e_neg)
    k_h = each(lambda t, e: t * e, k2, e_tail)
    b_h = each(lambda t, e: t * e, b, e_tail)

    lhs1 = each(lambda x_, y_: jnp.concatenate([x_, y_], axis=0), a_t, r_t)
    rhs1 = each(lambda x_, y_: jnp.concatenate([x_, y_], axis=0), k_t, b_t)
    x = each(lambda l_, r_: _mm(l_, r_, nt=True), lhs1, rhs1)
    a_k = each(lambda t: jnp.where(strict, t[0:n_st, 0:n_st], 0.0), x)
    a_b = each(lambda t: jnp.where(strict, t[0:n_st, n_st:], 0.0), x)
    r_kb = each(lambda t: jnp.concatenate([jnp.where(incl, t[n_st:, 0:n_st], 0.0),
                                           jnp.where(incl, t[n_st:, n_st:], 0.0)], axis=1), x)

    p1 = each(lambda l_, s_: _mm(l_, s_, nt=True), lhs1, states)
    rhs_u = each(lambda p_, ak, vt: p_[0:n_st] + _mm(ak, vt), p1, a_k, v_s)

    near = (row // 2) == (lane // 2)
    t_inv = each(lambda t: eye + jnp.where(near, t, 0.0), a_b)
    s = 2
    while s < SLAB:
        off = ((row // (2 * s)) == (lane // (2 * s))) & ((row // s) != (lane // s))
        w = each(lambda ab, ti: _mm(jnp.where(off, ab, 0.0), ti), a_b, t_inv)
        t_inv = each(lambda ti, w_: ti + _mm(ti, w_), t_inv, w)
        s *= 2
    u = each(_mm, t_inv, rhs_u)

    vu = each(lambda x_, y_: jnp.concatenate([x_, y_], axis=0), v_s, u)
    y = each(lambda p_, rk_, vu_: p_[n_st:] + _mm(rk_, vu_), p1, r_kb, vu)

    def finish(y_, w_, b_, bonus_, g_):
        mean = jnp.sum(y_, axis=1, keepdims=True) * (1.0 / HEAD_DIM)
        dev = jnp.where(own, y_ - mean, 0.0)
        var = jnp.sum(dev * dev, axis=1, keepdims=True) * (1.0 / HEAD_DIM)
        out = jnp.where(own, dev * lax.rsqrt(var + GN_EPS) * w_ + b_, 0.0) + bonus_
        return (out[0:SLAB] + out[SLAB:n_st]) * g_
    z = each(finish, y, gnw, gnb, bonus, g)

    vu_t = each(lambda t: t.T, vu)
    kb_h = each(lambda x_, y_: jnp.concatenate([x_, y_], axis=0), k_h, b_h)
    new_states = each(lambda s_, t_, vt, kb: s_ * jnp.exp(t_[0:1, :]) + _mm(vt, kb), states, tot, vu_t, kb_h)
    return z, new_states


def _scan_kernel(n_seq, n_pp, r_ref, ld_ref, k_ref, v_ref, a_ref, g_ref, kk_ref, ka_ref, rk_ref, gw_ref,
                 gb_ref, z_ref, s_out_ref, s_scr):
    c = pl.program_id(1)

    @pl.when(c == 0)
    def _():
        s_scr[...] = jnp.zeros_like(s_scr)

    chains = [(b, pp) for b in range(n_seq) for pp in range(n_pp)]
    lanes = [slice(pp * LANES, (pp + 1) * LANES) for _, pp in chains]
    acts = [tuple(ref[b, :, sl] for ref in (r_ref, ld_ref, k_ref, v_ref, a_ref, g_ref))
            for (b, _), sl in zip(chains, lanes)]
    params = [tuple(ref[:, sl] for ref in (kk_ref, ka_ref, rk_ref, gw_ref, gb_ref)) for sl in lanes]
    states = [s_scr[n] for n in range(len(chains))]
    z, new_states = _scan_chains(acts, params, states)
    for n, (b, _) in enumerate(chains):
        z_ref[b, :, lanes[n]] = z[n].astype(bf16)
        s_scr[n] = new_states[n]

    @pl.when(c == pl.num_programs(1) - 1)
    def _():
        for n, (b, pp) in enumerate(chains):
            s = s_scr[n]
            s_out_ref[b, 2 * pp] = s[0:HEAD_DIM, 0:HEAD_DIM]
            s_out_ref[b, 2 * pp + 1] = s[HEAD_DIM:, HEAD_DIM:]


def _scan(proj, rwp, n_batch, seq_len):
    n_pp = SCAN_CHAINS // n_batch
    w = LANES * n_pp
    act = pl.BlockSpec((n_batch, SLAB, w), lambda p, c: (0, c, p))
    vec = pl.BlockSpec((1, w), lambda p, c: (0, p))
    st_spec = pl.BlockSpec((n_batch, 2 * n_pp, HEAD_DIM, HEAD_DIM), lambda p, c: (0, p, 0, 0))
    z, s_out = pl.pallas_call(
        functools.partial(_scan_kernel, n_batch, n_pp),
        grid=(N_PAIRS // n_pp, seq_len // SLAB),
        in_specs=[act] * 6 + [vec] * 5,
        out_specs=[act, st_spec],
        out_shape=[jax.ShapeDtypeStruct((n_batch, seq_len, D_MODEL), bf16),
                   jax.ShapeDtypeStruct((n_batch, N_HEADS, HEAD_DIM, HEAD_DIM), f32)],
        scratch_shapes=[pltpu.VMEM((n_batch * n_pp, 2 * SLAB, LANES), f32)],
        compiler_params=_cparams(2),
        name="rwkv_scan",
    )(*[p.reshape(n_batch, seq_len, D_MODEL) for p in proj],
      rwp["kk"], rwp["ka"], rwp["rk"], rwp["gn_w"], rwp["gn_b"])
    return z.reshape(n_batch * seq_len, D_MODEL), s_out


def _scan_lanes_kernel(n_tok, r_ref, ld_ref, k_ref, v_ref, a_ref, g_ref, kk_ref, ka_ref, rk_ref, gw_ref, gb_ref,
                       s_in_ref, z_ref, s_out_ref, kk_s, w_s, b_s, k2_s, r_s, v_s, y_s):
    hd = HEAD_DIM
    colsum = lambda t: jnp.sum(t, axis=0, keepdims=True)
    for t in range(n_tok):
        r_t, k_t, v_t, a_t = r_ref[t].T, k_ref[t].T, v_ref[t].T, a_ref[t].T
        kk_raw = k_t * kk_ref[...]
        k2 = k_t * (1.0 + (a_t - 1.0) * ka_ref[...])
        w = jnp.exp(ld_ref[t].T)
        for e in range(2):
            sl = slice(e * hd, (e + 1) * hd)
            kr = kk_raw[sl]
            kk = kr * (1.0 / jnp.maximum(jnp.sqrt(colsum(kr * kr)), 1e-12))
            kk_s[e, t] = kk
            b_s[e, t] = kk * a_t[sl]
            k2_s[e, t] = k2[sl]
            w_s[e, t] = w[sl]
            r_s[e, t] = r_t[sl]
            v_s[e, t] = v_t[sl]

    rows_per_iter = 2

    def per_rows(i, carry):
        chains = [(e, rows_per_iter * i + d) for e in range(2) for d in range(rows_per_iter)]
        s = [s_in_ref[e, vi] for e, vi in chains]
        for t in range(n_tok):
            for c, (e, vi) in enumerate(chains):
                s_kk = colsum(s[c] * kk_s[e, t])
                s[c] = s[c] * w_s[e, t] - s_kk * b_s[e, t] + v_s[e, t, pl.ds(vi, 1), :] * k2_s[e, t]
                y_s[e, t, pl.ds(vi, 1), :] = colsum(s[c] * r_s[e, t])
        for c, (e, vi) in enumerate(chains):
            s_out_ref[e, vi] = s[c]
        return carry
    lax.fori_loop(0, hd // rows_per_iter, per_rows, 0)

    for t in range(n_tok):
        outs = []
        for e in range(2):
            sl = slice(e * hd, (e + 1) * hd)
            y = y_s[e, t]
            dev = y - colsum(y) * (1.0 / hd)
            var = colsum(dev * dev) * (1.0 / hd)
            bonus = colsum(r_s[e, t] * k2_s[e, t] * rk_ref[sl, :]) * v_s[e, t]
            outs.append(dev * lax.rsqrt(var + GN_EPS) * gw_ref[sl, :] + gb_ref[sl, :] + bonus)
        z_ref[t] = (jnp.concatenate(outs, axis=0).T * g_ref[t]).astype(bf16)


def _scan_lanes(proj, rwp, n_batch, n_tok, state):
    acts = [p.reshape(n_tok, n_batch, D_MODEL) for p in proj]
    lane_bc = lambda v: jnp.broadcast_to(v.reshape(D_MODEL, 1), (D_MODEL, n_batch))
    params = [lane_bc(rwp[n]) for n in ("kk", "ka", "rk", "gn_w", "gn_b")]
    act = pl.BlockSpec((n_tok, n_batch, LANES), lambda p: (0, 0, p))
    par = pl.BlockSpec((LANES, n_batch), lambda p: (p, 0))
    st_spec = pl.BlockSpec((2, HEAD_DIM, HEAD_DIM, n_batch), lambda p: (p, 0, 0, 0))
    per_tok = pltpu.VMEM((2, n_tok, HEAD_DIM, n_batch), f32)
    z, s_out = pl.pallas_call(
        functools.partial(_scan_lanes_kernel, n_tok),
        grid=(N_PAIRS,),
        in_specs=[act] * 6 + [par] * 5 + [st_spec],
        out_specs=[act, st_spec],
        out_shape=[jax.ShapeDtypeStruct((n_tok, n_batch, D_MODEL), bf16),
                   jax.ShapeDtypeStruct(state.shape, f32)],
        scratch_shapes=[per_tok] * 7,
        compiler_params=_cparams(1),
        name="rwkv_decode_scan",
    )(*acts, *params, state)
    return z.reshape(n_tok * n_batch, D_MODEL), s_out


def _proj_res_kernel(z_ref, w_ref, b_ref, x_ref, gt_ref, o_ref):
    out = _dot(z_ref[...], w_ref[...]) + b_ref[...]
    o_ref[...] = x_ref[...] + _rows(gt_ref[...], x_ref.shape[0]) * out


def _proj_res(z, w, bias, x, mod, tok_major, seq_len):
    m = x.shape[0]
    tm = min(ROW_TILE, m)
    rows = pl.BlockSpec((tm, D_MODEL), lambda i: (i, 0))
    return pl.pallas_call(
        _proj_res_kernel,
        grid=(m // tm,),
        in_specs=[rows,
                  pl.BlockSpec((D_MODEL, D_MODEL), lambda i: (0, 0), pipeline_mode=pl.Buffered(1)),
                  pl.BlockSpec((1, D_MODEL), lambda i: (0, 0)),
                  rows,
                  _mod_spec(tok_major, tm, seq_len, 5)],
        out_specs=rows,
        out_shape=jax.ShapeDtypeStruct((m, D_MODEL), f32),
        compiler_params=_cparams(1),
        name="proj_res",
    )(z, w, bias.reshape(1, D_MODEL), x, mod)


QKV_COLS = D_MODEL + 2 * N_KV_HEADS * LANES


def _qkv_kernel(x_ref, ng_ref, sh_ref, sc_ref, w_ref, b_ref, cos_ref, sin_ref, o_ref, h_scr):
    @pl.when(pl.program_id(1) == 0)
    def _():
        tm = x_ref.shape[0]
        h_scr[...] = _rms_mod(x_ref[...], ng_ref[...], _rows(sh_ref[...], tm), _rows(sc_ref[...], tm)).astype(bf16)

    cos = cos_ref[...]
    sin = sin_ref[...]
    first = (lax.broadcasted_iota(jnp.int32, cos.shape, 1) % HEAD_DIM) < HEAD_DIM // 2
    h = h_scr[...]
    for c0 in range(0, o_ref.shape[1], MXU_COLS):
        acc = _dot(h, w_ref[:, c0:c0 + MXU_COLS]) + b_ref[:, c0:c0 + MXU_COLS]
        for c in range(0, MXU_COLS, LANES):
            xc = acc[:, c:c + LANES]
            rot = jnp.where(first, pltpu.roll(xc, LANES - HEAD_DIM // 2, 1),
                            pltpu.roll(xc, HEAD_DIM // 2, 1))
            o_ref[:, c0 + c:c0 + c + LANES] = xc * cos + rot * sin


def _qkv(x, mod, tok_major, seq_len, norm_g, w, b, cos, sin):
    m = x.shape[0]
    tm = min(ROW_TILE, m)
    tn = 512
    n_rope = (D_MODEL + N_KV_HEADS * LANES) // tn
    n_pos_blocks = cos.shape[0] // tm
    vec = pl.BlockSpec((1, D_MODEL), lambda i, j: (0, 0))
    tab = pl.BlockSpec((tm, LANES), lambda i, j: (i % n_pos_blocks, j // n_rope))
    return pl.pallas_call(
        _qkv_kernel,
        grid=(m // tm, QKV_COLS // tn),
        in_specs=[pl.BlockSpec((tm, D_MODEL), lambda i, j: (i, 0)),
                  vec,
                  _mod_spec(tok_major, tm, seq_len, 3),
                  _mod_spec(tok_major, tm, seq_len, 4),
                  pl.BlockSpec((D_MODEL, tn), lambda i, j: (0, j)),
                  pl.BlockSpec((1, tn), lambda i, j: (0, j)),
                  tab, tab],
        out_specs=pl.BlockSpec((tm, tn), lambda i, j: (i, j)),
        out_shape=jax.ShapeDtypeStruct((m, QKV_COLS), f32),
        scratch_shapes=[pltpu.VMEM((tm, D_MODEL), bf16)],
        compiler_params=_cparams(2),
        name="swa_qkv",
    )(x, norm_g.reshape(1, D_MODEL), mod, mod, w, b.reshape(1, QKV_COLS), cos, sin)


def _rope_tables(positions):
    half = HEAD_DIM // 2
    inv_freq = ROPE_THETA ** (-jnp.arange(half, dtype=f32) / half)
    ang = positions.astype(f32)[:, None] * inv_freq[None, :]
    cos = jnp.tile(jnp.cos(ang), (1, LANES // half))
    sin = jnp.sin(ang)
    sin = jnp.tile(jnp.concatenate([-sin, sin], axis=1), (1, LANES // HEAD_DIM))
    return (jnp.concatenate([cos, jnp.ones_like(cos)], axis=1),
            jnp.concatenate([sin, jnp.zeros_like(sin)], axis=1))


def _head_masks(rows):
    lane = lax.broadcasted_iota(jnp.int32, (rows, LANES), 1)
    return lane < HEAD_DIM, lane >= HEAD_DIM


def _attn_prompt_kernel(q_ref, kp_ref, kc_ref, vp_ref, vc_ref, sink_ref, o_ref):
    n = pl.program_id(1)
    blk = WINDOW
    m0, m1 = _head_masks(blk)
    m0k, m1k = _head_masks(2 * blk)
    qi = lax.broadcasted_iota(jnp.int32, (2 * blk, 2 * blk), 0) % blk
    sj = lax.broadcasted_iota(jnp.int32, (2 * blk, 2 * blk), 1)
    visible = (sj > qi) & (sj <= qi + blk) & ((n > 0) | (sj >= blk))
    top = lax.broadcasted_iota(jnp.int32, (2 * blk, 1), 0) < blk
    for c in range(N_KV_HEADS):
        sl = slice(c * LANES, (c + 1) * LANES)
        kd = jnp.concatenate([kp_ref[:, sl], kc_ref[:, sl]], axis=0).astype(bf16)
        vd = jnp.concatenate([vp_ref[:, sl], vc_ref[:, sl]], axis=0)
        vcat = jnp.concatenate([jnp.where(m0k, vd, 0.0), jnp.where(m1k, vd, 0.0)], axis=0).astype(bf16)
        pairs = [c * (GQA_GROUP // 2) + jj for jj in range(GQA_GROUP // 2)]
        each = lambda fn, *lists: [fn(*args) for args in zip(*lists)]
        qp = [q_ref[:, pair * LANES:(pair + 1) * LANES] * ATTN_SCALE for pair in pairs]
        qs = each(lambda t: jnp.concatenate([jnp.where(m0, t, 0.0), jnp.where(m1, t, 0.0)], axis=0).astype(bf16), qp)
        s = each(lambda t: jnp.where(visible, _dot_nt(t, kd), NEG_BIG), qs)
        sk = [jnp.where(top, sink_ref[2 * pair], sink_ref[2 * pair + 1]) for pair in pairs]
        mx = each(lambda t, k_: jnp.maximum(jnp.max(t, axis=1, keepdims=True), k_), s, sk)
        p = each(lambda t, m_: jnp.exp(t - m_), s, mx)
        den = each(lambda t, k_, m_: jnp.sum(t, axis=1, keepdims=True) + jnp.exp(k_ - m_), p, sk, mx)
        p = each(lambda t, d_: (t * (1.0 / d_)).astype(bf16), p, den)
        pcat = each(lambda t: jnp.concatenate([t[0:blk], t[blk:]], axis=1), p)
        for pair, t in zip(pairs, pcat):
            o_ref[:, pair * LANES:(pair + 1) * LANES] = _dot(t, vcat).astype(bf16)


def _attn_prompt(qkv, sink, n_batch, seq_len):
    nb = seq_len // WINDOW
    kw = N_KV_HEADS * LANES
    k_blk = D_MODEL // kw
    cur = lambda off: pl.BlockSpec((WINDOW, kw), lambda b, n: (b * nb + n, k_blk + off))
    prev = lambda off: pl.BlockSpec((WINDOW, kw), lambda b, n: (b * nb + jnp.maximum(n - 1, 0), k_blk + off))
    return pl.pallas_call(
        _attn_prompt_kernel,
        grid=(n_batch, nb),
        in_specs=[pl.BlockSpec((WINDOW, D_MODEL), lambda b, n: (b * nb + n, 0)),
                  prev(0), cur(0), prev(1), cur(1),
                  pl.BlockSpec(memory_space=pltpu.SMEM)],
        out_specs=pl.BlockSpec((WINDOW, D_MODEL), lambda b, n: (b * nb + n, 0)),
        out_shape=jax.ShapeDtypeStruct((n_batch * seq_len, D_MODEL), bf16),
        compiler_params=_cparams(2),
        name="swa_prompt",
    )(qkv, qkv, qkv, qkv, qkv, sink)


SEQ_PER_GROUP = 4


def _attn_cached_kernel(n_tok, groups, q_ref, kn_ref, vn_ref, kc_ref, vc_ref, sink_ref, o_ref):
    rows = SEQ_PER_GROUP * n_tok
    n_st = GQA_GROUP * rows
    m0, m1 = _head_masks(rows)
    srow = lax.broadcasted_iota(jnp.int32, (n_st, 1), 0)
    row_seq = (srow % rows) // n_tok
    row_tok = srow % n_tok
    key_c = lax.broadcasted_iota(jnp.int32, (n_st, WINDOW), 1)
    vis_c = key_c > row_tok
    key_n = lax.broadcasted_iota(jnp.int32, (n_st, rows), 1)
    vis_n = ((key_n // n_tok) == row_seq) & ((key_n % n_tok) <= row_tok)
    for gi in range(groups):
        rs = slice(gi * rows, (gi + 1) * rows)
        for cp in range(N_KV_HEADS // 2):
            for ce in range(2):
                c = 2 * cp + ce
                sl = slice(c * LANES, (c + 1) * LANES)
                kt = [kc_ref[gi * SEQ_PER_GROUP + b, c * HEAD_DIM:(c + 1) * HEAD_DIM, :] for b in range(SEQ_PER_GROUP)]
                vt = [vc_ref[gi * SEQ_PER_GROUP + b, c * HEAD_DIM:(c + 1) * HEAD_DIM, :] for b in range(SEQ_PER_GROUP)]
                kn = kn_ref[rs, sl].astype(bf16)
                vn = vn_ref[rs, sl].astype(bf16)
                pieces = []
                for jj in range(GQA_GROUP // 2):
                    pair = c * (GQA_GROUP // 2) + jj
                    qp = q_ref[rs, pair * LANES:(pair + 1) * LANES]
                    pieces += [jnp.where(m0, qp, 0.0), jnp.where(m1, qp, 0.0)]
                qs = jnp.concatenate(pieces, axis=0).astype(bf16)
                s_c = jnp.zeros((n_st, WINDOW), f32)
                for b in range(SEQ_PER_GROUP):
                    kx = jnp.concatenate([kt[b], kt[b]], axis=0).astype(bf16)
                    s_c = jnp.where(row_seq == b, _dot(qs, kx), s_c)
                s_c = jnp.where(vis_c, s_c * ATTN_SCALE, NEG_BIG)
                s_n = jnp.where(vis_n, _dot_nt(qs, kn) * ATTN_SCALE, NEG_BIG)
                sk = sink_ref[c]
                sk = sk[:, 0:1]
                mx = jnp.maximum(jnp.maximum(jnp.max(s_c, axis=1, keepdims=True),
                                             jnp.max(s_n, axis=1, keepdims=True)), sk)
                p_c = jnp.exp(s_c - mx)
                p_n = jnp.exp(s_n - mx)
                den = (jnp.sum(p_c, axis=1, keepdims=True) + jnp.sum(p_n, axis=1, keepdims=True)
                       + jnp.exp(sk - mx))
                inv_den = 1.0 / den
                p_c = (p_c * inv_den).astype(bf16)
                p_n = (p_n * inv_den).astype(bf16)
                o = _dot(p_n, vn)
                for b in range(SEQ_PER_GROUP):
                    vx = jnp.concatenate([vt[b], vt[b]], axis=0).astype(bf16)
                    o = o + jnp.where(row_seq == b, _dot_nt(p_c, vx), 0.0)
                for jj in range(GQA_GROUP // 2):
                    pair = c * (GQA_GROUP // 2) + jj
                    o0 = o[(2 * jj) * rows:(2 * jj + 1) * rows]
                    o1 = o[(2 * jj + 1) * rows:(2 * jj + 2) * rows]
                    o_ref[rs, pair * LANES:(pair + 1) * LANES] = jnp.where(m0, o0, o1).astype(bf16)


def _attn_cached(qkv, k_cache, v_cache, sink, n_batch, n_tok):
    groups = 2
    seqs = SEQ_PER_GROUP * groups
    rows = seqs * n_tok
    kw = N_KV_HEADS * LANES
    k_blk = D_MODEL // kw
    n_st = GQA_GROUP * SEQ_PER_GROUP * n_tok
    head = (jnp.arange(N_KV_HEADS)[:, None] * GQA_GROUP
            + (jnp.arange(n_st)[None, :] // (SEQ_PER_GROUP * n_tok)))
    sink_tab = jnp.broadcast_to(sink[head][:, :, None], (N_KV_HEADS, n_st, LANES))
    cache_spec = pl.BlockSpec((seqs, N_KV_HEADS * HEAD_DIM, WINDOW), lambda i: (i, 0, 0))
    return pl.pallas_call(
        functools.partial(_attn_cached_kernel, n_tok, groups),
        grid=(n_batch // seqs,),
        in_specs=[pl.BlockSpec((rows, D_MODEL), lambda i: (i, 0)),
                  pl.BlockSpec((rows, kw), lambda i: (i, k_blk)),
                  pl.BlockSpec((rows, kw), lambda i: (i, k_blk + 1)),
                  cache_spec, cache_spec,
                  pl.BlockSpec((N_KV_HEADS, n_st, LANES), lambda i: (0, 0, 0))],
        out_specs=pl.BlockSpec((rows, D_MODEL), lambda i: (i, 0)),
        out_shape=jax.ShapeDtypeStruct((n_batch * n_tok, D_MODEL), bf16),
        compiler_params=_cparams(1),
        name="swa_cached",
    )(qkv, qkv, qkv, k_cache, v_cache, sink_tab)


def _undup(t):
    return t.reshape(t.shape[0], N_KV_HEADS, 2, HEAD_DIM)[:, :, 0, :]


def _trunk(x, mod_all, tok_major, n_batch, seq_len, pos0, wkv_in, shift_in, k_in, v_in, wts):
    to_seq_major = lambda t: t.reshape(seq_len, n_batch, -1).transpose(1, 0, 2).reshape(n_batch * seq_len, -1)
    to_tok_major = lambda t: t.reshape(n_batch, seq_len, -1).transpose(1, 0, 2).reshape(n_batch * seq_len, -1)
    mod = mod_all[0]
    ffn_w = dict(wts["ffn"])
    x, ffn_w[0, 0] = _ffn(x, mod, tok_major, seq_len, 0, wts["norm_g"][0, 0], ffn_w[0, 0])
    s0 = shift_in[0] if tok_major else jnp.zeros((n_batch, 1, D_MODEL), f32)
    *proj, h_tail = _rwkv_proj(x, mod, tok_major, seq_len, wts["norm_g"][0, 1], s0, wts["rw"])
    if tok_major:
        z, wkv_t = _scan_lanes(proj, wts["rw"], n_batch, seq_len, jnp.transpose(wkv_in[0], (1, 2, 3, 0)))
        wkv_out = jnp.transpose(wkv_t, (3, 0, 1, 2))
        shift_out = h_tail
    else:
        z, wkv_out = _scan(proj, wts["rw"], n_batch, seq_len)
        shift_out = h_tail.reshape(n_batch, -1, 8, D_MODEL)[:, -1, -1]
    x = _proj_res(z, wts["rw"]["wo"], jnp.zeros((D_MODEL,), f32), x, mod, tok_major, seq_len)
    x, ffn_w[0, 1] = _ffn(x, mod, tok_major, seq_len, 6, wts["norm_g"][0, 2], ffn_w[0, 1])
    mod = mod_all[1]
    x, ffn_w[1, 0] = _ffn(x, mod, tok_major, seq_len, 0, wts["norm_g"][1, 0], ffn_w[1, 0])
    positions = pos0 + jnp.arange(seq_len)
    if tok_major:
        positions = jnp.repeat(positions, n_batch)
    cos, sin = _rope_tables(positions)
    qkv = _qkv(x, mod, tok_major, seq_len, wts["norm_g"][1, 1], wts["sw_wqkv"], wts["sw_bqkv"], cos, sin)
    kw = N_KV_HEADS * LANES
    if k_in is None:
        att = _attn_prompt(qkv, wts["sw_sink"], n_batch, seq_len)
        tail = qkv.reshape(n_batch, seq_len, QKV_COLS)[:, -WINDOW:, D_MODEL:]
        k_new = _undup(tail[..., :kw].reshape(n_batch * WINDOW, kw)).reshape(n_batch, WINDOW, N_KV_HEADS, HEAD_DIM)
        v_new = _undup(tail[..., kw:].reshape(n_batch * WINDOW, kw)).reshape(n_batch, WINDOW, N_KV_HEADS, HEAD_DIM)
    else:
        win = k_in.shape[2]
        qkv = to_seq_major(qkv)
        cache_t = lambda t: jnp.transpose(t[0], (0, 2, 3, 1)).reshape(n_batch, N_KV_HEADS * HEAD_DIM, win)
        att = _attn_cached(qkv, cache_t(k_in), cache_t(v_in), wts["sw_sink"], n_batch, seq_len)
        att = to_tok_major(att)
        k_tok = _undup(qkv[:, D_MODEL:D_MODEL + kw]).reshape(n_batch, seq_len, N_KV_HEADS, HEAD_DIM)
        v_tok = _undup(qkv[:, D_MODEL + kw:]).reshape(n_batch, seq_len, N_KV_HEADS, HEAD_DIM)
        k_new = jnp.concatenate([k_in[0], k_tok], axis=1)[:, -win:]
        v_new = jnp.concatenate([v_in[0], v_tok], axis=1)[:, -win:]
    x = _proj_res(att, wts["sw_wo"], wts["sw_bo"], x, mod, tok_major, seq_len)
    y, ffn_w[1, 1] = _ffn(x, mod, tok_major, seq_len, 6, wts["norm_g"][1, 2], ffn_w[1, 1], final_g=wts["final_g"])
    return (y, wkv_out[None], shift_out[None], k_new[None], v_new[None]), ffn_w


def _dup_heads(w):
    lead = w.shape[:-1]
    w4 = w.reshape(lead + (N_KV_HEADS, 1, HEAD_DIM))
    return jnp.broadcast_to(w4, lead + (N_KV_HEADS, 2, HEAD_DIM)).reshape(lead + (N_KV_HEADS * LANES,))


def _pad_cols(w):
    return jnp.pad(w, ((0, 0), (0, LORA_PAD - w.shape[1])))


def _pad_rows(w):
    return jnp.pad(w, ((0, LORA_PAD - w.shape[0]), (0, 0)))


def kernel(x_prompt, x_sample, state_rwkv_wkv, state_rwkv_shift, cache_swa_k, cache_swa_v, c_prompt, c_sample, norm_g, w_ada, b_ada, w_ffn_in, w_ffn_out, rw_mu, rw_wrkv, rw_w0, rw_w1, rw_w2, rw_a0, rw_a1, rw_a2, rw_g1, rw_g2, rw_kk, rw_ka, rw_rk, rw_gn_w, rw_gn_b, rw_wo, sw_wqkv, sw_bqkv, sw_sink, sw_wo, sw_bo, final_g):
    n_p, seq_p, _ = x_prompt.shape
    n_s, seq_s, _ = x_sample.shape
    nq = N_HEADS * HEAD_DIM
    nkv = N_KV_HEADS * HEAD_DIM
    row = lambda t: t.reshape(1, D_MODEL)
    rw = dict(
        mu=rw_mu[0], wrkv=rw_wrkv[0].astype(bf16),
        w1=_pad_cols(rw_w1[0]).astype(bf16), a1=_pad_cols(rw_a1[0]).astype(bf16), g1=rw_g1[0].astype(bf16),
        w2=_pad_rows(rw_w2[0]).astype(bf16), a2=_pad_rows(rw_a2[0]).astype(bf16), g2=rw_g2[0].astype(bf16),
        w0=row(rw_w0[0]), a0=row(rw_a0[0]), kk=row(rw_kk[0]), ka=row(rw_ka[0]), rk=row(rw_rk[0]),
        gn_w=row(rw_gn_w[0]), gn_b=row(rw_gn_b[0]), wo=rw_wo[0].astype(bf16))
    wq = sw_wqkv[0]
    bq = sw_bqkv[0]
    wts = dict(
        norm_g=norm_g, final_g=final_g,
        ffn={(l, s): (w_ffn_in, w_ffn_out, l, s) for l in range(2) for s in range(2)}, rw=rw,
        sw_wqkv=jnp.concatenate([wq[:, :nq], _dup_heads(wq[:, nq:nq + nkv]), _dup_heads(wq[:, nq + nkv:])],
                                axis=1).astype(bf16),
        sw_bqkv=jnp.concatenate([bq[:nq], _dup_heads(bq[nq:nq + nkv]), _dup_heads(bq[nq + nkv:])]),
        sw_sink=sw_sink[0], sw_wo=sw_wo[0].astype(bf16), sw_bo=sw_bo[0])

    n_c = n_p + n_s
    pad = (-n_c) % 8
    c_all = jnp.concatenate([c_sample, c_prompt, jnp.zeros((pad, D_MODEL), f32)], axis=0)
    mod_s = _ada(c_all, w_ada, b_ada)
    mod_p = [m[n_s:n_c].reshape(n_p, 1, N_MOD * D_MODEL) for m in mod_s]

    (y_s, s_wkv, s_shift, s_k, s_v), wts["ffn"] = _trunk(
        jnp.transpose(x_sample, (1, 0, 2)).reshape(n_s * seq_s, D_MODEL), mod_s, True, n_s, seq_s, PAST_LEN,
        state_rwkv_wkv, state_rwkv_shift, cache_swa_k, cache_swa_v, wts)
    (y_p, p_wkv, p_shift, p_k, p_v), _ = _trunk(
        x_prompt.reshape(n_p * seq_p, D_MODEL), mod_p, False, n_p, seq_p, 0, None, None, None, None, wts)
    y_s = jnp.transpose(y_s.reshape(seq_s, n_s, D_MODEL), (1, 0, 2))
    return (y_p.reshape(n_p, seq_p, D_MODEL), y_s, p_wkv, p_shift, p_k, p_v, s_wkv, s_shift, s_k, s_v)
```

```python
import functools
import math

import jax
import jax.numpy as jnp
from jax import lax
from jax.experimental import pallas as pl
from jax.experimental.pallas import tpu as pltpu

f32 = jnp.float32
bf16 = jnp.bfloat16

D_MODEL = 2048
HEAD_DIM = 64
N_HEADS = D_MODEL // HEAD_DIM
N_KV_HEADS = 4
GQA_GROUP = N_HEADS // N_KV_HEADS
WINDOW = 128
ATTN_SCALE = HEAD_DIM ** -0.5
ROPE_THETA = 10000.0
D_FF = 5632
N_MOD = 9
RMS_EPS = 1e-6
GN_EPS = 64e-5
PAST_LEN = 8192
LANES = 128
MXU_COLS = 256
N_PAIRS = D_MODEL // LANES
LORA_PAD = 128
SLAB = 64
SCAN_CHAINS = 16
VMEM_LIMIT = 56 * 1024 * 1024
NEG_BIG = -1e30
ROW_TILE = 512


def _cparams(n_axes):
    return pltpu.CompilerParams(dimension_semantics=("arbitrary",) * n_axes,
                                vmem_limit_bytes=VMEM_LIMIT)


def _dot(a, b):
    return jnp.dot(a, b, preferred_element_type=f32)


def _dot_nt(a, b):
    return lax.dot_general(a, b, (((1,), (1,)), ((), ())), preferred_element_type=f32)


def _split(x, n):
    if x.dtype == bf16:
        return [x]
    parts = []
    rem = x
    for i in range(n):
        p = rem.astype(bf16)
        parts.append(p)
        if i + 1 < n:
            rem = rem - p.astype(f32)
    return parts


def _mm(a, b, pa=1, pb=1, nt=False):
    a_parts = _split(a, pa)
    b_parts = _split(b, pb)
    order = max(len(a_parts), len(b_parts))
    acc = None
    for i, x in enumerate(a_parts):
        for j, y in enumerate(b_parts):
            if i + j >= order:
                continue
            t = _dot_nt(x, y) if nt else _dot(x, y)
            acc = t if acc is None else acc + t
    return acc


def _rms_mod(x, g, shift, scale):
    return x * lax.rsqrt(jnp.mean(x * x, axis=-1, keepdims=True) + RMS_EPS) * (g * (1.0 + scale)) + shift


def _mod_spec(tok_major, tm, seq_len, idx):
    if tok_major:
        return pl.BlockSpec((tm // seq_len, D_MODEL), lambda i, *_: (0, idx))
    return pl.BlockSpec((None, 1, D_MODEL), lambda i, *_: ((i * tm) // seq_len, 0, idx))


def _rows(v, tm):
    n = v.shape[0]
    return v if n in (1, tm) else jnp.concatenate([v] * (tm // n), axis=0)


def _ada_kernel(c_ref, w_ref, b_ref, *o_refs):
    c = c_ref[...]
    s = (c * jax.nn.sigmoid(c)).astype(bf16)
    for layer, o_ref in enumerate(o_refs):
        o_ref[...] = _dot(s, w_ref[layer].astype(bf16)) + b_ref[layer]


def _ada(c_all, w_ada, b_ada):
    n_layers, _, n_out = w_ada.shape
    rows = c_all.shape[0]
    tn = 1024
    return pl.pallas_call(
        _ada_kernel,
        grid=(n_out // tn,),
        in_specs=[pl.BlockSpec((rows, D_MODEL), lambda j: (0, 0)),
                  pl.BlockSpec((n_layers, D_MODEL, tn), lambda j: (0, 0, j)),
                  pl.BlockSpec((n_layers, 1, tn), lambda j: (0, 0, j))],
        out_specs=[pl.BlockSpec((rows, tn), lambda j: (0, j))] * n_layers,
        out_shape=[jax.ShapeDtypeStruct((rows, n_out), f32)] * n_layers,
        compiler_params=_cparams(1),
        name="ada_mod",
    )(c_all, w_ada, b_ada.reshape(n_layers, 1, n_out))


def _ffn_kernel(final_norm, emit_bf16, tiled_gate, x_ref, ng_ref, sh_ref, sc_ref, gt_ref, wg_ref, wu_ref, wo_ref,
                fg_ref, o_ref, *rest):
    rest = list(rest)
    if emit_bf16:
        wg_out, wu_out, wo_out = rest[:3]
        del rest[:3]
    h_scr = rest.pop(0)
    gt_scr = rest.pop(0) if tiled_gate else None
    f = pl.program_id(1)
    tm = x_ref.shape[0]

    @pl.when(f == 0)
    def _():
        h_scr[...] = _rms_mod(x_ref[...], ng_ref[...], _rows(sh_ref[...], tm), _rows(sc_ref[...], tm)).astype(bf16)
        o_ref[...] = x_ref[...]
        if tiled_gate:
            gt_scr[...] = 0.5 * _rows(gt_ref[...], tm)

    wg, wu, wo = wg_ref[...], wu_ref[...], wo_ref[...]
    if emit_bf16:
        wg, wu, wo = wg.astype(bf16), wu.astype(bf16), wo.astype(bf16)
        wg_out[...] = wg
        wu_out[...] = wu
        wo_out[...] = wo
    h = h_scr[...]
    gate = _dot(h, wg)
    up = _dot(h, wu)
    act = (gate * jax.nn.sigmoid(gate) * up).astype(bf16)
    half_gate = gt_scr[...] if tiled_gate else 0.5 * gt_ref[...]
    o_ref[...] += half_gate * _dot(act, wo)

    if final_norm:
        @pl.when(f == pl.num_programs(1) - 1)
        def _():
            y = o_ref[...]
            o_ref[...] = y * lax.rsqrt(jnp.mean(y * y, axis=-1, keepdims=True) + RMS_EPS) * fg_ref[...]


def _ffn(x, mod, tok_major, seq_len, mod_base, norm_g, weights, final_g=None):
    m = x.shape[0]
    tm = min(ROW_TILE, m)
    emit = len(weights) == 4
    tf = 256 if emit else 512
    nf = D_FF // tf
    vec = pl.BlockSpec((1, D_MODEL), lambda i, j: (0, 0))
    fg = jnp.ones((1, D_MODEL), f32) if final_g is None else final_g.reshape(1, D_MODEL)
    w_in_spec = pl.BlockSpec((D_MODEL, tf), lambda i, j: (0, j))
    w_out_spec = pl.BlockSpec((tf, D_MODEL), lambda i, j: (j, 0))
    out_specs = [pl.BlockSpec((tm, D_MODEL), lambda i, j: (i, 0))]
    out_shape = [jax.ShapeDtypeStruct((m, D_MODEL), f32)]
    if emit:
        assert m == tm, "the weight copies are written once, by a single row tile"
        w_ffn_in, w_ffn_out, layer, slot = weights
        w_args = (w_ffn_in, w_ffn_in, w_ffn_out)
        w_specs = [pl.BlockSpec((None, None, D_MODEL, tf), lambda i, j: (layer, slot, 0, j)),
                   pl.BlockSpec((None, None, D_MODEL, tf), lambda i, j: (layer, slot, 0, j + nf)),
                   pl.BlockSpec((None, None, tf, D_MODEL), lambda i, j: (layer, slot, j, 0))]
        out_specs += [w_in_spec, w_in_spec, w_out_spec]
        out_shape += [jax.ShapeDtypeStruct((D_MODEL, D_FF), bf16), jax.ShapeDtypeStruct((D_MODEL, D_FF), bf16),
                      jax.ShapeDtypeStruct((D_FF, D_MODEL), bf16)]
    else:
        w_args = weights
        w_specs = [w_in_spec, w_in_spec, w_out_spec]
    outs = pl.pallas_call(
        functools.partial(_ffn_kernel, final_g is not None, emit, tok_major),
        grid=(m // tm, nf),
        in_specs=[pl.BlockSpec((tm, D_MODEL), lambda i, j: (i, 0)),
                  vec,
                  _mod_spec(tok_major, tm, seq_len, mod_base),
                  _mod_spec(tok_major, tm, seq_len, mod_base + 1),
                  _mod_spec(tok_major, tm, seq_len, mod_base + 2)] + w_specs + [vec],
        out_specs=out_specs,
        out_shape=out_shape,
        scratch_shapes=[pltpu.VMEM((tm, D_MODEL), bf16)] + ([pltpu.VMEM((tm, D_MODEL), f32)] if tok_major else []),
        compiler_params=_cparams(2),
        name="ffn",
    )(x, norm_g.reshape(1, D_MODEL), mod, mod, mod, *w_args, fg)
    return (outs[0], tuple(outs[1:])) if emit else (outs[0], weights)


def _rwkv_proj_kernel(seq_len, tm, h_rows, tok_major,
                      x_ref, ng_ref, sh_ref, sc_ref, s0_ref, mu_ref, w1_ref, a1_ref, g1_ref,
                      w_ref, w2_ref, a2_ref, g2_ref, w0_ref, a0_ref,
                      r_ref, ld_ref, k_ref, v_ref, a_ref, g_ref, h_ref,
                      hs, xr, xk, xv, tw, ta, sg):
    i = pl.program_id(0)
    j = pl.program_id(1)

    @pl.when(j == 0)
    def _():
        h = _rms_mod(x_ref[...], ng_ref[...], _rows(sh_ref[...], tm), _rows(sc_ref[...], tm))
        h_ref[...] = h[tm - h_rows:tm, :]
        if tok_major:
            n_seq = s0_ref.shape[0]
            prev = jnp.concatenate([s0_ref[...], h[0:tm - n_seq, :]], axis=0)
        else:
            @pl.when(i == 0)
            def _():
                hs[0:8, :] = jnp.zeros((8, D_MODEL), f32)

            @pl.when(i > 0)
            def _():
                hs[0:8, :] = hs[tm:tm + 8, :]

            hs[8:tm + 8, :] = h
            row = i * tm + lax.broadcasted_iota(jnp.int32, (tm, 1), 0)
            prev = jnp.where(row % seq_len == 0, s0_ref[...], hs[7:tm + 7, :])
        xx = prev - h
        mu = mu_ref[...]
        xr[...] = (h + xx * mu[0:1, :]).astype(bf16)
        xk[...] = (h + xx * mu[2:3, :]).astype(bf16)
        xv[...] = (h + xx * mu[3:4, :]).astype(bf16)
        xw = (h + xx * mu[1:2, :]).astype(bf16)
        tw[...] = jnp.tanh(_dot(xw, w1_ref[...])).astype(bf16)
        xa = (h + xx * mu[4:5, :]).astype(bf16)
        ta[...] = _dot(xa, a1_ref[...]).astype(bf16)
        xg = (h + xx * mu[5:6, :]).astype(bf16)
        sg[...] = jax.nn.sigmoid(_dot(xg, g1_ref[...])).astype(bf16)

    cols = pl.ds(pl.multiple_of(j * r_ref.shape[1], r_ref.shape[1]), r_ref.shape[1])
    r_ref[...] = _dot(xr[...], w_ref[0, :, cols])
    k_ref[...] = _dot(xk[...], w_ref[1, :, cols])
    v_ref[...] = _dot(xv[...], w_ref[2, :, cols])
    z = w0_ref[...] + _dot(tw[...], w2_ref[...])
    ld_ref[...] = -jax.nn.sigmoid(z) * math.exp(-0.5)
    a_ref[...] = jax.nn.sigmoid(a0_ref[...] + _dot(ta[...], a2_ref[...]))
    g_ref[...] = _dot(sg[...], g2_ref[...])


def _rwkv_proj(x, mod, tok_major, seq_len, norm_g, s0, rwp):
    m = x.shape[0]
    tm = min(ROW_TILE, m)
    tn = 256
    h_rows = tm // seq_len if tok_major else 8
    full = lambda shape: pl.BlockSpec(shape, lambda i, j: (0,) * len(shape))
    col = lambda rows: pl.BlockSpec((rows, tn), lambda i, j: (0, j))
    w_res = pl.BlockSpec((3, D_MODEL, D_MODEL), lambda i, j: (0, 0, 0), pipeline_mode=pl.Buffered(1))
    if tok_major:
        assert m == tm
        s0_spec = pl.BlockSpec((tm // seq_len, D_MODEL), lambda i, j: (0, 0))
    else:
        s0_spec = pl.BlockSpec((None, 1, D_MODEL), lambda i, j: ((i * tm) // seq_len, 0, 0))
    out_spec = pl.BlockSpec((tm, tn), lambda i, j: (i, j))
    out_sds = jax.ShapeDtypeStruct((m, D_MODEL), f32)
    outs = pl.pallas_call(
        functools.partial(_rwkv_proj_kernel, seq_len, tm, h_rows, tok_major),
        grid=(m // tm, D_MODEL // tn),
        in_specs=[pl.BlockSpec((tm, D_MODEL), lambda i, j: (i, 0)),
                  full((1, D_MODEL)),
                  _mod_spec(tok_major, tm, seq_len, 3),
                  _mod_spec(tok_major, tm, seq_len, 4),
                  s0_spec,
                  full((6, D_MODEL)),
                  full((D_MODEL, LORA_PAD)), full((D_MODEL, LORA_PAD)), full((D_MODEL, 256)),
                  w_res,
                  col(LORA_PAD), col(LORA_PAD), col(256), col(1), col(1)],
        out_specs=[out_spec] * 6 + [pl.BlockSpec((h_rows, D_MODEL), lambda i, j: (i, 0))],
        out_shape=[out_sds] * 6 + [jax.ShapeDtypeStruct((m // tm * h_rows, D_MODEL), f32)],
        scratch_shapes=[pltpu.VMEM((tm + 8, D_MODEL), f32),
                        pltpu.VMEM((tm, D_MODEL), bf16), pltpu.VMEM((tm, D_MODEL), bf16),
                        pltpu.VMEM((tm, D_MODEL), bf16),
                        pltpu.VMEM((tm, LORA_PAD), bf16), pltpu.VMEM((tm, LORA_PAD), bf16),
                        pltpu.VMEM((tm, 256), bf16)],
        compiler_params=_cparams(2),
        name="rwkv_proj",
    )(x, norm_g.reshape(1, D_MODEL), mod, mod, s0, rwp["mu"], rwp["w1"], rwp["a1"], rwp["g1"],
      rwp["wrkv"], rwp["w2"], rwp["a2"], rwp["g2"], rwp["w0"], rwp["a0"])
    return outs


def _scan_chains(acts, params, states):
    n_st = 2 * SLAB
    row = lax.broadcasted_iota(jnp.int32, (n_st, LANES), 0)
    lane = lax.broadcasted_iota(jnp.int32, (n_st, LANES), 1)
    own = (row // SLAB) == (lane // HEAD_DIM)
    blk = (row // SLAB) == (lane // SLAB)
    strict = blk & (lane < row)
    incl = blk & (lane <= row)
    eye = jnp.where(row == lane, 1.0, 0.0)
    twice = lambda t: jnp.concatenate([t, t], axis=0)
    st = lambda t: jnp.where(own, twice(t), 0.0)
    each = lambda fn, *lists: [fn(*args) for args in zip(*lists)]

    r, ld, k, v, a, g = [[act[i] for act in acts] for i in range(6)]
    kkp, kap, rkp, gnw, gnb = [[par[i] for par in params] for i in range(5)]

    r64 = lax.broadcasted_iota(jnp.int32, (SLAB, SLAB), 0)
    c64 = lax.broadcasted_iota(jnp.int32, (SLAB, SLAB), 1)
    tri_ones = jnp.concatenate([jnp.where(c64 <= r64, 1.0, 0.0),
                                jnp.ones((SLAB, SLAB), f32)], axis=0).astype(bf16)
    sums = each(lambda t: _mm(tri_ones, t, pb=3), ld)
    cs = each(lambda t: t[0:SLAB], sums)
    tot = each(lambda t: t[SLAB:n_st], sums)

    kk_raw = each(lambda t, p: st(t * p), k, kkp)
    kk = each(lambda t: t * (1.0 / jnp.maximum(jnp.sqrt(jnp.sum(t * t, axis=1, keepdims=True)), 1e-12)), kk_raw)
    k2 = each(lambda kt, at, p: st(kt * (1.0 + (at - 1.0) * p)), k, a, kap)
    b = each(lambda t, at: t * twice(at), kk, a)
    r_s = each(st, r)
    v_s = each(st, v)
    bonus = each(lambda rt, kt, p, vt: jnp.sum(rt * kt * p, axis=1, keepdims=True) * vt, r_s, k2, rkp, v_s)

    e_neg = each(lambda c_: twice(jnp.exp(-c_)), cs)
    e_tail = each(lambda t_, c_: twice(jnp.exp(t_ - c_)), tot, cs)
    a_t = each(lambda t, c_, l_: -t * twice(jnp.exp(c_ - l_)), kk, cs, ld)
    r_t = each(lambda t, c_: t * twice(jnp.exp(c_)), r_s, cs)
    k_t = each(lambda t, e: t * e, k2, e_neg)
    b_t = each(lambda t, e: t * e, b, e_neg)
    k_h = each(lambda t, e: t * e, k2, e_tail)
    b_h = each(lambda t, e: t * e, b, e_tail)

    lhs1 = each(lambda x_, y_: jnp.concatenate([x_, y_], axis=0), a_t, r_t)
    rhs1 = each(lambda x_, y_: jnp.concatenate([x_, y_], axis=0), k_t, b_t)
    x = each(lambda l_, r_: _mm(l_, r_, nt=True), lhs1, rhs1)
    a_k = each(lambda t: jnp.where(strict, t[0:n_st, 0:n_st], 0.0), x)
    a_b = each(lambda t: jnp.where(strict, t[0:n_st, n_st:], 0.0), x)
    r_kb = each(lambda t: jnp.concatenate([jnp.where(incl, t[n_st:, 0:n_st], 0.0),
                                           jnp.where(incl, t[n_st:, n_st:], 0.0)], axis=1), x)

    p1 = each(lambda l_, s_: _mm(l_, s_, nt=True), lhs1, states)
    rhs_u = each(lambda p_, ak, vt: p_[0:n_st] + _mm(ak, vt), p1, a_k, v_s)

    near = (row // 2) == (lane // 2)
    t_inv = each(lambda t: eye + jnp.where(near, t, 0.0), a_b)
    s = 2
    while s < SLAB:
        off = ((row // (2 * s)) == (lane // (2 * s))) & ((row // s) != (lane // s))
        w = each(lambda ab, ti: _mm(jnp.where(off, ab, 0.0), ti), a_b, t_inv)
        t_inv = each(lambda ti, w_: ti + _mm(ti, w_), t_inv, w)
        s *= 2
    u = each(_mm, t_inv, rhs_u)

    vu = each(lambda x_, y_: jnp.concatenate([x_, y_], axis=0), v_s, u)
    y = each(lambda p_, rk_, vu_: p_[n_st:] + _mm(rk_, vu_), p1, r_kb, vu)

    def finish(y_, w_, b_, bonus_, g_):
        mean = jnp.sum(y_, axis=1, keepdims=True) * (1.0 / HEAD_DIM)
        dev = jnp.where(own, y_ - mean, 0.0)
        var = jnp.sum(dev * dev, axis=1, keepdims=True) * (1.0 / HEAD_DIM)
        out = jnp.where(own, dev * lax.rsqrt(var + GN_EPS) * w_ + b_, 0.0) + bonus_
        return (out[0:SLAB] + out[SLAB:n_st]) * g_
    z = each(finish, y, gnw, gnb, bonus, g)

    vu_t = each(lambda t: t.T, vu)
    kb_h = each(lambda x_, y_: jnp.concatenate([x_, y_], axis=0), k_h, b_h)
    new_states = each(lambda s_, t_, vt, kb: s_ * jnp.exp(t_[0:1, :]) + _mm(vt, kb), states, tot, vu_t, kb_h)
    return z, new_states


def _scan_kernel(n_seq, n_pp, r_ref, ld_ref, k_ref, v_ref, a_ref, g_ref, kk_ref, ka_ref, rk_ref, gw_ref,
                 gb_ref, z_ref, s_out_ref, s_scr):
    c = pl.program_id(1)

    @pl.when(c == 0)
    def _():
        s_scr[...] = jnp.zeros_like(s_scr)

    chains = [(b, pp) for b in range(n_seq) for pp in range(n_pp)]
    lanes = [slice(pp * LANES, (pp + 1) * LANES) for _, pp in chains]
    acts = [tuple(ref[b, :, sl] for ref in (r_ref, ld_ref, k_ref, v_ref, a_ref, g_ref))
            for (b, _), sl in zip(chains, lanes)]
    params = [tuple(ref[:, sl] for ref in (kk_ref, ka_ref, rk_ref, gw_ref, gb_ref)) for sl in lanes]
    states = [s_scr[n] for n in range(len(chains))]
    z, new_states = _scan_chains(acts, params, states)
    for n, (b, _) in enumerate(chains):
        z_ref[b, :, lanes[n]] = z[n].astype(bf16)
        s_scr[n] = new_states[n]

    @pl.when(c == pl.num_programs(1) - 1)
    def _():
        for n, (b, pp) in enumerate(chains):
            s = s_scr[n]
            s_out_ref[b, 2 * pp] = s[0:HEAD_DIM, 0:HEAD_DIM]
            s_out_ref[b, 2 * pp + 1] = s[HEAD_DIM:, HEAD_DIM:]


def _scan(proj, rwp, n_batch, seq_len):
    n_pp = SCAN_CHAINS // n_batch
    w = LANES * n_pp
    act = pl.BlockSpec((n_batch, SLAB, w), lambda p, c: (0, c, p))
    vec = pl.BlockSpec((1, w), lambda p, c: (0, p))
    st_spec = pl.BlockSpec((n_batch, 2 * n_pp, HEAD_DIM, HEAD_DIM), lambda p, c: (0, p, 0, 0))
    z, s_out = pl.pallas_call(
        functools.partial(_scan_kernel, n_batch, n_pp),
        grid=(N_PAIRS // n_pp, seq_len // SLAB),
        in_specs=[act] * 6 + [vec] * 5,
        out_specs=[act, st_spec],
        out_shape=[jax.ShapeDtypeStruct((n_batch, seq_len, D_MODEL), bf16),
                   jax.ShapeDtypeStruct((n_batch, N_HEADS, HEAD_DIM, HEAD_DIM), f32)],
        scratch_shapes=[pltpu.VMEM((n_batch * n_pp, 2 * SLAB, LANES), f32)],
        compiler_params=_cparams(2),
        name="rwkv_scan",
    )(*[p.reshape(n_batch, seq_len, D_MODEL) for p in proj],
      rwp["kk"], rwp["ka"], rwp["rk"], rwp["gn_w"], rwp["gn_b"])
    return z.reshape(n_batch * seq_len, D_MODEL), s_out


def _scan_lanes_kernel(n_tok, r_ref, ld_ref, k_ref, v_ref, a_ref, g_ref, kk_ref, ka_ref, rk_ref, gw_ref, gb_ref,
                       s_in_ref, z_ref, s_out_ref, kk_s, w_s, b_s, k2_s, r_s, v_s, y_s):
    hd = HEAD_DIM
    colsum = lambda t: jnp.sum(t, axis=0, keepdims=True)
    for t in range(n_tok):
        r_t, k_t, v_t, a_t = r_ref[t].T, k_ref[t].T, v_ref[t].T, a_ref[t].T
        kk_raw = k_t * kk_ref[...]
        k2 = k_t * (1.0 + (a_t - 1.0) * ka_ref[...])
        w = jnp.exp(ld_ref[t].T)
        for e in range(2):
            sl = slice(e * hd, (e + 1) * hd)
            kr = kk_raw[sl]
            kk = kr * (1.0 / jnp.maximum(jnp.sqrt(colsum(kr * kr)), 1e-12))
            kk_s[e, t] = kk
            b_s[e, t] = kk * a_t[sl]
            k2_s[e, t] = k2[sl]
            w_s[e, t] = w[sl]
            r_s[e, t] = r_t[sl]
            v_s[e, t] = v_t[sl]

    rows_per_iter = 2

    def per_rows(i, carry):
        chains = [(e, rows_per_iter * i + d) for e in range(2) for d in range(rows_per_iter)]
        s = [s_in_ref[e, vi] for e, vi in chains]
        for t in range(n_tok):
            for c, (e, vi) in enumerate(chains):
                s_kk = colsum(s[c] * kk_s[e, t])
                s[c] = s[c] * w_s[e, t] - s_kk * b_s[e, t] + v_s[e, t, pl.ds(vi, 1), :] * k2_s[e, t]
                y_s[e, t, pl.ds(vi, 1), :] = colsum(s[c] * r_s[e, t])
        for c, (e, vi) in enumerate(chains):
            s_out_ref[e, vi] = s[c]
        return carry
    lax.fori_loop(0, hd // rows_per_iter, per_rows, 0)

    for t in range(n_tok):
        outs = []
        for e in range(2):
            sl = slice(e * hd, (e + 1) * hd)
            y = y_s[e, t]
            dev = y - colsum(y) * (1.0 / hd)
            var = colsum(dev * dev) * (1.0 / hd)
            bonus = colsum(r_s[e, t] * k2_s[e, t] * rk_ref[sl, :]) * v_s[e, t]
            outs.append(dev * lax.rsqrt(var + GN_EPS) * gw_ref[sl, :] + gb_ref[sl, :] + bonus)
        z_ref[t] = (jnp.concatenate(outs, axis=0).T * g_ref[t]).astype(bf16)


def _scan_lanes(proj, rwp, n_batch, n_tok, state):
    acts = [p.reshape(n_tok, n_batch, D_MODEL) for p in proj]
    lane_bc = lambda v: jnp.broadcast_to(v.reshape(D_MODEL, 1), (D_MODEL, n_batch))
    params = [lane_bc(rwp[n]) for n in ("kk", "ka", "rk", "gn_w", "gn_b")]
    act = pl.BlockSpec((n_tok, n_batch, LANES), lambda p: (0, 0, p))
    par = pl.BlockSpec((LANES, n_batch), lambda p: (p, 0))
    st_spec = pl.BlockSpec((2, HEAD_DIM, HEAD_DIM, n_batch), lambda p: (p, 0, 0, 0))
    per_tok = pltpu.VMEM((2, n_tok, HEAD_DIM, n_batch), f32)
    z, s_out = pl.pallas_call(
        functools.partial(_scan_lanes_kernel, n_tok),
        grid=(N_PAIRS,),
        in_specs=[act] * 6 + [par] * 5 + [st_spec],
        out_specs=[act, st_spec],
        out_shape=[jax.ShapeDtypeStruct((n_tok, n_batch, D_MODEL), bf16),
                   jax.ShapeDtypeStruct(state.shape, f32)],
        scratch_shapes=[per_tok] * 7,
        compiler_params=_cparams(1),
        name="rwkv_decode_scan",
    )(*acts, *params, state)
    return z.reshape(n_tok * n_batch, D_MODEL), s_out


def _proj_res_kernel(z_ref, w_ref, b_ref, x_ref, gt_ref, o_ref):
    out = _dot(z_ref[...], w_ref[...]) + b_ref[...]
    o_ref[...] = x_ref[...] + _rows(gt_ref[...], x_ref.shape[0]) * out


def _proj_res(z, w, bias, x, mod, tok_major, seq_len):
    m = x.shape[0]
    tm = min(ROW_TILE, m)
    rows = pl.BlockSpec((tm, D_MODEL), lambda i: (i, 0))
    return pl.pallas_call(
        _proj_res_kernel,
        grid=(m // tm,),
        in_specs=[rows,
                  pl.BlockSpec((D_MODEL, D_MODEL), lambda i: (0, 0), pipeline_mode=pl.Buffered(1)),
                  pl.BlockSpec((1, D_MODEL), lambda i: (0, 0)),
                  rows,
                  _mod_spec(tok_major, tm, seq_len, 5)],
        out_specs=rows,
        out_shape=jax.ShapeDtypeStruct((m, D_MODEL), f32),
        compiler_params=_cparams(1),
        name="proj_res",
    )(z, w, bias.reshape(1, D_MODEL), x, mod)


QKV_COLS = D_MODEL + 2 * N_KV_HEADS * LANES


def _qkv_kernel(x_ref, ng_ref, sh_ref, sc_ref, w_ref, b_ref, cos_ref, sin_ref, o_ref, h_scr):
    @pl.when(pl.program_id(1) == 0)
    def _():
        tm = x_ref.shape[0]
        h_scr[...] = _rms_mod(x_ref[...], ng_ref[...], _rows(sh_ref[...], tm), _rows(sc_ref[...], tm)).astype(bf16)

    cos = cos_ref[...]
    sin = sin_ref[...]
    h = h_scr[...]
    for c0 in range(0, o_ref.shape[1], MXU_COLS):
        acc = _dot(h, w_ref[:, c0:c0 + MXU_COLS]) + b_ref[:, c0:c0 + MXU_COLS]
        for c in range(0, MXU_COLS, LANES):
            xc = acc[:, c:c + LANES]
            o_ref[:, c0 + c:c0 + c + LANES] = xc * cos + pltpu.roll(xc, LANES // 2, 1) * sin


def _qkv(x, mod, tok_major, seq_len, norm_g, w, b, cos, sin):
    m = x.shape[0]
    tm = min(ROW_TILE, m)
    tn = 512
    n_rope = (D_MODEL + N_KV_HEADS * LANES) // tn
    n_pos_blocks = cos.shape[0] // tm
    vec = pl.BlockSpec((1, D_MODEL), lambda i, j: (0, 0))
    tab = pl.BlockSpec((tm, LANES), lambda i, j: (i % n_pos_blocks, j // n_rope))
    return pl.pallas_call(
        _qkv_kernel,
        grid=(m // tm, QKV_COLS // tn),
        in_specs=[pl.BlockSpec((tm, D_MODEL), lambda i, j: (i, 0)),
                  vec,
                  _mod_spec(tok_major, tm, seq_len, 3),
                  _mod_spec(tok_major, tm, seq_len, 4),
                  pl.BlockSpec((D_MODEL, tn), lambda i, j: (0, j)),
                  pl.BlockSpec((1, tn), lambda i, j: (0, j)),
                  tab, tab],
        out_specs=pl.BlockSpec((tm, tn), lambda i, j: (i, j)),
        out_shape=jax.ShapeDtypeStruct((m, QKV_COLS), f32),
        scratch_shapes=[pltpu.VMEM((tm, D_MODEL), bf16)],
        compiler_params=_cparams(2),
        name="swa_qkv",
    )(x, norm_g.reshape(1, D_MODEL), mod, mod, w, b.reshape(1, QKV_COLS), cos, sin)


def _rope_tables(positions):
    half = HEAD_DIM // 2
    inv_freq = ROPE_THETA ** (-jnp.arange(half, dtype=f32) / half)
    ang = positions.astype(f32)[:, None] * inv_freq[None, :]
    cos = jnp.tile(jnp.cos(ang), (1, LANES // half))
    sin = jnp.sin(ang)
    sin = jnp.concatenate([-sin, -sin, sin, sin], axis=1)
    return (jnp.concatenate([cos, jnp.ones_like(cos)], axis=1),
            jnp.concatenate([sin, jnp.zeros_like(sin)], axis=1))


def _head_masks(rows):
    lane = lax.broadcasted_iota(jnp.int32, (rows, LANES), 1)
    first = (lane % HEAD_DIM) < HEAD_DIM // 2
    return first, ~first


def _attn_prompt_kernel(q_ref, kp_ref, kc_ref, vp_ref, vc_ref, sink_ref, o_ref):
    n = pl.program_id(1)
    blk = WINDOW
    m0, m1 = _head_masks(blk)
    m0k, m1k = _head_masks(2 * blk)
    qi = lax.broadcasted_iota(jnp.int32, (2 * blk, 2 * blk), 0) % blk
    sj = lax.broadcasted_iota(jnp.int32, (2 * blk, 2 * blk), 1)
    visible = (sj > qi) & (sj <= qi + blk) & ((n > 0) | (sj >= blk))
    top = lax.broadcasted_iota(jnp.int32, (2 * blk, 1), 0) < blk
    for c in range(N_KV_HEADS):
        sl = slice(c * LANES, (c + 1) * LANES)
        kd = jnp.concatenate([kp_ref[:, sl], kc_ref[:, sl]], axis=0).astype(bf16)
        vd = jnp.concatenate([vp_ref[:, sl], vc_ref[:, sl]], axis=0)
        vcat = jnp.concatenate([jnp.where(m0k, vd, 0.0), jnp.where(m1k, vd, 0.0)], axis=0).astype(bf16)
        pairs = [c * (GQA_GROUP // 2) + jj for jj in range(GQA_GROUP // 2)]
        each = lambda fn, *lists: [fn(*args) for args in zip(*lists)]
        qp = [q_ref[:, pair * LANES:(pair + 1) * LANES] * ATTN_SCALE for pair in pairs]
        qs = each(lambda t: jnp.concatenate([jnp.where(m0, t, 0.0), jnp.where(m1, t, 0.0)], axis=0).astype(bf16), qp)
        s = each(lambda t: jnp.where(visible, _dot_nt(t, kd), NEG_BIG), qs)
        sk = [jnp.where(top, sink_ref[2 * pair], sink_ref[2 * pair + 1]) for pair in pairs]
        mx = each(lambda t, k_: jnp.maximum(jnp.max(t, axis=1, keepdims=True), k_), s, sk)
        p = each(lambda t, m_: jnp.exp(t - m_), s, mx)
        den = each(lambda t, k_, m_: jnp.sum(t, axis=1, keepdims=True) + jnp.exp(k_ - m_), p, sk, mx)
        p = each(lambda t, d_: (t * (1.0 / d_)).astype(bf16), p, den)
        pcat = each(lambda t: jnp.concatenate([t[0:blk], t[blk:]], axis=1), p)
        for pair, t in zip(pairs, pcat):
            o_ref[:, pair * LANES:(pair + 1) * LANES] = _dot(t, vcat).astype(bf16)


def _attn_prompt(qkv, sink, n_batch, seq_len):
    nb = seq_len // WINDOW
    kw = N_KV_HEADS * LANES
    k_blk = D_MODEL // kw
    cur = lambda off: pl.BlockSpec((WINDOW, kw), lambda b, n: (b * nb + n, k_blk + off))
    prev = lambda off: pl.BlockSpec((WINDOW, kw), lambda b, n: (b * nb + jnp.maximum(n - 1, 0), k_blk + off))
    return pl.pallas_call(
        _attn_prompt_kernel,
        grid=(n_batch, nb),
        in_specs=[pl.BlockSpec((WINDOW, D_MODEL), lambda b, n: (b * nb + n, 0)),
                  prev(0), cur(0), prev(1), cur(1),
                  pl.BlockSpec(memory_space=pltpu.SMEM)],
        out_specs=pl.BlockSpec((WINDOW, D_MODEL), lambda b, n: (b * nb + n, 0)),
        out_shape=jax.ShapeDtypeStruct((n_batch * seq_len, D_MODEL), bf16),
        compiler_params=_cparams(2),
        name="swa_prompt",
    )(qkv, qkv, qkv, qkv, qkv, sink)


SEQ_PER_GROUP = 4


def _attn_cached_kernel(n_tok, groups, q_ref, kn_ref, vn_ref, kc_ref, vc_ref, sink_ref, o_ref):
    rows = SEQ_PER_GROUP * n_tok
    n_st = GQA_GROUP * rows
    m0, m1 = _head_masks(rows)
    half = HEAD_DIM // 2
    pair_rows = lambda t: jnp.concatenate([t[0:half], t[0:half], t[half:], t[half:]], axis=0)
    srow = lax.broadcasted_iota(jnp.int32, (n_st, 1), 0)
    row_seq = (srow % rows) // n_tok
    row_tok = srow % n_tok
    key_c = lax.broadcasted_iota(jnp.int32, (n_st, WINDOW), 1)
    vis_c = key_c > row_tok
    key_n = lax.broadcasted_iota(jnp.int32, (n_st, rows), 1)
    vis_n = ((key_n // n_tok) == row_seq) & ((key_n % n_tok) <= row_tok)
    for gi in range(groups):
        rs = slice(gi * rows, (gi + 1) * rows)
        for cp in range(N_KV_HEADS // 2):
            for ce in range(2):
                c = 2 * cp + ce
                sl = slice(c * LANES, (c + 1) * LANES)
                kt = [kc_ref[gi * SEQ_PER_GROUP + b, c * HEAD_DIM:(c + 1) * HEAD_DIM, :] for b in range(SEQ_PER_GROUP)]
                vt = [vc_ref[gi * SEQ_PER_GROUP + b, c * HEAD_DIM:(c + 1) * HEAD_DIM, :] for b in range(SEQ_PER_GROUP)]
                kn = kn_ref[rs, sl].astype(bf16)
                vn = vn_ref[rs, sl].astype(bf16)
                pieces = []
                for jj in range(GQA_GROUP // 2):
                    pair = c * (GQA_GROUP // 2) + jj
                    qp = q_ref[rs, pair * LANES:(pair + 1) * LANES]
                    pieces += [jnp.where(m0, qp, 0.0), jnp.where(m1, qp, 0.0)]
                qs = jnp.concatenate(pieces, axis=0).astype(bf16)
                s_c = jnp.zeros((n_st, WINDOW), f32)
                for b in range(SEQ_PER_GROUP):
                    kx = pair_rows(kt[b]).astype(bf16)
                    s_c = jnp.where(row_seq == b, _dot(qs, kx), s_c)
                s_c = jnp.where(vis_c, s_c * ATTN_SCALE, NEG_BIG)
                s_n = jnp.where(vis_n, _dot_nt(qs, kn) * ATTN_SCALE, NEG_BIG)
                sk = sink_ref[c]
                sk = sk[:, 0:1]
                mx = jnp.maximum(jnp.maximum(jnp.max(s_c, axis=1, keepdims=True),
                                             jnp.max(s_n, axis=1, keepdims=True)), sk)
                p_c = jnp.exp(s_c - mx)
                p_n = jnp.exp(s_n - mx)
                den = (jnp.sum(p_c, axis=1, keepdims=True) + jnp.sum(p_n, axis=1, keepdims=True)
                       + jnp.exp(sk - mx))
                inv_den = 1.0 / den
                p_c = (p_c * inv_den).astype(bf16)
                p_n = (p_n * inv_den).astype(bf16)
                o = _dot(p_n, vn)
                for b in range(SEQ_PER_GROUP):
                    vx = pair_rows(vt[b]).astype(bf16)
                    o = o + jnp.where(row_seq == b, _dot_nt(p_c, vx), 0.0)
                for jj in range(GQA_GROUP // 2):
                    pair = c * (GQA_GROUP // 2) + jj
                    o0 = o[(2 * jj) * rows:(2 * jj + 1) * rows]
                    o1 = o[(2 * jj + 1) * rows:(2 * jj + 2) * rows]
                    o_ref[rs, pair * LANES:(pair + 1) * LANES] = jnp.where(m0, o0, o1).astype(bf16)


def _attn_cached(qkv, k_cache, v_cache, sink, n_batch, n_tok):
    groups = 2
    seqs = SEQ_PER_GROUP * groups
    rows = seqs * n_tok
    kw = N_KV_HEADS * LANES
    k_blk = D_MODEL // kw
    n_st = GQA_GROUP * SEQ_PER_GROUP * n_tok
    head = (jnp.arange(N_KV_HEADS)[:, None] * GQA_GROUP
            + (jnp.arange(n_st)[None, :] // (SEQ_PER_GROUP * n_tok)))
    sink_tab = jnp.broadcast_to(sink[head][:, :, None], (N_KV_HEADS, n_st, LANES))
    cache_spec = pl.BlockSpec((seqs, N_KV_HEADS * HEAD_DIM, WINDOW), lambda i: (i, 0, 0))
    return pl.pallas_call(
        functools.partial(_attn_cached_kernel, n_tok, groups),
        grid=(n_batch // seqs,),
        in_specs=[pl.BlockSpec((rows, D_MODEL), lambda i: (i, 0)),
                  pl.BlockSpec((rows, kw), lambda i: (i, k_blk)),
                  pl.BlockSpec((rows, kw), lambda i: (i, k_blk + 1)),
                  cache_spec, cache_spec,
                  pl.BlockSpec((N_KV_HEADS, n_st, LANES), lambda i: (0, 0, 0))],
        out_specs=pl.BlockSpec((rows, D_MODEL), lambda i: (i, 0)),
        out_shape=jax.ShapeDtypeStruct((n_batch * n_tok, D_MODEL), bf16),
        compiler_params=_cparams(1),
        name="swa_cached",
    )(qkv, qkv, qkv, k_cache, v_cache, sink_tab)


def _undup(t):
    quarters = t.reshape(t.shape[0], N_KV_HEADS, 4, HEAD_DIM // 2)
    return jnp.concatenate([quarters[:, :, 0], quarters[:, :, 2]], axis=-1)


def _trunk(x, mod_all, tok_major, n_batch, seq_len, pos0, wkv_in, shift_in, k_in, v_in, wts):
    to_seq_major = lambda t: t.reshape(seq_len, n_batch, -1).transpose(1, 0, 2).reshape(n_batch * seq_len, -1)
    to_tok_major = lambda t: t.reshape(n_batch, seq_len, -1).transpose(1, 0, 2).reshape(n_batch * seq_len, -1)
    mod = mod_all[0]
    ffn_w = dict(wts["ffn"])
    x, ffn_w[0, 0] = _ffn(x, mod, tok_major, seq_len, 0, wts["norm_g"][0, 0], ffn_w[0, 0])
    s0 = shift_in[0] if tok_major else jnp.zeros((n_batch, 1, D_MODEL), f32)
    *proj, h_tail = _rwkv_proj(x, mod, tok_major, seq_len, wts["norm_g"][0, 1], s0, wts["rw"])
    if tok_major:
        z, wkv_t = _scan_lanes(proj, wts["rw"], n_batch, seq_len, jnp.transpose(wkv_in[0], (1, 2, 3, 0)))
        wkv_out = jnp.transpose(wkv_t, (3, 0, 1, 2))
        shift_out = h_tail
    else:
        z, wkv_out = _scan(proj, wts["rw"], n_batch, seq_len)
        shift_out = h_tail.reshape(n_batch, -1, 8, D_MODEL)[:, -1, -1]
    x = _proj_res(z, wts["rw"]["wo"], jnp.zeros((D_MODEL,), f32), x, mod, tok_major, seq_len)
    x, ffn_w[0, 1] = _ffn(x, mod, tok_major, seq_len, 6, wts["norm_g"][0, 2], ffn_w[0, 1])
    mod = mod_all[1]
    x, ffn_w[1, 0] = _ffn(x, mod, tok_major, seq_len, 0, wts["norm_g"][1, 0], ffn_w[1, 0])
    positions = pos0 + jnp.arange(seq_len)
    if tok_major:
        positions = jnp.repeat(positions, n_batch)
    cos, sin = _rope_tables(positions)
    qkv = _qkv(x, mod, tok_major, seq_len, wts["norm_g"][1, 1], wts["sw_wqkv"], wts["sw_bqkv"], cos, sin)
    kw = N_KV_HEADS * LANES
    if k_in is None:
        att = _attn_prompt(qkv, wts["sw_sink"], n_batch, seq_len)
        tail = qkv.reshape(n_batch, seq_len, QKV_COLS)[:, -WINDOW:, D_MODEL:]
        k_new = _undup(tail[..., :kw].reshape(n_batch * WINDOW, kw)).reshape(n_batch, WINDOW, N_KV_HEADS, HEAD_DIM)
        v_new = _undup(tail[..., kw:].reshape(n_batch * WINDOW, kw)).reshape(n_batch, WINDOW, N_KV_HEADS, HEAD_DIM)
    else:
        win = k_in.shape[2]
        qkv = to_seq_major(qkv)
        cache_t = lambda t: jnp.transpose(t[0], (0, 2, 3, 1)).reshape(n_batch, N_KV_HEADS * HEAD_DIM, win)
        att = _attn_cached(qkv, cache_t(k_in), cache_t(v_in), wts["sw_sink"], n_batch, seq_len)
        att = to_tok_major(att)
        k_tok = _undup(qkv[:, D_MODEL:D_MODEL + kw]).reshape(n_batch, seq_len, N_KV_HEADS, HEAD_DIM)
        v_tok = _undup(qkv[:, D_MODEL + kw:]).reshape(n_batch, seq_len, N_KV_HEADS, HEAD_DIM)
        k_new = jnp.concatenate([k_in[0], k_tok], axis=1)[:, -win:]
        v_new = jnp.concatenate([v_in[0], v_tok], axis=1)[:, -win:]
    x = _proj_res(att, wts["sw_wo"], wts["sw_bo"], x, mod, tok_major, seq_len)
    y, ffn_w[1, 1] = _ffn(x, mod, tok_major, seq_len, 6, wts["norm_g"][1, 2], ffn_w[1, 1], final_g=wts["final_g"])
    return (y, wkv_out[None], shift_out[None], k_new[None], v_new[None]), ffn_w


def _dup_heads(w):
    lead = w.shape[:-1]
    w4 = w.reshape(lead + (N_KV_HEADS, 1, HEAD_DIM))
    return jnp.broadcast_to(w4, lead + (N_KV_HEADS, 2, HEAD_DIM)).reshape(lead + (N_KV_HEADS * LANES,))


def _pair_layout(w, axis):
    shape = w.shape
    half = HEAD_DIM // 2
    w = w.reshape(shape[:axis] + (shape[axis] // LANES, 2, 2, half) + shape[axis + 1:])
    return jnp.swapaxes(w, axis + 1, axis + 2).reshape(shape)


def _pad_cols(w):
    return jnp.pad(w, ((0, 0), (0, LORA_PAD - w.shape[1])))


def _pad_rows(w):
    return jnp.pad(w, ((0, LORA_PAD - w.shape[0]), (0, 0)))


def kernel(x_prompt, x_sample, state_rwkv_wkv, state_rwkv_shift, cache_swa_k, cache_swa_v, c_prompt, c_sample, norm_g, w_ada, b_ada, w_ffn_in, w_ffn_out, rw_mu, rw_wrkv, rw_w0, rw_w1, rw_w2, rw_a0, rw_a1, rw_a2, rw_g1, rw_g2, rw_kk, rw_ka, rw_rk, rw_gn_w, rw_gn_b, rw_wo, sw_wqkv, sw_bqkv, sw_sink, sw_wo, sw_bo, final_g):
    n_p, seq_p, _ = x_prompt.shape
    n_s, seq_s, _ = x_sample.shape
    nq = N_HEADS * HEAD_DIM
    nkv = N_KV_HEADS * HEAD_DIM
    row = lambda t: t.reshape(1, D_MODEL)
    rw = dict(
        mu=rw_mu[0], wrkv=rw_wrkv[0].astype(bf16),
        w1=_pad_cols(rw_w1[0]).astype(bf16), a1=_pad_cols(rw_a1[0]).astype(bf16), g1=rw_g1[0].astype(bf16),
        w2=_pad_rows(rw_w2[0]).astype(bf16), a2=_pad_rows(rw_a2[0]).astype(bf16), g2=rw_g2[0].astype(bf16),
        w0=row(rw_w0[0]), a0=row(rw_a0[0]), kk=row(rw_kk[0]), ka=row(rw_ka[0]), rk=row(rw_rk[0]),
        gn_w=row(rw_gn_w[0]), gn_b=row(rw_gn_b[0]), wo=rw_wo[0].astype(bf16))
    wq = sw_wqkv[0]
    bq = sw_bqkv[0]
    wts = dict(
        norm_g=norm_g, final_g=final_g,
        ffn={(l, s): (w_ffn_in, w_ffn_out, l, s) for l in range(2) for s in range(2)}, rw=rw,
        sw_wqkv=_pair_layout(
            jnp.concatenate([wq[:, :nq], _dup_heads(wq[:, nq:nq + nkv]), _dup_heads(wq[:, nq + nkv:])], axis=1),
            1).astype(bf16),
        sw_bqkv=_pair_layout(
            jnp.concatenate([bq[:nq], _dup_heads(bq[nq:nq + nkv]), _dup_heads(bq[nq + nkv:])]), 0),
        sw_sink=sw_sink[0], sw_wo=_pair_layout(sw_wo[0], 0).astype(bf16), sw_bo=sw_bo[0])

    n_c = n_p + n_s
    pad = (-n_c) % 8
    c_all = jnp.concatenate([c_sample, c_prompt, jnp.zeros((pad, D_MODEL), f32)], axis=0)
    mod_s = _ada(c_all, w_ada, b_ada)
    mod_p = [m[n_s:n_c].reshape(n_p, 1, N_MOD * D_MODEL) for m in mod_s]

    (y_s, s_wkv, s_shift, s_k, s_v), wts["ffn"] = _trunk(
        jnp.transpose(x_sample, (1, 0, 2)).reshape(n_s * seq_s, D_MODEL), mod_s, True, n_s, seq_s, PAST_LEN,
        state_rwkv_wkv, state_rwkv_shift, cache_swa_k, cache_swa_v, wts)
    (y_p, p_wkv, p_shift, p_k, p_v), _ = _trunk(
        x_prompt.reshape(n_p * seq_p, D_MODEL), mod_p, False, n_p, seq_p, 0, None, None, None, None, wts)
    y_s = jnp.transpose(y_s.reshape(seq_s, n_s, D_MODEL), (1, 0, 2))
    return (y_p.reshape(n_p, seq_p, D_MODEL), y_s, p_wkv, p_shift, p_k, p_v, s_wkv, s_shift, s_k, s_v)
```

```python
import functools
import math

import jax
import jax.numpy as jnp
from jax import lax
from jax.experimental import pallas as pl
from jax.experimental.pallas import tpu as pltpu

f32 = jnp.float32
bf16 = jnp.bfloat16

D_MODEL = 2048
HEAD_DIM = 64
N_HEADS = D_MODEL // HEAD_DIM
N_KV_HEADS = 4
GQA_GROUP = N_HEADS // N_KV_HEADS
WINDOW = 128
ATTN_SCALE = HEAD_DIM ** -0.5
ROPE_THETA = 10000.0
D_FF = 5632
N_MOD = 9
RMS_EPS = 1e-6
GN_EPS = 64e-5
PAST_LEN = 8192
LANES = 128
MXU_COLS = 256
N_PAIRS = D_MODEL // LANES
LORA_PAD = 128
SLAB = 64
SCAN_CHAINS = 16
VMEM_LIMIT = 56 * 1024 * 1024
NEG_BIG = -1e30
ROW_TILE = 512


def _cparams(n_axes):
    return pltpu.CompilerParams(dimension_semantics=("arbitrary",) * n_axes,
                                vmem_limit_bytes=VMEM_LIMIT)


def _dot(a, b):
    return jnp.dot(a, b, preferred_element_type=f32)


def _dot_nt(a, b):
    return lax.dot_general(a, b, (((1,), (1,)), ((), ())), preferred_element_type=f32)


def _split(x, n):
    if x.dtype == bf16:
        return [x]
    parts = []
    rem = x
    for i in range(n):
        p = rem.astype(bf16)
        parts.append(p)
        if i + 1 < n:
            rem = rem - p.astype(f32)
    return parts


def _mm(a, b, pa=1, pb=1, nt=False):
    a_parts = _split(a, pa)
    b_parts = _split(b, pb)
    order = max(len(a_parts), len(b_parts))
    acc = None
    for i, x in enumerate(a_parts):
        for j, y in enumerate(b_parts):
            if i + j >= order:
                continue
            t = _dot_nt(x, y) if nt else _dot(x, y)
            acc = t if acc is None else acc + t
    return acc


def _rms_mod(x, g, shift, scale):
    return x * lax.rsqrt(jnp.mean(x * x, axis=-1, keepdims=True) + RMS_EPS) * (g * (1.0 + scale)) + shift


def _mod_spec(tok_major, tm, seq_len, idx):
    if tok_major:
        return pl.BlockSpec((tm // seq_len, D_MODEL), lambda i, *_: (0, idx))
    return pl.BlockSpec((None, 1, D_MODEL), lambda i, *_: ((i * tm) // seq_len, 0, idx))


def _rows(v, tm):
    n = v.shape[0]
    return v if n in (1, tm) else jnp.concatenate([v] * (tm // n), axis=0)


def _ada_kernel(c_ref, w_ref, b_ref, *o_refs):
    c = c_ref[...]
    s = (c * jax.nn.sigmoid(c)).astype(bf16)
    for layer, o_ref in enumerate(o_refs):
        o_ref[...] = _dot(s, w_ref[layer].astype(bf16)) + b_ref[layer]


def _ada(c_all, w_ada, b_ada):
    n_layers, _, n_out = w_ada.shape
    rows = c_all.shape[0]
    tn = 1024
    return pl.pallas_call(
        _ada_kernel,
        grid=(n_out // tn,),
        in_specs=[pl.BlockSpec((rows, D_MODEL), lambda j: (0, 0)),
                  pl.BlockSpec((n_layers, D_MODEL, tn), lambda j: (0, 0, j)),
                  pl.BlockSpec((n_layers, 1, tn), lambda j: (0, 0, j))],
        out_specs=[pl.BlockSpec((rows, tn), lambda j: (0, j))] * n_layers,
        out_shape=[jax.ShapeDtypeStruct((rows, n_out), f32)] * n_layers,
        compiler_params=_cparams(1),
        name="ada_mod",
    )(c_all, w_ada, b_ada.reshape(n_layers, 1, n_out))


def _ffn_kernel(final_norm, emit_bf16, tiled_gate, x_ref, ng_ref, sh_ref, sc_ref, gt_ref, wg_ref, wu_ref, wo_ref,
                fg_ref, o_ref, *rest):
    rest = list(rest)
    if emit_bf16:
        wg_out, wu_out, wo_out = rest[:3]
        del rest[:3]
    h_scr = rest.pop(0)
    gt_scr = rest.pop(0) if tiled_gate else None
    f = pl.program_id(1)
    tm = x_ref.shape[0]

    @pl.when(f == 0)
    def _():
        h_scr[...] = _rms_mod(x_ref[...], ng_ref[...], _rows(sh_ref[...], tm), _rows(sc_ref[...], tm)).astype(bf16)
        o_ref[...] = x_ref[...]
        if tiled_gate:
            gt_scr[...] = 0.5 * _rows(gt_ref[...], tm)

    wg, wu, wo = wg_ref[...], wu_ref[...], wo_ref[...]
    if emit_bf16:
        wg, wu, wo = wg.astype(bf16), wu.astype(bf16), wo.astype(bf16)
        wg_out[...] = wg
        wu_out[...] = wu
        wo_out[...] = wo
    h = h_scr[...]
    gate = _dot(h, wg)
    up = _dot(h, wu)
    act = (gate * jax.nn.sigmoid(gate) * up).astype(bf16)
    half_gate = gt_scr[...] if tiled_gate else 0.5 * gt_ref[...]
    o_ref[...] += half_gate * _dot(act, wo)

    if final_norm:
        @pl.when(f == pl.num_programs(1) - 1)
        def _():
            y = o_ref[...]
            o_ref[...] = y * lax.rsqrt(jnp.mean(y * y, axis=-1, keepdims=True) + RMS_EPS) * fg_ref[...]


def _ffn(x, mod, tok_major, seq_len, mod_base, norm_g, weights, final_g=None):
    m = x.shape[0]
    tm = min(ROW_TILE, m)
    emit = len(weights) == 4
    tf = 256 if emit else 512
    nf = D_FF // tf
    vec = pl.BlockSpec((1, D_MODEL), lambda i, j: (0, 0))
    fg = jnp.ones((1, D_MODEL), f32) if final_g is None else final_g.reshape(1, D_MODEL)
    w_in_spec = pl.BlockSpec((D_MODEL, tf), lambda i, j: (0, j))
    w_out_spec = pl.BlockSpec((tf, D_MODEL), lambda i, j: (j, 0))
    out_specs = [pl.BlockSpec((tm, D_MODEL), lambda i, j: (i, 0))]
    out_shape = [jax.ShapeDtypeStruct((m, D_MODEL), f32)]
    if emit:
        assert m == tm, "the weight copies are written once, by a single row tile"
        w_ffn_in, w_ffn_out, layer, slot = weights
        w_args = (w_ffn_in, w_ffn_in, w_ffn_out)
        w_specs = [pl.BlockSpec((None, None, D_MODEL, tf), lambda i, j: (layer, slot, 0, j)),
                   pl.BlockSpec((None, None, D_MODEL, tf), lambda i, j: (layer, slot, 0, j + nf)),
                   pl.BlockSpec((None, None, tf, D_MODEL), lambda i, j: (layer, slot, j, 0))]
        out_specs += [w_in_spec, w_in_spec, w_out_spec]
        out_shape += [jax.ShapeDtypeStruct((D_MODEL, D_FF), bf16), jax.ShapeDtypeStruct((D_MODEL, D_FF), bf16),
                      jax.ShapeDtypeStruct((D_FF, D_MODEL), bf16)]
    else:
        w_args = weights
        w_specs = [w_in_spec, w_in_spec, w_out_spec]
    outs = pl.pallas_call(
        functools.partial(_ffn_kernel, final_g is not None, emit, tok_major),
        grid=(m // tm, nf),
        in_specs=[pl.BlockSpec((tm, D_MODEL), lambda i, j: (i, 0)),
                  vec,
                  _mod_spec(tok_major, tm, seq_len, mod_base),
                  _mod_spec(tok_major, tm, seq_len, mod_base + 1),
                  _mod_spec(tok_major, tm, seq_len, mod_base + 2)] + w_specs + [vec],
        out_specs=out_specs,
        out_shape=out_shape,
        scratch_shapes=[pltpu.VMEM((tm, D_MODEL), bf16)] + ([pltpu.VMEM((tm, D_MODEL), f32)] if tok_major else []),
        compiler_params=_cparams(2),
        name="ffn",
    )(x, norm_g.reshape(1, D_MODEL), mod, mod, mod, *w_args, fg)
    return (outs[0], tuple(outs[1:])) if emit else (outs[0], weights)


def _rwkv_proj_kernel(seq_len, tm, h_rows, tok_major,
                      x_ref, ng_ref, sh_ref, sc_ref, s0_ref, mu_ref, w1_ref, a1_ref, g1_ref,
                      w_ref, w2_ref, a2_ref, g2_ref, w0_ref, a0_ref,
                      r_ref, ld_ref, k_ref, v_ref, a_ref, g_ref, h_ref,
                      hs, xr, xk, xv, tw, ta, sg):
    i = pl.program_id(0)
    j = pl.program_id(1)

    @pl.when(j == 0)
    def _():
        h = _rms_mod(x_ref[...], ng_ref[...], _rows(sh_ref[...], tm), _rows(sc_ref[...], tm))
        h_ref[...] = h[tm - h_rows:tm, :]
        if tok_major:
            n_seq = s0_ref.shape[0]
            prev = jnp.concatenate([s0_ref[...], h[0:tm - n_seq, :]], axis=0)
        else:
            @pl.when(i == 0)
            def _():
                hs[0:8, :] = jnp.zeros((8, D_MODEL), f32)

            @pl.when(i > 0)
            def _():
                hs[0:8, :] = hs[tm:tm + 8, :]

            hs[8:tm + 8, :] = h
            row = i * tm + lax.broadcasted_iota(jnp.int32, (tm, 1), 0)
            prev = jnp.where(row % seq_len == 0, s0_ref[...], hs[7:tm + 7, :])
        xx = prev - h
        mu = mu_ref[...]
        xr[...] = (h + xx * mu[0:1, :]).astype(bf16)
        xk[...] = (h + xx * mu[2:3, :]).astype(bf16)
        xv[...] = (h + xx * mu[3:4, :]).astype(bf16)
        xw = (h + xx * mu[1:2, :]).astype(bf16)
        tw[...] = jnp.tanh(_dot(xw, w1_ref[...])).astype(bf16)
        xa = (h + xx * mu[4:5, :]).astype(bf16)
        ta[...] = _dot(xa, a1_ref[...]).astype(bf16)
        xg = (h + xx * mu[5:6, :]).astype(bf16)
        sg[...] = jax.nn.sigmoid(_dot(xg, g1_ref[...])).astype(bf16)

    cols = pl.ds(pl.multiple_of(j * r_ref.shape[1], r_ref.shape[1]), r_ref.shape[1])
    r_ref[...] = _dot(xr[...], w_ref[0, :, cols])
    k_ref[...] = _dot(xk[...], w_ref[1, :, cols])
    v_ref[...] = _dot(xv[...], w_ref[2, :, cols])
    z = w0_ref[...] + _dot(tw[...], w2_ref[...])
    ld_ref[...] = -jax.nn.sigmoid(z) * math.exp(-0.5)
    a_ref[...] = jax.nn.sigmoid(a0_ref[...] + _dot(ta[...], a2_ref[...]))
    g_ref[...] = _dot(sg[...], g2_ref[...])


def _rwkv_proj(x, mod, tok_major, seq_len, norm_g, s0, rwp):
    m = x.shape[0]
    tm = min(ROW_TILE, m)
    tn = 256
    h_rows = tm // seq_len if tok_major else 8
    full = lambda shape: pl.BlockSpec(shape, lambda i, j: (0,) * len(shape))
    col = lambda rows: pl.BlockSpec((rows, tn), lambda i, j: (0, j))
    w_res = pl.BlockSpec((3, D_MODEL, D_MODEL), lambda i, j: (0, 0, 0), pipeline_mode=pl.Buffered(1))
    if tok_major:
        assert m == tm
        s0_spec = pl.BlockSpec((tm // seq_len, D_MODEL), lambda i, j: (0, 0))
    else:
        s0_spec = pl.BlockSpec((None, 1, D_MODEL), lambda i, j: ((i * tm) // seq_len, 0, 0))
    out_spec = pl.BlockSpec((tm, tn), lambda i, j: (i, j))
    out_sds = jax.ShapeDtypeStruct((m, D_MODEL), f32)
    outs = pl.pallas_call(
        functools.partial(_rwkv_proj_kernel, seq_len, tm, h_rows, tok_major),
        grid=(m // tm, D_MODEL // tn),
        in_specs=[pl.BlockSpec((tm, D_MODEL), lambda i, j: (i, 0)),
                  full((1, D_MODEL)),
                  _mod_spec(tok_major, tm, seq_len, 3),
                  _mod_spec(tok_major, tm, seq_len, 4),
                  s0_spec,
                  full((6, D_MODEL)),
                  full((D_MODEL, LORA_PAD)), full((D_MODEL, LORA_PAD)), full((D_MODEL, 256)),
                  w_res,
                  col(LORA_PAD), col(LORA_PAD), col(256), col(1), col(1)],
        out_specs=[out_spec] * 6 + [pl.BlockSpec((h_rows, D_MODEL), lambda i, j: (i, 0))],
        out_shape=[out_sds] * 6 + [jax.ShapeDtypeStruct((m // tm * h_rows, D_MODEL), f32)],
        scratch_shapes=[pltpu.VMEM((tm + 8, D_MODEL), f32),
                        pltpu.VMEM((tm, D_MODEL), bf16), pltpu.VMEM((tm, D_MODEL), bf16),
                        pltpu.VMEM((tm, D_MODEL), bf16),
                        pltpu.VMEM((tm, LORA_PAD), bf16), pltpu.VMEM((tm, LORA_PAD), bf16),
                        pltpu.VMEM((tm, 256), bf16)],
        compiler_params=_cparams(2),
        name="rwkv_proj",
    )(x, norm_g.reshape(1, D_MODEL), mod, mod, s0, rwp["mu"], rwp["w1"], rwp["a1"], rwp["g1"],
      rwp["wrkv"], rwp["w2"], rwp["a2"], rwp["g2"], rwp["w0"], rwp["a0"])
    return outs


def _scan_chains(acts, params, states):
    n_st = 2 * SLAB
    row = lax.broadcasted_iota(jnp.int32, (n_st, LANES), 0)
    lane = lax.broadcasted_iota(jnp.int32, (n_st, LANES), 1)
    own = (row // SLAB) == (lane // HEAD_DIM)
    blk = (row // SLAB) == (lane // SLAB)
    strict = blk & (lane < row)
    incl = blk & (lane <= row)
    eye = jnp.where(row == lane, 1.0, 0.0)
    twice = lambda t: jnp.concatenate([t, t], axis=0)
    st = lambda t: jnp.where(own, twice(t), 0.0)
    each = lambda fn, *lists: [fn(*args) for args in zip(*lists)]

    r, ld, k, v, a, g = [[act[i] for act in acts] for i in range(6)]
    kkp, kap, rkp, gnw, gnb = [[par[i] for par in params] for i in range(5)]

    r64 = lax.broadcasted_iota(jnp.int32, (SLAB, SLAB), 0)
    c64 = lax.broadcasted_iota(jnp.int32, (SLAB, SLAB), 1)
    tri_ones = jnp.concatenate([jnp.where(c64 <= r64, 1.0, 0.0),
                                jnp.ones((SLAB, SLAB), f32)], axis=0).astype(bf16)
    sums = each(lambda t: _mm(tri_ones, t, pb=3), ld)
    cs = each(lambda t: t[0:SLAB], sums)
    tot = each(lambda t: t[SLAB:n_st], sums)

    kk_raw = each(lambda t, p: st(t * p), k, kkp)
    kk = each(lambda t: t * (1.0 / jnp.maximum(jnp.sqrt(jnp.sum(t * t, axis=1, keepdims=True)), 1e-12)), kk_raw)
    k2 = each(lambda kt, at, p: st(kt * (1.0 + (at - 1.0) * p)), k, a, kap)
    b = each(lambda t, at: t * twice(at), kk, a)
    r_s = each(st, r)
    v_s = each(st, v)
    bonus = each(lambda rt, kt, p, vt: jnp.sum(rt * kt * p, axis=1, keepdims=True) * vt, r_s, k2, rkp, v_s)

    e_neg = each(lambda c_: twice(jnp.exp(-c_)), cs)
    e_tail = each(lambda t_, c_: twice(jnp.exp(t_ - c_)), tot, cs)
    a_t = each(lambda t, c_, l_: -t * twice(jnp.exp(c_ - l_)), kk, cs, ld)
    r_t = each(lambda t, c_: t * twice(jnp.exp(c_)), r_s, cs)
    k_t = each(lambda t, e: t * e, k2, e_neg)
    b_t = each(lambda t, e: t * e, b, e_neg)
    k_h = each(lambda t, e: t * e, k2, e_tail)
    b_h = each(lambda t, e: t * e, b, e_tail)

    lhs1 = each(lambda x_, y_: jnp.concatenate([x_, y_], axis=0), a_t, r_t)
    rhs1 = each(lambda x_, y_: jnp.concatenate([x_, y_], axis=0), k_t, b_t)
    x = each(lambda l_, r_: _mm(l_, r_, nt=True), lhs1, rhs1)
    a_k = each(lambda t: jnp.where(strict, t[0:n_st, 0:n_st], 0.0), x)
    a_b = each(lambda t: jnp.where(strict, t[0:n_st, n_st:], 0.0), x)
    r_kb = each(lambda t: jnp.concatenate([jnp.where(incl, t[n_st:, 0:n_st], 0.0),
                                           jnp.where(incl, t[n_st:, n_st:], 0.0)], axis=1), x)

    p1 = each(lambda l_, s_: _mm(l_, s_, nt=True), lhs1, states)
    rhs_u = each(lambda p_, ak, vt: p_[0:n_st] + _mm(ak, vt), p1, a_k, v_s)

    near = (row // 2) == (lane // 2)
    t_inv = each(lambda t: eye + jnp.where(near, t, 0.0), a_b)
    s = 2
    while s < SLAB:
        off = ((row // (2 * s)) == (lane // (2 * s))) & ((row // s) != (lane // s))
        w = each(lambda ab, ti: _mm(jnp.where(off, ab, 0.0), ti), a_b, t_inv)
        t_inv = each(lambda ti, w_: ti + _mm(ti, w_), t_inv, w)
        s *= 2
    u = each(_mm, t_inv, rhs_u)

    vu = each(lambda x_, y_: jnp.concatenate([x_, y_], axis=0), v_s, u)
    y = each(lambda p_, rk_, vu_: p_[n_st:] + _mm(rk_, vu_), p1, r_kb, vu)

    def finish(y_, w_, b_, bonus_, g_):
        mean = jnp.sum(y_, axis=1, keepdims=True) * (1.0 / HEAD_DIM)
        dev = jnp.where(own, y_ - mean, 0.0)
        var = jnp.sum(dev * dev, axis=1, keepdims=True) * (1.0 / HEAD_DIM)
        out = jnp.where(own, dev * lax.rsqrt(var + GN_EPS) * w_ + b_, 0.0) + bonus_
        return (out[0:SLAB] + out[SLAB:n_st]) * g_
    z = each(finish, y, gnw, gnb, bonus, g)

    vu_t = each(lambda t: t.T, vu)
    kb_h = each(lambda x_, y_: jnp.concatenate([x_, y_], axis=0), k_h, b_h)
    new_states = each(lambda s_, t_, vt, kb: s_ * jnp.exp(t_[0:1, :]) + _mm(vt, kb), states, tot, vu_t, kb_h)
    return z, new_states


def _scan_kernel(n_seq, n_pp, r_ref, ld_ref, k_ref, v_ref, a_ref, g_ref, kk_ref, ka_ref, rk_ref, gw_ref,
                 gb_ref, z_ref, s_out_ref, s_scr):
    c = pl.program_id(1)

    @pl.when(c == 0)
    def _():
        s_scr[...] = jnp.zeros_like(s_scr)

    chains = [(b, pp) for b in range(n_seq) for pp in range(n_pp)]
    lanes = [slice(pp * LANES, (pp + 1) * LANES) for _, pp in chains]
    acts = [tuple(ref[b, :, sl] for ref in (r_ref, ld_ref, k_ref, v_ref, a_ref, g_ref))
            for (b, _), sl in zip(chains, lanes)]
    params = [tuple(ref[:, sl] for ref in (kk_ref, ka_ref, rk_ref, gw_ref, gb_ref)) for sl in lanes]
    states = [s_scr[n] for n in range(len(chains))]
    z, new_states = _scan_chains(acts, params, states)
    for n, (b, _) in enumerate(chains):
        z_ref[b, :, lanes[n]] = z[n].astype(bf16)
        s_scr[n] = new_states[n]

    @pl.when(c == pl.num_programs(1) - 1)
    def _():
        for n, (b, pp) in enumerate(chains):
            s = s_scr[n]
            s_out_ref[b, 2 * pp] = s[0:HEAD_DIM, 0:HEAD_DIM]
            s_out_ref[b, 2 * pp + 1] = s[HEAD_DIM:, HEAD_DIM:]


def _scan(proj, rwp, n_batch, seq_len):
    n_pp = SCAN_CHAINS // n_batch
    w = LANES * n_pp
    act = pl.BlockSpec((n_batch, SLAB, w), lambda p, c: (0, c, p))
    vec = pl.BlockSpec((1, w), lambda p, c: (0, p))
    st_spec = pl.BlockSpec((n_batch, 2 * n_pp, HEAD_DIM, HEAD_DIM), lambda p, c: (0, p, 0, 0))
    z, s_out = pl.pallas_call(
        functools.partial(_scan_kernel, n_batch, n_pp),
        grid=(N_PAIRS // n_pp, seq_len // SLAB),
        in_specs=[act] * 6 + [vec] * 5,
        out_specs=[act, st_spec],
        out_shape=[jax.ShapeDtypeStruct((n_batch, seq_len, D_MODEL), bf16),
                   jax.ShapeDtypeStruct((n_batch, N_HEADS, HEAD_DIM, HEAD_DIM), f32)],
        scratch_shapes=[pltpu.VMEM((n_batch * n_pp, 2 * SLAB, LANES), f32)],
        compiler_params=_cparams(2),
        name="rwkv_scan",
    )(*[p.reshape(n_batch, seq_len, D_MODEL) for p in proj],
      rwp["kk"], rwp["ka"], rwp["rk"], rwp["gn_w"], rwp["gn_b"])
    return z.reshape(n_batch * seq_len, D_MODEL), s_out


def _scan_lanes_kernel(n_tok, r_ref, ld_ref, k_ref, v_ref, a_ref, g_ref, kk_ref, ka_ref, rk_ref, gw_ref, gb_ref,
                       s_in_ref, z_ref, s_out_ref, kk_s, w_s, b_s, k2_s, r_s, v_s, y_s):
    hd = HEAD_DIM
    colsum = lambda t: jnp.sum(t, axis=0, keepdims=True)
    for t in range(n_tok):
        r_t, k_t, v_t, a_t = r_ref[t].T, k_ref[t].T, v_ref[t].T, a_ref[t].T
        kk_raw = k_t * kk_ref[...]
        k2 = k_t * (1.0 + (a_t - 1.0) * ka_ref[...])
        w = jnp.exp(ld_ref[t].T)
        for e in range(2):
            sl = slice(e * hd, (e + 1) * hd)
            kr = kk_raw[sl]
            kk = kr * (1.0 / jnp.maximum(jnp.sqrt(colsum(kr * kr)), 1e-12))
            kk_s[e, t] = kk
            b_s[e, t] = kk * a_t[sl]
            k2_s[e, t] = k2[sl]
            w_s[e, t] = w[sl]
            r_s[e, t] = r_t[sl]
            v_s[e, t] = v_t[sl]

    rows_per_iter = 2

    def per_rows(i, carry):
        chains = [(e, rows_per_iter * i + d) for e in range(2) for d in range(rows_per_iter)]
        s = [s_in_ref[e, vi] for e, vi in chains]
        for t in range(n_tok):
            for c, (e, vi) in enumerate(chains):
                s_kk = colsum(s[c] * kk_s[e, t])
                s[c] = s[c] * w_s[e, t] - s_kk * b_s[e, t] + v_s[e, t, pl.ds(vi, 1), :] * k2_s[e, t]
                y_s[e, t, pl.ds(vi, 1), :] = colsum(s[c] * r_s[e, t])
        for c, (e, vi) in enumerate(chains):
            s_out_ref[e, vi] = s[c]
        return carry
    lax.fori_loop(0, hd // rows_per_iter, per_rows, 0)

    for t in range(n_tok):
        outs = []
        for e in range(2):
            sl = slice(e * hd, (e + 1) * hd)
            y = y_s[e, t]
            dev = y - colsum(y) * (1.0 / hd)
            var = colsum(dev * dev) * (1.0 / hd)
            bonus = colsum(r_s[e, t] * k2_s[e, t] * rk_ref[sl, :]) * v_s[e, t]
            outs.append(dev * lax.rsqrt(var + GN_EPS) * gw_ref[sl, :] + gb_ref[sl, :] + bonus)
        z_ref[t] = (jnp.concatenate(outs, axis=0).T * g_ref[t]).astype(bf16)


def _scan_lanes(proj, rwp, n_batch, n_tok, state):
    acts = [p.reshape(n_tok, n_batch, D_MODEL) for p in proj]
    lane_bc = lambda v: jnp.broadcast_to(v.reshape(D_MODEL, 1), (D_MODEL, n_batch))
    params = [lane_bc(rwp[n]) for n in ("kk", "ka", "rk", "gn_w", "gn_b")]
    act = pl.BlockSpec((n_tok, n_batch, LANES), lambda p: (0, 0, p))
    par = pl.BlockSpec((LANES, n_batch), lambda p: (p, 0))
    st_spec = pl.BlockSpec((2, HEAD_DIM, HEAD_DIM, n_batch), lambda p: (p, 0, 0, 0))
    per_tok = pltpu.VMEM((2, n_tok, HEAD_DIM, n_batch), f32)
    z, s_out = pl.pallas_call(
        functools.partial(_scan_lanes_kernel, n_tok),
        grid=(N_PAIRS,),
        in_specs=[act] * 6 + [par] * 5 + [st_spec],
        out_specs=[act, st_spec],
        out_shape=[jax.ShapeDtypeStruct((n_tok, n_batch, D_MODEL), bf16),
                   jax.ShapeDtypeStruct(state.shape, f32)],
        scratch_shapes=[per_tok] * 7,
        compiler_params=_cparams(1),
        name="rwkv_decode_scan",
    )(*acts, *params, state)
    return z.reshape(n_tok * n_batch, D_MODEL), s_out


def _proj_res_kernel(z_ref, w_ref, b_ref, x_ref, gt_ref, o_ref):
    out = _dot(z_ref[...], w_ref[...]) + b_ref[...]
    o_ref[...] = x_ref[...] + _rows(gt_ref[...], x_ref.shape[0]) * out


def _proj_res(z, w, bias, x, mod, tok_major, seq_len):
    m = x.shape[0]
    tm = min(ROW_TILE, m)
    rows = pl.BlockSpec((tm, D_MODEL), lambda i: (i, 0))
    return pl.pallas_call(
        _proj_res_kernel,
        grid=(m // tm,),
        in_specs=[rows,
                  pl.BlockSpec((D_MODEL, D_MODEL), lambda i: (0, 0), pipeline_mode=pl.Buffered(1)),
                  pl.BlockSpec((1, D_MODEL), lambda i: (0, 0)),
                  rows,
                  _mod_spec(tok_major, tm, seq_len, 5)],
        out_specs=rows,
        out_shape=jax.ShapeDtypeStruct((m, D_MODEL), f32),
        compiler_params=_cparams(1),
        name="proj_res",
    )(z, w, bias.reshape(1, D_MODEL), x, mod)


QKV_COLS = D_MODEL + 2 * N_KV_HEADS * LANES


def _qkv_kernel(x_ref, ng_ref, sh_ref, sc_ref, w_ref, b_ref, cos_ref, sin_ref, o_ref, h_scr):
    @pl.when(pl.program_id(1) == 0)
    def _():
        tm = x_ref.shape[0]
        h_scr[...] = _rms_mod(x_ref[...], ng_ref[...], _rows(sh_ref[...], tm), _rows(sc_ref[...], tm)).astype(bf16)

    cos = cos_ref[...]
    sin = sin_ref[...]
    first = (lax.broadcasted_iota(jnp.int32, cos.shape, 1) % HEAD_DIM) < HEAD_DIM // 2
    h = h_scr[...]
    tn = o_ref.shape[1]
    for c0 in range(0, tn, MXU_COLS):
        cols = pl.ds(pl.multiple_of(pl.program_id(1) * tn + c0, MXU_COLS), MXU_COLS)
        acc = _dot(h, w_ref[:, cols]) + b_ref[:, c0:c0 + MXU_COLS]
        for c in range(0, MXU_COLS, LANES):
            xc = acc[:, c:c + LANES]
            rot = jnp.where(first, pltpu.roll(xc, LANES - HEAD_DIM // 2, 1),
                            pltpu.roll(xc, HEAD_DIM // 2, 1))
            o_ref[:, c0 + c:c0 + c + LANES] = xc * cos + rot * sin


def _qkv(x, mod, tok_major, seq_len, norm_g, w, b, cos, sin):
    m = x.shape[0]
    tm = min(ROW_TILE, m)
    tn = 512
    n_rope = (D_MODEL + N_KV_HEADS * LANES) // tn
    n_pos_blocks = cos.shape[0] // tm
    vec = pl.BlockSpec((1, D_MODEL), lambda i, j: (0, 0))
    tab = pl.BlockSpec((tm, LANES), lambda i, j: (i % n_pos_blocks, j // n_rope))
    return pl.pallas_call(
        _qkv_kernel,
        grid=(m // tm, QKV_COLS // tn),
        in_specs=[pl.BlockSpec((tm, D_MODEL), lambda i, j: (i, 0)),
                  vec,
                  _mod_spec(tok_major, tm, seq_len, 3),
                  _mod_spec(tok_major, tm, seq_len, 4),
                  pl.BlockSpec((D_MODEL, QKV_COLS), lambda i, j: (0, 0), pipeline_mode=pl.Buffered(1)),
                  pl.BlockSpec((1, tn), lambda i, j: (0, j)),
                  tab, tab],
        out_specs=pl.BlockSpec((tm, tn), lambda i, j: (i, j)),
        out_shape=jax.ShapeDtypeStruct((m, QKV_COLS), f32),
        scratch_shapes=[pltpu.VMEM((tm, D_MODEL), bf16)],
        compiler_params=_cparams(2),
        name="swa_qkv",
    )(x, norm_g.reshape(1, D_MODEL), mod, mod, w, b.reshape(1, QKV_COLS), cos, sin)


def _rope_tables(positions):
    half = HEAD_DIM // 2
    inv_freq = ROPE_THETA ** (-jnp.arange(half, dtype=f32) / half)
    ang = positions.astype(f32)[:, None] * inv_freq[None, :]
    cos = jnp.tile(jnp.cos(ang), (1, LANES // half))
    sin = jnp.sin(ang)
    sin = jnp.tile(jnp.concatenate([-sin, sin], axis=1), (1, LANES // HEAD_DIM))
    return (jnp.concatenate([cos, jnp.ones_like(cos)], axis=1),
            jnp.concatenate([sin, jnp.zeros_like(sin)], axis=1))


def _head_masks(rows):
    lane = lax.broadcasted_iota(jnp.int32, (rows, LANES), 1)
    return lane < HEAD_DIM, lane >= HEAD_DIM


def _attn_prompt_kernel(q_ref, kp_ref, kc_ref, vp_ref, vc_ref, sink_ref, o_ref):
    n = pl.program_id(1)
    blk = WINDOW
    m0, m1 = _head_masks(blk)
    m0k, m1k = _head_masks(2 * blk)
    qi = lax.broadcasted_iota(jnp.int32, (2 * blk, 2 * blk), 0) % blk
    sj = lax.broadcasted_iota(jnp.int32, (2 * blk, 2 * blk), 1)
    visible = (sj > qi) & (sj <= qi + blk) & ((n > 0) | (sj >= blk))
    top = lax.broadcasted_iota(jnp.int32, (2 * blk, 1), 0) < blk
    for c in range(N_KV_HEADS):
        sl = slice(c * LANES, (c + 1) * LANES)
        kd = jnp.concatenate([kp_ref[:, sl], kc_ref[:, sl]], axis=0).astype(bf16)
        vd = jnp.concatenate([vp_ref[:, sl], vc_ref[:, sl]], axis=0)
        vcat = jnp.concatenate([jnp.where(m0k, vd, 0.0), jnp.where(m1k, vd, 0.0)], axis=0).astype(bf16)
        pairs = [c * (GQA_GROUP // 2) + jj for jj in range(GQA_GROUP // 2)]
        each = lambda fn, *lists: [fn(*args) for args in zip(*lists)]
        qp = [q_ref[:, pair * LANES:(pair + 1) * LANES] * ATTN_SCALE for pair in pairs]
        qs = each(lambda t: jnp.concatenate([jnp.where(m0, t, 0.0), jnp.where(m1, t, 0.0)], axis=0).astype(bf16), qp)
        s = each(lambda t: jnp.where(visible, _dot_nt(t, kd), NEG_BIG), qs)
        sk = [jnp.where(top, sink_ref[2 * pair], sink_ref[2 * pair + 1]) for pair in pairs]
        mx = each(lambda t, k_: jnp.maximum(jnp.max(t, axis=1, keepdims=True), k_), s, sk)
        p = each(lambda t, m_: jnp.exp(t - m_), s, mx)
        den = each(lambda t, k_, m_: jnp.sum(t, axis=1, keepdims=True) + jnp.exp(k_ - m_), p, sk, mx)
        p = each(lambda t, d_: (t * (1.0 / d_)).astype(bf16), p, den)
        pcat = each(lambda t: jnp.concatenate([t[0:blk], t[blk:]], axis=1), p)
        for pair, t in zip(pairs, pcat):
            o_ref[:, pair * LANES:(pair + 1) * LANES] = _dot(t, vcat).astype(bf16)


def _attn_prompt(qkv, sink, n_batch, seq_len):
    nb = seq_len // WINDOW
    kw = N_KV_HEADS * LANES
    k_blk = D_MODEL // kw
    cur = lambda off: pl.BlockSpec((WINDOW, kw), lambda b, n: (b * nb + n, k_blk + off))
    prev = lambda off: pl.BlockSpec((WINDOW, kw), lambda b, n: (b * nb + jnp.maximum(n - 1, 0), k_blk + off))
    return pl.pallas_call(
        _attn_prompt_kernel,
        grid=(n_batch, nb),
        in_specs=[pl.BlockSpec((WINDOW, D_MODEL), lambda b, n: (b * nb + n, 0)),
                  prev(0), cur(0), prev(1), cur(1),
                  pl.BlockSpec(memory_space=pltpu.SMEM)],
        out_specs=pl.BlockSpec((WINDOW, D_MODEL), lambda b, n: (b * nb + n, 0)),
        out_shape=jax.ShapeDtypeStruct((n_batch * seq_len, D_MODEL), bf16),
        compiler_params=_cparams(2),
        name="swa_prompt",
    )(qkv, qkv, qkv, qkv, qkv, sink)


SEQ_PER_GROUP = 4


def _attn_cached_kernel(n_tok, groups, q_ref, kn_ref, vn_ref, kc_ref, vc_ref, sink_ref, o_ref):
    rows = SEQ_PER_GROUP * n_tok
    n_st = GQA_GROUP * rows
    m0, m1 = _head_masks(rows)
    srow = lax.broadcasted_iota(jnp.int32, (n_st, 1), 0)
    row_seq = (srow % rows) // n_tok
    row_tok = srow % n_tok
    key_c = lax.broadcasted_iota(jnp.int32, (n_st, WINDOW), 1)
    vis_c = key_c > row_tok
    key_n = lax.broadcasted_iota(jnp.int32, (n_st, rows), 1)
    vis_n = ((key_n // n_tok) == row_seq) & ((key_n % n_tok) <= row_tok)
    for gi in range(groups):
        rs = slice(gi * rows, (gi + 1) * rows)
        for cp in range(N_KV_HEADS // 2):
            for ce in range(2):
                c = 2 * cp + ce
                sl = slice(c * LANES, (c + 1) * LANES)
                kt = [kc_ref[gi * SEQ_PER_GROUP + b, c * HEAD_DIM:(c + 1) * HEAD_DIM, :] for b in range(SEQ_PER_GROUP)]
                vt = [vc_ref[gi * SEQ_PER_GROUP + b, c * HEAD_DIM:(c + 1) * HEAD_DIM, :] for b in range(SEQ_PER_GROUP)]
                kn = kn_ref[rs, sl].astype(bf16)
                vn = vn_ref[rs, sl].astype(bf16)
                pieces = []
                for jj in range(GQA_GROUP // 2):
                    pair = c * (GQA_GROUP // 2) + jj
                    qp = q_ref[rs, pair * LANES:(pair + 1) * LANES]
                    pieces += [jnp.where(m0, qp, 0.0), jnp.where(m1, qp, 0.0)]
                qs = jnp.concatenate(pieces, axis=0).astype(bf16)
                s_c = jnp.zeros((n_st, WINDOW), f32)
                for b in range(SEQ_PER_GROUP):
                    kx = jnp.concatenate([kt[b], kt[b]], axis=0).astype(bf16)
                    s_c = jnp.where(row_seq == b, _dot(qs, kx), s_c)
                s_c = jnp.where(vis_c, s_c * ATTN_SCALE, NEG_BIG)
                s_n = jnp.where(vis_n, _dot_nt(qs, kn) * ATTN_SCALE, NEG_BIG)
                sk = sink_ref[c]
                sk = sk[:, 0:1]
                mx = jnp.maximum(jnp.maximum(jnp.max(s_c, axis=1, keepdims=True),
                                             jnp.max(s_n, axis=1, keepdims=True)), sk)
                p_c = jnp.exp(s_c - mx)
                p_n = jnp.exp(s_n - mx)
                den = (jnp.sum(p_c, axis=1, keepdims=True) + jnp.sum(p_n, axis=1, keepdims=True)
                       + jnp.exp(sk - mx))
                inv_den = 1.0 / den
                p_c = (p_c * inv_den).astype(bf16)
                p_n = (p_n * inv_den).astype(bf16)
                o = _dot(p_n, vn)
                for b in range(SEQ_PER_GROUP):
                    vx = jnp.concatenate([vt[b], vt[b]], axis=0).astype(bf16)
                    o = o + jnp.where(row_seq == b, _dot_nt(p_c, vx), 0.0)
                for jj in range(GQA_GROUP // 2):
                    pair = c * (GQA_GROUP // 2) + jj
                    o0 = o[(2 * jj) * rows:(2 * jj + 1) * rows]
                    o1 = o[(2 * jj + 1) * rows:(2 * jj + 2) * rows]
                    o_ref[rs, pair * LANES:(pair + 1) * LANES] = jnp.where(m0, o0, o1).astype(bf16)


def _attn_cached(qkv, k_cache, v_cache, sink, n_batch, n_tok):
    groups = 2
    seqs = SEQ_PER_GROUP * groups
    rows = seqs * n_tok
    kw = N_KV_HEADS * LANES
    k_blk = D_MODEL // kw
    n_st = GQA_GROUP * SEQ_PER_GROUP * n_tok
    head = (jnp.arange(N_KV_HEADS)[:, None] * GQA_GROUP
            + (jnp.arange(n_st)[None, :] // (SEQ_PER_GROUP * n_tok)))
    sink_tab = jnp.broadcast_to(sink[head][:, :, None], (N_KV_HEADS, n_st, LANES))
    cache_spec = pl.BlockSpec((seqs, N_KV_HEADS * HEAD_DIM, WINDOW), lambda i: (i, 0, 0))
    return pl.pallas_call(
        functools.partial(_attn_cached_kernel, n_tok, groups),
        grid=(n_batch // seqs,),
        in_specs=[pl.BlockSpec((rows, D_MODEL), lambda i: (i, 0)),
                  pl.BlockSpec((rows, kw), lambda i: (i, k_blk)),
                  pl.BlockSpec((rows, kw), lambda i: (i, k_blk + 1)),
                  cache_spec, cache_spec,
                  pl.BlockSpec((N_KV_HEADS, n_st, LANES), lambda i: (0, 0, 0))],
        out_specs=pl.BlockSpec((rows, D_MODEL), lambda i: (i, 0)),
        out_shape=jax.ShapeDtypeStruct((n_batch * n_tok, D_MODEL), bf16),
        compiler_params=_cparams(1),
        name="swa_cached",
    )(qkv, qkv, qkv, k_cache, v_cache, sink_tab)


def _undup(t):
    return t.reshape(t.shape[0], N_KV_HEADS, 2, HEAD_DIM)[:, :, 0, :]


def _trunk(x, mod_all, tok_major, n_batch, seq_len, pos0, wkv_in, shift_in, k_in, v_in, wts):
    to_seq_major = lambda t: t.reshape(seq_len, n_batch, -1).transpose(1, 0, 2).reshape(n_batch * seq_len, -1)
    to_tok_major = lambda t: t.reshape(n_batch, seq_len, -1).transpose(1, 0, 2).reshape(n_batch * seq_len, -1)
    mod = mod_all[0]
    ffn_w = dict(wts["ffn"])
    x, ffn_w[0, 0] = _ffn(x, mod, tok_major, seq_len, 0, wts["norm_g"][0, 0], ffn_w[0, 0])
    s0 = shift_in[0] if tok_major else jnp.zeros((n_batch, 1, D_MODEL), f32)
    *proj, h_tail = _rwkv_proj(x, mod, tok_major, seq_len, wts["norm_g"][0, 1], s0, wts["rw"])
    if tok_major:
        z, wkv_t = _scan_lanes(proj, wts["rw"], n_batch, seq_len, jnp.transpose(wkv_in[0], (1, 2, 3, 0)))
        wkv_out = jnp.transpose(wkv_t, (3, 0, 1, 2))
        shift_out = h_tail
    else:
        z, wkv_out = _scan(proj, wts["rw"], n_batch, seq_len)
        shift_out = h_tail.reshape(n_batch, -1, 8, D_MODEL)[:, -1, -1]
    x = _proj_res(z, wts["rw"]["wo"], jnp.zeros((D_MODEL,), f32), x, mod, tok_major, seq_len)
    x, ffn_w[0, 1] = _ffn(x, mod, tok_major, seq_len, 6, wts["norm_g"][0, 2], ffn_w[0, 1])
    mod = mod_all[1]
    x, ffn_w[1, 0] = _ffn(x, mod, tok_major, seq_len, 0, wts["norm_g"][1, 0], ffn_w[1, 0])
    positions = pos0 + jnp.arange(seq_len)
    if tok_major:
        positions = jnp.repeat(positions, n_batch)
    cos, sin = _rope_tables(positions)
    qkv = _qkv(x, mod, tok_major, seq_len, wts["norm_g"][1, 1], wts["sw_wqkv"], wts["sw_bqkv"], cos, sin)
    kw = N_KV_HEADS * LANES
    if k_in is None:
        att = _attn_prompt(qkv, wts["sw_sink"], n_batch, seq_len)
        tail = qkv.reshape(n_batch, seq_len, QKV_COLS)[:, -WINDOW:, D_MODEL:]
        k_new = _undup(tail[..., :kw].reshape(n_batch * WINDOW, kw)).reshape(n_batch, WINDOW, N_KV_HEADS, HEAD_DIM)
        v_new = _undup(tail[..., kw:].reshape(n_batch * WINDOW, kw)).reshape(n_batch, WINDOW, N_KV_HEADS, HEAD_DIM)
    else:
        win = k_in.shape[2]
        qkv = to_seq_major(qkv)
        cache_t = lambda t: jnp.transpose(t[0], (0, 2, 3, 1)).reshape(n_batch, N_KV_HEADS * HEAD_DIM, win)
        att = _attn_cached(qkv, cache_t(k_in), cache_t(v_in), wts["sw_sink"], n_batch, seq_len)
        att = to_tok_major(att)
        k_tok = _undup(qkv[:, D_MODEL:D_MODEL + kw]).reshape(n_batch, seq_len, N_KV_HEADS, HEAD_DIM)
        v_tok = _undup(qkv[:, D_MODEL + kw:]).reshape(n_batch, seq_len, N_KV_HEADS, HEAD_DIM)
        k_new = jnp.concatenate([k_in[0], k_tok], axis=1)[:, -win:]
        v_new = jnp.concatenate([v_in[0], v_tok], axis=1)[:, -win:]
    x = _proj_res(att, wts["sw_wo"], wts["sw_bo"], x, mod, tok_major, seq_len)
    y, ffn_w[1, 1] = _ffn(x, mod, tok_major, seq_len, 6, wts["norm_g"][1, 2], ffn_w[1, 1], final_g=wts["final_g"])
    return (y, wkv_out[None], shift_out[None], k_new[None], v_new[None]), ffn_w


def _dup_heads(w):
    lead = w.shape[:-1]
    w4 = w.reshape(lead + (N_KV_HEADS, 1, HEAD_DIM))
    return jnp.broadcast_to(w4, lead + (N_KV_HEADS, 2, HEAD_DIM)).reshape(lead + (N_KV_HEADS * LANES,))


def _pad_cols(w):
    return jnp.pad(w, ((0, 0), (0, LORA_PAD - w.shape[1])))


def _pad_rows(w):
    return jnp.pad(w, ((0, LORA_PAD - w.shape[0]), (0, 0)))


def kernel(x_prompt, x_sample, state_rwkv_wkv, state_rwkv_shift, cache_swa_k, cache_swa_v, c_prompt, c_sample, norm_g, w_ada, b_ada, w_ffn_in, w_ffn_out, rw_mu, rw_wrkv, rw_w0, rw_w1, rw_w2, rw_a0, rw_a1, rw_a2, rw_g1, rw_g2, rw_kk, rw_ka, rw_rk, rw_gn_w, rw_gn_b, rw_wo, sw_wqkv, sw_bqkv, sw_sink, sw_wo, sw_bo, final_g):
    n_p, seq_p, _ = x_prompt.shape
    n_s, seq_s, _ = x_sample.shape
    nq = N_HEADS * HEAD_DIM
    nkv = N_KV_HEADS * HEAD_DIM
    row = lambda t: t.reshape(1, D_MODEL)
    rw = dict(
        mu=rw_mu[0], wrkv=rw_wrkv[0].astype(bf16),
        w1=_pad_cols(rw_w1[0]).astype(bf16), a1=_pad_cols(rw_a1[0]).astype(bf16), g1=rw_g1[0].astype(bf16),
        w2=_pad_rows(rw_w2[0]).astype(bf16), a2=_pad_rows(rw_a2[0]).astype(bf16), g2=rw_g2[0].astype(bf16),
        w0=row(rw_w0[0]), a0=row(rw_a0[0]), kk=row(rw_kk[0]), ka=row(rw_ka[0]), rk=row(rw_rk[0]),
        gn_w=row(rw_gn_w[0]), gn_b=row(rw_gn_b[0]), wo=rw_wo[0].astype(bf16))
    wq = sw_wqkv[0]
    bq = sw_bqkv[0]
    wts = dict(
        norm_g=norm_g, final_g=final_g,
        ffn={(l, s): (w_ffn_in, w_ffn_out, l, s) for l in range(2) for s in range(2)}, rw=rw,
        sw_wqkv=jnp.concatenate([wq[:, :nq], _dup_heads(wq[:, nq:nq + nkv]), _dup_heads(wq[:, nq + nkv:])],
                                axis=1).astype(bf16),
        sw_bqkv=jnp.concatenate([bq[:nq], _dup_heads(bq[nq:nq + nkv]), _dup_heads(bq[nq + nkv:])]),
        sw_sink=sw_sink[0], sw_wo=sw_wo[0].astype(bf16), sw_bo=sw_bo[0])

    n_c = n_p + n_s
    pad = (-n_c) % 8
    c_all = jnp.concatenate([c_sample, c_prompt, jnp.zeros((pad, D_MODEL), f32)], axis=0)
    mod_s = _ada(c_all, w_ada, b_ada)
    mod_p = [m[n_s:n_c].reshape(n_p, 1, N_MOD * D_MODEL) for m in mod_s]

    (y_s, s_wkv, s_shift, s_k, s_v), wts["ffn"] = _trunk(
        jnp.transpose(x_sample, (1, 0, 2)).reshape(n_s * seq_s, D_MODEL), mod_s, True, n_s, seq_s, PAST_LEN,
        state_rwkv_wkv, state_rwkv_shift, cache_swa_k, cache_swa_v, wts)
    (y_p, p_wkv, p_shift, p_k, p_v), _ = _trunk(
        x_prompt.reshape(n_p * seq_p, D_MODEL), mod_p, False, n_p, seq_p, 0, None, None, None, None, wts)
    y_s = jnp.transpose(y_s.reshape(seq_s, n_s, D_MODEL), (1, 0, 2))
    return (y_p.reshape(n_p, seq_p, D_MODEL), y_s, p_wkv, p_shift, p_k, p_v, s_wkv, s_shift, s_k, s_v)
```

```python
import functools
import math

import jax
import jax.numpy as jnp
from jax import lax
from jax.experimental import pallas as pl
from jax.experimental.pallas import tpu as pltpu

f32 = jnp.float32
bf16 = jnp.bfloat16

D_MODEL = 2048
HEAD_DIM = 64
N_HEADS = D_MODEL // HEAD_DIM
N_KV_HEADS = 4
GQA_GROUP = N_HEADS // N_KV_HEADS
WINDOW = 128
ATTN_SCALE = HEAD_DIM ** -0.5
ROPE_THETA = 10000.0
D_FF = 5632
N_MOD = 9
RMS_EPS = 1e-6
GN_EPS = 64e-5
PAST_LEN = 8192
LANES = 128
MXU_COLS = 256
N_PAIRS = D_MODEL // LANES
LORA_PAD = 128
SLAB = 64
SCAN_CHAINS = 16
VMEM_LIMIT = 56 * 1024 * 1024
NEG_BIG = -1e30
ROW_TILE = 512


def _cparams(n_axes):
    return pltpu.CompilerParams(dimension_semantics=("arbitrary",) * n_axes,
                                vmem_limit_bytes=VMEM_LIMIT)


def _dot(a, b):
    return jnp.dot(a, b, preferred_element_type=f32)


def _dot_nt(a, b):
    return lax.dot_general(a, b, (((1,), (1,)), ((), ())), preferred_element_type=f32)


def _split(x, n):
    if x.dtype == bf16:
        return [x]
    parts = []
    rem = x
    for i in range(n):
        p = rem.astype(bf16)
        parts.append(p)
        if i + 1 < n:
            rem = rem - p.astype(f32)
    return parts


def _mm(a, b, pa=1, pb=1, nt=False):
    a_parts = _split(a, pa)
    b_parts = _split(b, pb)
    order = max(len(a_parts), len(b_parts))
    acc = None
    for i, x in enumerate(a_parts):
        for j, y in enumerate(b_parts):
            if i + j >= order:
                continue
            t = _dot_nt(x, y) if nt else _dot(x, y)
            acc = t if acc is None else acc + t
    return acc


def _rms_mod(x, g, shift, scale):
    return x * lax.rsqrt(jnp.mean(x * x, axis=-1, keepdims=True) + RMS_EPS) * (g * (1.0 + scale)) + shift


def _mod_spec(tok_major, tm, seq_len, idx):
    if tok_major:
        return pl.BlockSpec((tm // seq_len, D_MODEL), lambda i, *_: (0, idx))
    return pl.BlockSpec((None, 1, D_MODEL), lambda i, *_: ((i * tm) // seq_len, 0, idx))


def _rows(v, tm):
    n = v.shape[0]
    return v if n in (1, tm) else jnp.concatenate([v] * (tm // n), axis=0)


def _ada_kernel(c_ref, w_ref, b_ref, *o_refs):
    c = c_ref[...]
    s = (c * jax.nn.sigmoid(c)).astype(bf16)
    for layer, o_ref in enumerate(o_refs):
        o_ref[...] = _dot(s, w_ref[layer].astype(bf16)) + b_ref[layer]


def _ada(c_all, w_ada, b_ada):
    n_layers, _, n_out = w_ada.shape
    rows = c_all.shape[0]
    tn = 1024
    return pl.pallas_call(
        _ada_kernel,
        grid=(n_out // tn,),
        in_specs=[pl.BlockSpec((rows, D_MODEL), lambda j: (0, 0)),
                  pl.BlockSpec((n_layers, D_MODEL, tn), lambda j: (0, 0, j)),
                  pl.BlockSpec((n_layers, 1, tn), lambda j: (0, 0, j))],
        out_specs=[pl.BlockSpec((rows, tn), lambda j: (0, j))] * n_layers,
        out_shape=[jax.ShapeDtypeStruct((rows, n_out), f32)] * n_layers,
        compiler_params=_cparams(1),
        name="ada_mod",
    )(c_all, w_ada, b_ada.reshape(n_layers, 1, n_out))


def _ffn_kernel(final_norm, emit_bf16, tiled_gate, x_ref, ng_ref, sh_ref, sc_ref, gt_ref, wg_ref, wu_ref, wo_ref,
                fg_ref, o_ref, *rest):
    rest = list(rest)
    if emit_bf16:
        wg_out, wu_out, wo_out = rest[:3]
        del rest[:3]
    h_scr = rest.pop(0)
    gt_scr = rest.pop(0) if tiled_gate else None
    f = pl.program_id(1)
    tm = x_ref.shape[0]

    @pl.when(f == 0)
    def _():
        h_scr[...] = _rms_mod(x_ref[...], ng_ref[...], _rows(sh_ref[...], tm), _rows(sc_ref[...], tm)).astype(bf16)
        o_ref[...] = x_ref[...]
        if tiled_gate:
            gt_scr[...] = 0.5 * _rows(gt_ref[...], tm)

    wg, wu, wo = wg_ref[...], wu_ref[...], wo_ref[...]
    if emit_bf16:
        wg, wu, wo = wg.astype(bf16), wu.astype(bf16), wo.astype(bf16)
        wg_out[...] = wg
        wu_out[...] = wu
        wo_out[...] = wo
    h = h_scr[...]
    gate = _dot(h, wg)
    up = _dot(h, wu)
    act = (gate * jax.nn.sigmoid(gate) * up).astype(bf16)
    half_gate = gt_scr[...] if tiled_gate else 0.5 * gt_ref[...]
    o_ref[...] += half_gate * _dot(act, wo)

    if final_norm:
        @pl.when(f == pl.num_programs(1) - 1)
        def _():
            y = o_ref[...]
            o_ref[...] = y * lax.rsqrt(jnp.mean(y * y, axis=-1, keepdims=True) + RMS_EPS) * fg_ref[...]


def _ffn(x, mod, tok_major, seq_len, mod_base, norm_g, weights, final_g=None):
    m = x.shape[0]
    tm = min(ROW_TILE, m)
    emit = len(weights) == 4
    tf = 256 if emit else 512
    nf = D_FF // tf
    vec = pl.BlockSpec((1, D_MODEL), lambda i, j: (0, 0))
    fg = jnp.ones((1, D_MODEL), f32) if final_g is None else final_g.reshape(1, D_MODEL)
    w_in_spec = pl.BlockSpec((D_MODEL, tf), lambda i, j: (0, j))
    w_out_spec = pl.BlockSpec((tf, D_MODEL), lambda i, j: (j, 0))
    out_specs = [pl.BlockSpec((tm, D_MODEL), lambda i, j: (i, 0))]
    out_shape = [jax.ShapeDtypeStruct((m, D_MODEL), f32)]
    if emit:
        assert m == tm, "the weight copies are written once, by a single row tile"
        w_ffn_in, w_ffn_out, layer, slot = weights
        w_args = (w_ffn_in, w_ffn_in, w_ffn_out)
        w_specs = [pl.BlockSpec((None, None, D_MODEL, tf), lambda i, j: (layer, slot, 0, j)),
                   pl.BlockSpec((None, None, D_MODEL, tf), lambda i, j: (layer, slot, 0, j + nf)),
                   pl.BlockSpec((None, None, tf, D_MODEL), lambda i, j: (layer, slot, j, 0))]
        out_specs += [w_in_spec, w_in_spec, w_out_spec]
        out_shape += [jax.ShapeDtypeStruct((D_MODEL, D_FF), bf16), jax.ShapeDtypeStruct((D_MODEL, D_FF), bf16),
                      jax.ShapeDtypeStruct((D_FF, D_MODEL), bf16)]
    else:
        w_args = weights
        w_specs = [w_in_spec, w_in_spec, w_out_spec]
    outs = pl.pallas_call(
        functools.partial(_ffn_kernel, final_g is not None, emit, tok_major),
        grid=(m // tm, nf),
        in_specs=[pl.BlockSpec((tm, D_MODEL), lambda i, j: (i, 0)),
                  vec,
                  _mod_spec(tok_major, tm, seq_len, mod_base),
                  _mod_spec(tok_major, tm, seq_len, mod_base + 1),
                  _mod_spec(tok_major, tm, seq_len, mod_base + 2)] + w_specs + [vec],
        out_specs=out_specs,
        out_shape=out_shape,
        scratch_shapes=[pltpu.VMEM((tm, D_MODEL), bf16)] + ([pltpu.VMEM((tm, D_MODEL), f32)] if tok_major else []),
        compiler_params=_cparams(2),
        name="ffn",
    )(x, norm_g.reshape(1, D_MODEL), mod, mod, mod, *w_args, fg)
    return (outs[0], tuple(outs[1:])) if emit else (outs[0], weights)


def _rwkv_proj_kernel(seq_len, tm, h_rows, tok_major,
                      x_ref, ng_ref, sh_ref, sc_ref, s0_ref, mu_ref, w1_ref, a1_ref, g1_ref,
                      w_ref, w2_ref, a2_ref, g2_ref, w0_ref, a0_ref,
                      r_ref, ld_ref, k_ref, v_ref, a_ref, g_ref, h_ref,
                      hs, xr, xk, xv, tw, ta, sg):
    i = pl.program_id(0)
    j = pl.program_id(1)

    @pl.when(j == 0)
    def _():
        h = _rms_mod(x_ref[...], ng_ref[...], _rows(sh_ref[...], tm), _rows(sc_ref[...], tm))
        h_ref[...] = h[tm - h_rows:tm, :]
        if tok_major:
            n_seq = s0_ref.shape[0]
            prev = jnp.concatenate([s0_ref[...], h[0:tm - n_seq, :]], axis=0)
        else:
            @pl.when(i == 0)
            def _():
                hs[0:8, :] = jnp.zeros((8, D_MODEL), f32)

            @pl.when(i > 0)
            def _():
                hs[0:8, :] = hs[tm:tm + 8, :]

            hs[8:tm + 8, :] = h
            row = i * tm + lax.broadcasted_iota(jnp.int32, (tm, 1), 0)
            prev = jnp.where(row % seq_len == 0, s0_ref[...], hs[7:tm + 7, :])
        xx = prev - h
        mu = mu_ref[...]
        xr[...] = (h + xx * mu[0:1, :]).astype(bf16)
        xk[...] = (h + xx * mu[2:3, :]).astype(bf16)
        xv[...] = (h + xx * mu[3:4, :]).astype(bf16)
        xw = (h + xx * mu[1:2, :]).astype(bf16)
        tw[...] = jnp.tanh(_dot(xw, w1_ref[...])).astype(bf16)
        xa = (h + xx * mu[4:5, :]).astype(bf16)
        ta[...] = _dot(xa, a1_ref[...]).astype(bf16)
        xg = (h + xx * mu[5:6, :]).astype(bf16)
        sg[...] = jax.nn.sigmoid(_dot(xg, g1_ref[...])).astype(bf16)

    cols = pl.ds(pl.multiple_of(j * r_ref.shape[1], r_ref.shape[1]), r_ref.shape[1])
    r_ref[...] = _dot(xr[...], w_ref[0, :, cols])
    k_ref[...] = _dot(xk[...], w_ref[1, :, cols])
    v_ref[...] = _dot(xv[...], w_ref[2, :, cols])
    z = w0_ref[...] + _dot(tw[...], w2_ref[...])
    ld_ref[...] = -jax.nn.sigmoid(z) * math.exp(-0.5)
    a_ref[...] = jax.nn.sigmoid(a0_ref[...] + _dot(ta[...], a2_ref[...]))
    g_ref[...] = _dot(sg[...], g2_ref[...])


def _rwkv_proj(x, mod, tok_major, seq_len, norm_g, s0, rwp):
    m = x.shape[0]
    tm = min(ROW_TILE, m)
    tn = 256
    h_rows = tm // seq_len if tok_major else 8
    full = lambda shape: pl.BlockSpec(shape, lambda i, j: (0,) * len(shape))
    col = lambda rows: pl.BlockSpec((rows, tn), lambda i, j: (0, j))
    w_res = pl.BlockSpec((3, D_MODEL, D_MODEL), lambda i, j: (0, 0, 0), pipeline_mode=pl.Buffered(1))
    if tok_major:
        assert m == tm
        s0_spec = pl.BlockSpec((tm // seq_len, D_MODEL), lambda i, j: (0, 0))
    else:
        s0_spec = pl.BlockSpec((None, 1, D_MODEL), lambda i, j: ((i * tm) // seq_len, 0, 0))
    out_spec = pl.BlockSpec((tm, tn), lambda i, j: (i, j))
    out_sds = jax.ShapeDtypeStruct((m, D_MODEL), f32)
    outs = pl.pallas_call(
        functools.partial(_rwkv_proj_kernel, seq_len, tm, h_rows, tok_major),
        grid=(m // tm, D_MODEL // tn),
        in_specs=[pl.BlockSpec((tm, D_MODEL), lambda i, j: (i, 0)),
                  full((1, D_MODEL)),
                  _mod_spec(tok_major, tm, seq_len, 3),
                  _mod_spec(tok_major, tm, seq_len, 4),
                  s0_spec,
                  full((6, D_MODEL)),
                  full((D_MODEL, LORA_PAD)), full((D_MODEL, LORA_PAD)), full((D_MODEL, 256)),
                  w_res,
                  col(LORA_PAD), col(LORA_PAD), col(256), col(1), col(1)],
        out_specs=[out_spec] * 6 + [pl.BlockSpec((h_rows, D_MODEL), lambda i, j: (i, 0))],
        out_shape=[out_sds] * 6 + [jax.ShapeDtypeStruct((m // tm * h_rows, D_MODEL), f32)],
        scratch_shapes=[pltpu.VMEM((tm + 8, D_MODEL), f32),
                        pltpu.VMEM((tm, D_MODEL), bf16), pltpu.VMEM((tm, D_MODEL), bf16),
                        pltpu.VMEM((tm, D_MODEL), bf16),
                        pltpu.VMEM((tm, LORA_PAD), bf16), pltpu.VMEM((tm, LORA_PAD), bf16),
                        pltpu.VMEM((tm, 256), bf16)],
        compiler_params=_cparams(2),
        name="rwkv_proj",
    )(x, norm_g.reshape(1, D_MODEL), mod, mod, s0, rwp["mu"], rwp["w1"], rwp["a1"], rwp["g1"],
      rwp["wrkv"], rwp["w2"], rwp["a2"], rwp["g2"], rwp["w0"], rwp["a0"])
    return outs


def _scan_chains(acts, params, states):
    n_st = 2 * SLAB
    row = lax.broadcasted_iota(jnp.int32, (n_st, LANES), 0)
    lane = lax.broadcasted_iota(jnp.int32, (n_st, LANES), 1)
    own = (row // SLAB) == (lane // HEAD_DIM)
    blk = (row // SLAB) == (lane // SLAB)
    strict = blk & (lane < row)
    incl = blk & (lane <= row)
    eye = jnp.where(row == lane, 1.0, 0.0)
    twice = lambda t: jnp.concatenate([t, t], axis=0)
    st = lambda t: jnp.where(own, twice(t), 0.0)
    each = lambda fn, *lists: [fn(*args) for args in zip(*lists)]

    r, ld, k, v, a, g = [[act[i] for act in acts] for i in range(6)]
    kkp, kap, rkp, gnw, gnb = [[par[i] for par in params] for i in range(5)]

    r64 = lax.broadcasted_iota(jnp.int32, (SLAB, SLAB), 0)
    c64 = lax.broadcasted_iota(jnp.int32, (SLAB, SLAB), 1)
    tri_ones = jnp.concatenate([jnp.where(c64 <= r64, 1.0, 0.0),
                                jnp.ones((SLAB, SLAB), f32)], axis=0).astype(bf16)
    sums = each(lambda t: _mm(tri_ones, t, pb=3), ld)
    cs = each(lambda t: t[0:SLAB], sums)
    tot = each(lambda t: t[SLAB:n_st], sums)

    kk_raw = each(lambda t, p: st(t * p), k, kkp)
    kk = each(lambda t: t * (1.0 / jnp.maximum(jnp.sqrt(jnp.sum(t * t, axis=1, keepdims=True)), 1e-12)), kk_raw)
    k2 = each(lambda kt, at, p: st(kt * (1.0 + (at - 1.0) * p)), k, a, kap)
    b = each(lambda t, at: t * twice(at), kk, a)
    r_s = each(st, r)
    v_s = each(st, v)
    bonus = each(lambda rt, kt, p, vt: jnp.sum(rt * kt * p, axis=1, keepdims=True) * vt, r_s, k2, rkp, v_s)

    e_neg = each(lambda c_: twice(jnp.exp(-c_)), cs)
    e_tail = each(lambda t_, c_: twice(jnp.exp(t_ - c_)), tot, cs)
    a_t = each(lambda t, c_, l_: -t * twice(jnp.exp(c_ - l_)), kk, cs, ld)
    r_t = each(lambda t, c_: t * twice(jnp.exp(c_)), r_s, cs)
    k_t = each(lambda t, e: t * e, k2, e_neg)
    b_t = each(lambda t, e: t * e, b, e_neg)
    k_h = each(lambda t, e: t * e, k2, e_tail)
    b_h = each(lambda t, e: t * e, b, e_tail)

    lhs1 = each(lambda x_, y_: jnp.concatenate([x_, y_], axis=0), a_t, r_t)
    rhs1 = each(lambda x_, y_: jnp.concatenate([x_, y_], axis=0), k_t, b_t)
    x = each(lambda l_, r_: _mm(l_, r_, nt=True), lhs1, rhs1)
    a_k = each(lambda t: jnp.where(strict, t[0:n_st, 0:n_st], 0.0), x)
    a_b = each(lambda t: jnp.where(strict, t[0:n_st, n_st:], 0.0), x)
    r_kb = each(lambda t: jnp.concatenate([jnp.where(incl, t[n_st:, 0:n_st], 0.0),
                                           jnp.where(incl, t[n_st:, n_st:], 0.0)], axis=1), x)

    p1 = each(lambda l_, s_: _mm(l_, s_, nt=True), lhs1, states)
    rhs_u = each(lambda p_, ak, vt: p_[0:n_st] + _mm(ak, vt), p1, a_k, v_s)

    near = (row // 2) == (lane // 2)
    t_inv = each(lambda t: eye + jnp.where(near, t, 0.0), a_b)
    s = 2
    while s < SLAB:
        off = ((row // (2 * s)) == (lane // (2 * s))) & ((row // s) != (lane // s))
        w = each(lambda ab, ti: _mm(jnp.where(off, ab, 0.0), ti), a_b, t_inv)
        t_inv = each(lambda ti, w_: ti + _mm(ti, w_), t_inv, w)
        s *= 2
    u = each(_mm, t_inv, rhs_u)

    vu = each(lambda x_, y_: jnp.concatenate([x_, y_], axis=0), v_s, u)
    y = each(lambda p_, rk_, vu_: p_[n_st:] + _mm(rk_, vu_), p1, r_kb, vu)

    def finish(y_, w_, b_, bonus_, g_):
        mean = jnp.sum(y_, axis=1, keepdims=True) * (1.0 / HEAD_DIM)
        dev = jnp.where(own, y_ - mean, 0.0)
        var = jnp.sum(dev * dev, axis=1, keepdims=True) * (1.0 / HEAD_DIM)
        out = jnp.where(own, dev * lax.rsqrt(var + GN_EPS) * w_ + b_, 0.0) + bonus_
        return (out[0:SLAB] + out[SLAB:n_st]) * g_
    z = each(finish, y, gnw, gnb, bonus, g)

    vu_t = each(lambda t: t.T, vu)
    kb_h = each(lambda x_, y_: jnp.concatenate([x_, y_], axis=0), k_h, b_h)
    new_states = each(lambda s_, t_, vt, kb: s_ * jnp.exp(t_[0:1, :]) + _mm(vt, kb), states, tot, vu_t, kb_h)
    return z, new_states


def _scan_kernel(n_seq, n_pp, r_ref, ld_ref, k_ref, v_ref, a_ref, g_ref, kk_ref, ka_ref, rk_ref, gw_ref,
                 gb_ref, z_ref, s_out_ref, s_scr):
    c = pl.program_id(1)

    @pl.when(c == 0)
    def _():
        s_scr[...] = jnp.zeros_like(s_scr)

    chains = [(b, pp) for b in range(n_seq) for pp in range(n_pp)]
    lanes = [slice(pp * LANES, (pp + 1) * LANES) for _, pp in chains]
    acts = [tuple(ref[b, :, sl] for ref in (r_ref, ld_ref, k_ref, v_ref, a_ref, g_ref))
            for (b, _), sl in zip(chains, lanes)]
    params = [tuple(ref[:, sl] for ref in (kk_ref, ka_ref, rk_ref, gw_ref, gb_ref)) for sl in lanes]
    states = [s_scr[n] for n in range(len(chains))]
    z, new_states = _scan_chains(acts, params, states)
    for n, (b, _) in enumerate(chains):
        z_ref[b, :, lanes[n]] = z[n].astype(bf16)
        s_scr[n] = new_states[n]

    @pl.when(c == pl.num_programs(1) - 1)
    def _():
        for n, (b, pp) in enumerate(chains):
            s = s_scr[n]
            s_out_ref[b, 2 * pp] = s[0:HEAD_DIM, 0:HEAD_DIM]
            s_out_ref[b, 2 * pp + 1] = s[HEAD_DIM:, HEAD_DIM:]


def _scan(proj, rwp, n_batch, seq_len):
    n_pp = SCAN_CHAINS // n_batch
    w = LANES * n_pp
    act = pl.BlockSpec((n_batch, SLAB, w), lambda p, c: (0, c, p))
    vec = pl.BlockSpec((1, w), lambda p, c: (0, p))
    st_spec = pl.BlockSpec((n_batch, 2 * n_pp, HEAD_DIM, HEAD_DIM), lambda p, c: (0, p, 0, 0))
    z, s_out = pl.pallas_call(
        functools.partial(_scan_kernel, n_batch, n_pp),
        grid=(N_PAIRS // n_pp, seq_len // SLAB),
        in_specs=[act] * 6 + [vec] * 5,
        out_specs=[act, st_spec],
        out_shape=[jax.ShapeDtypeStruct((n_batch, seq_len, D_MODEL), bf16),
                   jax.ShapeDtypeStruct((n_batch, N_HEADS, HEAD_DIM, HEAD_DIM), f32)],
        scratch_shapes=[pltpu.VMEM((n_batch * n_pp, 2 * SLAB, LANES), f32)],
        compiler_params=_cparams(2),
        name="rwkv_scan",
    )(*[p.reshape(n_batch, seq_len, D_MODEL) for p in proj],
      rwp["kk"], rwp["ka"], rwp["rk"], rwp["gn_w"], rwp["gn_b"])
    return z.reshape(n_batch * seq_len, D_MODEL), s_out


def _scan_lanes_kernel(n_tok, r_ref, ld_ref, k_ref, v_ref, a_ref, g_ref, kk_ref, ka_ref, rk_ref, gw_ref, gb_ref,
                       s_in_ref, z_ref, s_out_ref, kk_s, w_s, b_s, k2_s, r_s, v_s, y_s):
    hd = HEAD_DIM
    colsum = lambda t: jnp.sum(t, axis=0, keepdims=True)
    for t in range(n_tok):
        r_t, k_t, v_t, a_t = r_ref[t].T, k_ref[t].T, v_ref[t].T, a_ref[t].T
        kk_raw = k_t * kk_ref[...]
        k2 = k_t * (1.0 + (a_t - 1.0) * ka_ref[...])
        w = jnp.exp(ld_ref[t].T)
        for e in range(2):
            sl = slice(e * hd, (e + 1) * hd)
            kr = kk_raw[sl]
            kk = kr * (1.0 / jnp.maximum(jnp.sqrt(colsum(kr * kr)), 1e-12))
            kk_s[e, t] = kk
            b_s[e, t] = kk * a_t[sl]
            k2_s[e, t] = k2[sl]
            w_s[e, t] = w[sl]
            r_s[e, t] = r_t[sl]
            v_s[e, t] = v_t[sl]

    rows_per_iter = 2

    def per_rows(i, carry):
        chains = [(e, rows_per_iter * i + d) for e in range(2) for d in range(rows_per_iter)]
        s = [s_in_ref[e, vi] for e, vi in chains]
        for t in range(n_tok):
            for c, (e, vi) in enumerate(chains):
                s_kk = colsum(s[c] * kk_s[e, t])
                s[c] = s[c] * w_s[e, t] - s_kk * b_s[e, t] + v_s[e, t, pl.ds(vi, 1), :] * k2_s[e, t]
                y_s[e, t, pl.ds(vi, 1), :] = colsum(s[c] * r_s[e, t])
        for c, (e, vi) in enumerate(chains):
            s_out_ref[e, vi] = s[c]
        return carry
    lax.fori_loop(0, hd // rows_per_iter, per_rows, 0)

    for t in range(n_tok):
        outs = []
        for e in range(2):
            sl = slice(e * hd, (e + 1) * hd)
            y = y_s[e, t]
            dev = y - colsum(y) * (1.0 / hd)
            var = colsum(dev * dev) * (1.0 / hd)
            bonus = colsum(r_s[e, t] * k2_s[e, t] * rk_ref[sl, :]) * v_s[e, t]
            outs.append(dev * lax.rsqrt(var + GN_EPS) * gw_ref[sl, :] + gb_ref[sl, :] + bonus)
        z_ref[t] = (jnp.concatenate(outs, axis=0).T * g_ref[t]).astype(bf16)


def _scan_lanes(proj, rwp, n_batch, n_tok, state):
    acts = [p.reshape(n_tok, n_batch, D_MODEL) for p in proj]
    lane_bc = lambda v: jnp.broadcast_to(v.reshape(D_MODEL, 1), (D_MODEL, n_batch))
    params = [lane_bc(rwp[n]) for n in ("kk", "ka", "rk", "gn_w", "gn_b")]
    act = pl.BlockSpec((n_tok, n_batch, LANES), lambda p: (0, 0, p))
    par = pl.BlockSpec((LANES, n_batch), lambda p: (p, 0))
    st_spec = pl.BlockSpec((2, HEAD_DIM, HEAD_DIM, n_batch), lambda p: (p, 0, 0, 0))
    per_tok = pltpu.VMEM((2, n_tok, HEAD_DIM, n_batch), f32)
    z, s_out = pl.pallas_call(
        functools.partial(_scan_lanes_kernel, n_tok),
        grid=(N_PAIRS,),
        in_specs=[act] * 6 + [par] * 5 + [st_spec],
        out_specs=[act, st_spec],
        out_shape=[jax.ShapeDtypeStruct((n_tok, n_batch, D_MODEL), bf16),
                   jax.ShapeDtypeStruct(state.shape, f32)],
        scratch_shapes=[per_tok] * 7,
        compiler_params=_cparams(1),
        name="rwkv_decode_scan",
    )(*acts, *params, state)
    return z.reshape(n_tok * n_batch, D_MODEL), s_out


def _proj_res_kernel(z_ref, w_ref, b_ref, x_ref, gt_ref, o_ref):
    out = _dot(z_ref[...], w_ref[...]) + b_ref[...]
    o_ref[...] = x_ref[...] + _rows(gt_ref[...], x_ref.shape[0]) * out


def _proj_res(z, w, bias, x, mod, tok_major, seq_len):
    m = x.shape[0]
    tm = min(ROW_TILE, m)
    rows = pl.BlockSpec((tm, D_MODEL), lambda i: (i, 0))
    return pl.pallas_call(
        _proj_res_kernel,
        grid=(m // tm,),
        in_specs=[rows,
                  pl.BlockSpec((D_MODEL, D_MODEL), lambda i: (0, 0), pipeline_mode=pl.Buffered(1)),
                  pl.BlockSpec((1, D_MODEL), lambda i: (0, 0)),
                  rows,
                  _mod_spec(tok_major, tm, seq_len, 5)],
        out_specs=rows,
        out_shape=jax.ShapeDtypeStruct((m, D_MODEL), f32),
        compiler_params=_cparams(1),
        name="proj_res",
    )(z, w, bias.reshape(1, D_MODEL), x, mod)


KV_COLS = N_KV_HEADS * HEAD_DIM
QKV_COLS = D_MODEL + 2 * KV_COLS


def _qkv_kernel(x_ref, ng_ref, sh_ref, sc_ref, w_ref, b_ref, cos_ref, sin_ref, o_ref, h_scr):
    @pl.when(pl.program_id(1) == 0)
    def _():
        tm = x_ref.shape[0]
        h_scr[...] = _rms_mod(x_ref[...], ng_ref[...], _rows(sh_ref[...], tm), _rows(sc_ref[...], tm)).astype(bf16)

    first = (lax.broadcasted_iota(jnp.int32, (x_ref.shape[0], LANES), 1) % HEAD_DIM) < HEAD_DIM // 2
    h = h_scr[...]
    for c0 in range(0, o_ref.shape[1], MXU_COLS):
        acc = _dot(h, w_ref[:, c0:c0 + MXU_COLS]) + b_ref[:, c0:c0 + MXU_COLS]
        group = slice(c0 // MXU_COLS * LANES, (c0 // MXU_COLS + 1) * LANES)
        cos, sin = cos_ref[:, group], sin_ref[:, group]
        for c in range(0, MXU_COLS, LANES):
            xc = acc[:, c:c + LANES]
            rot = jnp.where(first, pltpu.roll(xc, LANES - HEAD_DIM // 2, 1),
                            pltpu.roll(xc, HEAD_DIM // 2, 1))
            o_ref[:, c0 + c:c0 + c + LANES] = xc * cos + rot * sin


def _qkv(x, mod, tok_major, seq_len, norm_g, w, b, cos, sin):
    m = x.shape[0]
    tm = min(ROW_TILE, m)
    tn = 512
    assert tn == 2 * MXU_COLS == 2 * KV_COLS
    n_pos_blocks = cos.shape[0] // tm
    vec = pl.BlockSpec((1, D_MODEL), lambda i, j: (0, 0))
    tab = pl.BlockSpec((tm, 2 * LANES), lambda i, j: (i % n_pos_blocks, j // (D_MODEL // tn)))
    return pl.pallas_call(
        _qkv_kernel,
        grid=(m // tm, QKV_COLS // tn),
        in_specs=[pl.BlockSpec((tm, D_MODEL), lambda i, j: (i, 0)),
                  vec,
                  _mod_spec(tok_major, tm, seq_len, 3),
                  _mod_spec(tok_major, tm, seq_len, 4),
                  pl.BlockSpec((D_MODEL, tn), lambda i, j: (0, j)),
                  pl.BlockSpec((1, tn), lambda i, j: (0, j)),
                  tab, tab],
        out_specs=pl.BlockSpec((tm, tn), lambda i, j: (i, j)),
        out_shape=jax.ShapeDtypeStruct((m, QKV_COLS), f32),
        scratch_shapes=[pltpu.VMEM((tm, D_MODEL), bf16)],
        compiler_params=_cparams(2),
        name="swa_qkv",
    )(x, norm_g.reshape(1, D_MODEL), mod, mod, w, b.reshape(1, QKV_COLS), cos, sin)


def _rope_tables(positions):
    half = HEAD_DIM // 2
    inv_freq = ROPE_THETA ** (-jnp.arange(half, dtype=f32) / half)
    ang = positions.astype(f32)[:, None] * inv_freq[None, :]
    cos = jnp.tile(jnp.cos(ang), (1, LANES // half))
    sin = jnp.sin(ang)
    sin = jnp.tile(jnp.concatenate([-sin, sin], axis=1), (1, LANES // HEAD_DIM))
    return (jnp.concatenate([cos, cos, cos, jnp.ones_like(cos)], axis=1),
            jnp.concatenate([sin, sin, sin, jnp.zeros_like(sin)], axis=1))


def _head_masks(rows):
    lane = lax.broadcasted_iota(jnp.int32, (rows, LANES), 1)
    return lane < HEAD_DIM, lane >= HEAD_DIM


def _both_halves(pair):
    first, second = _head_masks(pair.shape[0])
    swapped = pltpu.roll(pair, HEAD_DIM, 1)
    return jnp.where(first, pair, swapped), jnp.where(second, pair, swapped)


def _attn_prompt_kernel(q_ref, kp_ref, kc_ref, vp_ref, vc_ref, sink_ref, o_ref):
    n = pl.program_id(1)
    blk = WINDOW
    m0, m1 = _head_masks(blk)
    m0k, m1k = _head_masks(2 * blk)
    qi = lax.broadcasted_iota(jnp.int32, (2 * blk, 2 * blk), 0) % blk
    sj = lax.broadcasted_iota(jnp.int32, (2 * blk, 2 * blk), 1)
    visible = (sj > qi) & (sj <= qi + blk) & ((n > 0) | (sj >= blk))
    top = lax.broadcasted_iota(jnp.int32, (2 * blk, 1), 0) < blk
    tiles = [slice(t * LANES, (t + 1) * LANES) for t in range(N_KV_HEADS // 2)]
    k_heads = [h for sl in tiles for h in _both_halves(jnp.concatenate([kp_ref[:, sl], kc_ref[:, sl]], axis=0))]
    v_heads = [h for sl in tiles for h in _both_halves(jnp.concatenate([vp_ref[:, sl], vc_ref[:, sl]], axis=0))]
    for c in range(N_KV_HEADS):
        kd = k_heads[c].astype(bf16)
        vd = v_heads[c]
        vcat = jnp.concatenate([jnp.where(m0k, vd, 0.0), jnp.where(m1k, vd, 0.0)], axis=0).astype(bf16)
        pairs = [c * (GQA_GROUP // 2) + jj for jj in range(GQA_GROUP // 2)]
        each = lambda fn, *lists: [fn(*args) for args in zip(*lists)]
        qp = [q_ref[:, pair * LANES:(pair + 1) * LANES] * ATTN_SCALE for pair in pairs]
        qs = each(lambda t: jnp.concatenate([jnp.where(m0, t, 0.0), jnp.where(m1, t, 0.0)], axis=0).astype(bf16), qp)
        s = each(lambda t: jnp.where(visible, _dot_nt(t, kd), NEG_BIG), qs)
        sk = [jnp.where(top, sink_ref[2 * pair], sink_ref[2 * pair + 1]) for pair in pairs]
        mx = each(lambda t, k_: jnp.maximum(jnp.max(t, axis=1, keepdims=True), k_), s, sk)
        p = each(lambda t, m_: jnp.exp(t - m_), s, mx)
        den = each(lambda t, k_, m_: jnp.sum(t, axis=1, keepdims=True) + jnp.exp(k_ - m_), p, sk, mx)
        p = each(lambda t, d_: (t * (1.0 / d_)).astype(bf16), p, den)
        pcat = each(lambda t: jnp.concatenate([t[0:blk], t[blk:]], axis=1), p)
        for pair, t in zip(pairs, pcat):
            o_ref[:, pair * LANES:(pair + 1) * LANES] = _dot(t, vcat).astype(bf16)


def _attn_prompt(qkv, sink, n_batch, seq_len):
    nb = seq_len // WINDOW
    k_blk = D_MODEL // KV_COLS
    cur = lambda off: pl.BlockSpec((WINDOW, KV_COLS), lambda b, n: (b * nb + n, k_blk + off))
    prev = lambda off: pl.BlockSpec((WINDOW, KV_COLS), lambda b, n: (b * nb + jnp.maximum(n - 1, 0), k_blk + off))
    return pl.pallas_call(
        _attn_prompt_kernel,
        grid=(n_batch, nb),
        in_specs=[pl.BlockSpec((WINDOW, D_MODEL), lambda b, n: (b * nb + n, 0)),
                  prev(0), cur(0), prev(1), cur(1),
                  pl.BlockSpec(memory_space=pltpu.SMEM)],
        out_specs=pl.BlockSpec((WINDOW, D_MODEL), lambda b, n: (b * nb + n, 0)),
        out_shape=jax.ShapeDtypeStruct((n_batch * seq_len, D_MODEL), bf16),
        compiler_params=_cparams(2),
        name="swa_prompt",
    )(qkv, qkv, qkv, qkv, qkv, sink)


SEQ_PER_GROUP = 4


def _attn_cached_kernel(n_tok, groups, q_ref, kn_ref, vn_ref, kc_ref, vc_ref, sink_ref, o_ref):
    rows = SEQ_PER_GROUP * n_tok
    n_st = GQA_GROUP * rows
    m0, m1 = _head_masks(rows)
    srow = lax.broadcasted_iota(jnp.int32, (n_st, 1), 0)
    row_seq = (srow % rows) // n_tok
    row_tok = srow % n_tok
    key_c = lax.broadcasted_iota(jnp.int32, (n_st, WINDOW), 1)
    vis_c = key_c > row_tok
    key_n = lax.broadcasted_iota(jnp.int32, (n_st, rows), 1)
    vis_n = ((key_n // n_tok) == row_seq) & ((key_n % n_tok) <= row_tok)
    for gi in range(groups):
        rs = slice(gi * rows, (gi + 1) * rows)
        for cp in range(N_KV_HEADS // 2):
            for ce in range(2):
                c = 2 * cp + ce
                sl = slice(cp * LANES, (cp + 1) * LANES)
                kt = [kc_ref[gi * SEQ_PER_GROUP + b, c * HEAD_DIM:(c + 1) * HEAD_DIM, :] for b in range(SEQ_PER_GROUP)]
                vt = [vc_ref[gi * SEQ_PER_GROUP + b, c * HEAD_DIM:(c + 1) * HEAD_DIM, :] for b in range(SEQ_PER_GROUP)]
                kn = _both_halves(kn_ref[rs, sl])[ce].astype(bf16)
                vn = _both_halves(vn_ref[rs, sl])[ce].astype(bf16)
                pieces = []
                for jj in range(GQA_GROUP // 2):
                    pair = c * (GQA_GROUP // 2) + jj
                    qp = q_ref[rs, pair * LANES:(pair + 1) * LANES]
                    pieces += [jnp.where(m0, qp, 0.0), jnp.where(m1, qp, 0.0)]
                qs = jnp.concatenate(pieces, axis=0).astype(bf16)
                s_c = jnp.zeros((n_st, WINDOW), f32)
                for b in range(SEQ_PER_GROUP):
                    kx = jnp.concatenate([kt[b], kt[b]], axis=0).astype(bf16)
                    s_c = jnp.where(row_seq == b, _dot(qs, kx), s_c)
                s_c = jnp.where(vis_c, s_c * ATTN_SCALE, NEG_BIG)
                s_n = jnp.where(vis_n, _dot_nt(qs, kn) * ATTN_SCALE, NEG_BIG)
                sk = sink_ref[c]
                sk = sk[:, 0:1]
                mx = jnp.maximum(jnp.maximum(jnp.max(s_c, axis=1, keepdims=True),
                                             jnp.max(s_n, axis=1, keepdims=True)), sk)
                p_c = jnp.exp(s_c - mx)
                p_n = jnp.exp(s_n - mx)
                den = (jnp.sum(p_c, axis=1, keepdims=True) + jnp.sum(p_n, axis=1, keepdims=True)
                       + jnp.exp(sk - mx))
                inv_den = 1.0 / den
                p_c = (p_c * inv_den).astype(bf16)
                p_n = (p_n * inv_den).astype(bf16)
                o = _dot(p_n, vn)
                for b in range(SEQ_PER_GROUP):
                    vx = jnp.concatenate([vt[b], vt[b]], axis=0).astype(bf16)
                    o = o + jnp.where(row_seq == b, _dot_nt(p_c, vx), 0.0)
                for jj in range(GQA_GROUP // 2):
                    pair = c * (GQA_GROUP // 2) + jj
                    o0 = o[(2 * jj) * rows:(2 * jj + 1) * rows]
                    o1 = o[(2 * jj + 1) * rows:(2 * jj + 2) * rows]
                    o_ref[rs, pair * LANES:(pair + 1) * LANES] = jnp.where(m0, o0, o1).astype(bf16)


def _attn_cached(qkv, k_cache, v_cache, sink, n_batch, n_tok):
    groups = 2
    seqs = SEQ_PER_GROUP * groups
    rows = seqs * n_tok
    k_blk = D_MODEL // KV_COLS
    n_st = GQA_GROUP * SEQ_PER_GROUP * n_tok
    head = (jnp.arange(N_KV_HEADS)[:, None] * GQA_GROUP
            + (jnp.arange(n_st)[None, :] // (SEQ_PER_GROUP * n_tok)))
    sink_tab = jnp.broadcast_to(sink[head][:, :, None], (N_KV_HEADS, n_st, LANES))
    cache_spec = pl.BlockSpec((seqs, N_KV_HEADS * HEAD_DIM, WINDOW), lambda i: (i, 0, 0))
    return pl.pallas_call(
        functools.partial(_attn_cached_kernel, n_tok, groups),
        grid=(n_batch // seqs,),
        in_specs=[pl.BlockSpec((rows, D_MODEL), lambda i: (i, 0)),
                  pl.BlockSpec((rows, KV_COLS), lambda i: (i, k_blk)),
                  pl.BlockSpec((rows, KV_COLS), lambda i: (i, k_blk + 1)),
                  cache_spec, cache_spec,
                  pl.BlockSpec((N_KV_HEADS, n_st, LANES), lambda i: (0, 0, 0))],
        out_specs=pl.BlockSpec((rows, D_MODEL), lambda i: (i, 0)),
        out_shape=jax.ShapeDtypeStruct((n_batch * n_tok, D_MODEL), bf16),
        compiler_params=_cparams(1),
        name="swa_cached",
    )(qkv, qkv, qkv, k_cache, v_cache, sink_tab)


def _trunk(x, mod_all, tok_major, n_batch, seq_len, pos0, wkv_in, shift_in, k_in, v_in, wts):
    to_seq_major = lambda t: t.reshape(seq_len, n_batch, -1).transpose(1, 0, 2).reshape(n_batch * seq_len, -1)
    to_tok_major = lambda t: t.reshape(n_batch, seq_len, -1).transpose(1, 0, 2).reshape(n_batch * seq_len, -1)
    mod = mod_all[0]
    ffn_w = dict(wts["ffn"])
    x, ffn_w[0, 0] = _ffn(x, mod, tok_major, seq_len, 0, wts["norm_g"][0, 0], ffn_w[0, 0])
    s0 = shift_in[0] if tok_major else jnp.zeros((n_batch, 1, D_MODEL), f32)
    *proj, h_tail = _rwkv_proj(x, mod, tok_major, seq_len, wts["norm_g"][0, 1], s0, wts["rw"])
    if tok_major:
        z, wkv_t = _scan_lanes(proj, wts["rw"], n_batch, seq_len, jnp.transpose(wkv_in[0], (1, 2, 3, 0)))
        wkv_out = jnp.transpose(wkv_t, (3, 0, 1, 2))
        shift_out = h_tail
    else:
        z, wkv_out = _scan(proj, wts["rw"], n_batch, seq_len)
        shift_out = h_tail.reshape(n_batch, -1, 8, D_MODEL)[:, -1, -1]
    x = _proj_res(z, wts["rw"]["wo"], jnp.zeros((D_MODEL,), f32), x, mod, tok_major, seq_len)
    x, ffn_w[0, 1] = _ffn(x, mod, tok_major, seq_len, 6, wts["norm_g"][0, 2], ffn_w[0, 1])
    mod = mod_all[1]
    x, ffn_w[1, 0] = _ffn(x, mod, tok_major, seq_len, 0, wts["norm_g"][1, 0], ffn_w[1, 0])
    positions = pos0 + jnp.arange(seq_len)
    if tok_major:
        positions = jnp.repeat(positions, n_batch)
    cos, sin = _rope_tables(positions)
    qkv = _qkv(x, mod, tok_major, seq_len, wts["norm_g"][1, 1], wts["sw_wqkv"], wts["sw_bqkv"], cos, sin)
    heads = lambda t: t.reshape(n_batch, -1, N_KV_HEADS, HEAD_DIM)
    if k_in is None:
        att = _attn_prompt(qkv, wts["sw_sink"], n_batch, seq_len)
        tail = qkv.reshape(n_batch, seq_len, QKV_COLS)[:, -WINDOW:, D_MODEL:]
        k_new, v_new = heads(tail[..., :KV_COLS]), heads(tail[..., KV_COLS:])
    else:
        win = k_in.shape[2]
        qkv = to_seq_major(qkv)
        cache_t = lambda t: jnp.transpose(t[0], (0, 2, 3, 1)).reshape(n_batch, N_KV_HEADS * HEAD_DIM, win)
        att = _attn_cached(qkv, cache_t(k_in), cache_t(v_in), wts["sw_sink"], n_batch, seq_len)
        att = to_tok_major(att)
        k_tok, v_tok = heads(qkv[:, D_MODEL:D_MODEL + KV_COLS]), heads(qkv[:, D_MODEL + KV_COLS:])
        k_new = jnp.concatenate([k_in[0], k_tok], axis=1)[:, -win:]
        v_new = jnp.concatenate([v_in[0], v_tok], axis=1)[:, -win:]
    x = _proj_res(att, wts["sw_wo"], wts["sw_bo"], x, mod, tok_major, seq_len)
    y, ffn_w[1, 1] = _ffn(x, mod, tok_major, seq_len, 6, wts["norm_g"][1, 2], ffn_w[1, 1], final_g=wts["final_g"])
    return (y, wkv_out[None], shift_out[None], k_new[None], v_new[None]), ffn_w


def _pad_cols(w):
    return jnp.pad(w, ((0, 0), (0, LORA_PAD - w.shape[1])))


def _pad_rows(w):
    return jnp.pad(w, ((0, LORA_PAD - w.shape[0]), (0, 0)))


def kernel(x_prompt, x_sample, state_rwkv_wkv, state_rwkv_shift, cache_swa_k, cache_swa_v, c_prompt, c_sample, norm_g, w_ada, b_ada, w_ffn_in, w_ffn_out, rw_mu, rw_wrkv, rw_w0, rw_w1, rw_w2, rw_a0, rw_a1, rw_a2, rw_g1, rw_g2, rw_kk, rw_ka, rw_rk, rw_gn_w, rw_gn_b, rw_wo, sw_wqkv, sw_bqkv, sw_sink, sw_wo, sw_bo, final_g):
    n_p, seq_p, _ = x_prompt.shape
    n_s, seq_s, _ = x_sample.shape
    row = lambda t: t.reshape(1, D_MODEL)
    rw = dict(
        mu=rw_mu[0], wrkv=rw_wrkv[0].astype(bf16),
        w1=_pad_cols(rw_w1[0]).astype(bf16), a1=_pad_cols(rw_a1[0]).astype(bf16), g1=rw_g1[0].astype(bf16),
        w2=_pad_rows(rw_w2[0]).astype(bf16), a2=_pad_rows(rw_a2[0]).astype(bf16), g2=rw_g2[0].astype(bf16),
        w0=row(rw_w0[0]), a0=row(rw_a0[0]), kk=row(rw_kk[0]), ka=row(rw_ka[0]), rk=row(rw_rk[0]),
        gn_w=row(rw_gn_w[0]), gn_b=row(rw_gn_b[0]), wo=rw_wo[0].astype(bf16))
    wts = dict(
        norm_g=norm_g, final_g=final_g,
        ffn={(l, s): (w_ffn_in, w_ffn_out, l, s) for l in range(2) for s in range(2)}, rw=rw,
        sw_wqkv=sw_wqkv[0].astype(bf16), sw_bqkv=sw_bqkv[0],
        sw_sink=sw_sink[0], sw_wo=sw_wo[0].astype(bf16), sw_bo=sw_bo[0])

    n_c = n_p + n_s
    pad = (-n_c) % 8
    c_all = jnp.concatenate([c_sample, c_prompt, jnp.zeros((pad, D_MODEL), f32)], axis=0)
    mod_s = _ada(c_all, w_ada, b_ada)
    mod_p = [m[n_s:n_c].reshape(n_p, 1, N_MOD * D_MODEL) for m in mod_s]

    (y_s, s_wkv, s_shift, s_k, s_v), wts["ffn"] = _trunk(
        jnp.transpose(x_sample, (1, 0, 2)).reshape(n_s * seq_s, D_MODEL), mod_s, True, n_s, seq_s, PAST_LEN,
        state_rwkv_wkv, state_rwkv_shift, cache_swa_k, cache_swa_v, wts)
    (y_p, p_wkv, p_shift, p_k, p_v), _ = _trunk(
        x_prompt.reshape(n_p * seq_p, D_MODEL), mod_p, False, n_p, seq_p, 0, None, None, None, None, wts)
    y_s = jnp.transpose(y_s.reshape(seq_s, n_s, D_MODEL), (1, 0, 2))
    return (y_p.reshape(n_p, seq_p, D_MODEL), y_s, p_wkv, p_shift, p_k, p_v, s_wkv, s_shift, s_k, s_v)
```

```python
import functools
import math

import jax
import jax.numpy as jnp
from jax import lax
from jax.experimental import pallas as pl
from jax.experimental.pallas import tpu as pltpu

f32 = jnp.float32
bf16 = jnp.bfloat16

D_MODEL = 2048
HEAD_DIM = 64
N_HEADS = D_MODEL // HEAD_DIM
N_KV_HEADS = 4
GQA_GROUP = N_HEADS // N_KV_HEADS
WINDOW = 128
ATTN_SCALE = HEAD_DIM ** -0.5
ROPE_THETA = 10000.0
D_FF = 5632
N_MOD = 9
RMS_EPS = 1e-6
GN_EPS = 64e-5
PAST_LEN = 8192
LANES = 128
SUBLANES = 8
MXU_COLS = 256
N_PAIRS = D_MODEL // LANES
LORA_PAD = 128
GATE_LORA = 256
SLAB = 64
SCAN_CHAINS = 16
VMEM_LIMIT = 56 * 1024 * 1024
NEG_BIG = -1e30
ROW_TILE = 512
FFN_COLS = 512
PROJ_COLS = 256
QKV_TILE = 512
ADA_COLS = 1024


def _cparams(n_axes):
    return pltpu.CompilerParams(dimension_semantics=("arbitrary",) * n_axes,
                                vmem_limit_bytes=VMEM_LIMIT)


def _dot(a, b):
    return jnp.dot(a, b, preferred_element_type=f32)


def _dot_nt(a, b):
    return lax.dot_general(a, b, (((1,), (1,)), ((), ())), preferred_element_type=f32)


def _split(x, n):
    if x.dtype == bf16:
        return [x]
    parts = []
    rem = x
    for i in range(n):
        p = rem.astype(bf16)
        parts.append(p)
        if i + 1 < n:
            rem = rem - p.astype(f32)
    return parts


def _mm(a, b, pa=1, pb=1, nt=False):
    a_parts = _split(a, pa)
    b_parts = _split(b, pb)
    order = max(len(a_parts), len(b_parts))
    acc = None
    for i, x in enumerate(a_parts):
        for j, y in enumerate(b_parts):
            if i + j >= order:
                continue
            t = _dot_nt(x, y) if nt else _dot(x, y)
            acc = t if acc is None else acc + t
    return acc


def _rms_mod(x, g, shift, scale):
    return x * lax.rsqrt(jnp.mean(x * x, axis=-1, keepdims=True) + RMS_EPS) * (g * (1.0 + scale)) + shift


def _mod_spec(tok_major, tm, seq_len, idx):
    if tok_major:
        return pl.BlockSpec((tm // seq_len, D_MODEL), lambda i, *_: (0, idx))
    return pl.BlockSpec((None, 1, D_MODEL), lambda i, *_: ((i * tm) // seq_len, 0, idx))


def _rows(v, tm):
    n = v.shape[0]
    return v if n in (1, tm) else jnp.concatenate([v] * (tm // n), axis=0)


def _ada_kernel(c_ref, w_ref, b_ref, *o_refs):
    c = c_ref[...]
    s = (c * jax.nn.sigmoid(c)).astype(bf16)
    for layer, o_ref in enumerate(o_refs):
        o_ref[...] = _dot(s, w_ref[layer].astype(bf16)) + b_ref[layer]


def _ada(c_all, w_ada, b_ada):
    n_layers, _, n_out = w_ada.shape
    rows = c_all.shape[0]
    tn = ADA_COLS
    return pl.pallas_call(
        _ada_kernel,
        grid=(n_out // tn,),
        in_specs=[pl.BlockSpec((rows, D_MODEL), lambda j: (0, 0)),
                  pl.BlockSpec((n_layers, D_MODEL, tn), lambda j: (0, 0, j)),
                  pl.BlockSpec((n_layers, 1, tn), lambda j: (0, 0, j))],
        out_specs=[pl.BlockSpec((rows, tn), lambda j: (0, j))] * n_layers,
        out_shape=[jax.ShapeDtypeStruct((rows, n_out), f32)] * n_layers,
        compiler_params=_cparams(1),
        name="ada_mod",
    )(c_all, w_ada, b_ada.reshape(n_layers, 1, n_out))


def _ffn_kernel(final_norm, emit_bf16, tiled_gate, x_ref, ng_ref, sh_ref, sc_ref, gt_ref, wg_ref, wu_ref, wo_ref,
                fg_ref, o_ref, *rest):
    rest = list(rest)
    if emit_bf16:
        wg_out, wu_out, wo_out = rest[:3]
        del rest[:3]
    h_scr = rest.pop(0)
    gt_scr = rest.pop(0) if tiled_gate else None
    f = pl.program_id(1)
    tm = x_ref.shape[0]

    @pl.when(f == 0)
    def _():
        h_scr[...] = _rms_mod(x_ref[...], ng_ref[...], _rows(sh_ref[...], tm), _rows(sc_ref[...], tm)).astype(bf16)
        o_ref[...] = x_ref[...]
        if tiled_gate:
            gt_scr[...] = 0.5 * _rows(gt_ref[...], tm)

    wg, wu, wo = wg_ref[...], wu_ref[...], wo_ref[...]
    if emit_bf16:
        wg, wu, wo = wg.astype(bf16), wu.astype(bf16), wo.astype(bf16)
        wg_out[...] = wg
        wu_out[...] = wu
        wo_out[...] = wo
    h = h_scr[...]
    gate = _dot(h, wg)
    up = _dot(h, wu)
    act = (gate * jax.nn.sigmoid(gate) * up).astype(bf16)
    half_gate = gt_scr[...] if tiled_gate else 0.5 * gt_ref[...]
    o_ref[...] += half_gate * _dot(act, wo)

    if final_norm:
        @pl.when(f == pl.num_programs(1) - 1)
        def _():
            y = o_ref[...]
            o_ref[...] = y * lax.rsqrt(jnp.mean(y * y, axis=-1, keepdims=True) + RMS_EPS) * fg_ref[...]


def _ffn(x, mod, tok_major, seq_len, mod_base, norm_g, weights, final_g=None):
    m = x.shape[0]
    tm = min(ROW_TILE, m)
    emit = len(weights) == 4
    tf = FFN_COLS // 2 if emit else FFN_COLS
    nf = D_FF // tf
    vec = pl.BlockSpec((1, D_MODEL), lambda i, j: (0, 0))
    fg = jnp.ones((1, D_MODEL), f32) if final_g is None else final_g.reshape(1, D_MODEL)
    w_in_spec = pl.BlockSpec((D_MODEL, tf), lambda i, j: (0, j))
    w_out_spec = pl.BlockSpec((tf, D_MODEL), lambda i, j: (j, 0))
    out_specs = [pl.BlockSpec((tm, D_MODEL), lambda i, j: (i, 0))]
    out_shape = [jax.ShapeDtypeStruct((m, D_MODEL), f32)]
    if emit:
        assert m == tm, "the weight copies are written once, by a single row tile"
        w_ffn_in, w_ffn_out, layer, slot = weights
        w_args = (w_ffn_in, w_ffn_in, w_ffn_out)
        w_specs = [pl.BlockSpec((None, None, D_MODEL, tf), lambda i, j: (layer, slot, 0, j)),
                   pl.BlockSpec((None, None, D_MODEL, tf), lambda i, j: (layer, slot, 0, j + nf)),
                   pl.BlockSpec((None, None, tf, D_MODEL), lambda i, j: (layer, slot, j, 0))]
        out_specs += [w_in_spec, w_in_spec, w_out_spec]
        out_shape += [jax.ShapeDtypeStruct((D_MODEL, D_FF), bf16), jax.ShapeDtypeStruct((D_MODEL, D_FF), bf16),
                      jax.ShapeDtypeStruct((D_FF, D_MODEL), bf16)]
    else:
        w_args = weights
        w_specs = [w_in_spec, w_in_spec, w_out_spec]
    outs = pl.pallas_call(
        functools.partial(_ffn_kernel, final_g is not None, emit, tok_major),
        grid=(m // tm, nf),
        in_specs=[pl.BlockSpec((tm, D_MODEL), lambda i, j: (i, 0)),
                  vec,
                  _mod_spec(tok_major, tm, seq_len, mod_base),
                  _mod_spec(tok_major, tm, seq_len, mod_base + 1),
                  _mod_spec(tok_major, tm, seq_len, mod_base + 2)] + w_specs + [vec],
        out_specs=out_specs,
        out_shape=out_shape,
        scratch_shapes=[pltpu.VMEM((tm, D_MODEL), bf16)] + ([pltpu.VMEM((tm, D_MODEL), f32)] if tok_major else []),
        compiler_params=_cparams(2),
        name="ffn",
    )(x, norm_g.reshape(1, D_MODEL), mod, mod, mod, *w_args, fg)
    return (outs[0], tuple(outs[1:])) if emit else (outs[0], weights)


def _rwkv_proj_kernel(seq_len, tm, h_rows, tok_major,
                      x_ref, ng_ref, sh_ref, sc_ref, s0_ref, mu_ref, w1_ref, a1_ref, g1_ref,
                      w_ref, w2_ref, a2_ref, g2_ref, w0_ref, a0_ref,
                      r_ref, ld_ref, k_ref, v_ref, a_ref, g_ref, h_ref,
                      hs, xr, xk, xv, tw, ta, sg):
    i = pl.program_id(0)
    j = pl.program_id(1)

    @pl.when(j == 0)
    def _():
        h = _rms_mod(x_ref[...], ng_ref[...], _rows(sh_ref[...], tm), _rows(sc_ref[...], tm))
        h_ref[...] = h[tm - h_rows:tm, :]
        if tok_major:
            n_seq = s0_ref.shape[0]
            prev = jnp.concatenate([s0_ref[...], h[0:tm - n_seq, :]], axis=0)
        else:
            @pl.when(i == 0)
            def _():
                hs[0:SUBLANES, :] = jnp.zeros((SUBLANES, D_MODEL), f32)

            @pl.when(i > 0)
            def _():
                hs[0:SUBLANES, :] = hs[tm:tm + SUBLANES, :]

            hs[SUBLANES:tm + SUBLANES, :] = h
            row = i * tm + lax.broadcasted_iota(jnp.int32, (tm, 1), 0)
            prev = jnp.where(row % seq_len == 0, s0_ref[...], hs[SUBLANES - 1:tm + SUBLANES - 1, :])
        xx = prev - h
        mu = mu_ref[...]
        xr[...] = (h + xx * mu[0:1, :]).astype(bf16)
        xk[...] = (h + xx * mu[2:3, :]).astype(bf16)
        xv[...] = (h + xx * mu[3:4, :]).astype(bf16)
        xw = (h + xx * mu[1:2, :]).astype(bf16)
        tw[...] = jnp.tanh(_dot(xw, w1_ref[...])).astype(bf16)
        xa = (h + xx * mu[4:5, :]).astype(bf16)
        ta[...] = _dot(xa, a1_ref[...]).astype(bf16)
        xg = (h + xx * mu[5:6, :]).astype(bf16)
        sg[...] = jax.nn.sigmoid(_dot(xg, g1_ref[...])).astype(bf16)

    cols = pl.ds(pl.multiple_of(j * r_ref.shape[1], r_ref.shape[1]), r_ref.shape[1])
    r_ref[...] = _dot(xr[...], w_ref[0, :, cols])
    k_ref[...] = _dot(xk[...], w_ref[1, :, cols])
    v_ref[...] = _dot(xv[...], w_ref[2, :, cols])
    z = w0_ref[...] + _dot(tw[...], w2_ref[...])
    ld_ref[...] = -jax.nn.sigmoid(z) * math.exp(-0.5)
    a_ref[...] = jax.nn.sigmoid(a0_ref[...] + _dot(ta[...], a2_ref[...]))
    g_ref[...] = _dot(sg[...], g2_ref[...])


def _rwkv_proj(x, mod, tok_major, seq_len, norm_g, s0, rwp):
    m = x.shape[0]
    tm = min(ROW_TILE, m)
    tn = PROJ_COLS
    h_rows = tm // seq_len if tok_major else SUBLANES
    full = lambda shape: pl.BlockSpec(shape, lambda i, j: (0,) * len(shape))
    col = lambda rows: pl.BlockSpec((rows, tn), lambda i, j: (0, j))
    w_res = pl.BlockSpec((3, D_MODEL, D_MODEL), lambda i, j: (0, 0, 0), pipeline_mode=pl.Buffered(1))
    if tok_major:
        assert m == tm
        s0_spec = pl.BlockSpec((tm // seq_len, D_MODEL), lambda i, j: (0, 0))
    else:
        s0_spec = pl.BlockSpec((None, 1, D_MODEL), lambda i, j: ((i * tm) // seq_len, 0, 0))
    out_spec = pl.BlockSpec((tm, tn), lambda i, j: (i, j))
    out_sds = jax.ShapeDtypeStruct((m, D_MODEL), f32)
    outs = pl.pallas_call(
        functools.partial(_rwkv_proj_kernel, seq_len, tm, h_rows, tok_major),
        grid=(m // tm, D_MODEL // tn),
        in_specs=[pl.BlockSpec((tm, D_MODEL), lambda i, j: (i, 0)),
                  full((1, D_MODEL)),
                  _mod_spec(tok_major, tm, seq_len, 3),
                  _mod_spec(tok_major, tm, seq_len, 4),
                  s0_spec,
                  full((6, D_MODEL)),
                  full((D_MODEL, LORA_PAD)), full((D_MODEL, LORA_PAD)), full((D_MODEL, GATE_LORA)),
                  w_res,
                  col(LORA_PAD), col(LORA_PAD), col(GATE_LORA), col(1), col(1)],
        out_specs=[out_spec] * 6 + [pl.BlockSpec((h_rows, D_MODEL), lambda i, j: (i, 0))],
        out_shape=[out_sds] * 6 + [jax.ShapeDtypeStruct((m // tm * h_rows, D_MODEL), f32)],
        scratch_shapes=[pltpu.VMEM((tm + SUBLANES, D_MODEL), f32),
                        pltpu.VMEM((tm, D_MODEL), bf16), pltpu.VMEM((tm, D_MODEL), bf16),
                        pltpu.VMEM((tm, D_MODEL), bf16),
                        pltpu.VMEM((tm, LORA_PAD), bf16), pltpu.VMEM((tm, LORA_PAD), bf16),
                        pltpu.VMEM((tm, GATE_LORA), bf16)],
        compiler_params=_cparams(2),
        name="rwkv_proj",
    )(x, norm_g.reshape(1, D_MODEL), mod, mod, s0, rwp["mu"], rwp["w1"], rwp["a1"], rwp["g1"],
      rwp["wrkv"], rwp["w2"], rwp["a2"], rwp["g2"], rwp["w0"], rwp["a0"])
    return outs


def _scan_chains(acts, params, states):
    n_st = 2 * SLAB
    row = lax.broadcasted_iota(jnp.int32, (n_st, LANES), 0)
    lane = lax.broadcasted_iota(jnp.int32, (n_st, LANES), 1)
    own = (row // SLAB) == (lane // HEAD_DIM)
    blk = (row // SLAB) == (lane // SLAB)
    strict = blk & (lane < row)
    incl = blk & (lane <= row)
    eye = jnp.where(row == lane, 1.0, 0.0)
    twice = lambda t: jnp.concatenate([t, t], axis=0)
    st = lambda t: jnp.where(own, twice(t), 0.0)
    each = lambda fn, *lists: [fn(*args) for args in zip(*lists)]

    r, ld, k, v, a, g = [[act[i] for act in acts] for i in range(6)]
    kkp, kap, rkp, gnw, gnb = [[par[i] for par in params] for i in range(5)]

    r64 = lax.broadcasted_iota(jnp.int32, (SLAB, SLAB), 0)
    c64 = lax.broadcasted_iota(jnp.int32, (SLAB, SLAB), 1)
    tri_ones = jnp.concatenate([jnp.where(c64 <= r64, 1.0, 0.0),
                                jnp.ones((SLAB, SLAB), f32)], axis=0).astype(bf16)
    sums = each(lambda t: _mm(tri_ones, t, pb=3), ld)
    cs = each(lambda t: t[0:SLAB], sums)
    tot = each(lambda t: t[SLAB:n_st], sums)

    kk_raw = each(lambda t, p: st(t * p), k, kkp)
    kk = each(lambda t: t * (1.0 / jnp.maximum(jnp.sqrt(jnp.sum(t * t, axis=1, keepdims=True)), 1e-12)), kk_raw)
    k2 = each(lambda kt, at, p: st(kt * (1.0 + (at - 1.0) * p)), k, a, kap)
    b = each(lambda t, at: t * twice(at), kk, a)
    r_s = each(st, r)
    v_s = each(st, v)
    bonus = each(lambda rt, kt, p, vt: jnp.sum(rt * kt * p, axis=1, keepdims=True) * vt, r_s, k2, rkp, v_s)

    e_neg = each(lambda c_: twice(jnp.exp(-c_)), cs)
    e_tail = each(lambda t_, c_: twice(jnp.exp(t_ - c_)), tot, cs)
    a_t = each(lambda t, c_, l_: -t * twice(jnp.exp(c_ - l_)), kk, cs, ld)
    r_t = each(lambda t, c_: t * twice(jnp.exp(c_)), r_s, cs)
    k_t = each(lambda t, e: t * e, k2, e_neg)
    b_t = each(lambda t, e: t * e, b, e_neg)
    k_h = each(lambda t, e: t * e, k2, e_tail)
    b_h = each(lambda t, e: t * e, b, e_tail)

    lhs1 = each(lambda x_, y_: jnp.concatenate([x_, y_], axis=0), a_t, r_t)
    rhs1 = each(lambda x_, y_: jnp.concatenate([x_, y_], axis=0), k_t, b_t)
    x = each(lambda l_, r_: _mm(l_, r_, nt=True), lhs1, rhs1)
    a_k = each(lambda t: jnp.where(strict, t[0:n_st, 0:n_st], 0.0), x)
    a_b = each(lambda t: jnp.where(strict, t[0:n_st, n_st:], 0.0), x)
    r_kb = each(lambda t: jnp.concatenate([jnp.where(incl, t[n_st:, 0:n_st], 0.0),
                                           jnp.where(incl, t[n_st:, n_st:], 0.0)], axis=1), x)

    p1 = each(lambda l_, s_: _mm(l_, s_, nt=True), lhs1, states)
    rhs_u = each(lambda p_, ak, vt: p_[0:n_st] + _mm(ak, vt), p1, a_k, v_s)

    near = (row // 2) == (lane // 2)
    t_inv = each(lambda t: eye + jnp.where(near, t, 0.0), a_b)
    s = 2
    while s < SLAB:
        off = ((row // (2 * s)) == (lane // (2 * s))) & ((row // s) != (lane // s))
        w = each(lambda ab, ti: _mm(jnp.where(off, ab, 0.0), ti), a_b, t_inv)
        t_inv = each(lambda ti, w_: ti + _mm(ti, w_), t_inv, w)
        s *= 2
    u = each(_mm, t_inv, rhs_u)

    vu = each(lambda x_, y_: jnp.concatenate([x_, y_], axis=0), v_s, u)
    y = each(lambda p_, rk_, vu_: p_[n_st:] + _mm(rk_, vu_), p1, r_kb, vu)

    def finish(y_, w_, b_, bonus_, g_):
        mean = jnp.sum(y_, axis=1, keepdims=True) * (1.0 / HEAD_DIM)
        dev = jnp.where(own, y_ - mean, 0.0)
        var = jnp.sum(dev * dev, axis=1, keepdims=True) * (1.0 / HEAD_DIM)
        out = jnp.where(own, dev * lax.rsqrt(var + GN_EPS) * w_ + b_, 0.0) + bonus_
        return (out[0:SLAB] + out[SLAB:n_st]) * g_
    z = each(finish, y, gnw, gnb, bonus, g)

    vu_t = each(lambda t: t.T, vu)
    kb_h = each(lambda x_, y_: jnp.concatenate([x_, y_], axis=0), k_h, b_h)
    new_states = each(lambda s_, t_, vt, kb: s_ * jnp.exp(t_[0:1, :]) + _mm(vt, kb), states, tot, vu_t, kb_h)
    return z, new_states


def _scan_kernel(n_seq, n_pp, r_ref, ld_ref, k_ref, v_ref, a_ref, g_ref, kk_ref, ka_ref, rk_ref, gw_ref,
                 gb_ref, z_ref, s_out_ref, s_scr):
    c = pl.program_id(1)

    @pl.when(c == 0)
    def _():
        s_scr[...] = jnp.zeros_like(s_scr)

    chains = [(b, pp) for b in range(n_seq) for pp in range(n_pp)]
    lanes = [slice(pp * LANES, (pp + 1) * LANES) for _, pp in chains]
    acts = [tuple(ref[b, :, sl] for ref in (r_ref, ld_ref, k_ref, v_ref, a_ref, g_ref))
            for (b, _), sl in zip(chains, lanes)]
    params = [tuple(ref[:, sl] for ref in (kk_ref, ka_ref, rk_ref, gw_ref, gb_ref)) for sl in lanes]
    states = [s_scr[n] for n in range(len(chains))]
    z, new_states = _scan_chains(acts, params, states)
    for n, (b, _) in enumerate(chains):
        z_ref[b, :, lanes[n]] = z[n].astype(bf16)
        s_scr[n] = new_states[n]

    @pl.when(c == pl.num_programs(1) - 1)
    def _():
        for n, (b, pp) in enumerate(chains):
            s = s_scr[n]
            s_out_ref[b, 2 * pp] = s[0:HEAD_DIM, 0:HEAD_DIM]
            s_out_ref[b, 2 * pp + 1] = s[HEAD_DIM:, HEAD_DIM:]


def _scan(proj, rwp, n_batch, seq_len):
    n_pp = SCAN_CHAINS // n_batch
    w = LANES * n_pp
    act = pl.BlockSpec((n_batch, SLAB, w), lambda p, c: (0, c, p))
    vec = pl.BlockSpec((1, w), lambda p, c: (0, p))
    st_spec = pl.BlockSpec((n_batch, 2 * n_pp, HEAD_DIM, HEAD_DIM), lambda p, c: (0, p, 0, 0))
    z, s_out = pl.pallas_call(
        functools.partial(_scan_kernel, n_batch, n_pp),
        grid=(N_PAIRS // n_pp, seq_len // SLAB),
        in_specs=[act] * 6 + [vec] * 5,
        out_specs=[act, st_spec],
        out_shape=[jax.ShapeDtypeStruct((n_batch, seq_len, D_MODEL), bf16),
                   jax.ShapeDtypeStruct((n_batch, N_HEADS, HEAD_DIM, HEAD_DIM), f32)],
        scratch_shapes=[pltpu.VMEM((n_batch * n_pp, 2 * SLAB, LANES), f32)],
        compiler_params=_cparams(2),
        name="rwkv_scan",
    )(*[p.reshape(n_batch, seq_len, D_MODEL) for p in proj],
      rwp["kk"], rwp["ka"], rwp["rk"], rwp["gn_w"], rwp["gn_b"])
    return z.reshape(n_batch * seq_len, D_MODEL), s_out


def _scan_lanes_kernel(n_tok, r_ref, ld_ref, k_ref, v_ref, a_ref, g_ref, kk_ref, ka_ref, rk_ref, gw_ref, gb_ref,
                       s_in_ref, z_ref, s_out_ref, kk_s, w_s, b_s, k2_s, r_s, v_s, y_s):
    hd = HEAD_DIM
    colsum = lambda t: jnp.sum(t, axis=0, keepdims=True)
    for t in range(n_tok):
        r_t, k_t, v_t, a_t = r_ref[t].T, k_ref[t].T, v_ref[t].T, a_ref[t].T
        kk_raw = k_t * kk_ref[...]
        k2 = k_t * (1.0 + (a_t - 1.0) * ka_ref[...])
        w = jnp.exp(ld_ref[t].T)
        for e in range(2):
            sl = slice(e * hd, (e + 1) * hd)
            kr = kk_raw[sl]
            kk = kr * (1.0 / jnp.maximum(jnp.sqrt(colsum(kr * kr)), 1e-12))
            kk_s[e, t] = kk
            b_s[e, t] = kk * a_t[sl]
            k2_s[e, t] = k2[sl]
            w_s[e, t] = w[sl]
            r_s[e, t] = r_t[sl]
            v_s[e, t] = v_t[sl]

    rows_per_iter = 2

    def per_rows(i, carry):
        chains = [(e, rows_per_iter * i + d) for e in range(2) for d in range(rows_per_iter)]
        s = [s_in_ref[e, vi] for e, vi in chains]
        for t in range(n_tok):
            for c, (e, vi) in enumerate(chains):
                s_kk = colsum(s[c] * kk_s[e, t])
                s[c] = s[c] * w_s[e, t] - s_kk * b_s[e, t] + v_s[e, t, pl.ds(vi, 1), :] * k2_s[e, t]
                y_s[e, t, pl.ds(vi, 1), :] = colsum(s[c] * r_s[e, t])
        for c, (e, vi) in enumerate(chains):
            s_out_ref[e, vi] = s[c]
        return carry
    lax.fori_loop(0, hd // rows_per_iter, per_rows, 0)

    for t in range(n_tok):
        outs = []
        for e in range(2):
            sl = slice(e * hd, (e + 1) * hd)
            y = y_s[e, t]
            dev = y - colsum(y) * (1.0 / hd)
            var = colsum(dev * dev) * (1.0 / hd)
            bonus = colsum(r_s[e, t] * k2_s[e, t] * rk_ref[sl, :]) * v_s[e, t]
            outs.append(dev * lax.rsqrt(var + GN_EPS) * gw_ref[sl, :] + gb_ref[sl, :] + bonus)
        z_ref[t] = (jnp.concatenate(outs, axis=0).T * g_ref[t]).astype(bf16)


def _scan_lanes(proj, rwp, n_batch, n_tok, state):
    acts = [p.reshape(n_tok, n_batch, D_MODEL) for p in proj]
    lane_bc = lambda v: jnp.broadcast_to(v.reshape(D_MODEL, 1), (D_MODEL, n_batch))
    params = [lane_bc(rwp[n]) for n in ("kk", "ka", "rk", "gn_w", "gn_b")]
    act = pl.BlockSpec((n_tok, n_batch, LANES), lambda p: (0, 0, p))
    par = pl.BlockSpec((LANES, n_batch), lambda p: (p, 0))
    st_spec = pl.BlockSpec((2, HEAD_DIM, HEAD_DIM, n_batch), lambda p: (p, 0, 0, 0))
    per_tok = pltpu.VMEM((2, n_tok, HEAD_DIM, n_batch), f32)
    z, s_out = pl.pallas_call(
        functools.partial(_scan_lanes_kernel, n_tok),
        grid=(N_PAIRS,),
        in_specs=[act] * 6 + [par] * 5 + [st_spec],
        out_specs=[act, st_spec],
        out_shape=[jax.ShapeDtypeStruct((n_tok, n_batch, D_MODEL), bf16),
                   jax.ShapeDtypeStruct(state.shape, f32)],
        scratch_shapes=[per_tok] * 7,
        compiler_params=_cparams(1),
        name="rwkv_decode_scan",
    )(*acts, *params, state)
    return z.reshape(n_tok * n_batch, D_MODEL), s_out


def _proj_res_kernel(z_ref, w_ref, b_ref, x_ref, gt_ref, o_ref):
    out = _dot(z_ref[...], w_ref[...]) + b_ref[...]
    o_ref[...] = x_ref[...] + _rows(gt_ref[...], x_ref.shape[0]) * out


def _proj_res(z, w, bias, x, mod, tok_major, seq_len):
    m = x.shape[0]
    tm = min(ROW_TILE, m)
    rows = pl.BlockSpec((tm, D_MODEL), lambda i: (i, 0))
    return pl.pallas_call(
        _proj_res_kernel,
        grid=(m // tm,),
        in_specs=[rows,
                  pl.BlockSpec((D_MODEL, D_MODEL), lambda i: (0, 0), pipeline_mode=pl.Buffered(1)),
                  pl.BlockSpec((1, D_MODEL), lambda i: (0, 0)),
                  rows,
                  _mod_spec(tok_major, tm, seq_len, 5)],
        out_specs=rows,
        out_shape=jax.ShapeDtypeStruct((m, D_MODEL), f32),
        compiler_params=_cparams(1),
        name="proj_res",
    )(z, w, bias.reshape(1, D_MODEL), x, mod)


KV_COLS = N_KV_HEADS * HEAD_DIM
QKV_COLS = D_MODEL + 2 * KV_COLS


def _qkv_kernel(x_ref, ng_ref, sh_ref, sc_ref, w_ref, b_ref, cos_ref, sin_ref, o_ref, h_scr):
    @pl.when(pl.program_id(1) == 0)
    def _():
        tm = x_ref.shape[0]
        h_scr[...] = _rms_mod(x_ref[...], ng_ref[...], _rows(sh_ref[...], tm), _rows(sc_ref[...], tm)).astype(bf16)

    first = (lax.broadcasted_iota(jnp.int32, (x_ref.shape[0], LANES), 1) % HEAD_DIM) < HEAD_DIM // 2
    h = h_scr[...]
    for c0 in range(0, o_ref.shape[1], MXU_COLS):
        acc = _dot(h, w_ref[:, c0:c0 + MXU_COLS]) + b_ref[:, c0:c0 + MXU_COLS]
        group = slice(c0 // MXU_COLS * LANES, (c0 // MXU_COLS + 1) * LANES)
        cos, sin = cos_ref[:, group], sin_ref[:, group]
        for c in range(0, MXU_COLS, LANES):
            xc = acc[:, c:c + LANES]
            rot = jnp.where(first, pltpu.roll(xc, LANES - HEAD_DIM // 2, 1),
                            pltpu.roll(xc, HEAD_DIM // 2, 1))
            o_ref[:, c0 + c:c0 + c + LANES] = xc * cos + rot * sin


def _qkv(x, mod, tok_major, seq_len, norm_g, w, b, cos, sin):
    m = x.shape[0]
    tm = min(ROW_TILE, m)
    tn = QKV_TILE
    assert tn == 2 * MXU_COLS == 2 * KV_COLS
    n_pos_blocks = cos.shape[0] // tm
    vec = pl.BlockSpec((1, D_MODEL), lambda i, j: (0, 0))
    tab = pl.BlockSpec((tm, 2 * LANES), lambda i, j: (i % n_pos_blocks, j // (D_MODEL // tn)))
    return pl.pallas_call(
        _qkv_kernel,
        grid=(m // tm, QKV_COLS // tn),
        in_specs=[pl.BlockSpec((tm, D_MODEL), lambda i, j: (i, 0)),
                  vec,
                  _mod_spec(tok_major, tm, seq_len, 3),
                  _mod_spec(tok_major, tm, seq_len, 4),
                  pl.BlockSpec((D_MODEL, tn), lambda i, j: (0, j)),
                  pl.BlockSpec((1, tn), lambda i, j: (0, j)),
                  tab, tab],
        out_specs=pl.BlockSpec((tm, tn), lambda i, j: (i, j)),
        out_shape=jax.ShapeDtypeStruct((m, QKV_COLS), f32),
        scratch_shapes=[pltpu.VMEM((tm, D_MODEL), bf16)],
        compiler_params=_cparams(2),
        name="swa_qkv",
    )(x, norm_g.reshape(1, D_MODEL), mod, mod, w, b.reshape(1, QKV_COLS), cos, sin)


def _rope_tables(positions):
    half = HEAD_DIM // 2
    inv_freq = ROPE_THETA ** (-jnp.arange(half, dtype=f32) / half)
    ang = positions.astype(f32)[:, None] * inv_freq[None, :]
    cos = jnp.tile(jnp.cos(ang), (1, LANES // half))
    sin = jnp.sin(ang)
    sin = jnp.tile(jnp.concatenate([-sin, sin], axis=1), (1, LANES // HEAD_DIM))
    return (jnp.concatenate([cos, cos, cos, jnp.ones_like(cos)], axis=1),
            jnp.concatenate([sin, sin, sin, jnp.zeros_like(sin)], axis=1))


def _head_masks(rows):
    lane = lax.broadcasted_iota(jnp.int32, (rows, LANES), 1)
    return lane < HEAD_DIM, lane >= HEAD_DIM


def _both_halves(pair):
    first, second = _head_masks(pair.shape[0])
    swapped = pltpu.roll(pair, HEAD_DIM, 1)
    return jnp.where(first, pair, swapped), jnp.where(second, pair, swapped)


def _attn_prompt_kernel(q_ref, kp_ref, kc_ref, vp_ref, vc_ref, sink_ref, o_ref):
    n = pl.program_id(1)
    blk = WINDOW
    m0, m1 = _head_masks(blk)
    m0k, m1k = _head_masks(2 * blk)
    qi = lax.broadcasted_iota(jnp.int32, (2 * blk, 2 * blk), 0) % blk
    sj = lax.broadcasted_iota(jnp.int32, (2 * blk, 2 * blk), 1)
    visible = (sj > qi) & (sj <= qi + blk) & ((n > 0) | (sj >= blk))
    top = lax.broadcasted_iota(jnp.int32, (2 * blk, 1), 0) < blk
    tiles = [slice(t * LANES, (t + 1) * LANES) for t in range(N_KV_HEADS // 2)]
    k_heads = [h for sl in tiles for h in _both_halves(jnp.concatenate([kp_ref[:, sl], kc_ref[:, sl]], axis=0))]
    v_heads = [h for sl in tiles for h in _both_halves(jnp.concatenate([vp_ref[:, sl], vc_ref[:, sl]], axis=0))]
    for c in range(N_KV_HEADS):
        kd = k_heads[c].astype(bf16)
        vd = v_heads[c]
        vcat = jnp.concatenate([jnp.where(m0k, vd, 0.0), jnp.where(m1k, vd, 0.0)], axis=0).astype(bf16)
        pairs = [c * (GQA_GROUP // 2) + jj for jj in range(GQA_GROUP // 2)]
        each = lambda fn, *lists: [fn(*args) for args in zip(*lists)]
        qp = [q_ref[:, pair * LANES:(pair + 1) * LANES] * ATTN_SCALE for pair in pairs]
        qs = each(lambda t: jnp.concatenate([jnp.where(m0, t, 0.0), jnp.where(m1, t, 0.0)], axis=0).astype(bf16), qp)
        s = each(lambda t: jnp.where(visible, _dot_nt(t, kd), NEG_BIG), qs)
        sk = [jnp.where(top, sink_ref[2 * pair], sink_ref[2 * pair + 1]) for pair in pairs]
        mx = each(lambda t, k_: jnp.maximum(jnp.max(t, axis=1, keepdims=True), k_), s, sk)
        p = each(lambda t, m_: jnp.exp(t - m_), s, mx)
        den = each(lambda t, k_, m_: jnp.sum(t, axis=1, keepdims=True) + jnp.exp(k_ - m_), p, sk, mx)
        p = each(lambda t, d_: (t * (1.0 / d_)).astype(bf16), p, den)
        pcat = each(lambda t: jnp.concatenate([t[0:blk], t[blk:]], axis=1), p)
        for pair, t in zip(pairs, pcat):
            o_ref[:, pair * LANES:(pair + 1) * LANES] = _dot(t, vcat).astype(bf16)


def _attn_prompt(qkv, sink, n_batch, seq_len):
    nb = seq_len // WINDOW
    k_blk = D_MODEL // KV_COLS
    cur = lambda off: pl.BlockSpec((WINDOW, KV_COLS), lambda b, n: (b * nb + n, k_blk + off))
    prev = lambda off: pl.BlockSpec((WINDOW, KV_COLS), lambda b, n: (b * nb + jnp.maximum(n - 1, 0), k_blk + off))
    return pl.pallas_call(
        _attn_prompt_kernel,
        grid=(n_batch, nb),
        in_specs=[pl.BlockSpec((WINDOW, D_MODEL), lambda b, n: (b * nb + n, 0)),
                  prev(0), cur(0), prev(1), cur(1),
                  pl.BlockSpec(memory_space=pltpu.SMEM)],
        out_specs=pl.BlockSpec((WINDOW, D_MODEL), lambda b, n: (b * nb + n, 0)),
        out_shape=jax.ShapeDtypeStruct((n_batch * seq_len, D_MODEL), bf16),
        compiler_params=_cparams(2),
        name="swa_prompt",
    )(qkv, qkv, qkv, qkv, qkv, sink)


SEQ_PER_GROUP = 4
GROUPS_PER_STEP = 2


def _attn_cached_kernel(n_tok, groups, q_ref, kn_ref, vn_ref, kc_ref, vc_ref, sink_ref, o_ref):
    rows = SEQ_PER_GROUP * n_tok
    n_st = GQA_GROUP * rows
    m0, m1 = _head_masks(rows)
    srow = lax.broadcasted_iota(jnp.int32, (n_st, 1), 0)
    row_seq = (srow % rows) // n_tok
    row_tok = srow % n_tok
    key_c = lax.broadcasted_iota(jnp.int32, (n_st, WINDOW), 1)
    vis_c = key_c > row_tok
    key_n = lax.broadcasted_iota(jnp.int32, (n_st, rows), 1)
    vis_n = ((key_n // n_tok) == row_seq) & ((key_n % n_tok) <= row_tok)
    for gi in range(groups):
        rs = slice(gi * rows, (gi + 1) * rows)
        for cp in range(N_KV_HEADS // 2):
            for ce in range(2):
                c = 2 * cp + ce
                sl = slice(cp * LANES, (cp + 1) * LANES)
                kt = [kc_ref[gi * SEQ_PER_GROUP + b, c * HEAD_DIM:(c + 1) * HEAD_DIM, :] for b in range(SEQ_PER_GROUP)]
                vt = [vc_ref[gi * SEQ_PER_GROUP + b, c * HEAD_DIM:(c + 1) * HEAD_DIM, :] for b in range(SEQ_PER_GROUP)]
                kn = _both_halves(kn_ref[rs, sl])[ce].astype(bf16)
                vn = _both_halves(vn_ref[rs, sl])[ce].astype(bf16)
                pieces = []
                for jj in range(GQA_GROUP // 2):
                    pair = c * (GQA_GROUP // 2) + jj
                    qp = q_ref[rs, pair * LANES:(pair + 1) * LANES]
                    pieces += [jnp.where(m0, qp, 0.0), jnp.where(m1, qp, 0.0)]
                qs = jnp.concatenate(pieces, axis=0).astype(bf16)
                s_c = jnp.zeros((n_st, WINDOW), f32)
                for b in range(SEQ_PER_GROUP):
                    kx = jnp.concatenate([kt[b], kt[b]], axis=0).astype(bf16)
                    s_c = jnp.where(row_seq == b, _dot(qs, kx), s_c)
                s_c = jnp.where(vis_c, s_c * ATTN_SCALE, NEG_BIG)
                s_n = jnp.where(vis_n, _dot_nt(qs, kn) * ATTN_SCALE, NEG_BIG)
                sk = sink_ref[c]
                sk = sk[:, 0:1]
                mx = jnp.maximum(jnp.maximum(jnp.max(s_c, axis=1, keepdims=True),
                                             jnp.max(s_n, axis=1, keepdims=True)), sk)
                p_c = jnp.exp(s_c - mx)
                p_n = jnp.exp(s_n - mx)
                den = (jnp.sum(p_c, axis=1, keepdims=True) + jnp.sum(p_n, axis=1, keepdims=True)
                       + jnp.exp(sk - mx))
                inv_den = 1.0 / den
                p_c = (p_c * inv_den).astype(bf16)
                p_n = (p_n * inv_den).astype(bf16)
                o = _dot(p_n, vn)
                for b in range(SEQ_PER_GROUP):
                    vx = jnp.concatenate([vt[b], vt[b]], axis=0).astype(bf16)
                    o = o + jnp.where(row_seq == b, _dot_nt(p_c, vx), 0.0)
                for jj in range(GQA_GROUP // 2):
                    pair = c * (GQA_GROUP // 2) + jj
                    o0 = o[(2 * jj) * rows:(2 * jj + 1) * rows]
                    o1 = o[(2 * jj + 1) * rows:(2 * jj + 2) * rows]
                    o_ref[rs, pair * LANES:(pair + 1) * LANES] = jnp.where(m0, o0, o1).astype(bf16)


def _attn_cached(qkv, k_cache, v_cache, sink, n_batch, n_tok):
    groups = GROUPS_PER_STEP
    seqs = SEQ_PER_GROUP * groups
    rows = seqs * n_tok
    k_blk = D_MODEL // KV_COLS
    n_st = GQA_GROUP * SEQ_PER_GROUP * n_tok
    head = (jnp.arange(N_KV_HEADS)[:, None] * GQA_GROUP
            + (jnp.arange(n_st)[None, :] // (SEQ_PER_GROUP * n_tok)))
    sink_tab = jnp.broadcast_to(sink[head][:, :, None], (N_KV_HEADS, n_st, LANES))
    cache_spec = pl.BlockSpec((seqs, N_KV_HEADS * HEAD_DIM, WINDOW), lambda i: (i, 0, 0))
    return pl.pallas_call(
        functools.partial(_attn_cached_kernel, n_tok, groups),
        grid=(n_batch // seqs,),
        in_specs=[pl.BlockSpec((rows, D_MODEL), lambda i: (i, 0)),
                  pl.BlockSpec((rows, KV_COLS), lambda i: (i, k_blk)),
                  pl.BlockSpec((rows, KV_COLS), lambda i: (i, k_blk + 1)),
                  cache_spec, cache_spec,
                  pl.BlockSpec((N_KV_HEADS, n_st, LANES), lambda i: (0, 0, 0))],
        out_specs=pl.BlockSpec((rows, D_MODEL), lambda i: (i, 0)),
        out_shape=jax.ShapeDtypeStruct((n_batch * n_tok, D_MODEL), bf16),
        compiler_params=_cparams(1),
        name="swa_cached",
    )(qkv, qkv, qkv, k_cache, v_cache, sink_tab)


def _trunk(x, mod_all, tok_major, n_batch, seq_len, pos0, wkv_in, shift_in, k_in, v_in, wts):
    to_seq_major = lambda t: t.reshape(seq_len, n_batch, -1).transpose(1, 0, 2).reshape(n_batch * seq_len, -1)
    to_tok_major = lambda t: t.reshape(n_batch, seq_len, -1).transpose(1, 0, 2).reshape(n_batch * seq_len, -1)
    mod = mod_all[0]
    ffn_w = dict(wts["ffn"])
    x, ffn_w[0, 0] = _ffn(x, mod, tok_major, seq_len, 0, wts["norm_g"][0, 0], ffn_w[0, 0])
    s0 = shift_in[0] if tok_major else jnp.zeros((n_batch, 1, D_MODEL), f32)
    *proj, h_tail = _rwkv_proj(x, mod, tok_major, seq_len, wts["norm_g"][0, 1], s0, wts["rw"])
    if tok_major:
        z, wkv_t = _scan_lanes(proj, wts["rw"], n_batch, seq_len, jnp.transpose(wkv_in[0], (1, 2, 3, 0)))
        wkv_out = jnp.transpose(wkv_t, (3, 0, 1, 2))
        shift_out = h_tail
    else:
        z, wkv_out = _scan(proj, wts["rw"], n_batch, seq_len)
        shift_out = h_tail.reshape(n_batch, -1, SUBLANES, D_MODEL)[:, -1, -1]
    x = _proj_res(z, wts["rw"]["wo"], jnp.zeros((D_MODEL,), f32), x, mod, tok_major, seq_len)
    x, ffn_w[0, 1] = _ffn(x, mod, tok_major, seq_len, 6, wts["norm_g"][0, 2], ffn_w[0, 1])
    mod = mod_all[1]
    x, ffn_w[1, 0] = _ffn(x, mod, tok_major, seq_len, 0, wts["norm_g"][1, 0], ffn_w[1, 0])
    positions = pos0 + jnp.arange(seq_len)
    if tok_major:
        positions = jnp.repeat(positions, n_batch)
    cos, sin = _rope_tables(positions)
    qkv = _qkv(x, mod, tok_major, seq_len, wts["norm_g"][1, 1], wts["sw_wqkv"], wts["sw_bqkv"], cos, sin)
    heads = lambda t: t.reshape(n_batch, -1, N_KV_HEADS, HEAD_DIM)
    if k_in is None:
        att = _attn_prompt(qkv, wts["sw_sink"], n_batch, seq_len)
        tail = qkv.reshape(n_batch, seq_len, QKV_COLS)[:, -WINDOW:, D_MODEL:]
        k_new, v_new = heads(tail[..., :KV_COLS]), heads(tail[..., KV_COLS:])
    else:
        win = k_in.shape[2]
        qkv = to_seq_major(qkv)
        cache_t = lambda t: jnp.transpose(t[0], (0, 2, 3, 1)).reshape(n_batch, N_KV_HEADS * HEAD_DIM, win)
        att = _attn_cached(qkv, cache_t(k_in), cache_t(v_in), wts["sw_sink"], n_batch, seq_len)
        att = to_tok_major(att)
        k_tok, v_tok = heads(qkv[:, D_MODEL:D_MODEL + KV_COLS]), heads(qkv[:, D_MODEL + KV_COLS:])
        k_new = jnp.concatenate([k_in[0], k_tok], axis=1)[:, -win:]
        v_new = jnp.concatenate([v_in[0], v_tok], axis=1)[:, -win:]
    x = _proj_res(att, wts["sw_wo"], wts["sw_bo"], x, mod, tok_major, seq_len)
    y, ffn_w[1, 1] = _ffn(x, mod, tok_major, seq_len, 6, wts["norm_g"][1, 2], ffn_w[1, 1], final_g=wts["final_g"])
    return (y, wkv_out[None], shift_out[None], k_new[None], v_new[None]), ffn_w


def _pad_cols(w):
    return jnp.pad(w, ((0, 0), (0, LORA_PAD - w.shape[1])))


def _pad_rows(w):
    return jnp.pad(w, ((0, LORA_PAD - w.shape[0]), (0, 0)))


def kernel(x_prompt, x_sample, state_rwkv_wkv, state_rwkv_shift, cache_swa_k, cache_swa_v, c_prompt, c_sample, norm_g, w_ada, b_ada, w_ffn_in, w_ffn_out, rw_mu, rw_wrkv, rw_w0, rw_w1, rw_w2, rw_a0, rw_a1, rw_a2, rw_g1, rw_g2, rw_kk, rw_ka, rw_rk, rw_gn_w, rw_gn_b, rw_wo, sw_wqkv, sw_bqkv, sw_sink, sw_wo, sw_bo, final_g):
    n_p, seq_p, _ = x_prompt.shape
    n_s, seq_s, _ = x_sample.shape
    row = lambda t: t.reshape(1, D_MODEL)
    rw = dict(
        mu=rw_mu[0], wrkv=rw_wrkv[0].astype(bf16),
        w1=_pad_cols(rw_w1[0]).astype(bf16), a1=_pad_cols(rw_a1[0]).astype(bf16), g1=rw_g1[0].astype(bf16),
        w2=_pad_rows(rw_w2[0]).astype(bf16), a2=_pad_rows(rw_a2[0]).astype(bf16), g2=rw_g2[0].astype(bf16),
        w0=row(rw_w0[0]), a0=row(rw_a0[0]), kk=row(rw_kk[0]), ka=row(rw_ka[0]), rk=row(rw_rk[0]),
        gn_w=row(rw_gn_w[0]), gn_b=row(rw_gn_b[0]), wo=rw_wo[0].astype(bf16))
    wts = dict(
        norm_g=norm_g, final_g=final_g,
        ffn={(l, s): (w_ffn_in, w_ffn_out, l, s) for l in range(2) for s in range(2)}, rw=rw,
        sw_wqkv=sw_wqkv[0].astype(bf16), sw_bqkv=sw_bqkv[0],
        sw_sink=sw_sink[0], sw_wo=sw_wo[0].astype(bf16), sw_bo=sw_bo[0])

    n_c = n_p + n_s
    pad = (-n_c) % SUBLANES
    c_all = jnp.concatenate([c_sample, c_prompt, jnp.zeros((pad, D_MODEL), f32)], axis=0)
    mod_s = _ada(c_all, w_ada, b_ada)
    mod_p = [m[n_s:n_c].reshape(n_p, 1, N_MOD * D_MODEL) for m in mod_s]

    (y_s, s_wkv, s_shift, s_k, s_v), wts["ffn"] = _trunk(
        jnp.transpose(x_sample, (1, 0, 2)).reshape(n_s * seq_s, D_MODEL), mod_s, True, n_s, seq_s, PAST_LEN,
        state_rwkv_wkv, state_rwkv_shift, cache_swa_k, cache_swa_v, wts)
    (y_p, p_wkv, p_shift, p_k, p_v), _ = _trunk(
        x_prompt.reshape(n_p * seq_p, D_MODEL), mod_p, False, n_p, seq_p, 0, None, None, None, None, wts)
    y_s = jnp.transpose(y_s.reshape(seq_s, n_s, D_MODEL), (1, 0, 2))
    return (y_p.reshape(n_p, seq_p, D_MODEL), y_s, p_wkv, p_shift, p_k, p_v, s_wkv, s_shift, s_k, s_v)
```

```python
import functools
import math

import jax
import jax.numpy as jnp
from jax import lax
from jax.experimental import pallas as pl
from jax.experimental.pallas import tpu as pltpu

f32 = jnp.float32
bf16 = jnp.bfloat16

D_MODEL = 2048
HEAD_DIM = 64
N_HEADS = D_MODEL // HEAD_DIM
N_KV_HEADS = 4
GQA_GROUP = N_HEADS // N_KV_HEADS
WINDOW = 128
ATTN_SCALE = HEAD_DIM ** -0.5
ROPE_THETA = 10000.0
D_FF = 5632
N_MOD = 9
RMS_EPS = 1e-6
GN_EPS = 64e-5
PAST_LEN = 8192
LANES = 128
SUBLANES = 8
MXU_COLS = 256
N_PAIRS = D_MODEL // LANES
LORA_PAD = 128
GATE_LORA = 256
SLAB = 64
SCAN_CHAINS = 16
VMEM_LIMIT = 56 * 1024 * 1024
NEG_BIG = -1e30
ROW_TILE = 512
FFN_COLS = 512
PROJ_COLS = 256
QKV_TILE = 512
ADA_COLS = 1024


def _cparams(n_axes):
    return pltpu.CompilerParams(dimension_semantics=("arbitrary",) * n_axes,
                                vmem_limit_bytes=VMEM_LIMIT)


def _dot(a, b):
    return jnp.dot(a, b, preferred_element_type=f32)


def _dot_nt(a, b):
    return lax.dot_general(a, b, (((1,), (1,)), ((), ())), preferred_element_type=f32)


def _split(x, n):
    if x.dtype == bf16:
        return [x]
    parts = []
    rem = x
    for i in range(n):
        p = rem.astype(bf16)
        parts.append(p)
        if i + 1 < n:
            rem = rem - p.astype(f32)
    return parts


def _mm(a, b, pa=1, pb=1, nt=False):
    a_parts = _split(a, pa)
    b_parts = _split(b, pb)
    order = max(len(a_parts), len(b_parts))
    acc = None
    for i, x in enumerate(a_parts):
        for j, y in enumerate(b_parts):
            if i + j >= order:
                continue
            t = _dot_nt(x, y) if nt else _dot(x, y)
            acc = t if acc is None else acc + t
    return acc


def _rms_mod(x, g, shift, scale):
    return x * lax.rsqrt(jnp.mean(x * x, axis=-1, keepdims=True) + RMS_EPS) * (g * (1.0 + scale)) + shift


def _mod_spec(tok_major, tm, seq_len, idx):
    if tok_major:
        return pl.BlockSpec((tm // seq_len, D_MODEL), lambda i, *_: (0, idx))
    return pl.BlockSpec((None, 1, D_MODEL), lambda i, *_: ((i * tm) // seq_len, 0, idx))


def _rows(v, tm):
    n = v.shape[0]
    return v if n in (1, tm) else jnp.concatenate([v] * (tm // n), axis=0)


def _ada_kernel(c_ref, w_ref, b_ref, *o_refs):
    c = c_ref[...]
    s = (c * jax.nn.sigmoid(c)).astype(bf16)
    for layer, o_ref in enumerate(o_refs):
        o_ref[...] = _dot(s, w_ref[layer].astype(bf16)) + b_ref[layer]


def _ada(c_all, w_ada, b_ada):
    n_layers, _, n_out = w_ada.shape
    rows = c_all.shape[0]
    tn = ADA_COLS
    return pl.pallas_call(
        _ada_kernel,
        grid=(n_out // tn,),
        in_specs=[pl.BlockSpec((rows, D_MODEL), lambda j: (0, 0)),
                  pl.BlockSpec((n_layers, D_MODEL, tn), lambda j: (0, 0, j)),
                  pl.BlockSpec((n_layers, 1, tn), lambda j: (0, 0, j))],
        out_specs=[pl.BlockSpec((rows, tn), lambda j: (0, j))] * n_layers,
        out_shape=[jax.ShapeDtypeStruct((rows, n_out), f32)] * n_layers,
        compiler_params=_cparams(1),
        name="ada_mod",
    )(c_all, w_ada, b_ada.reshape(n_layers, 1, n_out))


def _ffn_kernel(final_norm, emit_bf16, tiled_gate, x_ref, ng_ref, sh_ref, sc_ref, gt_ref, wg_ref, wu_ref, wo_ref,
                fg_ref, o_ref, *rest):
    rest = list(rest)
    if emit_bf16:
        wg_out, wu_out, wo_out = rest[:3]
        del rest[:3]
    h_scr = rest.pop(0)
    gt_scr = rest.pop(0) if tiled_gate else None
    f = pl.program_id(1)
    tm = x_ref.shape[0]

    @pl.when(f == 0)
    def _():
        h_scr[...] = _rms_mod(x_ref[...], ng_ref[...], _rows(sh_ref[...], tm), _rows(sc_ref[...], tm)).astype(bf16)
        o_ref[...] = x_ref[...]
        if tiled_gate:
            gt_scr[...] = 0.5 * _rows(gt_ref[...], tm)

    wg, wu, wo = wg_ref[...], wu_ref[...], wo_ref[...]
    if emit_bf16:
        wg, wu, wo = wg.astype(bf16), wu.astype(bf16), wo.astype(bf16)
        wg_out[...] = wg
        wu_out[...] = wu
        wo_out[...] = wo
    h = h_scr[...]
    gate = _dot(h, wg)
    up = _dot(h, wu)
    act = (gate * jax.nn.sigmoid(gate) * up).astype(bf16)
    half_gate = gt_scr[...] if tiled_gate else 0.5 * gt_ref[...]
    o_ref[...] += half_gate * _dot(act, wo)

    if final_norm:
        @pl.when(f == pl.num_programs(1) - 1)
        def _():
            y = o_ref[...]
            o_ref[...] = y * lax.rsqrt(jnp.mean(y * y, axis=-1, keepdims=True) + RMS_EPS) * fg_ref[...]


def _ffn(x, mod, tok_major, seq_len, mod_base, norm_g, weights, final_g=None):
    m = x.shape[0]
    tm = min(ROW_TILE, m)
    emit = len(weights) == 4
    tf = FFN_COLS // 2 if emit else FFN_COLS
    nf = D_FF // tf
    vec = pl.BlockSpec((1, D_MODEL), lambda i, j: (0, 0))
    fg = jnp.ones((1, D_MODEL), f32) if final_g is None else final_g.reshape(1, D_MODEL)
    w_in_spec = pl.BlockSpec((D_MODEL, tf), lambda i, j: (0, j))
    w_out_spec = pl.BlockSpec((tf, D_MODEL), lambda i, j: (j, 0))
    out_specs = [pl.BlockSpec((tm, D_MODEL), lambda i, j: (i, 0))]
    out_shape = [jax.ShapeDtypeStruct((m, D_MODEL), f32)]
    if emit:
        assert m == tm, "the weight copies are written once, by a single row tile"
        w_ffn_in, w_ffn_out, layer, slot = weights
        w_args = (w_ffn_in, w_ffn_in, w_ffn_out)
        w_specs = [pl.BlockSpec((None, None, D_MODEL, tf), lambda i, j: (layer, slot, 0, j)),
                   pl.BlockSpec((None, None, D_MODEL, tf), lambda i, j: (layer, slot, 0, j + nf)),
                   pl.BlockSpec((None, None, tf, D_MODEL), lambda i, j: (layer, slot, j, 0))]
        out_specs += [w_in_spec, w_in_spec, w_out_spec]
        out_shape += [jax.ShapeDtypeStruct((D_MODEL, D_FF), bf16), jax.ShapeDtypeStruct((D_MODEL, D_FF), bf16),
                      jax.ShapeDtypeStruct((D_FF, D_MODEL), bf16)]
    else:
        w_args = weights
        w_specs = [w_in_spec, w_in_spec, w_out_spec]
    outs = pl.pallas_call(
        functools.partial(_ffn_kernel, final_g is not None, emit, tok_major),
        grid=(m // tm, nf),
        in_specs=[pl.BlockSpec((tm, D_MODEL), lambda i, j: (i, 0)),
                  vec,
                  _mod_spec(tok_major, tm, seq_len, mod_base),
                  _mod_spec(tok_major, tm, seq_len, mod_base + 1),
                  _mod_spec(tok_major, tm, seq_len, mod_base + 2)] + w_specs + [vec],
        out_specs=out_specs,
        out_shape=out_shape,
        scratch_shapes=[pltpu.VMEM((tm, D_MODEL), bf16)] + ([pltpu.VMEM((tm, D_MODEL), f32)] if tok_major else []),
        compiler_params=_cparams(2),
        name="ffn",
    )(x, norm_g.reshape(1, D_MODEL), mod, mod, mod, *w_args, fg)
    return (outs[0], tuple(outs[1:])) if emit else (outs[0], weights)


def _rwkv_proj_kernel(seq_len, tm, h_rows, tok_major,
                      x_ref, ng_ref, sh_ref, sc_ref, s0_ref, mu_ref, w1_ref, a1_ref, g1_ref,
                      w_ref, w2_ref, a2_ref, g2_ref, w0_ref, a0_ref,
                      r_ref, ld_ref, k_ref, v_ref, a_ref, g_ref, h_ref,
                      hs, xr, xk, xv, tw, ta, sg):
    i = pl.program_id(0)
    j = pl.program_id(1)

    @pl.when(j == 0)
    def _():
        h = _rms_mod(x_ref[...], ng_ref[...], _rows(sh_ref[...], tm), _rows(sc_ref[...], tm))
        h_ref[...] = h[tm - h_rows:tm, :]
        if tok_major:
            n_seq = s0_ref.shape[0]
            prev = jnp.concatenate([s0_ref[...], h[0:tm - n_seq, :]], axis=0)
        else:
            @pl.when(i == 0)
            def _():
                hs[0:SUBLANES, :] = jnp.zeros((SUBLANES, D_MODEL), f32)

            @pl.when(i > 0)
            def _():
                hs[0:SUBLANES, :] = hs[tm:tm + SUBLANES, :]

            hs[SUBLANES:tm + SUBLANES, :] = h
            row = i * tm + lax.broadcasted_iota(jnp.int32, (tm, 1), 0)
            prev = jnp.where(row % seq_len == 0, s0_ref[...], hs[SUBLANES - 1:tm + SUBLANES - 1, :])
        xx = prev - h
        mu = mu_ref[...]
        xr[...] = (h + xx * mu[0:1, :]).astype(bf16)
        xk[...] = (h + xx * mu[2:3, :]).astype(bf16)
        xv[...] = (h + xx * mu[3:4, :]).astype(bf16)
        xw = (h + xx * mu[1:2, :]).astype(bf16)
        tw[...] = jnp.tanh(_dot(xw, w1_ref[...])).astype(bf16)
        xa = (h + xx * mu[4:5, :]).astype(bf16)
        ta[...] = _dot(xa, a1_ref[...]).astype(bf16)
        xg = (h + xx * mu[5:6, :]).astype(bf16)
        sg[...] = jax.nn.sigmoid(_dot(xg, g1_ref[...])).astype(bf16)

    cols = pl.ds(pl.multiple_of(j * r_ref.shape[1], r_ref.shape[1]), r_ref.shape[1])
    r_ref[...] = _dot(xr[...], w_ref[0, :, cols])
    k_ref[...] = _dot(xk[...], w_ref[1, :, cols])
    v_ref[...] = _dot(xv[...], w_ref[2, :, cols])
    z = w0_ref[...] + _dot(tw[...], w2_ref[...])
    ld_ref[...] = -jax.nn.sigmoid(z) * math.exp(-0.5)
    a_ref[...] = jax.nn.sigmoid(a0_ref[...] + _dot(ta[...], a2_ref[...]))
    g_ref[...] = _dot(sg[...], g2_ref[...])


def _rwkv_proj(x, mod, tok_major, seq_len, norm_g, s0, rwp):
    m = x.shape[0]
    tm = min(ROW_TILE, m)
    tn = PROJ_COLS
    h_rows = tm // seq_len if tok_major else SUBLANES
    full = lambda shape: pl.BlockSpec(shape, lambda i, j: (0,) * len(shape))
    col = lambda rows: pl.BlockSpec((rows, tn), lambda i, j: (0, j))
    w_res = pl.BlockSpec((3, D_MODEL, D_MODEL), lambda i, j: (0, 0, 0), pipeline_mode=pl.Buffered(1))
    if tok_major:
        assert m == tm
        s0_spec = pl.BlockSpec((tm // seq_len, D_MODEL), lambda i, j: (0, 0))
    else:
        s0_spec = pl.BlockSpec((None, 1, D_MODEL), lambda i, j: ((i * tm) // seq_len, 0, 0))
    out_spec = pl.BlockSpec((tm, tn), lambda i, j: (i, j))
    out_sds = jax.ShapeDtypeStruct((m, D_MODEL), f32)
    outs = pl.pallas_call(
        functools.partial(_rwkv_proj_kernel, seq_len, tm, h_rows, tok_major),
        grid=(m // tm, D_MODEL // tn),
        in_specs=[pl.BlockSpec((tm, D_MODEL), lambda i, j: (i, 0)),
                  full((1, D_MODEL)),
                  _mod_spec(tok_major, tm, seq_len, 3),
                  _mod_spec(tok_major, tm, seq_len, 4),
                  s0_spec,
                  full((6, D_MODEL)),
                  full((D_MODEL, LORA_PAD)), full((D_MODEL, LORA_PAD)), full((D_MODEL, GATE_LORA)),
                  w_res,
                  col(LORA_PAD), col(LORA_PAD), col(GATE_LORA), col(1), col(1)],
        out_specs=[out_spec] * 6 + [pl.BlockSpec((h_rows, D_MODEL), lambda i, j: (i, 0))],
        out_shape=[out_sds] * 6 + [jax.ShapeDtypeStruct((m // tm * h_rows, D_MODEL), f32)],
        scratch_shapes=[pltpu.VMEM((tm + SUBLANES, D_MODEL), f32),
                        pltpu.VMEM((tm, D_MODEL), bf16), pltpu.VMEM((tm, D_MODEL), bf16),
                        pltpu.VMEM((tm, D_MODEL), bf16),
                        pltpu.VMEM((tm, LORA_PAD), bf16), pltpu.VMEM((tm, LORA_PAD), bf16),
                        pltpu.VMEM((tm, GATE_LORA), bf16)],
        compiler_params=_cparams(2),
        name="rwkv_proj",
    )(x, norm_g.reshape(1, D_MODEL), mod, mod, s0, rwp["mu"], rwp["w1"], rwp["a1"], rwp["g1"],
      rwp["wrkv"], rwp["w2"], rwp["a2"], rwp["g2"], rwp["w0"], rwp["a0"])
    return outs


def _scan_chains(acts, params, states):
    n_st = 2 * SLAB
    row = lax.broadcasted_iota(jnp.int32, (n_st, LANES), 0)
    lane = lax.broadcasted_iota(jnp.int32, (n_st, LANES), 1)
    own = (row // SLAB) == (lane // HEAD_DIM)
    blk = (row // SLAB) == (lane // SLAB)
    strict = blk & (lane < row)
    incl = blk & (lane <= row)
    eye = jnp.where(row == lane, 1.0, 0.0)
    twice = lambda t: jnp.concatenate([t, t], axis=0)
    st = lambda t: jnp.where(own, twice(t), 0.0)
    each = lambda fn, *lists: [fn(*args) for args in zip(*lists)]

    r, ld, k, v, a, g = [[act[i] for act in acts] for i in range(6)]
    kkp, kap, rkp, gnw, gnb = [[par[i] for par in params] for i in range(5)]

    r64 = lax.broadcasted_iota(jnp.int32, (SLAB, SLAB), 0)
    c64 = lax.broadcasted_iota(jnp.int32, (SLAB, SLAB), 1)
    tri_ones = jnp.concatenate([jnp.where(c64 <= r64, 1.0, 0.0),
                                jnp.ones((SLAB, SLAB), f32)], axis=0).astype(bf16)
    sums = each(lambda t: _mm(tri_ones, t, pb=3), ld)
    cs = each(lambda t: t[0:SLAB], sums)
    tot = each(lambda t: t[SLAB:n_st], sums)

    kk_raw = each(lambda t, p: st(t * p), k, kkp)
    kk = each(lambda t: t * (1.0 / jnp.maximum(jnp.sqrt(jnp.sum(t * t, axis=1, keepdims=True)), 1e-12)), kk_raw)
    k2 = each(lambda kt, at, p: st(kt * (1.0 + (at - 1.0) * p)), k, a, kap)
    b = each(lambda t, at: t * twice(at), kk, a)
    r_s = each(st, r)
    v_s = each(st, v)
    bonus = each(lambda rt, kt, p, vt: jnp.sum(rt * kt * p, axis=1, keepdims=True) * vt, r_s, k2, rkp, v_s)

    e_neg = each(lambda c_: twice(jnp.exp(-c_)), cs)
    e_tail = each(lambda t_, c_: twice(jnp.exp(t_ - c_)), tot, cs)
    a_t = each(lambda t, c_, l_: -t * twice(jnp.exp(c_ - l_)), kk, cs, ld)
    r_t = each(lambda t, c_: t * twice(jnp.exp(c_)), r_s, cs)
    k_t = each(lambda t, e: t * e, k2, e_neg)
    b_t = each(lambda t, e: t * e, b, e_neg)
    k_h = each(lambda t, e: t * e, k2, e_tail)
    b_h = each(lambda t, e: t * e, b, e_tail)

    lhs1 = each(lambda x_, y_: jnp.concatenate([x_, y_], axis=0), a_t, r_t)
    rhs1 = each(lambda x_, y_: jnp.concatenate([x_, y_], axis=0), k_t, b_t)
    x = each(lambda l_, r_: _mm(l_, r_, nt=True), lhs1, rhs1)
    a_k = each(lambda t: jnp.where(strict, t[0:n_st, 0:n_st], 0.0), x)
    a_b = each(lambda t: jnp.where(strict, t[0:n_st, n_st:], 0.0), x)
    r_kb = each(lambda t: jnp.concatenate([jnp.where(incl, t[n_st:, 0:n_st], 0.0),
                                           jnp.where(incl, t[n_st:, n_st:], 0.0)], axis=1), x)

    p1 = each(lambda l_, s_: _mm(l_, s_, nt=True), lhs1, states)
    rhs_u = each(lambda p_, ak, vt: p_[0:n_st] + _mm(ak, vt), p1, a_k, v_s)

    near = (row // 2) == (lane // 2)
    t_inv = each(lambda t: eye + jnp.where(near, t, 0.0), a_b)
    s = 2
    while s < SLAB:
        off = ((row // (2 * s)) == (lane // (2 * s))) & ((row // s) != (lane // s))
        w = each(lambda ab, ti: _mm(jnp.where(off, ab, 0.0), ti), a_b, t_inv)
        t_inv = each(lambda ti, w_: ti + _mm(ti, w_), t_inv, w)
        s *= 2
    u = each(_mm, t_inv, rhs_u)

    vu = each(lambda x_, y_: jnp.concatenate([x_, y_], axis=0), v_s, u)
    y = each(lambda p_, rk_, vu_: p_[n_st:] + _mm(rk_, vu_), p1, r_kb, vu)

    def finish(y_, w_, b_, bonus_, g_):
        mean = jnp.sum(y_, axis=1, keepdims=True) * (1.0 / HEAD_DIM)
        dev = jnp.where(own, y_ - mean, 0.0)
        var = jnp.sum(dev * dev, axis=1, keepdims=True) * (1.0 / HEAD_DIM)
        out = jnp.where(own, dev * lax.rsqrt(var + GN_EPS) * w_ + b_, 0.0) + bonus_
        return (out[0:SLAB] + out[SLAB:n_st]) * g_
    z = each(finish, y, gnw, gnb, bonus, g)

    vu_t = each(lambda t: t.T, vu)
    kb_h = each(lambda x_, y_: jnp.concatenate([x_, y_], axis=0), k_h, b_h)
    new_states = each(lambda s_, t_, vt, kb: s_ * jnp.exp(t_[0:1, :]) + _mm(vt, kb), states, tot, vu_t, kb_h)
    return z, new_states


def _scan_kernel(n_seq, n_pp, r_ref, ld_ref, k_ref, v_ref, a_ref, g_ref, kk_ref, ka_ref, rk_ref, gw_ref,
                 gb_ref, z_ref, s_out_ref, s_scr):
    c = pl.program_id(1)

    @pl.when(c == 0)
    def _():
        s_scr[...] = jnp.zeros_like(s_scr)

    chains = [(b, pp) for b in range(n_seq) for pp in range(n_pp)]
    lanes = [slice(pp * LANES, (pp + 1) * LANES) for _, pp in chains]
    acts = [tuple(ref[b, :, sl] for ref in (r_ref, ld_ref, k_ref, v_ref, a_ref, g_ref))
            for (b, _), sl in zip(chains, lanes)]
    params = [tuple(ref[:, sl] for ref in (kk_ref, ka_ref, rk_ref, gw_ref, gb_ref)) for sl in lanes]
    states = [s_scr[n] for n in range(len(chains))]
    z, new_states = _scan_chains(acts, params, states)
    for n, (b, _) in enumerate(chains):
        z_ref[b, :, lanes[n]] = z[n].astype(bf16)
        s_scr[n] = new_states[n]

    @pl.when(c == pl.num_programs(1) - 1)
    def _():
        for n, (b, pp) in enumerate(chains):
            s = s_scr[n]
            s_out_ref[b, 2 * pp] = s[0:HEAD_DIM, 0:HEAD_DIM]
            s_out_ref[b, 2 * pp + 1] = s[HEAD_DIM:, HEAD_DIM:]


def _scan(proj, rwp, n_batch, seq_len):
    n_pp = SCAN_CHAINS // n_batch
    w = LANES * n_pp
    act = pl.BlockSpec((n_batch, SLAB, w), lambda p, c: (0, c, p))
    vec = pl.BlockSpec((1, w), lambda p, c: (0, p))
    st_spec = pl.BlockSpec((n_batch, 2 * n_pp, HEAD_DIM, HEAD_DIM), lambda p, c: (0, p, 0, 0))
    z, s_out = pl.pallas_call(
        functools.partial(_scan_kernel, n_batch, n_pp),
        grid=(N_PAIRS // n_pp, seq_len // SLAB),
        in_specs=[act] * 6 + [vec] * 5,
        out_specs=[act, st_spec],
        out_shape=[jax.ShapeDtypeStruct((n_batch, seq_len, D_MODEL), bf16),
                   jax.ShapeDtypeStruct((n_batch, N_HEADS, HEAD_DIM, HEAD_DIM), f32)],
        scratch_shapes=[pltpu.VMEM((n_batch * n_pp, 2 * SLAB, LANES), f32)],
        compiler_params=_cparams(2),
        name="rwkv_scan",
    )(*[p.reshape(n_batch, seq_len, D_MODEL) for p in proj],
      rwp["kk"], rwp["ka"], rwp["rk"], rwp["gn_w"], rwp["gn_b"])
    return z.reshape(n_batch * seq_len, D_MODEL), s_out


def _scan_lanes_kernel(n_tok, r_ref, ld_ref, k_ref, v_ref, a_ref, g_ref, kk_ref, ka_ref, rk_ref, gw_ref, gb_ref,
                       s_in_ref, z_ref, s_out_ref, kk_s, w_s, b_s, k2_s, r_s, v_s, y_s):
    hd = HEAD_DIM
    colsum = lambda t: jnp.sum(t, axis=0, keepdims=True)
    for t in range(n_tok):
        r_t, k_t, v_t, a_t = r_ref[t].T, k_ref[t].T, v_ref[t].T, a_ref[t].T
        kk_raw = k_t * kk_ref[...]
        k2 = k_t * (1.0 + (a_t - 1.0) * ka_ref[...])
        w = jnp.exp(ld_ref[t].T)
        for e in range(2):
            sl = slice(e * hd, (e + 1) * hd)
            kr = kk_raw[sl]
            kk = kr * (1.0 / jnp.maximum(jnp.sqrt(colsum(kr * kr)), 1e-12))
            kk_s[e, t] = kk
            b_s[e, t] = kk * a_t[sl]
            k2_s[e, t] = k2[sl]
            w_s[e, t] = w[sl]
            r_s[e, t] = r_t[sl]
            v_s[e, t] = v_t[sl]

    rows_per_iter = 2

    def per_rows(i, carry):
        chains = [(e, rows_per_iter * i + d) for e in range(2) for d in range(rows_per_iter)]
        s = [s_in_ref[e, vi] for e, vi in chains]
        for t in range(n_tok):
            for c, (e, vi) in enumerate(chains):
                s_kk = colsum(s[c] * kk_s[e, t])
                s[c] = s[c] * w_s[e, t] - s_kk * b_s[e, t] + v_s[e, t, pl.ds(vi, 1), :] * k2_s[e, t]
                y_s[e, t, pl.ds(vi, 1), :] = colsum(s[c] * r_s[e, t])
        for c, (e, vi) in enumerate(chains):
            s_out_ref[e, vi] = s[c]
        return carry
    lax.fori_loop(0, hd // rows_per_iter, per_rows, 0)

    for t in range(n_tok):
        outs = []
        for e in range(2):
            sl = slice(e * hd, (e + 1) * hd)
            y = y_s[e, t]
            dev = y - colsum(y) * (1.0 / hd)
            var = colsum(dev * dev) * (1.0 / hd)
            bonus = colsum(r_s[e, t] * k2_s[e, t] * rk_ref[sl, :]) * v_s[e, t]
            outs.append(dev * lax.rsqrt(var + GN_EPS) * gw_ref[sl, :] + gb_ref[sl, :] + bonus)
        z_ref[t] = (jnp.concatenate(outs, axis=0).T * g_ref[t]).astype(bf16)


def _scan_lanes(proj, rwp, n_batch, n_tok, state):
    acts = [p.reshape(n_tok, n_batch, D_MODEL) for p in proj]
    lane_bc = lambda v: jnp.broadcast_to(v.reshape(D_MODEL, 1), (D_MODEL, n_batch))
    params = [lane_bc(rwp[n]) for n in ("kk", "ka", "rk", "gn_w", "gn_b")]
    act = pl.BlockSpec((n_tok, n_batch, LANES), lambda p: (0, 0, p))
    par = pl.BlockSpec((LANES, n_batch), lambda p: (p, 0))
    st_spec = pl.BlockSpec((2, HEAD_DIM, HEAD_DIM, n_batch), lambda p: (p, 0, 0, 0))
    per_tok = pltpu.VMEM((2, n_tok, HEAD_DIM, n_batch), f32)
    z, s_out = pl.pallas_call(
        functools.partial(_scan_lanes_kernel, n_tok),
        grid=(N_PAIRS,),
        in_specs=[act] * 6 + [par] * 5 + [st_spec],
        out_specs=[act, st_spec],
        out_shape=[jax.ShapeDtypeStruct((n_tok, n_batch, D_MODEL), bf16),
                   jax.ShapeDtypeStruct(state.shape, f32)],
        scratch_shapes=[per_tok] * 7,
        compiler_params=_cparams(1),
        name="rwkv_decode_scan",
    )(*acts, *params, state)
    return z.reshape(n_tok * n_batch, D_MODEL), s_out


def _proj_res_kernel(z_ref, w_ref, b_ref, x_ref, gt_ref, o_ref):
    out = _dot(z_ref[...], w_ref[...]) + b_ref[...]
    o_ref[...] = x_ref[...] + _rows(gt_ref[...], x_ref.shape[0]) * out


def _proj_res(z, w, bias, x, mod, tok_major, seq_len):
    m = x.shape[0]
    tm = min(ROW_TILE, m)
    rows = pl.BlockSpec((tm, D_MODEL), lambda i: (i, 0))
    return pl.pallas_call(
        _proj_res_kernel,
        grid=(m // tm,),
        in_specs=[rows,
                  pl.BlockSpec((D_MODEL, D_MODEL), lambda i: (0, 0), pipeline_mode=pl.Buffered(1)),
                  pl.BlockSpec((1, D_MODEL), lambda i: (0, 0)),
                  rows,
                  _mod_spec(tok_major, tm, seq_len, 5)],
        out_specs=rows,
        out_shape=jax.ShapeDtypeStruct((m, D_MODEL), f32),
        compiler_params=_cparams(1),
        name="proj_res",
    )(z, w, bias.reshape(1, D_MODEL), x, mod)


KV_COLS = N_KV_HEADS * HEAD_DIM
QKV_COLS = D_MODEL + 2 * KV_COLS


def _qkv_kernel(x_ref, ng_ref, sh_ref, sc_ref, w_ref, b_ref, cos_ref, sin_ref, o_ref, h_scr):
    @pl.when(pl.program_id(1) == 0)
    def _():
        tm = x_ref.shape[0]
        h_scr[...] = _rms_mod(x_ref[...], ng_ref[...], _rows(sh_ref[...], tm), _rows(sc_ref[...], tm)).astype(bf16)

    first = (lax.broadcasted_iota(jnp.int32, (x_ref.shape[0], LANES), 1) % HEAD_DIM) < HEAD_DIM // 2
    h = h_scr[...]
    for c0 in range(0, o_ref.shape[1], MXU_COLS):
        acc = _dot(h, w_ref[:, c0:c0 + MXU_COLS]) + b_ref[:, c0:c0 + MXU_COLS]
        group = slice(c0 // MXU_COLS * LANES, (c0 // MXU_COLS + 1) * LANES)
        cos, sin = cos_ref[:, group], sin_ref[:, group]
        for c in range(0, MXU_COLS, LANES):
            xc = acc[:, c:c + LANES]
            rot = jnp.where(first, pltpu.roll(xc, LANES - HEAD_DIM // 2, 1),
                            pltpu.roll(xc, HEAD_DIM // 2, 1))
            o_ref[:, c0 + c:c0 + c + LANES] = xc * cos + rot * sin


def _qkv(x, mod, tok_major, seq_len, norm_g, w, b, cos, sin):
    m = x.shape[0]
    tm = min(ROW_TILE, m)
    tn = QKV_TILE
    assert tn == 2 * MXU_COLS == 2 * KV_COLS
    n_pos_blocks = cos.shape[0] // tm
    vec = pl.BlockSpec((1, D_MODEL), lambda i, j: (0, 0))
    tab = pl.BlockSpec((tm, 2 * LANES), lambda i, j: (i % n_pos_blocks, j // (D_MODEL // tn)))
    return pl.pallas_call(
        _qkv_kernel,
        grid=(m // tm, QKV_COLS // tn),
        in_specs=[pl.BlockSpec((tm, D_MODEL), lambda i, j: (i, 0)),
                  vec,
                  _mod_spec(tok_major, tm, seq_len, 3),
                  _mod_spec(tok_major, tm, seq_len, 4),
                  pl.BlockSpec((D_MODEL, tn), lambda i, j: (0, j)),
                  pl.BlockSpec((1, tn), lambda i, j: (0, j)),
                  tab, tab],
        out_specs=pl.BlockSpec((tm, tn), lambda i, j: (i, j)),
        out_shape=jax.ShapeDtypeStruct((m, QKV_COLS), f32),
        scratch_shapes=[pltpu.VMEM((tm, D_MODEL), bf16)],
        compiler_params=_cparams(2),
        name="swa_qkv",
    )(x, norm_g.reshape(1, D_MODEL), mod, mod, w, b.reshape(1, QKV_COLS), cos, sin)


def _rope_tables(positions):
    half = HEAD_DIM // 2
    inv_freq = ROPE_THETA ** (-jnp.arange(half, dtype=f32) / half)
    ang = positions.astype(f32)[:, None] * inv_freq[None, :]
    cos = jnp.tile(jnp.cos(ang), (1, LANES // half))
    sin = jnp.sin(ang)
    sin = jnp.tile(jnp.concatenate([-sin, sin], axis=1), (1, LANES // HEAD_DIM))
    return (jnp.concatenate([cos, cos, cos, jnp.ones_like(cos)], axis=1),
            jnp.concatenate([sin, sin, sin, jnp.zeros_like(sin)], axis=1))


def _head_masks(rows):
    lane = lax.broadcasted_iota(jnp.int32, (rows, LANES), 1)
    return lane < HEAD_DIM, lane >= HEAD_DIM


def _both_halves(pair):
    first, second = _head_masks(pair.shape[0])
    swapped = pltpu.roll(pair, HEAD_DIM, 1)
    return jnp.where(first, pair, swapped), jnp.where(second, pair, swapped)


def _attn_prompt_kernel(q_ref, kc_ref, vc_ref, sink_ref, o_ref, k_prev, v_prev):
    n = pl.program_id(1)
    blk = WINDOW
    m0, m1 = _head_masks(blk)
    m0k, m1k = _head_masks(2 * blk)
    qi = lax.broadcasted_iota(jnp.int32, (2 * blk, 2 * blk), 0) % blk
    sj = lax.broadcasted_iota(jnp.int32, (2 * blk, 2 * blk), 1)
    visible = (sj > qi) & (sj <= qi + blk) & ((n > 0) | (sj >= blk))
    top = lax.broadcasted_iota(jnp.int32, (2 * blk, 1), 0) < blk
    @pl.when(n == 0)
    def _():
        k_prev[...] = jnp.zeros_like(k_prev)
        v_prev[...] = jnp.zeros_like(v_prev)

    tiles = [slice(t * LANES, (t + 1) * LANES) for t in range(N_KV_HEADS // 2)]
    k_cur = [h.astype(bf16) for sl in tiles for h in _both_halves(kc_ref[:, sl])]
    v_cur = [h for sl in tiles for h in _both_halves(vc_ref[:, sl])]
    for c in range(N_KV_HEADS):
        kd = jnp.concatenate([k_prev[c], k_cur[c]], axis=0)
        vd = jnp.concatenate([v_prev[c], v_cur[c]], axis=0)
        vcat = jnp.concatenate([jnp.where(m0k, vd, 0.0), jnp.where(m1k, vd, 0.0)], axis=0).astype(bf16)
        pairs = [c * (GQA_GROUP // 2) + jj for jj in range(GQA_GROUP // 2)]
        each = lambda fn, *lists: [fn(*args) for args in zip(*lists)]
        qp = [q_ref[:, pair * LANES:(pair + 1) * LANES] * ATTN_SCALE for pair in pairs]
        qs = each(lambda t: jnp.concatenate([jnp.where(m0, t, 0.0), jnp.where(m1, t, 0.0)], axis=0).astype(bf16), qp)
        s = each(lambda t: jnp.where(visible, _dot_nt(t, kd), NEG_BIG), qs)
        sk = [jnp.where(top, sink_ref[2 * pair], sink_ref[2 * pair + 1]) for pair in pairs]
        mx = each(lambda t, k_: jnp.maximum(jnp.max(t, axis=1, keepdims=True), k_), s, sk)
        p = each(lambda t, m_: jnp.exp(t - m_), s, mx)
        den = each(lambda t, k_, m_: jnp.sum(t, axis=1, keepdims=True) + jnp.exp(k_ - m_), p, sk, mx)
        p = each(lambda t, d_: (t * (1.0 / d_)).astype(bf16), p, den)
        pcat = each(lambda t: jnp.concatenate([t[0:blk], t[blk:]], axis=1), p)
        for pair, t in zip(pairs, pcat):
            o_ref[:, pair * LANES:(pair + 1) * LANES] = _dot(t, vcat).astype(bf16)
    for c in range(N_KV_HEADS):
        k_prev[c] = k_cur[c]
        v_prev[c] = v_cur[c]


def _attn_prompt(qkv, sink, n_batch, seq_len):
    nb = seq_len // WINDOW
    k_blk = D_MODEL // KV_COLS
    cur = lambda off: pl.BlockSpec((WINDOW, KV_COLS), lambda b, n: (b * nb + n, k_blk + off))
    return pl.pallas_call(
        _attn_prompt_kernel,
        grid=(n_batch, nb),
        in_specs=[pl.BlockSpec((WINDOW, D_MODEL), lambda b, n: (b * nb + n, 0)),
                  cur(0), cur(1),
                  pl.BlockSpec(memory_space=pltpu.SMEM)],
        out_specs=pl.BlockSpec((WINDOW, D_MODEL), lambda b, n: (b * nb + n, 0)),
        out_shape=jax.ShapeDtypeStruct((n_batch * seq_len, D_MODEL), bf16),
        scratch_shapes=[pltpu.VMEM((N_KV_HEADS, WINDOW, LANES), bf16), pltpu.VMEM((N_KV_HEADS, WINDOW, LANES), f32)],
        compiler_params=_cparams(2),
        name="swa_prompt",
    )(qkv, qkv, qkv, sink)


SEQ_PER_GROUP = 4
GROUPS_PER_STEP = 2


def _attn_cached_kernel(n_tok, groups, q_ref, kn_ref, vn_ref, kc_ref, vc_ref, sink_ref, o_ref):
    rows = SEQ_PER_GROUP * n_tok
    n_st = GQA_GROUP * rows
    m0, m1 = _head_masks(rows)
    srow = lax.broadcasted_iota(jnp.int32, (n_st, 1), 0)
    row_seq = (srow % rows) // n_tok
    row_tok = srow % n_tok
    key_c = lax.broadcasted_iota(jnp.int32, (n_st, WINDOW), 1)
    vis_c = key_c > row_tok
    key_n = lax.broadcasted_iota(jnp.int32, (n_st, rows), 1)
    vis_n = ((key_n // n_tok) == row_seq) & ((key_n % n_tok) <= row_tok)
    for gi in range(groups):
        rs = slice(gi * rows, (gi + 1) * rows)
        for cp in range(N_KV_HEADS // 2):
            for ce in range(2):
                c = 2 * cp + ce
                sl = slice(cp * LANES, (cp + 1) * LANES)
                kt = [kc_ref[gi * SEQ_PER_GROUP + b, c * HEAD_DIM:(c + 1) * HEAD_DIM, :] for b in range(SEQ_PER_GROUP)]
                vt = [vc_ref[gi * SEQ_PER_GROUP + b, c * HEAD_DIM:(c + 1) * HEAD_DIM, :] for b in range(SEQ_PER_GROUP)]
                kn = _both_halves(kn_ref[rs, sl])[ce].astype(bf16)
                vn = _both_halves(vn_ref[rs, sl])[ce].astype(bf16)
                pieces = []
                for jj in range(GQA_GROUP // 2):
                    pair = c * (GQA_GROUP // 2) + jj
                    qp = q_ref[rs, pair * LANES:(pair + 1) * LANES]
                    pieces += [jnp.where(m0, qp, 0.0), jnp.where(m1, qp, 0.0)]
                qs = jnp.concatenate(pieces, axis=0).astype(bf16)
                s_c = jnp.zeros((n_st, WINDOW), f32)
                for b in range(SEQ_PER_GROUP):
                    kx = jnp.concatenate([kt[b], kt[b]], axis=0).astype(bf16)
                    s_c = jnp.where(row_seq == b, _dot(qs, kx), s_c)
                s_c = jnp.where(vis_c, s_c * ATTN_SCALE, NEG_BIG)
                s_n = jnp.where(vis_n, _dot_nt(qs, kn) * ATTN_SCALE, NEG_BIG)
                sk = sink_ref[c]
                sk = sk[:, 0:1]
                mx = jnp.maximum(jnp.maximum(jnp.max(s_c, axis=1, keepdims=True),
                                             jnp.max(s_n, axis=1, keepdims=True)), sk)
                p_c = jnp.exp(s_c - mx)
                p_n = jnp.exp(s_n - mx)
                den = (jnp.sum(p_c, axis=1, keepdims=True) + jnp.sum(p_n, axis=1, keepdims=True)
                       + jnp.exp(sk - mx))
                inv_den = 1.0 / den
                p_c = (p_c * inv_den).astype(bf16)
                p_n = (p_n * inv_den).astype(bf16)
                o = _dot(p_n, vn)
                for b in range(SEQ_PER_GROUP):
                    vx = jnp.concatenate([vt[b], vt[b]], axis=0).astype(bf16)
                    o = o + jnp.where(row_seq == b, _dot_nt(p_c, vx), 0.0)
                for jj in range(GQA_GROUP // 2):
                    pair = c * (GQA_GROUP // 2) + jj
                    o0 = o[(2 * jj) * rows:(2 * jj + 1) * rows]
                    o1 = o[(2 * jj + 1) * rows:(2 * jj + 2) * rows]
                    o_ref[rs, pair * LANES:(pair + 1) * LANES] = jnp.where(m0, o0, o1).astype(bf16)


def _attn_cached(qkv, k_cache, v_cache, sink, n_batch, n_tok):
    groups = GROUPS_PER_STEP
    seqs = SEQ_PER_GROUP * groups
    rows = seqs * n_tok
    k_blk = D_MODEL // KV_COLS
    n_st = GQA_GROUP * SEQ_PER_GROUP * n_tok
    head = (jnp.arange(N_KV_HEADS)[:, None] * GQA_GROUP
            + (jnp.arange(n_st)[None, :] // (SEQ_PER_GROUP * n_tok)))
    sink_tab = jnp.broadcast_to(sink[head][:, :, None], (N_KV_HEADS, n_st, LANES))
    cache_spec = pl.BlockSpec((seqs, N_KV_HEADS * HEAD_DIM, WINDOW), lambda i: (i, 0, 0))
    return pl.pallas_call(
        functools.partial(_attn_cached_kernel, n_tok, groups),
        grid=(n_batch // seqs,),
        in_specs=[pl.BlockSpec((rows, D_MODEL), lambda i: (i, 0)),
                  pl.BlockSpec((rows, KV_COLS), lambda i: (i, k_blk)),
                  pl.BlockSpec((rows, KV_COLS), lambda i: (i, k_blk + 1)),
                  cache_spec, cache_spec,
                  pl.BlockSpec((N_KV_HEADS, n_st, LANES), lambda i: (0, 0, 0))],
        out_specs=pl.BlockSpec((rows, D_MODEL), lambda i: (i, 0)),
        out_shape=jax.ShapeDtypeStruct((n_batch * n_tok, D_MODEL), bf16),
        compiler_params=_cparams(1),
        name="swa_cached",
    )(qkv, qkv, qkv, k_cache, v_cache, sink_tab)


def _trunk(x, mod_all, tok_major, n_batch, seq_len, pos0, wkv_in, shift_in, k_in, v_in, wts):
    to_seq_major = lambda t: t.reshape(seq_len, n_batch, -1).transpose(1, 0, 2).reshape(n_batch * seq_len, -1)
    to_tok_major = lambda t: t.reshape(n_batch, seq_len, -1).transpose(1, 0, 2).reshape(n_batch * seq_len, -1)
    mod = mod_all[0]
    ffn_w = dict(wts["ffn"])
    x, ffn_w[0, 0] = _ffn(x, mod, tok_major, seq_len, 0, wts["norm_g"][0, 0], ffn_w[0, 0])
    s0 = shift_in[0] if tok_major else jnp.zeros((n_batch, 1, D_MODEL), f32)
    *proj, h_tail = _rwkv_proj(x, mod, tok_major, seq_len, wts["norm_g"][0, 1], s0, wts["rw"])
    if tok_major:
        z, wkv_t = _scan_lanes(proj, wts["rw"], n_batch, seq_len, jnp.transpose(wkv_in[0], (1, 2, 3, 0)))
        wkv_out = jnp.transpose(wkv_t, (3, 0, 1, 2))
        shift_out = h_tail
    else:
        z, wkv_out = _scan(proj, wts["rw"], n_batch, seq_len)
        shift_out = h_tail.reshape(n_batch, -1, SUBLANES, D_MODEL)[:, -1, -1]
    x = _proj_res(z, wts["rw"]["wo"], jnp.zeros((D_MODEL,), f32), x, mod, tok_major, seq_len)
    x, ffn_w[0, 1] = _ffn(x, mod, tok_major, seq_len, 6, wts["norm_g"][0, 2], ffn_w[0, 1])
    mod = mod_all[1]
    x, ffn_w[1, 0] = _ffn(x, mod, tok_major, seq_len, 0, wts["norm_g"][1, 0], ffn_w[1, 0])
    positions = pos0 + jnp.arange(seq_len)
    if tok_major:
        positions = jnp.repeat(positions, n_batch)
    cos, sin = _rope_tables(positions)
    qkv = _qkv(x, mod, tok_major, seq_len, wts["norm_g"][1, 1], wts["sw_wqkv"], wts["sw_bqkv"], cos, sin)
    heads = lambda t: t.reshape(n_batch, -1, N_KV_HEADS, HEAD_DIM)
    if k_in is None:
        att = _attn_prompt(qkv, wts["sw_sink"], n_batch, seq_len)
        tail = qkv.reshape(n_batch, seq_len, QKV_COLS)[:, -WINDOW:, D_MODEL:]
        k_new, v_new = heads(tail[..., :KV_COLS]), heads(tail[..., KV_COLS:])
    else:
        win = k_in.shape[2]
        qkv = to_seq_major(qkv)
        cache_t = lambda t: jnp.transpose(t[0], (0, 2, 3, 1)).reshape(n_batch, N_KV_HEADS * HEAD_DIM, win)
        att = _attn_cached(qkv, cache_t(k_in), cache_t(v_in), wts["sw_sink"], n_batch, seq_len)
        att = to_tok_major(att)
        k_tok, v_tok = heads(qkv[:, D_MODEL:D_MODEL + KV_COLS]), heads(qkv[:, D_MODEL + KV_COLS:])
        k_new = jnp.concatenate([k_in[0], k_tok], axis=1)[:, -win:]
        v_new = jnp.concatenate([v_in[0], v_tok], axis=1)[:, -win:]
    x = _proj_res(att, wts["sw_wo"], wts["sw_bo"], x, mod, tok_major, seq_len)
    y, ffn_w[1, 1] = _ffn(x, mod, tok_major, seq_len, 6, wts["norm_g"][1, 2], ffn_w[1, 1], final_g=wts["final_g"])
    return (y, wkv_out[None], shift_out[None], k_new[None], v_new[None]), ffn_w


def _pad_cols(w):
    return jnp.pad(w, ((0, 0), (0, LORA_PAD - w.shape[1])))


def _pad_rows(w):
    return jnp.pad(w, ((0, LORA_PAD - w.shape[0]), (0, 0)))


def kernel(x_prompt, x_sample, state_rwkv_wkv, state_rwkv_shift, cache_swa_k, cache_swa_v, c_prompt, c_sample, norm_g, w_ada, b_ada, w_ffn_in, w_ffn_out, rw_mu, rw_wrkv, rw_w0, rw_w1, rw_w2, rw_a0, rw_a1, rw_a2, rw_g1, rw_g2, rw_kk, rw_ka, rw_rk, rw_gn_w, rw_gn_b, rw_wo, sw_wqkv, sw_bqkv, sw_sink, sw_wo, sw_bo, final_g):
    n_p, seq_p, _ = x_prompt.shape
    n_s, seq_s, _ = x_sample.shape
    row = lambda t: t.reshape(1, D_MODEL)
    rw = dict(
        mu=rw_mu[0], wrkv=rw_wrkv[0].astype(bf16),
        w1=_pad_cols(rw_w1[0]).astype(bf16), a1=_pad_cols(rw_a1[0]).astype(bf16), g1=rw_g1[0].astype(bf16),
        w2=_pad_rows(rw_w2[0]).astype(bf16), a2=_pad_rows(rw_a2[0]).astype(bf16), g2=rw_g2[0].astype(bf16),
        w0=row(rw_w0[0]), a0=row(rw_a0[0]), kk=row(rw_kk[0]), ka=row(rw_ka[0]), rk=row(rw_rk[0]),
        gn_w=row(rw_gn_w[0]), gn_b=row(rw_gn_b[0]), wo=rw_wo[0].astype(bf16))
    wts = dict(
        norm_g=norm_g, final_g=final_g,
        ffn={(l, s): (w_ffn_in, w_ffn_out, l, s) for l in range(2) for s in range(2)}, rw=rw,
        sw_wqkv=sw_wqkv[0].astype(bf16), sw_bqkv=sw_bqkv[0],
        sw_sink=sw_sink[0], sw_wo=sw_wo[0].astype(bf16), sw_bo=sw_bo[0])

    n_c = n_p + n_s
    pad = (-n_c) % SUBLANES
    c_all = jnp.concatenate([c_sample, c_prompt, jnp.zeros((pad, D_MODEL), f32)], axis=0)
    mod_s = _ada(c_all, w_ada, b_ada)
    mod_p = [m[n_s:n_c].reshape(n_p, 1, N_MOD * D_MODEL) for m in mod_s]

    (y_s, s_wkv, s_shift, s_k, s_v), wts["ffn"] = _trunk(
        jnp.transpose(x_sample, (1, 0, 2)).reshape(n_s * seq_s, D_MODEL), mod_s, True, n_s, seq_s, PAST_LEN,
        state_rwkv_wkv, state_rwkv_shift, cache_swa_k, cache_swa_v, wts)
    (y_p, p_wkv, p_shift, p_k, p_v), _ = _trunk(
        x_prompt.reshape(n_p * seq_p, D_MODEL), mod_p, False, n_p, seq_p, 0, None, None, None, None, wts)
    y_s = jnp.transpose(y_s.reshape(seq_s, n_s, D_MODEL), (1, 0, 2))
    return (y_p.reshape(n_p, seq_p, D_MODEL), y_s, p_wkv, p_shift, p_k, p_v, s_wkv, s_shift, s_k, s_v)
```

```python
import functools
import math

import jax
import jax.numpy as jnp
from jax import lax
from jax.experimental import pallas as pl
from jax.experimental.pallas import tpu as pltpu

f32 = jnp.float32
bf16 = jnp.bfloat16

D_MODEL = 2048
HEAD_DIM = 64
N_HEADS = D_MODEL // HEAD_DIM
N_KV_HEADS = 4
GQA_GROUP = N_HEADS // N_KV_HEADS
WINDOW = 128
ATTN_SCALE = HEAD_DIM ** -0.5
ROPE_THETA = 10000.0
D_FF = 5632
N_MOD = 9
RMS_EPS = 1e-6
GN_EPS = 64e-5
PAST_LEN = 8192
LANES = 128
SUBLANES = 8
MXU_COLS = 256
N_PAIRS = D_MODEL // LANES
LORA_PAD = 128
GATE_LORA = 256
SLAB = 64
SCAN_CHAINS = 16
VMEM_LIMIT = 56 * 1024 * 1024
NEG_BIG = -1e30
ROW_TILE = 512
FFN_COLS = 512
PROJ_COLS = 256
QKV_TILE = 512
ADA_COLS = 1024


def _cparams(n_axes, lead="parallel"):
    return pltpu.CompilerParams(dimension_semantics=(lead,) + ("arbitrary",) * (n_axes - 1),
                                vmem_limit_bytes=VMEM_LIMIT)


def _dot(a, b):
    return jnp.dot(a, b, preferred_element_type=f32)


def _dot_nt(a, b):
    return lax.dot_general(a, b, (((1,), (1,)), ((), ())), preferred_element_type=f32)


def _split(x, n):
    if x.dtype == bf16:
        return [x]
    parts = []
    rem = x
    for i in range(n):
        p = rem.astype(bf16)
        parts.append(p)
        if i + 1 < n:
            rem = rem - p.astype(f32)
    return parts


def _mm(a, b, pa=1, pb=1, nt=False):
    a_parts = _split(a, pa)
    b_parts = _split(b, pb)
    order = max(len(a_parts), len(b_parts))
    acc = None
    for i, x in enumerate(a_parts):
        for j, y in enumerate(b_parts):
            if i + j >= order:
                continue
            t = _dot_nt(x, y) if nt else _dot(x, y)
            acc = t if acc is None else acc + t
    return acc


def _rms_mod(x, g, shift, scale):
    return x * lax.rsqrt(jnp.mean(x * x, axis=-1, keepdims=True) + RMS_EPS) * (g * (1.0 + scale)) + shift


def _mod_spec(tok_major, tm, seq_len, idx):
    if tok_major:
        return pl.BlockSpec((tm // seq_len, D_MODEL), lambda i, *_: (0, idx))
    return pl.BlockSpec((None, 1, D_MODEL), lambda i, *_: ((i * tm) // seq_len, 0, idx))


def _rows(v, tm):
    n = v.shape[0]
    return v if n in (1, tm) else jnp.concatenate([v] * (tm // n), axis=0)


def _ada_kernel(c_ref, w_ref, b_ref, *o_refs):
    c = c_ref[...]
    s = (c * jax.nn.sigmoid(c)).astype(bf16)
    for layer, o_ref in enumerate(o_refs):
        o_ref[...] = _dot(s, w_ref[layer].astype(bf16)) + b_ref[layer]


def _ada(c_all, w_ada, b_ada):
    n_layers, _, n_out = w_ada.shape
    rows = c_all.shape[0]
    tn = ADA_COLS
    return pl.pallas_call(
        _ada_kernel,
        grid=(n_out // tn,),
        in_specs=[pl.BlockSpec((rows, D_MODEL), lambda j: (0, 0)),
                  pl.BlockSpec((n_layers, D_MODEL, tn), lambda j: (0, 0, j)),
                  pl.BlockSpec((n_layers, 1, tn), lambda j: (0, 0, j))],
        out_specs=[pl.BlockSpec((rows, tn), lambda j: (0, j))] * n_layers,
        out_shape=[jax.ShapeDtypeStruct((rows, n_out), f32)] * n_layers,
        compiler_params=_cparams(1),
        name="ada_mod",
    )(c_all, w_ada, b_ada.reshape(n_layers, 1, n_out))


def _ffn_kernel(final_norm, emit_bf16, tiled_gate, x_ref, ng_ref, sh_ref, sc_ref, gt_ref, wg_ref, wu_ref, wo_ref,
                fg_ref, o_ref, *rest):
    rest = list(rest)
    if emit_bf16:
        wg_out, wu_out, wo_out = rest[:3]
        del rest[:3]
    h_scr = rest.pop(0)
    gt_scr = rest.pop(0) if tiled_gate else None
    f = pl.program_id(1)
    tm = x_ref.shape[0]

    @pl.when(f == 0)
    def _():
        h_scr[...] = _rms_mod(x_ref[...], ng_ref[...], _rows(sh_ref[...], tm), _rows(sc_ref[...], tm)).astype(bf16)
        o_ref[...] = x_ref[...]
        if tiled_gate:
            gt_scr[...] = 0.5 * _rows(gt_ref[...], tm)

    wg, wu, wo = wg_ref[...], wu_ref[...], wo_ref[...]
    if emit_bf16:
        wg, wu, wo = wg.astype(bf16), wu.astype(bf16), wo.astype(bf16)
        wg_out[...] = wg
        wu_out[...] = wu
        wo_out[...] = wo
    h = h_scr[...]
    gate = _dot(h, wg)
    up = _dot(h, wu)
    act = (gate * jax.nn.sigmoid(gate) * up).astype(bf16)
    half_gate = gt_scr[...] if tiled_gate else 0.5 * gt_ref[...]
    o_ref[...] += half_gate * _dot(act, wo)

    if final_norm:
        @pl.when(f == pl.num_programs(1) - 1)
        def _():
            y = o_ref[...]
            o_ref[...] = y * lax.rsqrt(jnp.mean(y * y, axis=-1, keepdims=True) + RMS_EPS) * fg_ref[...]


def _ffn(x, mod, tok_major, seq_len, mod_base, norm_g, weights, final_g=None):
    m = x.shape[0]
    tm = min(ROW_TILE, m)
    emit = len(weights) == 4
    tf = FFN_COLS // 2 if emit else FFN_COLS
    nf = D_FF // tf
    vec = pl.BlockSpec((1, D_MODEL), lambda i, j: (0, 0))
    fg = jnp.ones((1, D_MODEL), f32) if final_g is None else final_g.reshape(1, D_MODEL)
    w_in_spec = pl.BlockSpec((D_MODEL, tf), lambda i, j: (0, j))
    w_out_spec = pl.BlockSpec((tf, D_MODEL), lambda i, j: (j, 0))
    out_specs = [pl.BlockSpec((tm, D_MODEL), lambda i, j: (i, 0))]
    out_shape = [jax.ShapeDtypeStruct((m, D_MODEL), f32)]
    if emit:
        assert m == tm, "the weight copies are written once, by a single row tile"
        w_ffn_in, w_ffn_out, layer, slot = weights
        w_args = (w_ffn_in, w_ffn_in, w_ffn_out)
        w_specs = [pl.BlockSpec((None, None, D_MODEL, tf), lambda i, j: (layer, slot, 0, j)),
                   pl.BlockSpec((None, None, D_MODEL, tf), lambda i, j: (layer, slot, 0, j + nf)),
                   pl.BlockSpec((None, None, tf, D_MODEL), lambda i, j: (layer, slot, j, 0))]
        out_specs += [w_in_spec, w_in_spec, w_out_spec]
        out_shape += [jax.ShapeDtypeStruct((D_MODEL, D_FF), bf16), jax.ShapeDtypeStruct((D_MODEL, D_FF), bf16),
                      jax.ShapeDtypeStruct((D_FF, D_MODEL), bf16)]
    else:
        w_args = weights
        w_specs = [w_in_spec, w_in_spec, w_out_spec]
    outs = pl.pallas_call(
        functools.partial(_ffn_kernel, final_g is not None, emit, tok_major),
        grid=(m // tm, nf),
        in_specs=[pl.BlockSpec((tm, D_MODEL), lambda i, j: (i, 0)),
                  vec,
                  _mod_spec(tok_major, tm, seq_len, mod_base),
                  _mod_spec(tok_major, tm, seq_len, mod_base + 1),
                  _mod_spec(tok_major, tm, seq_len, mod_base + 2)] + w_specs + [vec],
        out_specs=out_specs,
        out_shape=out_shape,
        scratch_shapes=[pltpu.VMEM((tm, D_MODEL), bf16)] + ([pltpu.VMEM((tm, D_MODEL), f32)] if tok_major else []),
        compiler_params=_cparams(2),
        name="ffn",
    )(x, norm_g.reshape(1, D_MODEL), mod, mod, mod, *w_args, fg)
    return (outs[0], tuple(outs[1:])) if emit else (outs[0], weights)


def _rwkv_proj_kernel(seq_len, tm, h_rows, tok_major,
                      x_ref, ng_ref, sh_ref, sc_ref, s0_ref, mu_ref, w1_ref, a1_ref, g1_ref,
                      w_ref, w2_ref, a2_ref, g2_ref, w0_ref, a0_ref,
                      r_ref, ld_ref, k_ref, v_ref, a_ref, g_ref, h_ref,
                      hs, xr, xk, xv, tw, ta, sg):
    i = pl.program_id(0)
    j = pl.program_id(1)

    @pl.when(j == 0)
    def _():
        h = _rms_mod(x_ref[...], ng_ref[...], _rows(sh_ref[...], tm), _rows(sc_ref[...], tm))
        h_ref[...] = h[tm - h_rows:tm, :]
        if tok_major:
            n_seq = s0_ref.shape[0]
            prev = jnp.concatenate([s0_ref[...], h[0:tm - n_seq, :]], axis=0)
        else:
            @pl.when(i == 0)
            def _():
                hs[0:SUBLANES, :] = jnp.zeros((SUBLANES, D_MODEL), f32)

            @pl.when(i > 0)
            def _():
                hs[0:SUBLANES, :] = hs[tm:tm + SUBLANES, :]

            hs[SUBLANES:tm + SUBLANES, :] = h
            row = i * tm + lax.broadcasted_iota(jnp.int32, (tm, 1), 0)
            prev = jnp.where(row % seq_len == 0, s0_ref[...], hs[SUBLANES - 1:tm + SUBLANES - 1, :])
        xx = prev - h
        mu = mu_ref[...]
        xr[...] = (h + xx * mu[0:1, :]).astype(bf16)
        xk[...] = (h + xx * mu[2:3, :]).astype(bf16)
        xv[...] = (h + xx * mu[3:4, :]).astype(bf16)
        xw = (h + xx * mu[1:2, :]).astype(bf16)
        tw[...] = jnp.tanh(_dot(xw, w1_ref[...])).astype(bf16)
        xa = (h + xx * mu[4:5, :]).astype(bf16)
        ta[...] = _dot(xa, a1_ref[...]).astype(bf16)
        xg = (h + xx * mu[5:6, :]).astype(bf16)
        sg[...] = jax.nn.sigmoid(_dot(xg, g1_ref[...])).astype(bf16)

    cols = pl.ds(pl.multiple_of(j * r_ref.shape[1], r_ref.shape[1]), r_ref.shape[1])
    r_ref[...] = _dot(xr[...], w_ref[0, :, cols])
    k_ref[...] = _dot(xk[...], w_ref[1, :, cols])
    v_ref[...] = _dot(xv[...], w_ref[2, :, cols])
    z = w0_ref[...] + _dot(tw[...], w2_ref[...])
    ld_ref[...] = -jax.nn.sigmoid(z) * math.exp(-0.5)
    a_ref[...] = jax.nn.sigmoid(a0_ref[...] + _dot(ta[...], a2_ref[...]))
    g_ref[...] = _dot(sg[...], g2_ref[...])


def _rwkv_proj(x, mod, tok_major, seq_len, norm_g, s0, rwp):
    m = x.shape[0]
    tm = min(ROW_TILE, m)
    tn = PROJ_COLS
    h_rows = tm // seq_len if tok_major else SUBLANES
    full = lambda shape: pl.BlockSpec(shape, lambda i, j: (0,) * len(shape))
    col = lambda rows: pl.BlockSpec((rows, tn), lambda i, j: (0, j))
    w_res = pl.BlockSpec((3, D_MODEL, D_MODEL), lambda i, j: (0, 0, 0), pipeline_mode=pl.Buffered(1))
    if tok_major:
        assert m == tm
        s0_spec = pl.BlockSpec((tm // seq_len, D_MODEL), lambda i, j: (0, 0))
    else:
        s0_spec = pl.BlockSpec((None, 1, D_MODEL), lambda i, j: ((i * tm) // seq_len, 0, 0))
    out_spec = pl.BlockSpec((tm, tn), lambda i, j: (i, j))
    out_sds = jax.ShapeDtypeStruct((m, D_MODEL), f32)
    outs = pl.pallas_call(
        functools.partial(_rwkv_proj_kernel, seq_len, tm, h_rows, tok_major),
        grid=(m // tm, D_MODEL // tn),
        in_specs=[pl.BlockSpec((tm, D_MODEL), lambda i, j: (i, 0)),
                  full((1, D_MODEL)),
                  _mod_spec(tok_major, tm, seq_len, 3),
                  _mod_spec(tok_major, tm, seq_len, 4),
                  s0_spec,
                  full((6, D_MODEL)),
                  full((D_MODEL, LORA_PAD)), full((D_MODEL, LORA_PAD)), full((D_MODEL, GATE_LORA)),
                  w_res,
                  col(LORA_PAD), col(LORA_PAD), col(GATE_LORA), col(1), col(1)],
        out_specs=[out_spec] * 6 + [pl.BlockSpec((h_rows, D_MODEL), lambda i, j: (i, 0))],
        out_shape=[out_sds] * 6 + [jax.ShapeDtypeStruct((m // tm * h_rows, D_MODEL), f32)],
        scratch_shapes=[pltpu.VMEM((tm + SUBLANES, D_MODEL), f32),
                        pltpu.VMEM((tm, D_MODEL), bf16), pltpu.VMEM((tm, D_MODEL), bf16),
                        pltpu.VMEM((tm, D_MODEL), bf16),
                        pltpu.VMEM((tm, LORA_PAD), bf16), pltpu.VMEM((tm, LORA_PAD), bf16),
                        pltpu.VMEM((tm, GATE_LORA), bf16)],
        compiler_params=_cparams(2, lead="arbitrary"),
        name="rwkv_proj",
    )(x, norm_g.reshape(1, D_MODEL), mod, mod, s0, rwp["mu"], rwp["w1"], rwp["a1"], rwp["g1"],
      rwp["wrkv"], rwp["w2"], rwp["a2"], rwp["g2"], rwp["w0"], rwp["a0"])
    return outs


def _scan_chains(acts, params, states):
    n_st = 2 * SLAB
    row = lax.broadcasted_iota(jnp.int32, (n_st, LANES), 0)
    lane = lax.broadcasted_iota(jnp.int32, (n_st, LANES), 1)
    own = (row // SLAB) == (lane // HEAD_DIM)
    blk = (row // SLAB) == (lane // SLAB)
    strict = blk & (lane < row)
    incl = blk & (lane <= row)
    eye = jnp.where(row == lane, 1.0, 0.0)
    twice = lambda t: jnp.concatenate([t, t], axis=0)
    st = lambda t: jnp.where(own, twice(t), 0.0)
    each = lambda fn, *lists: [fn(*args) for args in zip(*lists)]

    r, ld, k, v, a, g = [[act[i] for act in acts] for i in range(6)]
    kkp, kap, rkp, gnw, gnb = [[par[i] for par in params] for i in range(5)]

    r64 = lax.broadcasted_iota(jnp.int32, (SLAB, SLAB), 0)
    c64 = lax.broadcasted_iota(jnp.int32, (SLAB, SLAB), 1)
    tri_ones = jnp.concatenate([jnp.where(c64 <= r64, 1.0, 0.0),
                                jnp.ones((SLAB, SLAB), f32)], axis=0).astype(bf16)
    sums = each(lambda t: _mm(tri_ones, t, pb=3), ld)
    cs = each(lambda t: t[0:SLAB], sums)
    tot = each(lambda t: t[SLAB:n_st], sums)

    kk_raw = each(lambda t, p: st(t * p), k, kkp)
    kk = each(lambda t: t * (1.0 / jnp.maximum(jnp.sqrt(jnp.sum(t * t, axis=1, keepdims=True)), 1e-12)), kk_raw)
    k2 = each(lambda kt, at, p: st(kt * (1.0 + (at - 1.0) * p)), k, a, kap)
    b = each(lambda t, at: t * twice(at), kk, a)
    r_s = each(st, r)
    v_s = each(st, v)
    bonus = each(lambda rt, kt, p, vt: jnp.sum(rt * kt * p, axis=1, keepdims=True) * vt, r_s, k2, rkp, v_s)

    e_neg = each(lambda c_: twice(jnp.exp(-c_)), cs)
    e_tail = each(lambda t_, c_: twice(jnp.exp(t_ - c_)), tot, cs)
    a_t = each(lambda t, c_, l_: -t * twice(jnp.exp(c_ - l_)), kk, cs, ld)
    r_t = each(lambda t, c_: t * twice(jnp.exp(c_)), r_s, cs)
    k_t = each(lambda t, e: t * e, k2, e_neg)
    b_t = each(lambda t, e: t * e, b, e_neg)
    k_h = each(lambda t, e: t * e, k2, e_tail)
    b_h = each(lambda t, e: t * e, b, e_tail)

    lhs1 = each(lambda x_, y_: jnp.concatenate([x_, y_], axis=0), a_t, r_t)
    rhs1 = each(lambda x_, y_: jnp.concatenate([x_, y_], axis=0), k_t, b_t)
    x = each(lambda l_, r_: _mm(l_, r_, nt=True), lhs1, rhs1)
    a_k = each(lambda t: jnp.where(strict, t[0:n_st, 0:n_st], 0.0), x)
    a_b = each(lambda t: jnp.where(strict, t[0:n_st, n_st:], 0.0), x)
    r_kb = each(lambda t: jnp.concatenate([jnp.where(incl, t[n_st:, 0:n_st], 0.0),
                                           jnp.where(incl, t[n_st:, n_st:], 0.0)], axis=1), x)

    p1 = each(lambda l_, s_: _mm(l_, s_, nt=True), lhs1, states)
    rhs_u = each(lambda p_, ak, vt: p_[0:n_st] + _mm(ak, vt), p1, a_k, v_s)

    near = (row // 2) == (lane // 2)
    t_inv = each(lambda t: eye + jnp.where(near, t, 0.0), a_b)
    s = 2
    while s < SLAB:
        off = ((row // (2 * s)) == (lane // (2 * s))) & ((row // s) != (lane // s))
        w = each(lambda ab, ti: _mm(jnp.where(off, ab, 0.0), ti), a_b, t_inv)
        t_inv = each(lambda ti, w_: ti + _mm(ti, w_), t_inv, w)
        s *= 2
    u = each(_mm, t_inv, rhs_u)

    vu = each(lambda x_, y_: jnp.concatenate([x_, y_], axis=0), v_s, u)
    y = each(lambda p_, rk_, vu_: p_[n_st:] + _mm(rk_, vu_), p1, r_kb, vu)

    def finish(y_, w_, b_, bonus_, g_):
        mean = jnp.sum(y_, axis=1, keepdims=True) * (1.0 / HEAD_DIM)
        dev = jnp.where(own, y_ - mean, 0.0)
        var = jnp.sum(dev * dev, axis=1, keepdims=True) * (1.0 / HEAD_DIM)
        out = jnp.where(own, dev * lax.rsqrt(var + GN_EPS) * w_ + b_, 0.0) + bonus_
        return (out[0:SLAB] + out[SLAB:n_st]) * g_
    z = each(finish, y, gnw, gnb, bonus, g)

    vu_t = each(lambda t: t.T, vu)
    kb_h = each(lambda x_, y_: jnp.concatenate([x_, y_], axis=0), k_h, b_h)
    new_states = each(lambda s_, t_, vt, kb: s_ * jnp.exp(t_[0:1, :]) + _mm(vt, kb), states, tot, vu_t, kb_h)
    return z, new_states


def _scan_kernel(n_seq, n_pp, r_ref, ld_ref, k_ref, v_ref, a_ref, g_ref, kk_ref, ka_ref, rk_ref, gw_ref,
                 gb_ref, z_ref, s_out_ref, s_scr):
    c = pl.program_id(1)

    @pl.when(c == 0)
    def _():
        s_scr[...] = jnp.zeros_like(s_scr)

    chains = [(b, pp) for b in range(n_seq) for pp in range(n_pp)]
    lanes = [slice(pp * LANES, (pp + 1) * LANES) for _, pp in chains]
    acts = [tuple(ref[b, :, sl] for ref in (r_ref, ld_ref, k_ref, v_ref, a_ref, g_ref))
            for (b, _), sl in zip(chains, lanes)]
    params = [tuple(ref[:, sl] for ref in (kk_ref, ka_ref, rk_ref, gw_ref, gb_ref)) for sl in lanes]
    states = [s_scr[n] for n in range(len(chains))]
    z, new_states = _scan_chains(acts, params, states)
    for n, (b, _) in enumerate(chains):
        z_ref[b, :, lanes[n]] = z[n].astype(bf16)
        s_scr[n] = new_states[n]

    @pl.when(c == pl.num_programs(1) - 1)
    def _():
        for n, (b, pp) in enumerate(chains):
            s = s_scr[n]
            s_out_ref[b, 2 * pp] = s[0:HEAD_DIM, 0:HEAD_DIM]
            s_out_ref[b, 2 * pp + 1] = s[HEAD_DIM:, HEAD_DIM:]


def _scan(proj, rwp, n_batch, seq_len):
    n_pp = SCAN_CHAINS // n_batch
    w = LANES * n_pp
    act = pl.BlockSpec((n_batch, SLAB, w), lambda p, c: (0, c, p))
    vec = pl.BlockSpec((1, w), lambda p, c: (0, p))
    st_spec = pl.BlockSpec((n_batch, 2 * n_pp, HEAD_DIM, HEAD_DIM), lambda p, c: (0, p, 0, 0))
    z, s_out = pl.pallas_call(
        functools.partial(_scan_kernel, n_batch, n_pp),
        grid=(N_PAIRS // n_pp, seq_len // SLAB),
        in_specs=[act] * 6 + [vec] * 5,
        out_specs=[act, st_spec],
        out_shape=[jax.ShapeDtypeStruct((n_batch, seq_len, D_MODEL), bf16),
                   jax.ShapeDtypeStruct((n_batch, N_HEADS, HEAD_DIM, HEAD_DIM), f32)],
        scratch_shapes=[pltpu.VMEM((n_batch * n_pp, 2 * SLAB, LANES), f32)],
        compiler_params=_cparams(2),
        name="rwkv_scan",
    )(*[p.reshape(n_batch, seq_len, D_MODEL) for p in proj],
      rwp["kk"], rwp["ka"], rwp["rk"], rwp["gn_w"], rwp["gn_b"])
    return z.reshape(n_batch * seq_len, D_MODEL), s_out


def _scan_lanes_kernel(n_tok, r_ref, ld_ref, k_ref, v_ref, a_ref, g_ref, kk_ref, ka_ref, rk_ref, gw_ref, gb_ref,
                       s_in_ref, z_ref, s_out_ref, kk_s, w_s, b_s, k2_s, r_s, v_s, y_s):
    hd = HEAD_DIM
    colsum = lambda t: jnp.sum(t, axis=0, keepdims=True)
    for t in range(n_tok):
        r_t, k_t, v_t, a_t = r_ref[t].T, k_ref[t].T, v_ref[t].T, a_ref[t].T
        kk_raw = k_t * kk_ref[...]
        k2 = k_t * (1.0 + (a_t - 1.0) * ka_ref[...])
        w = jnp.exp(ld_ref[t].T)
        for e in range(2):
            sl = slice(e * hd, (e + 1) * hd)
            kr = kk_raw[sl]
            kk = kr * (1.0 / jnp.maximum(jnp.sqrt(colsum(kr * kr)), 1e-12))
            kk_s[e, t] = kk
            b_s[e, t] = kk * a_t[sl]
            k2_s[e, t] = k2[sl]
            w_s[e, t] = w[sl]
            r_s[e, t] = r_t[sl]
            v_s[e, t] = v_t[sl]

    rows_per_iter = 2

    def per_rows(i, carry):
        chains = [(e, rows_per_iter * i + d) for e in range(2) for d in range(rows_per_iter)]
        s = [s_in_ref[e, vi] for e, vi in chains]
        for t in range(n_tok):
            for c, (e, vi) in enumerate(chains):
                s_kk = colsum(s[c] * kk_s[e, t])
                s[c] = s[c] * w_s[e, t] - s_kk * b_s[e, t] + v_s[e, t, pl.ds(vi, 1), :] * k2_s[e, t]
                y_s[e, t, pl.ds(vi, 1), :] = colsum(s[c] * r_s[e, t])
        for c, (e, vi) in enumerate(chains):
            s_out_ref[e, vi] = s[c]
        return carry
    lax.fori_loop(0, hd // rows_per_iter, per_rows, 0)

    for t in range(n_tok):
        outs = []
        for e in range(2):
            sl = slice(e * hd, (e + 1) * hd)
            y = y_s[e, t]
            dev = y - colsum(y) * (1.0 / hd)
            var = colsum(dev * dev) * (1.0 / hd)
            bonus = colsum(r_s[e, t] * k2_s[e, t] * rk_ref[sl, :]) * v_s[e, t]
            outs.append(dev * lax.rsqrt(var + GN_EPS) * gw_ref[sl, :] + gb_ref[sl, :] + bonus)
        z_ref[t] = (jnp.concatenate(outs, axis=0).T * g_ref[t]).astype(bf16)


def _scan_lanes(proj, rwp, n_batch, n_tok, state):
    acts = [p.reshape(n_tok, n_batch, D_MODEL) for p in proj]
    lane_bc = lambda v: jnp.broadcast_to(v.reshape(D_MODEL, 1), (D_MODEL, n_batch))
    params = [lane_bc(rwp[n]) for n in ("kk", "ka", "rk", "gn_w", "gn_b")]
    act = pl.BlockSpec((n_tok, n_batch, LANES), lambda p: (0, 0, p))
    par = pl.BlockSpec((LANES, n_batch), lambda p: (p, 0))
    st_spec = pl.BlockSpec((2, HEAD_DIM, HEAD_DIM, n_batch), lambda p: (p, 0, 0, 0))
    per_tok = pltpu.VMEM((2, n_tok, HEAD_DIM, n_batch), f32)
    z, s_out = pl.pallas_call(
        functools.partial(_scan_lanes_kernel, n_tok),
        grid=(N_PAIRS,),
        in_specs=[act] * 6 + [par] * 5 + [st_spec],
        out_specs=[act, st_spec],
        out_shape=[jax.ShapeDtypeStruct((n_tok, n_batch, D_MODEL), bf16),
                   jax.ShapeDtypeStruct(state.shape, f32)],
        scratch_shapes=[per_tok] * 7,
        compiler_params=_cparams(1),
        name="rwkv_decode_scan",
    )(*acts, *params, state)
    return z.reshape(n_tok * n_batch, D_MODEL), s_out


def _proj_res_kernel(z_ref, w_ref, b_ref, x_ref, gt_ref, o_ref):
    out = _dot(z_ref[...], w_ref[...]) + b_ref[...]
    o_ref[...] = x_ref[...] + _rows(gt_ref[...], x_ref.shape[0]) * out


def _proj_res(z, w, bias, x, mod, tok_major, seq_len):
    m = x.shape[0]
    tm = min(ROW_TILE, m)
    rows = pl.BlockSpec((tm, D_MODEL), lambda i: (i, 0))
    return pl.pallas_call(
        _proj_res_kernel,
        grid=(m // tm,),
        in_specs=[rows,
                  pl.BlockSpec((D_MODEL, D_MODEL), lambda i: (0, 0), pipeline_mode=pl.Buffered(1)),
                  pl.BlockSpec((1, D_MODEL), lambda i: (0, 0)),
                  rows,
                  _mod_spec(tok_major, tm, seq_len, 5)],
        out_specs=rows,
        out_shape=jax.ShapeDtypeStruct((m, D_MODEL), f32),
        compiler_params=_cparams(1),
        name="proj_res",
    )(z, w, bias.reshape(1, D_MODEL), x, mod)


KV_COLS = N_KV_HEADS * HEAD_DIM
QKV_COLS = D_MODEL + 2 * KV_COLS


def _qkv_kernel(x_ref, ng_ref, sh_ref, sc_ref, w_ref, b_ref, cos_ref, sin_ref, o_ref, h_scr):
    @pl.when(pl.program_id(1) == 0)
    def _():
        tm = x_ref.shape[0]
        h_scr[...] = _rms_mod(x_ref[...], ng_ref[...], _rows(sh_ref[...], tm), _rows(sc_ref[...], tm)).astype(bf16)

    first = (lax.broadcasted_iota(jnp.int32, (x_ref.shape[0], LANES), 1) % HEAD_DIM) < HEAD_DIM // 2
    h = h_scr[...]
    for c0 in range(0, o_ref.shape[1], MXU_COLS):
        acc = _dot(h, w_ref[:, c0:c0 + MXU_COLS]) + b_ref[:, c0:c0 + MXU_COLS]
        group = slice(c0 // MXU_COLS * LANES, (c0 // MXU_COLS + 1) * LANES)
        cos, sin = cos_ref[:, group], sin_ref[:, group]
        for c in range(0, MXU_COLS, LANES):
            xc = acc[:, c:c + LANES]
            rot = jnp.where(first, pltpu.roll(xc, LANES - HEAD_DIM // 2, 1),
                            pltpu.roll(xc, HEAD_DIM // 2, 1))
            o_ref[:, c0 + c:c0 + c + LANES] = xc * cos + rot * sin


def _qkv(x, mod, tok_major, seq_len, norm_g, w, b, cos, sin):
    m = x.shape[0]
    tm = min(ROW_TILE, m)
    tn = QKV_TILE
    assert tn == 2 * MXU_COLS == 2 * KV_COLS
    n_pos_blocks = cos.shape[0] // tm
    vec = pl.BlockSpec((1, D_MODEL), lambda i, j: (0, 0))
    tab = pl.BlockSpec((tm, 2 * LANES), lambda i, j: (i % n_pos_blocks, j // (D_MODEL // tn)))
    return pl.pallas_call(
        _qkv_kernel,
        grid=(m // tm, QKV_COLS // tn),
        in_specs=[pl.BlockSpec((tm, D_MODEL), lambda i, j: (i, 0)),
                  vec,
                  _mod_spec(tok_major, tm, seq_len, 3),
                  _mod_spec(tok_major, tm, seq_len, 4),
                  pl.BlockSpec((D_MODEL, tn), lambda i, j: (0, j)),
                  pl.BlockSpec((1, tn), lambda i, j: (0, j)),
                  tab, tab],
        out_specs=pl.BlockSpec((tm, tn), lambda i, j: (i, j)),
        out_shape=jax.ShapeDtypeStruct((m, QKV_COLS), f32),
        scratch_shapes=[pltpu.VMEM((tm, D_MODEL), bf16)],
        compiler_params=_cparams(2),
        name="swa_qkv",
    )(x, norm_g.reshape(1, D_MODEL), mod, mod, w, b.reshape(1, QKV_COLS), cos, sin)


def _rope_tables(positions):
    half = HEAD_DIM // 2
    inv_freq = ROPE_THETA ** (-jnp.arange(half, dtype=f32) / half)
    ang = positions.astype(f32)[:, None] * inv_freq[None, :]
    cos = jnp.tile(jnp.cos(ang), (1, LANES // half))
    sin = jnp.sin(ang)
    sin = jnp.tile(jnp.concatenate([-sin, sin], axis=1), (1, LANES // HEAD_DIM))
    return (jnp.concatenate([cos, cos, cos, jnp.ones_like(cos)], axis=1),
            jnp.concatenate([sin, sin, sin, jnp.zeros_like(sin)], axis=1))


def _head_masks(rows):
    lane = lax.broadcasted_iota(jnp.int32, (rows, LANES), 1)
    return lane < HEAD_DIM, lane >= HEAD_DIM


def _both_halves(pair):
    first, second = _head_masks(pair.shape[0])
    swapped = pltpu.roll(pair, HEAD_DIM, 1)
    return jnp.where(first, pair, swapped), jnp.where(second, pair, swapped)


def _attn_prompt_kernel(q_ref, kc_ref, vc_ref, sink_ref, o_ref, k_prev, v_prev):
    n = pl.program_id(1)
    blk = WINDOW
    m0, m1 = _head_masks(blk)
    m0k, m1k = _head_masks(2 * blk)
    qi = lax.broadcasted_iota(jnp.int32, (2 * blk, 2 * blk), 0) % blk
    sj = lax.broadcasted_iota(jnp.int32, (2 * blk, 2 * blk), 1)
    visible = (sj > qi) & (sj <= qi + blk) & ((n > 0) | (sj >= blk))
    top = lax.broadcasted_iota(jnp.int32, (2 * blk, 1), 0) < blk
    @pl.when(n == 0)
    def _():
        k_prev[...] = jnp.zeros_like(k_prev)
        v_prev[...] = jnp.zeros_like(v_prev)

    tiles = [slice(t * LANES, (t + 1) * LANES) for t in range(N_KV_HEADS // 2)]
    k_cur = [h.astype(bf16) for sl in tiles for h in _both_halves(kc_ref[:, sl])]
    v_cur = [h for sl in tiles for h in _both_halves(vc_ref[:, sl])]
    for c in range(N_KV_HEADS):
        kd = jnp.concatenate([k_prev[c], k_cur[c]], axis=0)
        vd = jnp.concatenate([v_prev[c], v_cur[c]], axis=0)
        vcat = jnp.concatenate([jnp.where(m0k, vd, 0.0), jnp.where(m1k, vd, 0.0)], axis=0).astype(bf16)
        pairs = [c * (GQA_GROUP // 2) + jj for jj in range(GQA_GROUP // 2)]
        each = lambda fn, *lists: [fn(*args) for args in zip(*lists)]
        qp = [q_ref[:, pair * LANES:(pair + 1) * LANES] * ATTN_SCALE for pair in pairs]
        qs = each(lambda t: jnp.concatenate([jnp.where(m0, t, 0.0), jnp.where(m1, t, 0.0)], axis=0).astype(bf16), qp)
        s = each(lambda t: jnp.where(visible, _dot_nt(t, kd), NEG_BIG), qs)
        sk = [jnp.where(top, sink_ref[2 * pair], sink_ref[2 * pair + 1]) for pair in pairs]
        mx = each(lambda t, k_: jnp.maximum(jnp.max(t, axis=1, keepdims=True), k_), s, sk)
        p = each(lambda t, m_: jnp.exp(t - m_), s, mx)
        den = each(lambda t, k_, m_: jnp.sum(t, axis=1, keepdims=True) + jnp.exp(k_ - m_), p, sk, mx)
        p = each(lambda t, d_: (t * (1.0 / d_)).astype(bf16), p, den)
        pcat = each(lambda t: jnp.concatenate([t[0:blk], t[blk:]], axis=1), p)
        for pair, t in zip(pairs, pcat):
            o_ref[:, pair * LANES:(pair + 1) * LANES] = _dot(t, vcat).astype(bf16)
    for c in range(N_KV_HEADS):
        k_prev[c] = k_cur[c]
        v_prev[c] = v_cur[c]


def _attn_prompt(qkv, sink, n_batch, seq_len):
    nb = seq_len // WINDOW
    k_blk = D_MODEL // KV_COLS
    cur = lambda off: pl.BlockSpec((WINDOW, KV_COLS), lambda b, n: (b * nb + n, k_blk + off))
    return pl.pallas_call(
        _attn_prompt_kernel,
        grid=(n_batch, nb),
        in_specs=[pl.BlockSpec((WINDOW, D_MODEL), lambda b, n: (b * nb + n, 0)),
                  cur(0), cur(1),
                  pl.BlockSpec(memory_space=pltpu.SMEM)],
        out_specs=pl.BlockSpec((WINDOW, D_MODEL), lambda b, n: (b * nb + n, 0)),
        out_shape=jax.ShapeDtypeStruct((n_batch * seq_len, D_MODEL), bf16),
        scratch_shapes=[pltpu.VMEM((N_KV_HEADS, WINDOW, LANES), bf16), pltpu.VMEM((N_KV_HEADS, WINDOW, LANES), f32)],
        compiler_params=_cparams(2),
        name="swa_prompt",
    )(qkv, qkv, qkv, sink)


SEQ_PER_GROUP = 4
GROUPS_PER_STEP = 2


def _attn_cached_kernel(n_tok, groups, q_ref, kn_ref, vn_ref, kc_ref, vc_ref, sink_ref, o_ref):
    rows = SEQ_PER_GROUP * n_tok
    n_st = GQA_GROUP * rows
    m0, m1 = _head_masks(rows)
    srow = lax.broadcasted_iota(jnp.int32, (n_st, 1), 0)
    row_seq = (srow % rows) // n_tok
    row_tok = srow % n_tok
    key_c = lax.broadcasted_iota(jnp.int32, (n_st, WINDOW), 1)
    vis_c = key_c > row_tok
    key_n = lax.broadcasted_iota(jnp.int32, (n_st, rows), 1)
    vis_n = ((key_n // n_tok) == row_seq) & ((key_n % n_tok) <= row_tok)
    for gi in range(groups):
        rs = slice(gi * rows, (gi + 1) * rows)
        for cp in range(N_KV_HEADS // 2):
            for ce in range(2):
                c = 2 * cp + ce
                sl = slice(cp * LANES, (cp + 1) * LANES)
                kt = [kc_ref[gi * SEQ_PER_GROUP + b, c * HEAD_DIM:(c + 1) * HEAD_DIM, :] for b in range(SEQ_PER_GROUP)]
                vt = [vc_ref[gi * SEQ_PER_GROUP + b, c * HEAD_DIM:(c + 1) * HEAD_DIM, :] for b in range(SEQ_PER_GROUP)]
                kn = _both_halves(kn_ref[rs, sl])[ce].astype(bf16)
                vn = _both_halves(vn_ref[rs, sl])[ce].astype(bf16)
                pieces = []
                for jj in range(GQA_GROUP // 2):
                    pair = c * (GQA_GROUP // 2) + jj
                    qp = q_ref[rs, pair * LANES:(pair + 1) * LANES]
                    pieces += [jnp.where(m0, qp, 0.0), jnp.where(m1, qp, 0.0)]
                qs = jnp.concatenate(pieces, axis=0).astype(bf16)
                s_c = jnp.zeros((n_st, WINDOW), f32)
                for b in range(SEQ_PER_GROUP):
                    kx = jnp.concatenate([kt[b], kt[b]], axis=0).astype(bf16)
                    s_c = jnp.where(row_seq == b, _dot(qs, kx), s_c)
                s_c = jnp.where(vis_c, s_c * ATTN_SCALE, NEG_BIG)
                s_n = jnp.where(vis_n, _dot_nt(qs, kn) * ATTN_SCALE, NEG_BIG)
                sk = sink_ref[c]
                sk = sk[:, 0:1]
                mx = jnp.maximum(jnp.maximum(jnp.max(s_c, axis=1, keepdims=True),
                                             jnp.max(s_n, axis=1, keepdims=True)), sk)
                p_c = jnp.exp(s_c - mx)
                p_n = jnp.exp(s_n - mx)
                den = (jnp.sum(p_c, axis=1, keepdims=True) + jnp.sum(p_n, axis=1, keepdims=True)
                       + jnp.exp(sk - mx))
                inv_den = 1.0 / den
                p_c = (p_c * inv_den).astype(bf16)
                p_n = (p_n * inv_den).astype(bf16)
                o = _dot(p_n, vn)
                for b in range(SEQ_PER_GROUP):
                    vx = jnp.concatenate([vt[b], vt[b]], axis=0).astype(bf16)
                    o = o + jnp.where(row_seq == b, _dot_nt(p_c, vx), 0.0)
                for jj in range(GQA_GROUP // 2):
                    pair = c * (GQA_GROUP // 2) + jj
                    o0 = o[(2 * jj) * rows:(2 * jj + 1) * rows]
                    o1 = o[(2 * jj + 1) * rows:(2 * jj + 2) * rows]
                    o_ref[rs, pair * LANES:(pair + 1) * LANES] = jnp.where(m0, o0, o1).astype(bf16)


def _attn_cached(qkv, k_cache, v_cache, sink, n_batch, n_tok):
    groups = GROUPS_PER_STEP
    seqs = SEQ_PER_GROUP * groups
    rows = seqs * n_tok
    k_blk = D_MODEL // KV_COLS
    n_st = GQA_GROUP * SEQ_PER_GROUP * n_tok
    head = (jnp.arange(N_KV_HEADS)[:, None] * GQA_GROUP
            + (jnp.arange(n_st)[None, :] // (SEQ_PER_GROUP * n_tok)))
    sink_tab = jnp.broadcast_to(sink[head][:, :, None], (N_KV_HEADS, n_st, LANES))
    cache_spec = pl.BlockSpec((seqs, N_KV_HEADS * HEAD_DIM, WINDOW), lambda i: (i, 0, 0))
    return pl.pallas_call(
        functools.partial(_attn_cached_kernel, n_tok, groups),
        grid=(n_batch // seqs,),
        in_specs=[pl.BlockSpec((rows, D_MODEL), lambda i: (i, 0)),
                  pl.BlockSpec((rows, KV_COLS), lambda i: (i, k_blk)),
                  pl.BlockSpec((rows, KV_COLS), lambda i: (i, k_blk + 1)),
                  cache_spec, cache_spec,
                  pl.BlockSpec((N_KV_HEADS, n_st, LANES), lambda i: (0, 0, 0))],
        out_specs=pl.BlockSpec((rows, D_MODEL), lambda i: (i, 0)),
        out_shape=jax.ShapeDtypeStruct((n_batch * n_tok, D_MODEL), bf16),
        compiler_params=_cparams(1),
        name="swa_cached",
    )(qkv, qkv, qkv, k_cache, v_cache, sink_tab)


def _trunk(x, mod_all, tok_major, n_batch, seq_len, pos0, wkv_in, shift_in, k_in, v_in, wts):
    to_seq_major = lambda t: t.reshape(seq_len, n_batch, -1).transpose(1, 0, 2).reshape(n_batch * seq_len, -1)
    to_tok_major = lambda t: t.reshape(n_batch, seq_len, -1).transpose(1, 0, 2).reshape(n_batch * seq_len, -1)
    mod = mod_all[0]
    ffn_w = dict(wts["ffn"])
    x, ffn_w[0, 0] = _ffn(x, mod, tok_major, seq_len, 0, wts["norm_g"][0, 0], ffn_w[0, 0])
    s0 = shift_in[0] if tok_major else jnp.zeros((n_batch, 1, D_MODEL), f32)
    *proj, h_tail = _rwkv_proj(x, mod, tok_major, seq_len, wts["norm_g"][0, 1], s0, wts["rw"])
    if tok_major:
        z, wkv_t = _scan_lanes(proj, wts["rw"], n_batch, seq_len, jnp.transpose(wkv_in[0], (1, 2, 3, 0)))
        wkv_out = jnp.transpose(wkv_t, (3, 0, 1, 2))
        shift_out = h_tail
    else:
        z, wkv_out = _scan(proj, wts["rw"], n_batch, seq_len)
        shift_out = h_tail.reshape(n_batch, -1, SUBLANES, D_MODEL)[:, -1, -1]
    x = _proj_res(z, wts["rw"]["wo"], jnp.zeros((D_MODEL,), f32), x, mod, tok_major, seq_len)
    x, ffn_w[0, 1] = _ffn(x, mod, tok_major, seq_len, 6, wts["norm_g"][0, 2], ffn_w[0, 1])
    mod = mod_all[1]
    x, ffn_w[1, 0] = _ffn(x, mod, tok_major, seq_len, 0, wts["norm_g"][1, 0], ffn_w[1, 0])
    positions = pos0 + jnp.arange(seq_len)
    if tok_major:
        positions = jnp.repeat(positions, n_batch)
    cos, sin = _rope_tables(positions)
    qkv = _qkv(x, mod, tok_major, seq_len, wts["norm_g"][1, 1], wts["sw_wqkv"], wts["sw_bqkv"], cos, sin)
    heads = lambda t: t.reshape(n_batch, -1, N_KV_HEADS, HEAD_DIM)
    if k_in is None:
        att = _attn_prompt(qkv, wts["sw_sink"], n_batch, seq_len)
        tail = qkv.reshape(n_batch, seq_len, QKV_COLS)[:, -WINDOW:, D_MODEL:]
        k_new, v_new = heads(tail[..., :KV_COLS]), heads(tail[..., KV_COLS:])
    else:
        win = k_in.shape[2]
        qkv = to_seq_major(qkv)
        cache_t = lambda t: jnp.transpose(t[0], (0, 2, 3, 1)).reshape(n_batch, N_KV_HEADS * HEAD_DIM, win)
        att = _attn_cached(qkv, cache_t(k_in), cache_t(v_in), wts["sw_sink"], n_batch, seq_len)
        att = to_tok_major(att)
        k_tok, v_tok = heads(qkv[:, D_MODEL:D_MODEL + KV_COLS]), heads(qkv[:, D_MODEL + KV_COLS:])
        k_new = jnp.concatenate([k_in[0], k_tok], axis=1)[:, -win:]
        v_new = jnp.concatenate([v_in[0], v_tok], axis=1)[:, -win:]
    x = _proj_res(att, wts["sw_wo"], wts["sw_bo"], x, mod, tok_major, seq_len)
    y, ffn_w[1, 1] = _ffn(x, mod, tok_major, seq_len, 6, wts["norm_g"][1, 2], ffn_w[1, 1], final_g=wts["final_g"])
    return (y, wkv_out[None], shift_out[None], k_new[None], v_new[None]), ffn_w


def _pad_cols(w):
    return jnp.pad(w, ((0, 0), (0, LORA_PAD - w.shape[1])))


def _pad_rows(w):
    return jnp.pad(w, ((0, LORA_PAD - w.shape[0]), (0, 0)))


def kernel(x_prompt, x_sample, state_rwkv_wkv, state_rwkv_shift, cache_swa_k, cache_swa_v, c_prompt, c_sample, norm_g, w_ada, b_ada, w_ffn_in, w_ffn_out, rw_mu, rw_wrkv, rw_w0, rw_w1, rw_w2, rw_a0, rw_a1, rw_a2, rw_g1, rw_g2, rw_kk, rw_ka, rw_rk, rw_gn_w, rw_gn_b, rw_wo, sw_wqkv, sw_bqkv, sw_sink, sw_wo, sw_bo, final_g):
    n_p, seq_p, _ = x_prompt.shape
    n_s, seq_s, _ = x_sample.shape
    row = lambda t: t.reshape(1, D_MODEL)
    rw = dict(
        mu=rw_mu[0], wrkv=rw_wrkv[0].astype(bf16),
        w1=_pad_cols(rw_w1[0]).astype(bf16), a1=_pad_cols(rw_a1[0]).astype(bf16), g1=rw_g1[0].astype(bf16),
        w2=_pad_rows(rw_w2[0]).astype(bf16), a2=_pad_rows(rw_a2[0]).astype(bf16), g2=rw_g2[0].astype(bf16),
        w0=row(rw_w0[0]), a0=row(rw_a0[0]), kk=row(rw_kk[0]), ka=row(rw_ka[0]), rk=row(rw_rk[0]),
        gn_w=row(rw_gn_w[0]), gn_b=row(rw_gn_b[0]), wo=rw_wo[0].astype(bf16))
    wts = dict(
        norm_g=norm_g, final_g=final_g,
        ffn={(l, s): (w_ffn_in, w_ffn_out, l, s) for l in range(2) for s in range(2)}, rw=rw,
        sw_wqkv=sw_wqkv[0].astype(bf16), sw_bqkv=sw_bqkv[0],
        sw_sink=sw_sink[0], sw_wo=sw_wo[0].astype(bf16), sw_bo=sw_bo[0])

    n_c = n_p + n_s
    pad = (-n_c) % SUBLANES
    c_all = jnp.concatenate([c_sample, c_prompt, jnp.zeros((pad, D_MODEL), f32)], axis=0)
    mod_s = _ada(c_all, w_ada, b_ada)
    mod_p = [m[n_s:n_c].reshape(n_p, 1, N_MOD * D_MODEL) for m in mod_s]

    (y_s, s_wkv, s_shift, s_k, s_v), wts["ffn"] = _trunk(
        jnp.transpose(x_sample, (1, 0, 2)).reshape(n_s * seq_s, D_MODEL), mod_s, True, n_s, seq_s, PAST_LEN,
        state_rwkv_wkv, state_rwkv_shift, cache_swa_k, cache_swa_v, wts)
    (y_p, p_wkv, p_shift, p_k, p_v), _ = _trunk(
        x_prompt.reshape(n_p * seq_p, D_MODEL), mod_p, False, n_p, seq_p, 0, None, None, None, None, wts)
    y_s = jnp.transpose(y_s.reshape(seq_s, n_s, D_MODEL), (1, 0, 2))
    return (y_p.reshape(n_p, seq_p, D_MODEL), y_s, p_wkv, p_shift, p_k, p_v, s_wkv, s_shift, s_k, s_v)
```
